```python
import jax, jax.numpy as jnp
from jax import lax
import numpy as np

D_MODEL = 1024
BATCH = 8
SEQ = 4096
DEPTH = 1

D_MIX = D_MODEL
HEAD_DIM = 128
GM_HEADS = 4
GM_WIDTH = GM_HEADS * HEAD_DIM
GM_CHUNK = 128
DN_HEADS = 4
DN_DK = 128
DN_DV = 128
DN_KEY = DN_HEADS * DN_DK
DN_VAL = DN_HEADS * DN_DV
DN_CHUNK = 64
CONV_W = 4
D_IN = 2 * GM_WIDTH + 2 * DN_KEY + 2 * DN_VAL + 2 * DN_HEADS
N_EXPERTS = 32
TOP_K = 4
D_FF = D_MODEL
SWIGLU_LIMIT = 7.0
SWIGLU_ALPHA = 1.702
MOE_BLOCK = 128
EPS = 1e-6

kernel_name = "hymba_gmlp_gdn_moe_layer"


def rmsnorm(x, g):
    xf = x.astype(jnp.float32)
    y = xf * lax.rsqrt(jnp.mean(xf * xf, axis=-1, keepdims=True) + EPS)
    return (y * g.astype(jnp.float32)).astype(x.dtype)


def l2norm(x):
    xf = x.astype(jnp.float32)
    return xf * lax.rsqrt(jnp.sum(xf * xf, axis=-1, keepdims=True) + EPS)


def gmlp_chunk_mixer(u, v, norm_g, ws, bs):
    b_, s_, _ = u.shape
    n = s_ // GM_CHUNK
    u = jax.nn.gelu(u, approximate=False).reshape(b_, n, GM_CHUNK, GM_HEADS, HEAD_DIM)
    v = jax.nn.gelu(v, approximate=False).reshape(b_, n, GM_CHUNK, GM_HEADS, HEAD_DIM)
    v = rmsnorm(v, norm_g)
    causal = jnp.tril(jnp.ones((GM_CHUNK, GM_CHUNK), dtype=bool))
    w = jnp.where(causal, ws, jnp.zeros_like(ws)).astype(v.dtype)
    gate = jnp.einsum('hts,bnshc->bnthc', w, v) + bs.T.astype(v.dtype)[None, None, :, :, None]
    return (u * gate).reshape(b_, s_, GM_WIDTH)


def causal_dwconv(x, w):
    c = x.shape[-1]
    return lax.conv_general_dilated(
        x, w[:, None, :].astype(x.dtype), window_strides=(1,),
        padding=[(CONV_W - 1, 0)], dimension_numbers=('NWC', 'WIO', 'NWC'),
        feature_group_count=c)


def gated_delta_rule(q, k, v, g, beta):
    b_, h_, s_, dk = q.shape
    dv = v.shape[-1]
    c = DN_CHUNK
    n = s_ // c
    q = q * (dk ** -0.5)
    qc = q.reshape(b_, h_, n, c, dk)
    kc = k.reshape(b_, h_, n, c, dk)
    vc = v.reshape(b_, h_, n, c, dv)
    bc = beta.reshape(b_, h_, n, c)
    gc = jnp.cumsum(g.reshape(b_, h_, n, c), axis=-1)
    tril_incl = jnp.tril(jnp.ones((c, c), dtype=bool))
    tril_strict = jnp.tril(jnp.ones((c, c), dtype=bool), -1)
    diff = gc[..., :, None] - gc[..., None, :]
    decay = jnp.where(tril_incl, jnp.exp(jnp.where(tril_incl, diff, 0.0)), 0.0)
    kb = kc * bc[..., None]
    lower = jnp.where(tril_strict, jnp.einsum('bhnid,bhnjd->bhnij', kb, kc) * decay, 0.0)
    a_mat = lower + jnp.eye(c, dtype=lower.dtype)
    rhs = jnp.concatenate([vc * bc[..., None], kb * jnp.exp(gc)[..., None]], axis=-1)
    sol = lax.linalg.triangular_solve(a_mat, rhs, left_side=True, lower=True, unit_diagonal=True)
    u_c = sol[..., :dv]
    w_c = sol[..., dv:]
    attn_intra = jnp.where(tril_incl, jnp.einsum('bhnid,bhnjd->bhnij', qc, kc) * decay, 0.0)
    g_last = gc[..., -1]
    k_to_end = kc * jnp.exp(g_last[..., None] - gc)[..., None]
    q_dec = qc * jnp.exp(gc)[..., None]

    def step(state, inp):
        q_i, u_i, w_i, a_i, k_e, gl = inp
        v_new = u_i - jnp.einsum('bhcd,bhde->bhce', w_i, state)
        o = jnp.einsum('bhcd,bhde->bhce', q_i, state) + jnp.einsum('bhij,bhje->bhie', a_i, v_new)
        state = state * jnp.exp(gl)[..., None, None] + jnp.einsum('bhcd,bhce->bhde', k_e, v_new)
        return state, o

    xs = tuple(jnp.moveaxis(t, 2, 0) for t in (q_dec, u_c, w_c, attn_intra, k_to_end, g_last))
    s0 = jnp.zeros((b_, h_, dk, dv), jnp.float32)
    _, o = lax.scan(step, s0, xs)
    return jnp.moveaxis(o, 0, 2).reshape(b_, h_, s_, dv)


def gated_deltanet_mixer(qkv, z, a, bb, conv_w, a_log, dt_bias, norm_g):
    b_, s_, _ = qkv.shape
    qkv = jax.nn.silu(causal_dwconv(qkv, conv_w))
    q, k, v = jnp.split(qkv, 3, axis=-1)
    q = l2norm(q.reshape(b_, s_, DN_HEADS, DN_DK)).transpose(0, 2, 1, 3)
    k = l2norm(k.reshape(b_, s_, DN_HEADS, DN_DK)).transpose(0, 2, 1, 3)
    v = v.reshape(b_, s_, DN_HEADS, DN_DV).astype(jnp.float32).transpose(0, 2, 1, 3)
    g = -jnp.exp(a_log.astype(jnp.float32)) * jax.nn.softplus(a.astype(jnp.float32) + dt_bias.astype(jnp.float32))
    beta = jax.nn.sigmoid(bb.astype(jnp.float32))
    o = gated_delta_rule(q, k, v, g.transpose(0, 2, 1), beta.transpose(0, 2, 1))
    o = o.transpose(0, 2, 1, 3)
    o = rmsnorm(o, norm_g) * jax.nn.silu(z.reshape(b_, s_, DN_HEADS, DN_DV).astype(jnp.float32))
    return o.reshape(b_, s_, DN_VAL).astype(z.dtype)


def moe_ffn(h, router_w, router_b, w_gu, b_gu, w_down, b_down):
    b_, s_, d = h.shape
    t = b_ * s_
    xf = h.reshape(t, d)
    logits = (xf @ router_w + router_b).astype(jnp.float32)
    top_val, top_idx = lax.top_k(logits, TOP_K)
    gates = jax.nn.softmax(top_val, axis=-1)
    n_assign = t * TOP_K
    flat_e = top_idx.reshape(n_assign)
    order = jnp.argsort(flat_e)
    sorted_e = flat_e[order]
    counts = jnp.zeros((N_EXPERTS,), jnp.int32).at[flat_e].add(1)
    padded = (counts + MOE_BLOCK - 1) // MOE_BLOCK * MOE_BLOCK
    pad_end = jnp.cumsum(padded)
    pad_start = pad_end - padded
    start = jnp.cumsum(counts) - counts
    dest = pad_start[sorted_e] + jnp.arange(n_assign, dtype=jnp.int32) - start[sorted_e]
    n_blocks = -(-(n_assign + N_EXPERTS * MOE_BLOCK) // MOE_BLOCK)
    p = n_blocks * MOE_BLOCK
    buf_tok = jnp.full((p,), t, jnp.int32).at[dest].set((order // TOP_K).astype(jnp.int32))
    buf_gate = jnp.zeros((p,), jnp.float32).at[dest].set(gates.reshape(n_assign)[order])
    block_e = jnp.minimum(jnp.searchsorted(pad_end, jnp.arange(n_blocks, dtype=jnp.int32) * MOE_BLOCK, side='right'), N_EXPERTS - 1)
    x_pad = jnp.concatenate([xf, jnp.zeros((1, d), xf.dtype)], axis=0)
    xin = x_pad[buf_tok].reshape(n_blocks, MOE_BLOCK, d)

    def expert_block(args):
        xb, e = args
        gu = xb @ w_gu[e] + b_gu[e]
        gate, up = jnp.split(gu, 2, axis=-1)
        gate = jnp.minimum(gate, SWIGLU_LIMIT)
        up = jnp.clip(up, -SWIGLU_LIMIT, SWIGLU_LIMIT)
        return ((up + 1.0) * (gate * jax.nn.sigmoid(SWIGLU_ALPHA * gate))) @ w_down[e] + b_down[e]

    y = lax.map(expert_block, (xin, block_e)).reshape(p, d)
    out = jnp.zeros((t + 1, d), y.dtype).at[buf_tok].add(y * buf_gate[:, None].astype(y.dtype))
    return out[:t].reshape(b_, s_, d)


def setup_inputs(seed: int = 0) -> dict:
    key = jax.random.key(seed)
    ks = jax.random.split(key, 20)
    f32 = jnp.float32
    nrm = lambda k, shape, scale: jax.random.normal(k, shape, f32) * scale
    gain = lambda k, shape: 1.0 + 0.02 * jax.random.normal(k, shape, f32)
    dt = jnp.exp(jax.random.uniform(ks[8], (DEPTH, DN_HEADS), f32, np.log(1e-3), np.log(1e-1)))
    return {
        "x": nrm(ks[0], (BATCH, SEQ, D_MODEL), 1.0),
        "norm_mix_g": gain(ks[1], (DEPTH, D_MODEL)),
        "w_in": nrm(ks[2], (DEPTH, D_MODEL, D_IN), D_MODEL ** -0.5),
        "gm_norm_g": gain(ks[3], (DEPTH, GM_HEADS, HEAD_DIM)),
        "gm_ws": nrm(ks[4], (DEPTH, GM_HEADS, GM_CHUNK, GM_CHUNK), GM_CHUNK ** -0.5),
        "gm_bs": gain(ks[5], (DEPTH, GM_HEADS, GM_CHUNK)),
        "dn_conv_w": nrm(ks[6], (DEPTH, CONV_W, 3 * DN_KEY), CONV_W ** -0.5),
        "dn_a_log": jnp.log(jax.random.uniform(ks[7], (DEPTH, DN_HEADS), f32, 1.0, 16.0)),
        "dn_dt_bias": dt + jnp.log(-jnp.expm1(-dt)),
        "dn_norm_g": gain(ks[9], (DEPTH, DN_DV)),
        "w_out": nrm(ks[10], (DEPTH, D_MIX, D_MODEL), D_MIX ** -0.5),
        "norm_ffn_g": gain(ks[11], (DEPTH, D_MODEL)),
        "router_w": nrm(ks[12], (DEPTH, D_MODEL, N_EXPERTS), D_MODEL ** -0.5),
        "router_b": nrm(ks[13], (DEPTH, N_EXPERTS), 0.01),
        "exp_w_gu": nrm(ks[14], (DEPTH, N_EXPERTS, D_MODEL, 2 * D_FF), D_MODEL ** -0.5),
        "exp_b_gu": nrm(ks[15], (DEPTH, N_EXPERTS, 2 * D_FF), 0.01),
        "exp_w_down": nrm(ks[16], (DEPTH, N_EXPERTS, D_FF, D_MODEL), D_FF ** -0.5),
        "exp_b_down": nrm(ks[17], (DEPTH, N_EXPERTS, D_MODEL), 0.01),
        "final_norm_g": gain(ks[18], (D_MODEL,)),
    }


def reference(x, norm_mix_g, w_in, gm_norm_g, gm_ws, gm_bs, dn_conv_w, dn_a_log, dn_dt_bias,
              dn_norm_g, w_out, norm_ffn_g, router_w, router_b, exp_w_gu, exp_b_gu,
              exp_w_down, exp_b_down, final_norm_g):
    split_points = [GM_WIDTH, 2 * GM_WIDTH, 2 * GM_WIDTH + 2 * DN_KEY + DN_VAL,
                    2 * GM_WIDTH + 2 * DN_KEY + 2 * DN_VAL,
                    2 * GM_WIDTH + 2 * DN_KEY + 2 * DN_VAL + DN_HEADS]
    h = x
    for l in range(DEPTH):
        y = rmsnorm(h, norm_mix_g[l])
        proj = y @ w_in[l]
        u, v_gm, qkv, z, a, bb = jnp.split(proj, split_points, axis=-1)
        out_a = gmlp_chunk_mixer(u, v_gm, gm_norm_g[l], gm_ws[l], gm_bs[l])
        out_b = gated_deltanet_mixer(qkv, z, a, bb, dn_conv_w[l], dn_a_log[l], dn_dt_bias[l], dn_norm_g[l])
        mix = jnp.concatenate([out_a, out_b], axis=-1)
        h = h + mix @ w_out[l]
        h = h + moe_ffn(rmsnorm(h, norm_ffn_g[l]), router_w[l], router_b[l], exp_w_gu[l],
                        exp_b_gu[l], exp_w_down[l], exp_b_down[l])
    return rmsnorm(h, final_norm_g)
```

```python
import functools

import jax
import jax.numpy as jnp
from jax import lax
from jax.experimental import pallas as pl
from jax.experimental.pallas import tpu as pltpu

F32 = jnp.float32
BF16 = jnp.bfloat16
I32 = jnp.int32
U32 = jnp.uint32

D_MODEL = 1024
HEAD_DIM = 128
GM_HEADS = 4
GM_WIDTH = GM_HEADS * HEAD_DIM
GM_CHUNK = 128
DN_HEADS = 4
DN_DK = 128
DN_KEY = DN_HEADS * DN_DK
DN_VAL = DN_HEADS * HEAD_DIM
DN_CHUNK = 64
CONV_W = 4
N_EXPERTS = 32
TOP_K = 4
D_FF = D_MODEL
SWIGLU_LIMIT = 7.0
SWIGLU_ALPHA = 1.702
EPS = 1e-6

LANES = 128
ROW_TILE = 512
FFN_BLOCK = 512
COMBINE_TILE = 256
VMEM_LIMIT = 56 * 1024 * 1024
NEG_BIG = -1e30


def _dot(a, b):
    return jnp.dot(a, b, preferred_element_type=F32)


def _dot_nt(a, b):
    return lax.dot_general(a, b, (((1,), (1,)), ((), ())), preferred_element_type=F32)


def _dot_tn(a, b):
    return lax.dot_general(a, b, (((0,), (0,)), ((), ())), preferred_element_type=F32)


def _rms(x, g):
    return x * lax.rsqrt(jnp.mean(x * x, axis=-1, keepdims=True) + EPS) * g


def _gelu(x):
    return 0.5 * x * (1.0 + lax.erf(x * (2.0 ** -0.5)))


def _sigmoid(x):
    return 1.0 / (1.0 + jnp.exp(-x))


def _softplus(x):
    return jnp.maximum(x, 0.0) + jnp.log1p(jnp.exp(-jnp.abs(x)))


def _inproj_body(x_ref, ng_ref, wu_ref, wv_ref, wqkv_ref, wz_ref, wab_ref, gmg_ref, ws_ref,
                 bs_ref, cw_ref, alog_ref, dtb_ref, tri_ref,
                 oa_ref, q_ref, k_ref, v_ref, z_ref, gb_ref, gt_ref, cbuf_ref,
                 *, tm, tiles_per_seq):
    i = pl.program_id(0)
    y = _rms(x_ref[...], ng_ref[...]).astype(BF16)

    u = _gelu(_dot(y, wu_ref[...]))
    vg = _gelu(_dot(y, wv_ref[...]))
    ri = lax.broadcasted_iota(I32, (GM_CHUNK, GM_CHUNK), 0)
    ci = lax.broadcasted_iota(I32, (GM_CHUNK, GM_CHUNK), 1)
    causal = ri >= ci
    for h in range(GM_HEADS):
        sl = slice(h * HEAD_DIM, (h + 1) * HEAD_DIM)
        vh = _rms(vg[:, sl], gmg_ref[:, sl]).astype(BF16)
        wm = jnp.where(causal, ws_ref[h], 0.0).astype(BF16)
        for c in range(tm // GM_CHUNK):
            rs = slice(c * GM_CHUNK, (c + 1) * GM_CHUNK)
            gate = _dot(wm, vh[rs]) + bs_ref[:, sl]
            oa_ref[rs, sl] = (u[rs, sl] * gate).astype(BF16)

    pq = _dot(y, wqkv_ref[...])

    @pl.when(i % tiles_per_seq == 0)
    def _():
        cbuf_ref[0:8, :] = jnp.zeros((8, 3 * DN_KEY), F32)

    cbuf_ref[8:8 + tm, :] = pq
    acc = cw_ref[CONV_W - 1:CONV_W, :] * pq
    for j in range(CONV_W - 1):
        off = 8 - (CONV_W - 1) + j
        acc = acc + cw_ref[j:j + 1, :] * cbuf_ref[off:off + tm, :]
    cbuf_ref[0:8, :] = pq[tm - 8:tm, :]
    s = acc * _sigmoid(acc)
    for h in range(DN_HEADS):
        sl = slice(h * DN_DK, (h + 1) * DN_DK)
        qh = s[:, sl]
        q_ref[:, sl] = (qh * lax.rsqrt(jnp.sum(qh * qh, axis=-1, keepdims=True) + EPS)
                        * (DN_DK ** -0.5)).astype(BF16)
        kh = s[:, DN_KEY + h * DN_DK:DN_KEY + (h + 1) * DN_DK]
        k_ref[:, sl] = (kh * lax.rsqrt(jnp.sum(kh * kh, axis=-1, keepdims=True) + EPS)).astype(BF16)
    v_ref[...] = s[:, 2 * DN_KEY:].astype(BF16)
    z_ref[...] = _dot(y, wz_ref[...]).astype(BF16)

    ab = _dot(y, wab_ref[...])
    g = -jnp.exp(alog_ref[...]) * _softplus(ab + dtb_ref[...])
    beta = _sigmoid(ab)
    g_hi = g.astype(BF16)
    r1 = g - g_hi.astype(F32)
    g_mid = r1.astype(BF16)
    g_lo = (r1 - g_mid.astype(F32)).astype(BF16)
    tri = tri_ref[...]
    gc = _dot(tri, g_hi) + _dot(tri, g_mid) + _dot(tri, g_lo)
    lane = lax.broadcasted_iota(I32, (tm, LANES), 1)
    gbv = jnp.where(lane < DN_HEADS, gc, beta)
    gb_ref[...] = gbv
    gt_ref[...] = gbv.T[0:8, :]


def _inproj(x2, ng, wu, wv, wqkv, wz, wab, gmg, ws, bsb, cw, alog, dtb, tri, *, seq):
    t = x2.shape[0]
    tm = ROW_TILE
    const2 = lambda i: (0, 0)
    row = lambda i: (i, 0)
    full = lambda a: pl.BlockSpec(a.shape, (lambda i: (0,) * a.ndim))
    out_shapes = (
        jax.ShapeDtypeStruct((t, GM_WIDTH), BF16),
        jax.ShapeDtypeStruct((t, DN_KEY), BF16),
        jax.ShapeDtypeStruct((t, DN_KEY), BF16),
        jax.ShapeDtypeStruct((t, DN_VAL), BF16),
        jax.ShapeDtypeStruct((t, DN_VAL), BF16),
        jax.ShapeDtypeStruct((t, LANES), F32),
        jax.ShapeDtypeStruct((8, t), F32),
    )
    return pl.pallas_call(
        functools.partial(_inproj_body, tm=tm, tiles_per_seq=seq // tm),
        grid=(t // tm,),
        in_specs=[pl.BlockSpec((tm, D_MODEL), row), full(ng), full(wu), full(wv), full(wqkv),
                  full(wz), full(wab), full(gmg), full(ws), full(bsb), full(cw), full(alog),
                  full(dtb), full(tri)],
        out_specs=(pl.BlockSpec((tm, GM_WIDTH), row), pl.BlockSpec((tm, DN_KEY), row),
                   pl.BlockSpec((tm, DN_KEY), row), pl.BlockSpec((tm, DN_VAL), row),
                   pl.BlockSpec((tm, DN_VAL), row), pl.BlockSpec((tm, LANES), row),
                   pl.BlockSpec((8, tm), lambda i: (0, i))),
        out_shape=out_shapes,
        scratch_shapes=[pltpu.VMEM((tm + 8, 3 * DN_KEY), F32)],
        compiler_params=pltpu.CompilerParams(dimension_semantics=("arbitrary",),
                                             vmem_limit_bytes=VMEM_LIMIT),
        name="inproj",
    )(x2, ng, wu, wv, wqkv, wz, wab, gmg, ws, bsb, cw, alog, dtb, tri)


def _gdn_body(q_ref, k_ref, v_ref, z_ref, gb_ref, gr_ref, ng_ref, ob_ref, s_ref, *, nchunk):
    j = pl.program_id(1)

    @pl.when(j == 0)
    def _():
        s_ref[...] = jnp.zeros(s_ref.shape, F32)

    c = DN_CHUNK
    n = DN_HEADS * c
    ri = lax.broadcasted_iota(I32, (n, n), 0)
    ci = lax.broadcasted_iota(I32, (n, n), 1)
    same = (ri // c) == (ci // c)
    incl = same & ((ri % c) >= (ci % c))
    strict = same & ((ri % c) > (ci % c))
    ng = ng_ref[...]

    def stack(a):
        return jnp.concatenate([a[:, h * HEAD_DIM:(h + 1) * HEAD_DIM] for h in range(DN_HEADS)], axis=0)

    def chunk(ic, carry):
        r0 = pl.multiple_of(ic * c, c)
        kst = stack(k_ref[pl.ds(r0, c), :])
        qst = stack(q_ref[pl.ds(r0, c), :])
        vst = stack(v_ref[pl.ds(r0, c), :])
        gbc = gb_ref[pl.ds(r0, c), :]
        grow = gr_ref[pl.ds(ic, 1), :]
        gcol = jnp.concatenate([gbc[:, h:h + 1] for h in range(DN_HEADS)], axis=0)
        bcol = jnp.concatenate([gbc[:, DN_HEADS + h:DN_HEADS + h + 1] for h in range(DN_HEADS)], axis=0)
        glast = jnp.concatenate(
            [jnp.broadcast_to(gbc[c - 1:c, h:h + 1], (c, 1)) for h in range(DN_HEADS)], axis=0)

        decay = jnp.where(incl, jnp.exp(jnp.where(incl, gcol - grow, 0.0)), 0.0)
        kf = kst.astype(F32)
        kb = kf * bcol
        lmat = jnp.where(strict, _dot_nt(kb.astype(BF16), kst) * decay, 0.0)
        tinv = -lmat
        pw = lmat
        for _ in range(5):
            pb = pw.astype(BF16)
            pw = _dot(pb, pb)
            tinv = tinv + pw + _dot(tinv.astype(BF16), pw.astype(BF16))
        eg = jnp.exp(gcol)
        rhs = jnp.concatenate([vst.astype(F32) * bcol, kb * eg], axis=1)
        sol = rhs + _dot(tinv.astype(BF16), rhs.astype(BF16))
        u = sol[:, :HEAD_DIM]
        wb = sol[:, HEAD_DIM:].astype(BF16)
        attn = jnp.where(incl, _dot_nt(qst, kst) * decay, 0.0).astype(BF16)
        qd = (qst.astype(F32) * eg).astype(BF16)
        ke = (kf * jnp.exp(glast - gcol)).astype(BF16)

        vn, qs = [], []
        for h in range(DN_HEADS):
            rs = slice(h * c, (h + 1) * c)
            sb = s_ref[h].astype(BF16)
            vn.append(u[rs] - _dot(wb[rs], sb))
            qs.append(_dot(qd[rs], sb))
        vnb = jnp.concatenate(vn, axis=0).astype(BF16)
        o = jnp.concatenate(qs, axis=0) + _dot(attn, vnb)
        zc = z_ref[pl.ds(r0, c), :]
        for h in range(DN_HEADS):
            rs = slice(h * c, (h + 1) * c)
            sl = slice(h * HEAD_DIM, (h + 1) * HEAD_DIM)
            s_ref[h] = s_ref[h] * jnp.exp(gbc[c - 1:c, h:h + 1]) + _dot_tn(ke[rs], vnb[rs])
            zf = zc[:, sl].astype(F32)
            ob_ref[pl.ds(r0, c), sl] = (_rms(o[rs], ng) * (zf * _sigmoid(zf))).astype(BF16)
        return carry

    lax.fori_loop(0, nchunk, chunk, 0)


def _gdn(q, k, v, z, gb, grow, ng, *, batch, seq):
    tm = ROW_TILE
    nchunk = tm // DN_CHUNK
    steps = seq // tm
    rows = lambda b, j: (b * steps + j, 0)
    return pl.pallas_call(
        functools.partial(_gdn_body, nchunk=nchunk),
        grid=(batch, steps),
        in_specs=[pl.BlockSpec((tm, DN_KEY), rows), pl.BlockSpec((tm, DN_KEY), rows),
                  pl.BlockSpec((tm, DN_VAL), rows), pl.BlockSpec((tm, DN_VAL), rows),
                  pl.BlockSpec((tm, LANES), rows),
                  pl.BlockSpec((nchunk, DN_HEADS * DN_CHUNK), rows),
                  pl.BlockSpec((1, HEAD_DIM), lambda b, j: (0, 0))],
        out_specs=pl.BlockSpec((tm, DN_VAL), rows),
        out_shape=jax.ShapeDtypeStruct((batch * seq, DN_VAL), BF16),
        scratch_shapes=[pltpu.VMEM((DN_HEADS, DN_DK, HEAD_DIM), F32)],
        compiler_params=pltpu.CompilerParams(dimension_semantics=("arbitrary", "arbitrary"),
                                             vmem_limit_bytes=VMEM_LIMIT),
        name="gdn",
    )(q, k, v, z, gb, grow, ng)


def _outproj_body(oa_ref, ob_ref, x_ref, woa_ref, wob_ref, ng_ref, rw_ref, rb_ref, tri_ref,
                  h_ref, hp_ref, meta_ref, gate_ref, cnt_ref, run_ref, *, tm):
    i = pl.program_id(0)

    @pl.when(i == 0)
    def _():
        run_ref[...] = jnp.zeros(run_ref.shape, F32)

    h = x_ref[...] + _dot(oa_ref[...], woa_ref[...]) + _dot(ob_ref[...], wob_ref[...])
    h_ref[...] = h
    hb = _rms(h, ng_ref[...]).astype(BF16)
    half = D_MODEL // 2
    lo = pltpu.bitcast(hb[:, :half].astype(F32), U32) >> 16
    hi = pltpu.bitcast(hb[:, half:].astype(F32), U32) & jnp.uint32(0xFFFF0000)
    hp_ref[...] = lo | hi

    logits = _dot(hb, rw_ref[...]) + rb_ref[...]
    lane = lax.broadcasted_iota(I32, (tm, LANES), 1)
    lanef = lane.astype(F32)
    work = logits
    onehot = jnp.zeros((tm, LANES), F32)
    vals, sels = [], []
    for _ in range(TOP_K):
        m = jnp.max(work, axis=-1, keepdims=True)
        idx = jnp.min(jnp.where(work == m, lanef, float(LANES)), axis=-1, keepdims=True)
        sel = lanef == idx
        work = jnp.where(sel, -3e38, work)
        onehot = onehot + jnp.where(sel, 1.0, 0.0)
        vals.append(m)
        sels.append((sel, idx))
    ex = [jnp.exp(v - vals[0]) for v in vals]
    den = ex[0] + ex[1] + ex[2] + ex[3]
    pref = _dot(tri_ref[...], onehot.astype(BF16)) + run_ref[0:1, :]
    meta = jnp.zeros((tm, LANES), F32)
    gates = jnp.zeros((tm, LANES), F32)
    for kk in range(TOP_K):
        sel, idx = sels[kk]
        rank = jnp.sum(jnp.where(sel, pref, 0.0), axis=-1, keepdims=True)
        meta = meta + jnp.where(lane == kk, idx, 0.0) + jnp.where(lane == TOP_K + kk, rank, 0.0)
        gates = gates + jnp.where(lane == kk, ex[kk] / den, 0.0)
    meta_ref[...] = meta.astype(I32)
    gate_ref[...] = gates
    run = run_ref[...] + jnp.sum(onehot, axis=0, keepdims=True)
    run_ref[...] = run
    cnt_ref[...] = run.astype(I32)


def _outproj(oa, ob, x2, woa, wob, ng, rw, rb, tri):
    t = x2.shape[0]
    tm = ROW_TILE
    row = lambda i: (i, 0)
    full = lambda a: pl.BlockSpec(a.shape, (lambda i: (0,) * a.ndim))
    out_shapes = (
        jax.ShapeDtypeStruct((t, D_MODEL), F32),
        jax.ShapeDtypeStruct((t, D_MODEL // 2), U32),
        jax.ShapeDtypeStruct((t, LANES), I32),
        jax.ShapeDtypeStruct((t, LANES), F32),
        jax.ShapeDtypeStruct((8, LANES), I32),
    )
    return pl.pallas_call(
        functools.partial(_outproj_body, tm=tm),
        grid=(t // tm,),
        in_specs=[pl.BlockSpec((tm, GM_WIDTH), row), pl.BlockSpec((tm, DN_VAL), row),
                  pl.BlockSpec((tm, D_MODEL), row), full(woa), full(wob), full(ng), full(rw),
                  full(rb), full(tri)],
        out_specs=(pl.BlockSpec((tm, D_MODEL), row), pl.BlockSpec((tm, D_MODEL // 2), row),
                   pl.BlockSpec((tm, LANES), row), pl.BlockSpec((tm, LANES), row),
                   pl.BlockSpec((8, LANES), lambda i: (0, 0))),
        out_shape=out_shapes,
        scratch_shapes=[pltpu.VMEM((8, LANES), F32)],
        compiler_params=pltpu.CompilerParams(dimension_semantics=("arbitrary",),
                                             vmem_limit_bytes=VMEM_LIMIT),
        name="outproj",
    )(oa, ob, x2, woa, wob, ng, rw, rb, tri)


def _dispatch_body(ps_ref, meta_ref, hp_ref, xin0_ref, xin_ref, sem, *, tm):
    del xin0_ref
    base = pl.program_id(0) * tm

    def row_copy(t, kk):
        e = meta_ref[t * 8 + kk]
        d = ps_ref[e] + meta_ref[t * 8 + TOP_K + kk]
        return pltpu.make_async_copy(hp_ref.at[pl.ds(base + t, 1)], xin_ref.at[pl.ds(d, 1)], sem)

    def start(t, carry):
        for kk in range(TOP_K):
            row_copy(t, kk).start()
        return carry

    def wait(t, carry):
        for kk in range(TOP_K):
            row_copy(t, kk).wait()
        return carry

    lax.fori_loop(0, tm, start, 0)
    lax.fori_loop(0, tm, wait, 0)


def _dispatch(pad_start, meta_flat, hp, xin0):
    t = hp.shape[0]
    tm = ROW_TILE
    return pl.pallas_call(
        functools.partial(_dispatch_body, tm=tm),
        grid_spec=pltpu.PrefetchScalarGridSpec(
            num_scalar_prefetch=1,
            grid=(t // tm,),
            in_specs=[pl.BlockSpec((tm * 8,), lambda i, ps: (i,), memory_space=pltpu.SMEM),
                      pl.BlockSpec(memory_space=pl.ANY),
                      pl.BlockSpec(memory_space=pl.ANY)],
            out_specs=pl.BlockSpec(memory_space=pl.ANY),
            scratch_shapes=[pltpu.SemaphoreType.DMA(())],
        ),
        out_shape=jax.ShapeDtypeStruct(xin0.shape, xin0.dtype),
        input_output_aliases={3: 0},
        compiler_params=pltpu.CompilerParams(dimension_semantics=("arbitrary",)),
        name="dispatch",
    )(pad_start, meta_flat, hp, xin0)


def _ffn_body(be_ref, nv_ref, x_ref, wgu_ref, bgu_ref, wd_ref, bd_ref, y_ref):
    del be_ref

    @pl.when(pl.program_id(0) < nv_ref[0])
    def _():
        half = D_MODEL // 2
        xp = x_ref[...]
        lo = pltpu.bitcast(xp << 16, F32).astype(BF16)
        hi = pltpu.bitcast(xp & jnp.uint32(0xFFFF0000), F32).astype(BF16)
        gu = _dot(lo, wgu_ref[0, :half, :]) + _dot(hi, wgu_ref[0, half:, :]) + bgu_ref[0]
        gate = jnp.minimum(gu[:, :D_FF], SWIGLU_LIMIT)
        up = jnp.clip(gu[:, D_FF:], -SWIGLU_LIMIT, SWIGLU_LIMIT)
        act = (up + 1.0) * (gate * _sigmoid(SWIGLU_ALPHA * gate))
        y_ref[...] = _dot(act.astype(BF16), wd_ref[0]) + bd_ref[0]


def _ffn(blk_e, nvalid, xin, wgu, bgu, wd, bd):
    p = xin.shape[0]
    bm = FFN_BLOCK
    rows = lambda i, be, nv: (jnp.minimum(i, nv[0] - 1), 0)
    wsel = lambda i, be, nv: (be[i], 0, 0)
    return pl.pallas_call(
        _ffn_body,
        grid_spec=pltpu.PrefetchScalarGridSpec(
            num_scalar_prefetch=2,
            grid=(p // bm,),
            in_specs=[pl.BlockSpec((bm, D_MODEL // 2), rows),
                      pl.BlockSpec((1, D_MODEL, 2 * D_FF), wsel),
                      pl.BlockSpec((1, 1, 2 * D_FF), wsel),
                      pl.BlockSpec((1, D_FF, D_MODEL), wsel),
                      pl.BlockSpec((1, 1, D_MODEL), wsel)],
            out_specs=pl.BlockSpec((bm, D_MODEL), rows),
        ),
        out_shape=jax.ShapeDtypeStruct((p, D_MODEL), F32),
        compiler_params=pltpu.CompilerParams(dimension_semantics=("arbitrary",),
                                             vmem_limit_bytes=VMEM_LIMIT),
        name="ffn",
    )(blk_e, nvalid, xin, wgu, bgu, wd, bd)


def _combine_body(ps_ref, meta_ref, h_ref, gate_ref, fg_ref, y_ref, o_ref, ybuf_ref, sem, *, tm):
    def row_copy(t, kk):
        e = meta_ref[t * 8 + kk]
        d = ps_ref[e] + meta_ref[t * 8 + TOP_K + kk]
        return pltpu.make_async_copy(y_ref.at[pl.ds(d, 1)], ybuf_ref.at[kk, pl.ds(t, 1)], sem)

    def start(t, carry):
        for kk in range(TOP_K):
            row_copy(t, kk).start()
        return carry

    def wait(t, carry):
        for kk in range(TOP_K):
            row_copy(t, kk).wait()
        return carry

    lax.fori_loop(0, tm, start, 0)
    lax.fori_loop(0, tm, wait, 0)
    gates = gate_ref[...]
    out = h_ref[...]
    for kk in range(TOP_K):
        out = out + gates[:, kk:kk + 1] * ybuf_ref[kk]
    o_ref[...] = _rms(out, fg_ref[...])


def _combine(pad_start, meta_flat, h, gates, fg, y):
    t = h.shape[0]
    tm = COMBINE_TILE
    return pl.pallas_call(
        functools.partial(_combine_body, tm=tm),
        grid_spec=pltpu.PrefetchScalarGridSpec(
            num_scalar_prefetch=1,
            grid=(t // tm,),
            in_specs=[pl.BlockSpec((tm * 8,), lambda i, ps: (i,), memory_space=pltpu.SMEM),
                      pl.BlockSpec((tm, D_MODEL), lambda i, ps: (i, 0)),
                      pl.BlockSpec((tm, LANES), lambda i, ps: (i, 0)),
                      pl.BlockSpec((1, D_MODEL), lambda i, ps: (0, 0)),
                      pl.BlockSpec(memory_space=pl.ANY)],
            out_specs=pl.BlockSpec((tm, D_MODEL), lambda i, ps: (i, 0)),
            scratch_shapes=[pltpu.VMEM((TOP_K, tm, D_MODEL), F32), pltpu.SemaphoreType.DMA(())],
        ),
        out_shape=jax.ShapeDtypeStruct((t, D_MODEL), F32),
        compiler_params=pltpu.CompilerParams(dimension_semantics=("arbitrary",),
                                             vmem_limit_bytes=VMEM_LIMIT),
        name="combine",
    )(pad_start, meta_flat, h, gates, fg, y)


def _block_tril(n, chunk, strict):
    r = jnp.arange(n)[:, None]
    c = jnp.arange(n)[None, :]
    keep = ((r // chunk) == (c // chunk)) & ((r > c) if strict else (r >= c))
    return keep.astype(BF16)


def _pad_lanes(a, fill=0.0):
    a = a.reshape(1, -1).astype(F32)
    return jnp.pad(a, ((0, 0), (0, LANES - a.shape[1])), constant_values=fill)


def _layer(h, norm_mix_g, w_in, gm_norm_g, gm_ws, gm_bs, dn_conv_w, dn_a_log, dn_dt_bias,
           dn_norm_g, w_out, norm_ffn_g, router_w, router_b, exp_w_gu, exp_b_gu, exp_w_down,
           exp_b_down, out_g):
    batch, seq, d = h.shape
    t = batch * seq
    x2 = h.reshape(t, d)

    c0, c1, c2 = GM_WIDTH, 2 * GM_WIDTH, 2 * GM_WIDTH + 3 * DN_KEY
    c3 = c2 + DN_VAL
    wb = w_in.astype(BF16)
    wu, wv, wqkv, wz = wb[:, :c0], wb[:, c0:c1], wb[:, c1:c2], wb[:, c2:c3]
    wab = jnp.pad(wb[:, c3:], ((0, 0), (0, LANES - 2 * DN_HEADS)))
    gmg = gm_norm_g.reshape(1, GM_WIDTH).astype(F32)
    bsb = jnp.repeat(gm_bs.T, HEAD_DIM, axis=1).astype(F32)
    alog = _pad_lanes(dn_a_log)
    dtb = _pad_lanes(dn_dt_bias)
    tri_incl = _block_tril(ROW_TILE, DN_CHUNK, strict=False)

    oa, q, k, v, z, gb, gt = _inproj(
        x2, norm_mix_g.reshape(1, d), wu, wv, wqkv, wz, wab, gmg, gm_ws.astype(F32), bsb,
        dn_conv_w.astype(F32), alog, dtb, tri_incl, seq=seq)

    grow = gt[:DN_HEADS].reshape(DN_HEADS, t // DN_CHUNK, DN_CHUNK).transpose(1, 0, 2)
    grow = grow.reshape(t // DN_CHUNK, DN_HEADS * DN_CHUNK)
    ob = _gdn(q, k, v, z, gb, grow, dn_norm_g.reshape(1, HEAD_DIM).astype(F32), batch=batch, seq=seq)

    wo = w_out.astype(BF16)
    rw = jnp.pad(router_w.astype(BF16), ((0, 0), (0, LANES - N_EXPERTS)))
    rb = _pad_lanes(router_b, fill=NEG_BIG)
    tri_strict = _block_tril(ROW_TILE, ROW_TILE, strict=True)
    hres, hp, meta, gates, cnt = _outproj(oa, ob, x2, wo[:GM_WIDTH], wo[GM_WIDTH:],
                                          norm_ffn_g.reshape(1, d), rw, rb, tri_strict)

    bm = FFN_BLOCK
    counts = cnt[0, :N_EXPERTS]
    padded = (counts + bm - 1) // bm * bm
    pad_end = jnp.cumsum(padded)
    pad_start = (pad_end - padded).astype(I32)
    n_blocks = (t * TOP_K + N_EXPERTS * bm) // bm
    nvalid = (pad_end[-1] // bm).astype(I32).reshape(1)
    blk = jnp.minimum(jnp.arange(n_blocks, dtype=I32), nvalid[0] - 1)
    blk_e = jnp.minimum(jnp.searchsorted(pad_end, blk * bm, side="right"), N_EXPERTS - 1).astype(I32)
    meta_flat = meta[:, :8].reshape(-1)

    xin = _dispatch(pad_start, meta_flat, hp, jnp.zeros((n_blocks * bm, d // 2), U32))
    y = _ffn(blk_e, nvalid, xin, exp_w_gu.astype(BF16), exp_b_gu[:, None, :].astype(F32),
             exp_w_down.astype(BF16), exp_b_down[:, None, :].astype(F32))
    out = _combine(pad_start, meta_flat, hres, gates, out_g.reshape(1, d).astype(F32), y)
    return out.reshape(batch, seq, d)


def kernel(x, norm_mix_g, w_in, gm_norm_g, gm_ws, gm_bs, dn_conv_w, dn_a_log, dn_dt_bias, dn_norm_g, w_out, norm_ffn_g, router_w, router_b, exp_w_gu, exp_b_gu, exp_w_down, exp_b_down, final_norm_g):
    depth = norm_mix_g.shape[0]
    assert depth == 1, "single-layer problem"
    return _layer(x, norm_mix_g[0], w_in[0], gm_norm_g[0], gm_ws[0], gm_bs[0], dn_conv_w[0],
                  dn_a_log[0], dn_dt_bias[0], dn_norm_g[0], w_out[0], norm_ffn_g[0], router_w[0],
                  router_b[0], exp_w_gu[0], exp_b_gu[0], exp_w_down[0], exp_b_down[0], final_norm_g)
```

```python
import functools

import jax
import jax.numpy as jnp
from jax import lax
from jax.experimental import pallas as pl
from jax.experimental.pallas import tpu as pltpu

F32 = jnp.float32
BF16 = jnp.bfloat16
I32 = jnp.int32
U32 = jnp.uint32

D_MODEL = 1024
HEAD_DIM = 128
GM_HEADS = 4
GM_WIDTH = GM_HEADS * HEAD_DIM
GM_CHUNK = 128
DN_HEADS = 4
DN_DK = 128
DN_KEY = DN_HEADS * DN_DK
DN_VAL = DN_HEADS * HEAD_DIM
DN_CHUNK = 64
CONV_W = 4
N_EXPERTS = 32
TOP_K = 4
D_FF = D_MODEL
SWIGLU_LIMIT = 7.0
SWIGLU_ALPHA = 1.702
EPS = 1e-6

LANES = 128
ROW_TILE = 512
FFN_BLOCK = 512
COMBINE_TILE = 256
VMEM_LIMIT = 56 * 1024 * 1024
NEG_BIG = -1e30


def _dot(a, b):
    return jnp.dot(a, b, preferred_element_type=F32)


def _dot_nt(a, b):
    return lax.dot_general(a, b, (((1,), (1,)), ((), ())), preferred_element_type=F32)


def _dot_tn(a, b):
    return lax.dot_general(a, b, (((0,), (0,)), ((), ())), preferred_element_type=F32)


def _rms(x, g):
    return x * lax.rsqrt(jnp.mean(x * x, axis=-1, keepdims=True) + EPS) * g


def _gelu(x):
    return 0.5 * x * (1.0 + lax.erf(x * (2.0 ** -0.5)))


def _sigmoid(x):
    return 1.0 / (1.0 + jnp.exp(-x))


def _softplus(x):
    return jnp.maximum(x, 0.0) + jnp.log1p(jnp.exp(-jnp.abs(x)))


def _inproj_body(x_ref, ng_ref, wu_ref, wv_ref, wqkv_ref, wz_ref, wab_ref, gmg_ref, ws_ref,
                 bs_ref, cw_ref, alog_ref, dtb_ref, tri_ref,
                 oa_ref, q_ref, k_ref, v_ref, z_ref, gb_ref, gt_ref, cbuf_ref,
                 *, tm, tiles_per_seq):
    i = pl.program_id(0)
    y = _rms(x_ref[...], ng_ref[...]).astype(BF16)

    u = _gelu(_dot(y, wu_ref[...]))
    vg = _gelu(_dot(y, wv_ref[...]))
    ri = lax.broadcasted_iota(I32, (GM_CHUNK, GM_CHUNK), 0)
    ci = lax.broadcasted_iota(I32, (GM_CHUNK, GM_CHUNK), 1)
    causal = ri >= ci
    for h in range(GM_HEADS):
        sl = slice(h * HEAD_DIM, (h + 1) * HEAD_DIM)
        vh = _rms(vg[:, sl], gmg_ref[:, sl]).astype(BF16)
        wm = jnp.where(causal, ws_ref[h], 0.0).astype(BF16)
        for c in range(tm // GM_CHUNK):
            rs = slice(c * GM_CHUNK, (c + 1) * GM_CHUNK)
            gate = _dot(wm, vh[rs]) + bs_ref[:, sl]
            oa_ref[rs, sl] = (u[rs, sl] * gate).astype(BF16)

    pq = _dot(y, wqkv_ref[...])

    @pl.when(i % tiles_per_seq == 0)
    def _():
        cbuf_ref[0:8, :] = jnp.zeros((8, 3 * DN_KEY), F32)

    cbuf_ref[8:8 + tm, :] = pq
    acc = cw_ref[CONV_W - 1:CONV_W, :] * pq
    for j in range(CONV_W - 1):
        off = 8 - (CONV_W - 1) + j
        acc = acc + cw_ref[j:j + 1, :] * cbuf_ref[off:off + tm, :]
    cbuf_ref[0:8, :] = pq[tm - 8:tm, :]
    s = acc * _sigmoid(acc)
    for h in range(DN_HEADS):
        sl = slice(h * DN_DK, (h + 1) * DN_DK)
        qh = s[:, sl]
        q_ref[:, sl] = (qh * lax.rsqrt(jnp.sum(qh * qh, axis=-1, keepdims=True) + EPS)
                        * (DN_DK ** -0.5)).astype(BF16)
        kh = s[:, DN_KEY + h * DN_DK:DN_KEY + (h + 1) * DN_DK]
        k_ref[:, sl] = (kh * lax.rsqrt(jnp.sum(kh * kh, axis=-1, keepdims=True) + EPS)).astype(BF16)
    v_ref[...] = s[:, 2 * DN_KEY:].astype(BF16)
    z_ref[...] = _dot(y, wz_ref[...]).astype(BF16)

    ab = _dot(y, wab_ref[...])
    g = -jnp.exp(alog_ref[...]) * _softplus(ab + dtb_ref[...])
    beta = _sigmoid(ab)
    g_hi = g.astype(BF16)
    r1 = g - g_hi.astype(F32)
    g_mid = r1.astype(BF16)
    g_lo = (r1 - g_mid.astype(F32)).astype(BF16)
    tri = tri_ref[...]
    gc = _dot(tri, g_hi) + _dot(tri, g_mid) + _dot(tri, g_lo)
    lane = lax.broadcasted_iota(I32, (tm, LANES), 1)
    gbv = jnp.where(lane < DN_HEADS, gc, beta)
    gb_ref[...] = gbv
    gt_ref[...] = gbv.T[0:8, :]


def _inproj(x2, ng, wu, wv, wqkv, wz, wab, gmg, ws, bsb, cw, alog, dtb, tri, *, seq):
    t = x2.shape[0]
    tm = ROW_TILE
    const2 = lambda i: (0, 0)
    row = lambda i: (i, 0)
    full = lambda a: pl.BlockSpec(a.shape, (lambda i: (0,) * a.ndim))
    out_shapes = (
        jax.ShapeDtypeStruct((t, GM_WIDTH), BF16),
        jax.ShapeDtypeStruct((t, DN_KEY), BF16),
        jax.ShapeDtypeStruct((t, DN_KEY), BF16),
        jax.ShapeDtypeStruct((t, DN_VAL), BF16),
        jax.ShapeDtypeStruct((t, DN_VAL), BF16),
        jax.ShapeDtypeStruct((t, LANES), F32),
        jax.ShapeDtypeStruct((8, t), F32),
    )
    return pl.pallas_call(
        functools.partial(_inproj_body, tm=tm, tiles_per_seq=seq // tm),
        grid=(t // tm,),
        in_specs=[pl.BlockSpec((tm, D_MODEL), row), full(ng), full(wu), full(wv), full(wqkv),
                  full(wz), full(wab), full(gmg), full(ws), full(bsb), full(cw), full(alog),
                  full(dtb), full(tri)],
        out_specs=(pl.BlockSpec((tm, GM_WIDTH), row), pl.BlockSpec((tm, DN_KEY), row),
                   pl.BlockSpec((tm, DN_KEY), row), pl.BlockSpec((tm, DN_VAL), row),
                   pl.BlockSpec((tm, DN_VAL), row), pl.BlockSpec((tm, LANES), row),
                   pl.BlockSpec((8, tm), lambda i: (0, i))),
        out_shape=out_shapes,
        scratch_shapes=[pltpu.VMEM((tm + 8, 3 * DN_KEY), F32)],
        compiler_params=pltpu.CompilerParams(dimension_semantics=("arbitrary",),
                                             vmem_limit_bytes=VMEM_LIMIT),
        name="inproj",
    )(x2, ng, wu, wv, wqkv, wz, wab, gmg, ws, bsb, cw, alog, dtb, tri)


def _gdn_body(q_ref, k_ref, v_ref, z_ref, gb_ref, gr_ref, ng_ref, ob_ref, s_ref, *, nchunk):
    j = pl.program_id(1)

    @pl.when(j == 0)
    def _():
        s_ref[...] = jnp.zeros(s_ref.shape, F32)

    c = DN_CHUNK
    n = DN_HEADS * c
    ri = lax.broadcasted_iota(I32, (n, n), 0)
    ci = lax.broadcasted_iota(I32, (n, n), 1)
    same = (ri // c) == (ci // c)
    incl = same & ((ri % c) >= (ci % c))
    strict = same & ((ri % c) > (ci % c))
    ng = ng_ref[...]

    def stack(a):
        return jnp.concatenate([a[:, h * HEAD_DIM:(h + 1) * HEAD_DIM] for h in range(DN_HEADS)], axis=0)

    def chunk(ic, carry):
        r0 = pl.multiple_of(ic * c, c)
        kst = stack(k_ref[pl.ds(r0, c), :])
        qst = stack(q_ref[pl.ds(r0, c), :])
        vst = stack(v_ref[pl.ds(r0, c), :])
        gbc = gb_ref[pl.ds(r0, c), :]
        grow = gr_ref[pl.ds(ic, 1), :]
        gcol = jnp.concatenate([gbc[:, h:h + 1] for h in range(DN_HEADS)], axis=0)
        bcol = jnp.concatenate([gbc[:, DN_HEADS + h:DN_HEADS + h + 1] for h in range(DN_HEADS)], axis=0)
        glast = jnp.concatenate(
            [jnp.broadcast_to(gbc[c - 1:c, h:h + 1], (c, 1)) for h in range(DN_HEADS)], axis=0)

        decay = jnp.where(incl, jnp.exp(jnp.where(incl, gcol - grow, 0.0)), 0.0)
        kf = kst.astype(F32)
        kb = kf * bcol
        lmat = jnp.where(strict, _dot_nt(kb.astype(BF16), kst) * decay, 0.0)
        tinv = -lmat
        pw = lmat
        for _ in range(5):
            pb = pw.astype(BF16)
            pw = _dot(pb, pb)
            tinv = tinv + pw + _dot(tinv.astype(BF16), pw.astype(BF16))
        eg = jnp.exp(gcol)
        rhs = jnp.concatenate([vst.astype(F32) * bcol, kb * eg], axis=1)
        sol = rhs + _dot(tinv.astype(BF16), rhs.astype(BF16))
        u = sol[:, :HEAD_DIM]
        wb = sol[:, HEAD_DIM:].astype(BF16)
        attn = jnp.where(incl, _dot_nt(qst, kst) * decay, 0.0).astype(BF16)
        qd = (qst.astype(F32) * eg).astype(BF16)
        ke = (kf * jnp.exp(glast - gcol)).astype(BF16)

        vn, qs = [], []
        for h in range(DN_HEADS):
            rs = slice(h * c, (h + 1) * c)
            sb = s_ref[h].astype(BF16)
            vn.append(u[rs] - _dot(wb[rs], sb))
            qs.append(_dot(qd[rs], sb))
        vnb = jnp.concatenate(vn, axis=0).astype(BF16)
        o = jnp.concatenate(qs, axis=0) + _dot(attn, vnb)
        zc = z_ref[pl.ds(r0, c), :]
        for h in range(DN_HEADS):
            rs = slice(h * c, (h + 1) * c)
            sl = slice(h * HEAD_DIM, (h + 1) * HEAD_DIM)
            s_ref[h] = s_ref[h] * jnp.exp(gbc[c - 1:c, h:h + 1]) + _dot_tn(ke[rs], vnb[rs])
            zf = zc[:, sl].astype(F32)
            ob_ref[pl.ds(r0, c), sl] = (_rms(o[rs], ng) * (zf * _sigmoid(zf))).astype(BF16)
        return carry

    lax.fori_loop(0, nchunk, chunk, 0)


def _gdn(q, k, v, z, gb, grow, ng, *, batch, seq):
    tm = ROW_TILE
    nchunk = tm // DN_CHUNK
    steps = seq // tm
    rows = lambda b, j: (b * steps + j, 0)
    return pl.pallas_call(
        functools.partial(_gdn_body, nchunk=nchunk),
        grid=(batch, steps),
        in_specs=[pl.BlockSpec((tm, DN_KEY), rows), pl.BlockSpec((tm, DN_KEY), rows),
                  pl.BlockSpec((tm, DN_VAL), rows), pl.BlockSpec((tm, DN_VAL), rows),
                  pl.BlockSpec((tm, LANES), rows),
                  pl.BlockSpec((nchunk, DN_HEADS * DN_CHUNK), rows),
                  pl.BlockSpec((1, HEAD_DIM), lambda b, j: (0, 0))],
        out_specs=pl.BlockSpec((tm, DN_VAL), rows),
        out_shape=jax.ShapeDtypeStruct((batch * seq, DN_VAL), BF16),
        scratch_shapes=[pltpu.VMEM((DN_HEADS, DN_DK, HEAD_DIM), F32)],
        compiler_params=pltpu.CompilerParams(dimension_semantics=("arbitrary", "arbitrary"),
                                             vmem_limit_bytes=VMEM_LIMIT),
        name="gdn",
    )(q, k, v, z, gb, grow, ng)


def _outproj_body(oa_ref, ob_ref, x_ref, woa_ref, wob_ref, ng_ref, rw_ref, rb_ref, tri_ref,
                  h_ref, hp_ref, meta_ref, gate_ref, cnt_ref, run_ref, *, tm):
    i = pl.program_id(0)

    @pl.when(i == 0)
    def _():
        run_ref[...] = jnp.zeros(run_ref.shape, F32)

    h = x_ref[...] + _dot(oa_ref[...], woa_ref[...]) + _dot(ob_ref[...], wob_ref[...])
    h_ref[...] = h
    hb = _rms(h, ng_ref[...]).astype(BF16)
    half = D_MODEL // 2
    lo = pltpu.bitcast(hb[:, :half].astype(F32), U32) >> 16
    hi = pltpu.bitcast(hb[:, half:].astype(F32), U32) & jnp.uint32(0xFFFF0000)
    hp_ref[...] = lo | hi

    logits = _dot(hb, rw_ref[...]) + rb_ref[...]
    lane = lax.broadcasted_iota(I32, (tm, LANES), 1)
    lanef = lane.astype(F32)
    work = logits
    onehot = jnp.zeros((tm, LANES), F32)
    vals, sels = [], []
    for _ in range(TOP_K):
        m = jnp.max(work, axis=-1, keepdims=True)
        idx = jnp.min(jnp.where(work == m, lanef, float(LANES)), axis=-1, keepdims=True)
        sel = lanef == idx
        work = jnp.where(sel, -3e38, work)
        onehot = onehot + jnp.where(sel, 1.0, 0.0)
        vals.append(m)
        sels.append((sel, idx))
    ex = [jnp.exp(v - vals[0]) for v in vals]
    den = ex[0] + ex[1] + ex[2] + ex[3]
    pref = _dot(tri_ref[...], onehot.astype(BF16)) + run_ref[0:1, :]
    meta = jnp.zeros((tm, LANES), F32)
    gates = jnp.zeros((tm, LANES), F32)
    for kk in range(TOP_K):
        sel, idx = sels[kk]
        rank = jnp.sum(jnp.where(sel, pref, 0.0), axis=-1, keepdims=True)
        meta = meta + jnp.where(lane == kk, idx, 0.0) + jnp.where(lane == TOP_K + kk, rank, 0.0)
        gates = gates + jnp.where(lane == kk, ex[kk] / den, 0.0)
    meta_ref[...] = meta.astype(I32)
    gate_ref[...] = gates
    run = run_ref[...] + jnp.sum(onehot, axis=0, keepdims=True)
    run_ref[...] = run
    cnt_ref[...] = run.astype(I32)


def _outproj(oa, ob, x2, woa, wob, ng, rw, rb, tri):
    t = x2.shape[0]
    tm = ROW_TILE
    row = lambda i: (i, 0)
    full = lambda a: pl.BlockSpec(a.shape, (lambda i: (0,) * a.ndim))
    out_shapes = (
        jax.ShapeDtypeStruct((t, D_MODEL), F32),
        jax.ShapeDtypeStruct((t, D_MODEL // 2), U32),
        jax.ShapeDtypeStruct((t, LANES), I32),
        jax.ShapeDtypeStruct((t, LANES), F32),
        jax.ShapeDtypeStruct((8, LANES), I32),
    )
    return pl.pallas_call(
        functools.partial(_outproj_body, tm=tm),
        grid=(t // tm,),
        in_specs=[pl.BlockSpec((tm, GM_WIDTH), row), pl.BlockSpec((tm, DN_VAL), row),
                  pl.BlockSpec((tm, D_MODEL), row), full(woa), full(wob), full(ng), full(rw),
                  full(rb), full(tri)],
        out_specs=(pl.BlockSpec((tm, D_MODEL), row), pl.BlockSpec((tm, D_MODEL // 2), row),
                   pl.BlockSpec((tm, LANES), row), pl.BlockSpec((tm, LANES), row),
                   pl.BlockSpec((8, LANES), lambda i: (0, 0))),
        out_shape=out_shapes,
        scratch_shapes=[pltpu.VMEM((8, LANES), F32)],
        compiler_params=pltpu.CompilerParams(dimension_semantics=("arbitrary",),
                                             vmem_limit_bytes=VMEM_LIMIT),
        name="outproj",
    )(oa, ob, x2, woa, wob, ng, rw, rb, tri)


def _dispatch_body(ps_ref, meta_ref, hp_ref, xin0_ref, xin_ref, sem, *, tm):
    del xin0_ref

    def row_copy(t, kk):
        e = meta_ref[t * 8 + kk]
        d = ps_ref[e] + meta_ref[t * 8 + TOP_K + kk]
        return pltpu.make_async_copy(hp_ref.at[pl.ds(t, 1)], xin_ref.at[pl.ds(d, 1)], sem)

    def start(t, carry):
        for kk in range(TOP_K):
            row_copy(t, kk).start()
        return carry

    def wait(t, carry):
        for kk in range(TOP_K):
            row_copy(t, kk).wait()
        return carry

    lax.fori_loop(0, tm, start, 0)
    lax.fori_loop(0, tm, wait, 0)


def _dispatch(pad_start, meta_flat, hp, xin0):
    t = hp.shape[0]
    tm = ROW_TILE
    return pl.pallas_call(
        functools.partial(_dispatch_body, tm=tm),
        grid_spec=pltpu.PrefetchScalarGridSpec(
            num_scalar_prefetch=1,
            grid=(t // tm,),
            in_specs=[pl.BlockSpec((tm * 8,), lambda i, ps: (i,), memory_space=pltpu.SMEM),
                      pl.BlockSpec((tm, D_MODEL // 2), lambda i, ps: (i, 0)),
                      pl.BlockSpec(memory_space=pl.ANY)],
            out_specs=pl.BlockSpec(memory_space=pl.ANY),
            scratch_shapes=[pltpu.SemaphoreType.DMA(())],
        ),
        out_shape=jax.ShapeDtypeStruct(xin0.shape, xin0.dtype),
        input_output_aliases={3: 0},
        compiler_params=pltpu.CompilerParams(dimension_semantics=("arbitrary",)),
        name="dispatch",
    )(pad_start, meta_flat, hp, xin0)


def _ffn_body(be_ref, nv_ref, x_ref, wgu_ref, bgu_ref, wd_ref, bd_ref, y_ref):
    del be_ref

    @pl.when(pl.program_id(0) < nv_ref[0])
    def _():
        half = D_MODEL // 2
        xp = x_ref[...]
        lo = pltpu.bitcast(xp << 16, F32).astype(BF16)
        hi = pltpu.bitcast(xp & jnp.uint32(0xFFFF0000), F32).astype(BF16)
        gu = _dot(lo, wgu_ref[0, :half, :]) + _dot(hi, wgu_ref[0, half:, :]) + bgu_ref[0]
        gate = jnp.minimum(gu[:, :D_FF], SWIGLU_LIMIT)
        up = jnp.clip(gu[:, D_FF:], -SWIGLU_LIMIT, SWIGLU_LIMIT)
        act = (up + 1.0) * (gate * _sigmoid(SWIGLU_ALPHA * gate))
        y_ref[...] = _dot(act.astype(BF16), wd_ref[0]) + bd_ref[0]


def _ffn(blk_e, nvalid, xin, wgu, bgu, wd, bd):
    p = xin.shape[0]
    bm = FFN_BLOCK
    rows = lambda i, be, nv: (jnp.minimum(i, nv[0] - 1), 0)
    wsel = lambda i, be, nv: (be[i], 0, 0)
    return pl.pallas_call(
        _ffn_body,
        grid_spec=pltpu.PrefetchScalarGridSpec(
            num_scalar_prefetch=2,
            grid=(p // bm,),
            in_specs=[pl.BlockSpec((bm, D_MODEL // 2), rows),
                      pl.BlockSpec((1, D_MODEL, 2 * D_FF), wsel),
                      pl.BlockSpec((1, 1, 2 * D_FF), wsel),
                      pl.BlockSpec((1, D_FF, D_MODEL), wsel),
                      pl.BlockSpec((1, 1, D_MODEL), wsel)],
            out_specs=pl.BlockSpec((bm, D_MODEL), rows),
        ),
        out_shape=jax.ShapeDtypeStruct((p, D_MODEL), F32),
        compiler_params=pltpu.CompilerParams(dimension_semantics=("arbitrary",),
                                             vmem_limit_bytes=VMEM_LIMIT),
        name="ffn",
    )(blk_e, nvalid, xin, wgu, bgu, wd, bd)


def _combine_body(ps_ref, meta_ref, h_ref, gate_ref, fg_ref, y_ref, o_ref, ybuf_ref, sem, *, tm):
    def row_copy(t, kk):
        e = meta_ref[t * 8 + kk]
        d = ps_ref[e] + meta_ref[t * 8 + TOP_K + kk]
        return pltpu.make_async_copy(y_ref.at[pl.ds(d, 1)], ybuf_ref.at[kk, pl.ds(t, 1)], sem)

    def start(t, carry):
        for kk in range(TOP_K):
            row_copy(t, kk).start()
        return carry

    def wait(t, carry):
        for kk in range(TOP_K):
            row_copy(t, kk).wait()
        return carry

    lax.fori_loop(0, tm, start, 0)
    lax.fori_loop(0, tm, wait, 0)
    gates = gate_ref[...]
    out = h_ref[...]
    for kk in range(TOP_K):
        out = out + gates[:, kk:kk + 1] * ybuf_ref[kk]
    o_ref[...] = _rms(out, fg_ref[...])


def _combine(pad_start, meta_flat, h, gates, fg, y):
    t = h.shape[0]
    tm = COMBINE_TILE
    return pl.pallas_call(
        functools.partial(_combine_body, tm=tm),
        grid_spec=pltpu.PrefetchScalarGridSpec(
            num_scalar_prefetch=1,
            grid=(t // tm,),
            in_specs=[pl.BlockSpec((tm * 8,), lambda i, ps: (i,), memory_space=pltpu.SMEM),
                      pl.BlockSpec((tm, D_MODEL), lambda i, ps: (i, 0)),
                      pl.BlockSpec((tm, LANES), lambda i, ps: (i, 0)),
                      pl.BlockSpec((1, D_MODEL), lambda i, ps: (0, 0)),
                      pl.BlockSpec(memory_space=pl.ANY)],
            out_specs=pl.BlockSpec((tm, D_MODEL), lambda i, ps: (i, 0)),
            scratch_shapes=[pltpu.VMEM((TOP_K, tm, D_MODEL), F32), pltpu.SemaphoreType.DMA(())],
        ),
        out_shape=jax.ShapeDtypeStruct((t, D_MODEL), F32),
        compiler_params=pltpu.CompilerParams(dimension_semantics=("arbitrary",),
                                             vmem_limit_bytes=VMEM_LIMIT),
        name="combine",
    )(pad_start, meta_flat, h, gates, fg, y)


def _block_tril(n, chunk, strict):
    r = jnp.arange(n)[:, None]
    c = jnp.arange(n)[None, :]
    keep = ((r // chunk) == (c // chunk)) & ((r > c) if strict else (r >= c))
    return keep.astype(BF16)


def _pad_lanes(a, fill=0.0):
    a = a.reshape(1, -1).astype(F32)
    return jnp.pad(a, ((0, 0), (0, LANES - a.shape[1])), constant_values=fill)


def _layer(h, norm_mix_g, w_in, gm_norm_g, gm_ws, gm_bs, dn_conv_w, dn_a_log, dn_dt_bias,
           dn_norm_g, w_out, norm_ffn_g, router_w, router_b, exp_w_gu, exp_b_gu, exp_w_down,
           exp_b_down, out_g):
    batch, seq, d = h.shape
    t = batch * seq
    x2 = h.reshape(t, d)

    c0, c1, c2 = GM_WIDTH, 2 * GM_WIDTH, 2 * GM_WIDTH + 3 * DN_KEY
    c3 = c2 + DN_VAL
    wb = w_in.astype(BF16)
    wu, wv, wqkv, wz = wb[:, :c0], wb[:, c0:c1], wb[:, c1:c2], wb[:, c2:c3]
    wab = jnp.pad(wb[:, c3:], ((0, 0), (0, LANES - 2 * DN_HEADS)))
    gmg = gm_norm_g.reshape(1, GM_WIDTH).astype(F32)
    bsb = jnp.repeat(gm_bs.T, HEAD_DIM, axis=1).astype(F32)
    alog = _pad_lanes(dn_a_log)
    dtb = _pad_lanes(dn_dt_bias)
    tri_incl = _block_tril(ROW_TILE, DN_CHUNK, strict=False)

    oa, q, k, v, z, gb, gt = _inproj(
        x2, norm_mix_g.reshape(1, d), wu, wv, wqkv, wz, wab, gmg, gm_ws.astype(F32), bsb,
        dn_conv_w.astype(F32), alog, dtb, tri_incl, seq=seq)

    grow = gt[:DN_HEADS].reshape(DN_HEADS, t // DN_CHUNK, DN_CHUNK).transpose(1, 0, 2)
    grow = grow.reshape(t // DN_CHUNK, DN_HEADS * DN_CHUNK)
    ob = _gdn(q, k, v, z, gb, grow, dn_norm_g.reshape(1, HEAD_DIM).astype(F32), batch=batch, seq=seq)

    wo = w_out.astype(BF16)
    rw = jnp.pad(router_w.astype(BF16), ((0, 0), (0, LANES - N_EXPERTS)))
    rb = _pad_lanes(router_b, fill=NEG_BIG)
    tri_strict = _block_tril(ROW_TILE, ROW_TILE, strict=True)
    hres, hp, meta, gates, cnt = _outproj(oa, ob, x2, wo[:GM_WIDTH], wo[GM_WIDTH:],
                                          norm_ffn_g.reshape(1, d), rw, rb, tri_strict)

    bm = FFN_BLOCK
    counts = cnt[0, :N_EXPERTS]
    padded = (counts + bm - 1) // bm * bm
    pad_end = jnp.cumsum(padded)
    pad_start = (pad_end - padded).astype(I32)
    n_blocks = (t * TOP_K + N_EXPERTS * bm) // bm
    nvalid = (pad_end[-1] // bm).astype(I32).reshape(1)
    blk = jnp.minimum(jnp.arange(n_blocks, dtype=I32), nvalid[0] - 1)
    blk_e = jnp.minimum(jnp.sum(pad_end[None, :] <= (blk * bm)[:, None], axis=1), N_EXPERTS - 1).astype(I32)
    meta_flat = meta[:, :8].reshape(-1)

    xin = _dispatch(pad_start, meta_flat, hp, jnp.zeros((n_blocks * bm, d // 2), U32))
    y = _ffn(blk_e, nvalid, xin, exp_w_gu.astype(BF16), exp_b_gu[:, None, :].astype(F32),
             exp_w_down.astype(BF16), exp_b_down[:, None, :].astype(F32))
    out = _combine(pad_start, meta_flat, hres, gates, out_g.reshape(1, d).astype(F32), y)
    return out.reshape(batch, seq, d)


def kernel(x, norm_mix_g, w_in, gm_norm_g, gm_ws, gm_bs, dn_conv_w, dn_a_log, dn_dt_bias, dn_norm_g, w_out, norm_ffn_g, router_w, router_b, exp_w_gu, exp_b_gu, exp_w_down, exp_b_down, final_norm_g):
    depth = norm_mix_g.shape[0]
    assert depth == 1, "single-layer problem"
    return _layer(x, norm_mix_g[0], w_in[0], gm_norm_g[0], gm_ws[0], gm_bs[0], dn_conv_w[0],
                  dn_a_log[0], dn_dt_bias[0], dn_norm_g[0], w_out[0], norm_ffn_g[0], router_w[0],
                  router_b[0], exp_w_gu[0], exp_b_gu[0], exp_w_down[0], exp_b_down[0], final_norm_g)
```

```python
import functools

import jax
import jax.numpy as jnp
from jax import lax
from jax.experimental import pallas as pl
from jax.experimental.pallas import tpu as pltpu

F32 = jnp.float32
BF16 = jnp.bfloat16
I32 = jnp.int32
U32 = jnp.uint32

D_MODEL = 1024
HEAD_DIM = 128
GM_HEADS = 4
GM_WIDTH = GM_HEADS * HEAD_DIM
GM_CHUNK = 128
DN_HEADS = 4
DN_DK = 128
DN_KEY = DN_HEADS * DN_DK
DN_VAL = DN_HEADS * HEAD_DIM
DN_CHUNK = 64
CONV_W = 4
N_EXPERTS = 32
TOP_K = 4
D_FF = D_MODEL
SWIGLU_LIMIT = 7.0
SWIGLU_ALPHA = 1.702
EPS = 1e-6

LANES = 128
ROW_TILE = 512
FFN_BLOCK = 512
COMBINE_TILE = 256
VMEM_LIMIT = 56 * 1024 * 1024
NEG_BIG = -1e30
DMA_UNROLL = 2
GDN_GROUP = 4


def _dot(a, b):
    return jnp.dot(a, b, preferred_element_type=F32)


def _dot_nt(a, b):
    return lax.dot_general(a, b, (((1,), (1,)), ((), ())), preferred_element_type=F32)


def _dot_tn(a, b):
    return lax.dot_general(a, b, (((0,), (0,)), ((), ())), preferred_element_type=F32)


def _rms(x, g):
    return x * lax.rsqrt(jnp.mean(x * x, axis=-1, keepdims=True) + EPS) * g


def _gelu(x):
    return 0.5 * x * (1.0 + lax.erf(x * (2.0 ** -0.5)))


def _sigmoid(x):
    return 1.0 / (1.0 + jnp.exp(-x))


def _softplus(x):
    return jnp.maximum(x, 0.0) + jnp.log1p(jnp.exp(-jnp.abs(x)))


def _inproj_body(x_ref, ng_ref, wu_ref, wv_ref, wqkv_ref, wz_ref, wab_ref, gmg_ref, ws_ref,
                 bs_ref, cw_ref, alog_ref, dtb_ref, tri_ref,
                 oa_ref, q_ref, k_ref, v_ref, z_ref, gb_ref, gt_ref, cbuf_ref,
                 *, tm, tiles_per_seq):
    i = pl.program_id(0)
    y = _rms(x_ref[...], ng_ref[...]).astype(BF16)

    u = _gelu(_dot(y, wu_ref[...]))
    vg = _gelu(_dot(y, wv_ref[...]))
    ri = lax.broadcasted_iota(I32, (GM_CHUNK, GM_CHUNK), 0)
    ci = lax.broadcasted_iota(I32, (GM_CHUNK, GM_CHUNK), 1)
    causal = ri >= ci
    for h in range(GM_HEADS):
        sl = slice(h * HEAD_DIM, (h + 1) * HEAD_DIM)
        vh = _rms(vg[:, sl], gmg_ref[:, sl]).astype(BF16)
        wm = jnp.where(causal, ws_ref[h], 0.0).astype(BF16)
        for c in range(tm // GM_CHUNK):
            rs = slice(c * GM_CHUNK, (c + 1) * GM_CHUNK)
            gate = _dot(wm, vh[rs]) + bs_ref[:, sl]
            oa_ref[rs, sl] = (u[rs, sl] * gate).astype(BF16)

    pq = _dot(y, wqkv_ref[...])

    @pl.when(i % tiles_per_seq == 0)
    def _():
        cbuf_ref[0:8, :] = jnp.zeros((8, 3 * DN_KEY), F32)

    cbuf_ref[8:8 + tm, :] = pq
    acc = cw_ref[CONV_W - 1:CONV_W, :] * pq
    for j in range(CONV_W - 1):
        off = 8 - (CONV_W - 1) + j
        acc = acc + cw_ref[j:j + 1, :] * cbuf_ref[off:off + tm, :]
    cbuf_ref[0:8, :] = pq[tm - 8:tm, :]
    s = acc * _sigmoid(acc)
    for h in range(DN_HEADS):
        sl = slice(h * DN_DK, (h + 1) * DN_DK)
        qh = s[:, sl]
        q_ref[:, sl] = (qh * lax.rsqrt(jnp.sum(qh * qh, axis=-1, keepdims=True) + EPS)
                        * (DN_DK ** -0.5)).astype(BF16)
        kh = s[:, DN_KEY + h * DN_DK:DN_KEY + (h + 1) * DN_DK]
        k_ref[:, sl] = (kh * lax.rsqrt(jnp.sum(kh * kh, axis=-1, keepdims=True) + EPS)).astype(BF16)
    v_ref[...] = s[:, 2 * DN_KEY:].astype(BF16)
    z_ref[...] = _dot(y, wz_ref[...]).astype(BF16)

    ab = _dot(y, wab_ref[...])
    g = -jnp.exp(alog_ref[...]) * _softplus(ab + dtb_ref[...])
    beta = _sigmoid(ab)
    g_hi = g.astype(BF16)
    r1 = g - g_hi.astype(F32)
    g_mid = r1.astype(BF16)
    g_lo = (r1 - g_mid.astype(F32)).astype(BF16)
    tri = tri_ref[...]
    gc = _dot(tri, g_hi) + _dot(tri, g_mid) + _dot(tri, g_lo)
    lane = lax.broadcasted_iota(I32, (tm, LANES), 1)
    gbv = jnp.where(lane < DN_HEADS, gc, beta)
    gb_ref[...] = gbv
    gt_ref[...] = gbv.T[0:8, :]


def _inproj(x2, ng, wu, wv, wqkv, wz, wab, gmg, ws, bsb, cw, alog, dtb, tri, *, seq):
    t = x2.shape[0]
    tm = ROW_TILE
    const2 = lambda i: (0, 0)
    row = lambda i: (i, 0)
    full = lambda a: pl.BlockSpec(a.shape, (lambda i: (0,) * a.ndim))
    out_shapes = (
        jax.ShapeDtypeStruct((t, GM_WIDTH), BF16),
        jax.ShapeDtypeStruct((t, DN_KEY), BF16),
        jax.ShapeDtypeStruct((t, DN_KEY), BF16),
        jax.ShapeDtypeStruct((t, DN_VAL), BF16),
        jax.ShapeDtypeStruct((t, DN_VAL), BF16),
        jax.ShapeDtypeStruct((t, LANES), F32),
        jax.ShapeDtypeStruct((8, t), F32),
    )
    return pl.pallas_call(
        functools.partial(_inproj_body, tm=tm, tiles_per_seq=seq // tm),
        grid=(t // tm,),
        in_specs=[pl.BlockSpec((tm, D_MODEL), row), full(ng), full(wu), full(wv), full(wqkv),
                  full(wz), full(wab), full(gmg), full(ws), full(bsb), full(cw), full(alog),
                  full(dtb), full(tri)],
        out_specs=(pl.BlockSpec((tm, GM_WIDTH), row), pl.BlockSpec((tm, DN_KEY), row),
                   pl.BlockSpec((tm, DN_KEY), row), pl.BlockSpec((tm, DN_VAL), row),
                   pl.BlockSpec((tm, DN_VAL), row), pl.BlockSpec((tm, LANES), row),
                   pl.BlockSpec((8, tm), lambda i: (0, i))),
        out_shape=out_shapes,
        scratch_shapes=[pltpu.VMEM((tm + 8, 3 * DN_KEY), F32)],
        compiler_params=pltpu.CompilerParams(dimension_semantics=("arbitrary",),
                                             vmem_limit_bytes=VMEM_LIMIT),
        name="inproj",
    )(x2, ng, wu, wv, wqkv, wz, wab, gmg, ws, bsb, cw, alog, dtb, tri)


def _gdn_body(q_ref, k_ref, v_ref, z_ref, gb_ref, gr_ref, ng_ref, ob_ref, s_ref, *, nchunk,
              group_size):
    j = pl.program_id(1)

    @pl.when(j == 0)
    def _():
        s_ref[...] = jnp.zeros(s_ref.shape, F32)

    c = DN_CHUNK
    n = DN_HEADS * c
    ri = lax.broadcasted_iota(I32, (n, n), 0)
    ci = lax.broadcasted_iota(I32, (n, n), 1)
    same = (ri // c) == (ci // c)
    incl = same & ((ri % c) >= (ci % c))
    strict = same & ((ri % c) > (ci % c))
    ng = ng_ref[...]

    def stack(a):
        return jnp.concatenate([a[:, h * HEAD_DIM:(h + 1) * HEAD_DIM] for h in range(DN_HEADS)], axis=0)

    def prepare(ic):
        r0 = pl.multiple_of(ic * c, c)
        kst = stack(k_ref[pl.ds(r0, c), :])
        qst = stack(q_ref[pl.ds(r0, c), :])
        vst = stack(v_ref[pl.ds(r0, c), :])
        gbc = gb_ref[pl.ds(r0, c), :]
        grow = gr_ref[pl.ds(ic, 1), :]
        gcol = jnp.concatenate([gbc[:, h:h + 1] for h in range(DN_HEADS)], axis=0)
        bcol = jnp.concatenate([gbc[:, DN_HEADS + h:DN_HEADS + h + 1] for h in range(DN_HEADS)], axis=0)
        glast = jnp.concatenate(
            [jnp.broadcast_to(gbc[c - 1:c, h:h + 1], (c, 1)) for h in range(DN_HEADS)], axis=0)
        decay = jnp.where(incl, jnp.exp(jnp.where(incl, gcol - grow, 0.0)), 0.0)
        kf = kst.astype(F32)
        kb = kf * bcol
        lmat = jnp.where(strict, _dot_nt(kb.astype(BF16), kst) * decay, 0.0)
        eg = jnp.exp(gcol)
        rhs = jnp.concatenate([vst.astype(F32) * bcol, kb * eg], axis=1)
        attn = jnp.where(incl, _dot_nt(qst, kst) * decay, 0.0).astype(BF16)
        qd = (qst.astype(F32) * eg).astype(BF16)
        ke = (kf * jnp.exp(glast - gcol)).astype(BF16)
        return dict(r0=r0, gbc=gbc, lmat=lmat, rhs=rhs, attn=attn, qd=qd, ke=ke)

    def advance_state(p, tinv):
        sol = p["rhs"] + _dot(tinv.astype(BF16), p["rhs"].astype(BF16))
        u = sol[:, :HEAD_DIM]
        wb = sol[:, HEAD_DIM:].astype(BF16)
        vn, qs = [], []
        for h in range(DN_HEADS):
            rs = slice(h * c, (h + 1) * c)
            sb = s_ref[h].astype(BF16)
            vn.append(u[rs] - _dot(wb[rs], sb))
            qs.append(_dot(p["qd"][rs], sb))
        vnb = jnp.concatenate(vn, axis=0).astype(BF16)
        o = jnp.concatenate(qs, axis=0) + _dot(p["attn"], vnb)
        zc = z_ref[pl.ds(p["r0"], c), :]
        for h in range(DN_HEADS):
            rs = slice(h * c, (h + 1) * c)
            sl = slice(h * HEAD_DIM, (h + 1) * HEAD_DIM)
            s_ref[h] = (s_ref[h] * jnp.exp(p["gbc"][c - 1:c, h:h + 1])
                        + _dot_tn(p["ke"][rs], vnb[rs]))
            zf = zc[:, sl].astype(F32)
            ob_ref[pl.ds(p["r0"], c), sl] = (_rms(o[rs], ng) * (zf * _sigmoid(zf))).astype(BF16)

    def group(ig, carry):
        ps = [prepare(ig * group_size + b) for b in range(group_size)]
        tinv = [-p["lmat"] for p in ps]
        pw = [p["lmat"] for p in ps]
        for _ in range(5):
            pb = [x.astype(BF16) for x in pw]
            pw = [_dot(x, x) for x in pb]
            pwb = [x.astype(BF16) for x in pw]
            tinv = [t + x + _dot(t.astype(BF16), xb) for t, x, xb in zip(tinv, pw, pwb)]
        for p, t in zip(ps, tinv):
            advance_state(p, t)
        return carry

    lax.fori_loop(0, nchunk // group_size, group, 0)


def _gdn(q, k, v, z, gb, grow, ng, *, batch, seq):
    tm = ROW_TILE
    nchunk = tm // DN_CHUNK
    steps = seq // tm
    rows = lambda b, j: (b * steps + j, 0)
    return pl.pallas_call(
        functools.partial(_gdn_body, nchunk=nchunk, group_size=GDN_GROUP),
        grid=(batch, steps),
        in_specs=[pl.BlockSpec((tm, DN_KEY), rows), pl.BlockSpec((tm, DN_KEY), rows),
                  pl.BlockSpec((tm, DN_VAL), rows), pl.BlockSpec((tm, DN_VAL), rows),
                  pl.BlockSpec((tm, LANES), rows),
                  pl.BlockSpec((nchunk, DN_HEADS * DN_CHUNK), rows),
                  pl.BlockSpec((1, HEAD_DIM), lambda b, j: (0, 0))],
        out_specs=pl.BlockSpec((tm, DN_VAL), rows),
        out_shape=jax.ShapeDtypeStruct((batch * seq, DN_VAL), BF16),
        scratch_shapes=[pltpu.VMEM((DN_HEADS, DN_DK, HEAD_DIM), F32)],
        compiler_params=pltpu.CompilerParams(dimension_semantics=("arbitrary", "arbitrary"),
                                             vmem_limit_bytes=VMEM_LIMIT),
        name="gdn",
    )(q, k, v, z, gb, grow, ng)


def _outproj_body(oa_ref, ob_ref, x_ref, woa_ref, wob_ref, ng_ref, rw_ref, rb_ref, tri_ref,
                  h_ref, hp_ref, meta_ref, gate_ref, cnt_ref, run_ref, *, tm):
    i = pl.program_id(0)

    @pl.when(i == 0)
    def _():
        run_ref[...] = jnp.zeros(run_ref.shape, F32)

    h = x_ref[...] + _dot(oa_ref[...], woa_ref[...]) + _dot(ob_ref[...], wob_ref[...])
    h_ref[...] = h
    hb = _rms(h, ng_ref[...]).astype(BF16)
    half = D_MODEL // 2
    lo = pltpu.bitcast(hb[:, :half].astype(F32), U32) >> 16
    hi = pltpu.bitcast(hb[:, half:].astype(F32), U32) & jnp.uint32(0xFFFF0000)
    hp_ref[...] = lo | hi

    logits = _dot(hb, rw_ref[...]) + rb_ref[...]
    lane = lax.broadcasted_iota(I32, (tm, LANES), 1)
    lanef = lane.astype(F32)
    work = logits
    onehot = jnp.zeros((tm, LANES), F32)
    vals, sels = [], []
    for _ in range(TOP_K):
        m = jnp.max(work, axis=-1, keepdims=True)
        idx = jnp.min(jnp.where(work == m, lanef, float(LANES)), axis=-1, keepdims=True)
        sel = lanef == idx
        work = jnp.where(sel, -3e38, work)
        onehot = onehot + jnp.where(sel, 1.0, 0.0)
        vals.append(m)
        sels.append((sel, idx))
    ex = [jnp.exp(v - vals[0]) for v in vals]
    den = ex[0] + ex[1] + ex[2] + ex[3]
    pref = _dot(tri_ref[...], onehot.astype(BF16)) + run_ref[0:1, :]
    meta = jnp.zeros((tm, LANES), F32)
    gates = jnp.zeros((tm, LANES), F32)
    for kk in range(TOP_K):
        sel, idx = sels[kk]
        rank = jnp.sum(jnp.where(sel, pref, 0.0), axis=-1, keepdims=True)
        meta = meta + jnp.where(lane == kk, idx, 0.0) + jnp.where(lane == TOP_K + kk, rank, 0.0)
        gates = gates + jnp.where(lane == kk, ex[kk] / den, 0.0)
    meta_ref[...] = meta.astype(I32)
    gate_ref[...] = gates
    run = run_ref[...] + jnp.sum(onehot, axis=0, keepdims=True)
    run_ref[...] = run
    cnt_ref[...] = run.astype(I32)


def _outproj(oa, ob, x2, woa, wob, ng, rw, rb, tri):
    t = x2.shape[0]
    tm = ROW_TILE
    row = lambda i: (i, 0)
    full = lambda a: pl.BlockSpec(a.shape, (lambda i: (0,) * a.ndim))
    out_shapes = (
        jax.ShapeDtypeStruct((t, D_MODEL), F32),
        jax.ShapeDtypeStruct((t, D_MODEL // 2), U32),
        jax.ShapeDtypeStruct((t, LANES), I32),
        jax.ShapeDtypeStruct((t, LANES), F32),
        jax.ShapeDtypeStruct((8, LANES), I32),
    )
    return pl.pallas_call(
        functools.partial(_outproj_body, tm=tm),
        grid=(t // tm,),
        in_specs=[pl.BlockSpec((tm, GM_WIDTH), row), pl.BlockSpec((tm, DN_VAL), row),
                  pl.BlockSpec((tm, D_MODEL), row), full(woa), full(wob), full(ng), full(rw),
                  full(rb), full(tri)],
        out_specs=(pl.BlockSpec((tm, D_MODEL), row), pl.BlockSpec((tm, D_MODEL // 2), row),
                   pl.BlockSpec((tm, LANES), row), pl.BlockSpec((tm, LANES), row),
                   pl.BlockSpec((8, LANES), lambda i: (0, 0))),
        out_shape=out_shapes,
        scratch_shapes=[pltpu.VMEM((8, LANES), F32)],
        compiler_params=pltpu.CompilerParams(dimension_semantics=("arbitrary",),
                                             vmem_limit_bytes=VMEM_LIMIT),
        name="outproj",
    )(oa, ob, x2, woa, wob, ng, rw, rb, tri)


def _dispatch_body(pe_ref, dest_ref, hp_ref, xin_ref, zero_ref, sem, zsem, *, tm, bm):
    @pl.when(pl.program_id(0) == 0)
    def _():
        zero_ref[...] = jnp.zeros(zero_ref.shape, U32)

        def fill(e, carry):
            end = pe_ref[e]
            prev = jnp.where(e == 0, 0, pe_ref[jnp.maximum(e - 1, 0)])

            @pl.when(end > prev)
            def _():
                cp = pltpu.make_async_copy(
                    zero_ref, xin_ref.at[pl.ds(pl.multiple_of(end - bm, bm), bm)], zsem)
                cp.start()
                cp.wait()
            return carry

        lax.fori_loop(0, N_EXPERTS, fill, 0)

        def fill_unused(b, carry):
            cp = pltpu.make_async_copy(
                zero_ref, xin_ref.at[pl.ds(pl.multiple_of(b * bm, bm), bm)], zsem)
            cp.start()
            cp.wait()
            return carry

        lax.fori_loop(pe_ref[N_EXPERTS - 1] // bm, xin_ref.shape[0] // bm, fill_unused, 0)

    def start(i, carry):
        for u in range(DMA_UNROLL):
            t = i * DMA_UNROLL + u
            for kk in range(TOP_K):
                pltpu.make_async_copy(hp_ref.at[pl.ds(t, 1)],
                                      xin_ref.at[pl.ds(dest_ref[t * TOP_K + kk], 1)],
                                      sem).start(priority=kk % 2)
        return carry

    lax.fori_loop(0, tm // DMA_UNROLL, start, 0)
    for _ in range(TOP_K):
        pltpu.make_async_copy(hp_ref, xin_ref.at[pl.ds(0, tm)], sem).wait()


def _dispatch(pad_end, dest_flat, hp, n_rows):
    t = hp.shape[0]
    tm = ROW_TILE
    return pl.pallas_call(
        functools.partial(_dispatch_body, tm=tm, bm=FFN_BLOCK),
        grid_spec=pltpu.PrefetchScalarGridSpec(
            num_scalar_prefetch=1,
            grid=(t // tm,),
            in_specs=[pl.BlockSpec((tm * TOP_K,), lambda i, pe: (i,), memory_space=pltpu.SMEM),
                      pl.BlockSpec((tm, D_MODEL // 2), lambda i, pe: (i, 0))],
            out_specs=pl.BlockSpec(memory_space=pl.ANY),
            scratch_shapes=[pltpu.VMEM((FFN_BLOCK, D_MODEL // 2), U32),
                            pltpu.SemaphoreType.DMA(()), pltpu.SemaphoreType.DMA(())],
        ),
        out_shape=jax.ShapeDtypeStruct((n_rows, D_MODEL // 2), U32),
        compiler_params=pltpu.CompilerParams(dimension_semantics=("arbitrary",)),
        name="dispatch",
    )(pad_end, dest_flat, hp)


def _ffn_body(be_ref, nv_ref, x_ref, wgu_ref, bgu_ref, wd_ref, bd_ref, y_ref):
    del be_ref

    @pl.when(pl.program_id(0) >= nv_ref[0])
    def _():
        y_ref[...] = jnp.zeros(y_ref.shape, F32)

    @pl.when(pl.program_id(0) < nv_ref[0])
    def _():
        half = D_MODEL // 2
        xp = x_ref[...]
        lo = pltpu.bitcast(xp << 16, F32).astype(BF16)
        hi = pltpu.bitcast(xp & jnp.uint32(0xFFFF0000), F32).astype(BF16)
        gu = (_dot(lo, wgu_ref[0, :half, :].astype(BF16))
              + _dot(hi, wgu_ref[0, half:, :].astype(BF16)) + bgu_ref[0])
        gate = jnp.minimum(gu[:, :D_FF], SWIGLU_LIMIT)
        up = jnp.clip(gu[:, D_FF:], -SWIGLU_LIMIT, SWIGLU_LIMIT)
        act = (up + 1.0) * (gate * _sigmoid(SWIGLU_ALPHA * gate))
        y_ref[...] = _dot(act.astype(BF16), wd_ref[0].astype(BF16)) + bd_ref[0]


def _ffn(blk_e, nvalid, xin, wgu, bgu, wd, bd):
    p = xin.shape[0]
    bm = FFN_BLOCK
    rows = lambda i, be, nv: (jnp.minimum(i, nv[0] - 1), 0)
    wsel = lambda i, be, nv: (be[i], 0, 0)
    return pl.pallas_call(
        _ffn_body,
        grid_spec=pltpu.PrefetchScalarGridSpec(
            num_scalar_prefetch=2,
            grid=(p // bm,),
            in_specs=[pl.BlockSpec((bm, D_MODEL // 2), rows),
                      pl.BlockSpec((1, D_MODEL, 2 * D_FF), wsel),
                      pl.BlockSpec((1, 1, 2 * D_FF), wsel),
                      pl.BlockSpec((1, D_FF, D_MODEL), wsel),
                      pl.BlockSpec((1, 1, D_MODEL), wsel)],
            out_specs=pl.BlockSpec((bm, D_MODEL), lambda i, be, nv: (i, 0)),
        ),
        out_shape=jax.ShapeDtypeStruct((p, D_MODEL), F32),
        compiler_params=pltpu.CompilerParams(dimension_semantics=("arbitrary",),
                                             vmem_limit_bytes=VMEM_LIMIT),
        name="ffn",
    )(blk_e, nvalid, xin, wgu, bgu, wd, bd)


def _combine_body(dest_ref, h_ref, gate_ref, fg_ref, y_ref, o_ref, ybuf_ref, sem, *, tm):
    def start(i, carry):
        for u in range(DMA_UNROLL):
            t = i * DMA_UNROLL + u
            for kk in range(TOP_K):
                pltpu.make_async_copy(y_ref.at[pl.ds(dest_ref[t * TOP_K + kk], 1)],
                                      ybuf_ref.at[kk, pl.ds(t, 1)], sem).start(priority=kk % 2)
        return carry

    lax.fori_loop(0, tm // DMA_UNROLL, start, 0)
    for kk in range(TOP_K):
        pltpu.make_async_copy(y_ref.at[pl.ds(0, tm)], ybuf_ref.at[kk], sem).wait()
    gates = gate_ref[...]
    out = h_ref[...]
    for kk in range(TOP_K):
        out = out + gates[:, kk:kk + 1] * ybuf_ref[kk]
    o_ref[...] = _rms(out, fg_ref[...])


def _combine(dest_flat, h, gates, fg, y):
    t = h.shape[0]
    tm = COMBINE_TILE
    return pl.pallas_call(
        functools.partial(_combine_body, tm=tm),
        grid=(t // tm,),
        in_specs=[pl.BlockSpec((tm * TOP_K,), lambda i: (i,), memory_space=pltpu.SMEM),
                  pl.BlockSpec((tm, D_MODEL), lambda i: (i, 0)),
                  pl.BlockSpec((tm, LANES), lambda i: (i, 0)),
                  pl.BlockSpec((1, D_MODEL), lambda i: (0, 0)),
                  pl.BlockSpec(memory_space=pl.ANY)],
        out_specs=pl.BlockSpec((tm, D_MODEL), lambda i: (i, 0)),
        scratch_shapes=[pltpu.VMEM((TOP_K, tm, D_MODEL), F32), pltpu.SemaphoreType.DMA(())],
        out_shape=jax.ShapeDtypeStruct((t, D_MODEL), F32),
        compiler_params=pltpu.CompilerParams(dimension_semantics=("arbitrary",),
                                             vmem_limit_bytes=VMEM_LIMIT),
        name="combine",
    )(dest_flat, h, gates, fg, y)


def _block_tril(n, chunk, strict):
    r = jnp.arange(n)[:, None]
    c = jnp.arange(n)[None, :]
    keep = ((r // chunk) == (c // chunk)) & ((r > c) if strict else (r >= c))
    return keep.astype(BF16)


def _pad_lanes(a, fill=0.0):
    a = a.reshape(1, -1).astype(F32)
    return jnp.pad(a, ((0, 0), (0, LANES - a.shape[1])), constant_values=fill)


def _layer(h, norm_mix_g, w_in, gm_norm_g, gm_ws, gm_bs, dn_conv_w, dn_a_log, dn_dt_bias,
           dn_norm_g, w_out, norm_ffn_g, router_w, router_b, exp_w_gu, exp_b_gu, exp_w_down,
           exp_b_down, out_g):
    batch, seq, d = h.shape
    t = batch * seq
    x2 = h.reshape(t, d)

    c0, c1, c2 = GM_WIDTH, 2 * GM_WIDTH, 2 * GM_WIDTH + 3 * DN_KEY
    c3 = c2 + DN_VAL
    wb = w_in.astype(BF16)
    wu, wv, wqkv, wz = wb[:, :c0], wb[:, c0:c1], wb[:, c1:c2], wb[:, c2:c3]
    wab = jnp.pad(wb[:, c3:], ((0, 0), (0, LANES - 2 * DN_HEADS)))
    gmg = gm_norm_g.reshape(1, GM_WIDTH).astype(F32)
    bsb = jnp.repeat(gm_bs.T, HEAD_DIM, axis=1).astype(F32)
    alog = _pad_lanes(dn_a_log)
    dtb = _pad_lanes(dn_dt_bias)
    tri_incl = _block_tril(ROW_TILE, DN_CHUNK, strict=False)

    oa, q, k, v, z, gb, gt = _inproj(
        x2, norm_mix_g.reshape(1, d), wu, wv, wqkv, wz, wab, gmg, gm_ws.astype(F32), bsb,
        dn_conv_w.astype(F32), alog, dtb, tri_incl, seq=seq)

    grow = gt[:DN_HEADS].reshape(DN_HEADS, t // DN_CHUNK, DN_CHUNK).transpose(1, 0, 2)
    grow = grow.reshape(t // DN_CHUNK, DN_HEADS * DN_CHUNK)
    ob = _gdn(q, k, v, z, gb, grow, dn_norm_g.reshape(1, HEAD_DIM).astype(F32), batch=batch, seq=seq)

    wo = w_out.astype(BF16)
    rw = jnp.pad(router_w.astype(BF16), ((0, 0), (0, LANES - N_EXPERTS)))
    rb = _pad_lanes(router_b, fill=NEG_BIG)
    tri_strict = _block_tril(ROW_TILE, ROW_TILE, strict=True)
    hres, hp, meta, gates, cnt = _outproj(oa, ob, x2, wo[:GM_WIDTH], wo[GM_WIDTH:],
                                          norm_ffn_g.reshape(1, d), rw, rb, tri_strict)

    bm = FFN_BLOCK
    counts = cnt[0, :N_EXPERTS]
    padded = (counts + bm - 1) // bm * bm
    pad_end = jnp.cumsum(padded)
    pad_start = (pad_end - padded).astype(I32)
    n_blocks = (t * TOP_K + N_EXPERTS * bm) // bm
    nvalid = (pad_end[-1] // bm).astype(I32).reshape(1)
    blk = jnp.minimum(jnp.arange(n_blocks, dtype=I32), nvalid[0] - 1)
    blk_e = jnp.minimum(jnp.sum(pad_end[None, :] <= (blk * bm)[:, None], axis=1), N_EXPERTS - 1).astype(I32)
    dest_flat = (jnp.take(pad_start, meta[:, :TOP_K]) + meta[:, TOP_K:2 * TOP_K]).reshape(-1)

    xin = _dispatch(pad_end.astype(I32), dest_flat, hp, n_blocks * bm)
    y = _ffn(blk_e, nvalid, xin, exp_w_gu, exp_b_gu[:, None, :].astype(F32),
             exp_w_down, exp_b_down[:, None, :].astype(F32))
    out = _combine(dest_flat, hres, gates, out_g.reshape(1, d).astype(F32), y)
    return out.reshape(batch, seq, d)


def kernel(x, norm_mix_g, w_in, gm_norm_g, gm_ws, gm_bs, dn_conv_w, dn_a_log, dn_dt_bias, dn_norm_g, w_out, norm_ffn_g, router_w, router_b, exp_w_gu, exp_b_gu, exp_w_down, exp_b_down, final_norm_g):
    depth = norm_mix_g.shape[0]
    assert depth == 1, "single-layer problem"
    return _layer(x, norm_mix_g[0], w_in[0], gm_norm_g[0], gm_ws[0], gm_bs[0], dn_conv_w[0],
                  dn_a_log[0], dn_dt_bias[0], dn_norm_g[0], w_out[0], norm_ffn_g[0], router_w[0],
                  router_b[0], exp_w_gu[0], exp_b_gu[0], exp_w_down[0], exp_b_down[0], final_norm_g)
```

```python
import functools

import jax
import jax.numpy as jnp
from jax import lax
from jax.experimental import pallas as pl
from jax.experimental.pallas import tpu as pltpu

F32 = jnp.float32
BF16 = jnp.bfloat16
I32 = jnp.int32
U32 = jnp.uint32

D_MODEL = 1024
HEAD_DIM = 128
GM_HEADS = 4
GM_WIDTH = GM_HEADS * HEAD_DIM
GM_CHUNK = 128
DN_HEADS = 4
DN_DK = 128
DN_KEY = DN_HEADS * DN_DK
DN_VAL = DN_HEADS * HEAD_DIM
DN_CHUNK = 64
CONV_W = 4
N_EXPERTS = 32
TOP_K = 4
D_FF = D_MODEL
SWIGLU_LIMIT = 7.0
SWIGLU_ALPHA = 1.702
EPS = 1e-6

LANES = 128
ROW_TILE = 512
FFN_BLOCK = 512
COMBINE_TILE = 256
VMEM_LIMIT = 56 * 1024 * 1024
NEG_BIG = -1e30
DMA_UNROLL = 4
GDN_GROUP = 4


def _dot(a, b):
    return jnp.dot(a, b, preferred_element_type=F32)


def _dot_nt(a, b):
    return lax.dot_general(a, b, (((1,), (1,)), ((), ())), preferred_element_type=F32)


def _dot_tn(a, b):
    return lax.dot_general(a, b, (((0,), (0,)), ((), ())), preferred_element_type=F32)


def _rms(x, g):
    return x * lax.rsqrt(jnp.mean(x * x, axis=-1, keepdims=True) + EPS) * g


def _gelu(x):
    return 0.5 * x * (1.0 + lax.erf(x * (2.0 ** -0.5)))


def _sigmoid(x):
    return 1.0 / (1.0 + jnp.exp(-x))


def _softplus(x):
    return jnp.maximum(x, 0.0) + jnp.log1p(jnp.exp(-jnp.abs(x)))


def _inproj_body(x_ref, ng_ref, wu_ref, wv_ref, wqkv_ref, wz_ref, wab_ref, gmg_ref, ws_ref,
                 bs_ref, cw_ref, alog_ref, dtb_ref, tri_ref,
                 oa_ref, q_ref, k_ref, v_ref, z_ref, gb_ref, gt_ref, cbuf_ref,
                 *, tm, tiles_per_seq):
    i = pl.program_id(0)
    y = _rms(x_ref[...], ng_ref[...]).astype(BF16)

    u = _gelu(_dot(y, wu_ref[...]))
    vg = _gelu(_dot(y, wv_ref[...]))
    ri = lax.broadcasted_iota(I32, (GM_CHUNK, GM_CHUNK), 0)
    ci = lax.broadcasted_iota(I32, (GM_CHUNK, GM_CHUNK), 1)
    causal = ri >= ci
    for h in range(GM_HEADS):
        sl = slice(h * HEAD_DIM, (h + 1) * HEAD_DIM)
        vh = _rms(vg[:, sl], gmg_ref[:, sl]).astype(BF16)
        wm = jnp.where(causal, ws_ref[h], 0.0).astype(BF16)
        for c in range(tm // GM_CHUNK):
            rs = slice(c * GM_CHUNK, (c + 1) * GM_CHUNK)
            gate = _dot(wm, vh[rs]) + bs_ref[:, sl]
            oa_ref[rs, sl] = (u[rs, sl] * gate).astype(BF16)

    pq = _dot(y, wqkv_ref[...])

    @pl.when(i % tiles_per_seq == 0)
    def _():
        cbuf_ref[0:8, :] = jnp.zeros((8, 3 * DN_KEY), F32)

    cbuf_ref[8:8 + tm, :] = pq
    acc = cw_ref[CONV_W - 1:CONV_W, :] * pq
    for j in range(CONV_W - 1):
        off = 8 - (CONV_W - 1) + j
        acc = acc + cw_ref[j:j + 1, :] * cbuf_ref[off:off + tm, :]
    cbuf_ref[0:8, :] = pq[tm - 8:tm, :]
    s = acc * _sigmoid(acc)
    for h in range(DN_HEADS):
        sl = slice(h * DN_DK, (h + 1) * DN_DK)
        qh = s[:, sl]
        q_ref[:, sl] = (qh * lax.rsqrt(jnp.sum(qh * qh, axis=-1, keepdims=True) + EPS)
                        * (DN_DK ** -0.5)).astype(BF16)
        kh = s[:, DN_KEY + h * DN_DK:DN_KEY + (h + 1) * DN_DK]
        k_ref[:, sl] = (kh * lax.rsqrt(jnp.sum(kh * kh, axis=-1, keepdims=True) + EPS)).astype(BF16)
    v_ref[...] = s[:, 2 * DN_KEY:].astype(BF16)
    z_ref[...] = _dot(y, wz_ref[...]).astype(BF16)

    ab = _dot(y, wab_ref[...])
    g = -jnp.exp(alog_ref[...]) * _softplus(ab + dtb_ref[...])
    beta = _sigmoid(ab)
    g_hi = g.astype(BF16)
    r1 = g - g_hi.astype(F32)
    g_mid = r1.astype(BF16)
    g_lo = (r1 - g_mid.astype(F32)).astype(BF16)
    tri = tri_ref[...]
    gc = _dot(tri, g_hi) + _dot(tri, g_mid) + _dot(tri, g_lo)
    lane = lax.broadcasted_iota(I32, (tm, LANES), 1)
    gbv = jnp.where(lane < DN_HEADS, gc, beta)
    gb_ref[...] = gbv
    gt_ref[...] = gbv.T[0:8, :]


def _inproj(x2, ng, wu, wv, wqkv, wz, wab, gmg, ws, bsb, cw, alog, dtb, tri, *, seq):
    t = x2.shape[0]
    tm = ROW_TILE
    const2 = lambda i: (0, 0)
    row = lambda i: (i, 0)
    full = lambda a: pl.BlockSpec(a.shape, (lambda i: (0,) * a.ndim))
    out_shapes = (
        jax.ShapeDtypeStruct((t, GM_WIDTH), BF16),
        jax.ShapeDtypeStruct((t, DN_KEY), BF16),
        jax.ShapeDtypeStruct((t, DN_KEY), BF16),
        jax.ShapeDtypeStruct((t, DN_VAL), BF16),
        jax.ShapeDtypeStruct((t, DN_VAL), BF16),
        jax.ShapeDtypeStruct((t, LANES), F32),
        jax.ShapeDtypeStruct((8, t), F32),
    )
    return pl.pallas_call(
        functools.partial(_inproj_body, tm=tm, tiles_per_seq=seq // tm),
        grid=(t // tm,),
        in_specs=[pl.BlockSpec((tm, D_MODEL), row), full(ng), full(wu), full(wv), full(wqkv),
                  full(wz), full(wab), full(gmg), full(ws), full(bsb), full(cw), full(alog),
                  full(dtb), full(tri)],
        out_specs=(pl.BlockSpec((tm, GM_WIDTH), row), pl.BlockSpec((tm, DN_KEY), row),
                   pl.BlockSpec((tm, DN_KEY), row), pl.BlockSpec((tm, DN_VAL), row),
                   pl.BlockSpec((tm, DN_VAL), row), pl.BlockSpec((tm, LANES), row),
                   pl.BlockSpec((8, tm), lambda i: (0, i))),
        out_shape=out_shapes,
        scratch_shapes=[pltpu.VMEM((tm + 8, 3 * DN_KEY), F32)],
        compiler_params=pltpu.CompilerParams(dimension_semantics=("arbitrary",),
                                             vmem_limit_bytes=VMEM_LIMIT),
        name="inproj",
    )(x2, ng, wu, wv, wqkv, wz, wab, gmg, ws, bsb, cw, alog, dtb, tri)


def _gdn_body(q_ref, k_ref, v_ref, z_ref, gb_ref, gr_ref, ng_ref, ob_ref, s_ref, *, nchunk,
              group_size):
    j = pl.program_id(1)

    @pl.when(j == 0)
    def _():
        s_ref[...] = jnp.zeros(s_ref.shape, F32)

    c = DN_CHUNK
    n = DN_HEADS * c
    ri = lax.broadcasted_iota(I32, (n, n), 0)
    ci = lax.broadcasted_iota(I32, (n, n), 1)
    same = (ri // c) == (ci // c)
    incl = same & ((ri % c) >= (ci % c))
    strict = same & ((ri % c) > (ci % c))
    ng = ng_ref[...]

    def stack(a):
        return jnp.concatenate([a[:, h * HEAD_DIM:(h + 1) * HEAD_DIM] for h in range(DN_HEADS)], axis=0)

    def prepare(ic):
        r0 = pl.multiple_of(ic * c, c)
        kst = stack(k_ref[pl.ds(r0, c), :])
        qst = stack(q_ref[pl.ds(r0, c), :])
        vst = stack(v_ref[pl.ds(r0, c), :])
        gbc = gb_ref[pl.ds(r0, c), :]
        grow = gr_ref[pl.ds(ic, 1), :]
        gcol = jnp.concatenate([gbc[:, h:h + 1] for h in range(DN_HEADS)], axis=0)
        bcol = jnp.concatenate([gbc[:, DN_HEADS + h:DN_HEADS + h + 1] for h in range(DN_HEADS)], axis=0)
        glast = jnp.concatenate(
            [jnp.broadcast_to(gbc[c - 1:c, h:h + 1], (c, 1)) for h in range(DN_HEADS)], axis=0)
        decay = jnp.where(incl, jnp.exp(jnp.where(incl, gcol - grow, 0.0)), 0.0)
        kf = kst.astype(F32)
        kb = kf * bcol
        lmat = jnp.where(strict, _dot_nt(kb.astype(BF16), kst) * decay, 0.0)
        eg = jnp.exp(gcol)
        rhs = jnp.concatenate([vst.astype(F32) * bcol, kb * eg], axis=1)
        attn = jnp.where(incl, _dot_nt(qst, kst) * decay, 0.0).astype(BF16)
        qd = (qst.astype(F32) * eg).astype(BF16)
        ke = (kf * jnp.exp(glast - gcol)).astype(BF16)
        return dict(r0=r0, gbc=gbc, lmat=lmat, rhs=rhs, attn=attn, qd=qd, ke=ke)

    def advance_state(p, tinv):
        sol = p["rhs"] + _dot(tinv.astype(BF16), p["rhs"].astype(BF16))
        u = sol[:, :HEAD_DIM]
        wb = sol[:, HEAD_DIM:].astype(BF16)
        vn, qs = [], []
        for h in range(DN_HEADS):
            rs = slice(h * c, (h + 1) * c)
            sb = s_ref[h].astype(BF16)
            vn.append(u[rs] - _dot(wb[rs], sb))
            qs.append(_dot(p["qd"][rs], sb))
        vnb = jnp.concatenate(vn, axis=0).astype(BF16)
        o = jnp.concatenate(qs, axis=0) + _dot(p["attn"], vnb)
        zc = z_ref[pl.ds(p["r0"], c), :]
        for h in range(DN_HEADS):
            rs = slice(h * c, (h + 1) * c)
            sl = slice(h * HEAD_DIM, (h + 1) * HEAD_DIM)
            s_ref[h] = (s_ref[h] * jnp.exp(p["gbc"][c - 1:c, h:h + 1])
                        + _dot_tn(p["ke"][rs], vnb[rs]))
            zf = zc[:, sl].astype(F32)
            ob_ref[pl.ds(p["r0"], c), sl] = (_rms(o[rs], ng) * (zf * _sigmoid(zf))).astype(BF16)

    def group(ig, carry):
        ps = [prepare(ig * group_size + b) for b in range(group_size)]
        tinv = [-p["lmat"] for p in ps]
        pw = [p["lmat"] for p in ps]
        for _ in range(5):
            pb = [x.astype(BF16) for x in pw]
            pw = [_dot(x, x) for x in pb]
            pwb = [x.astype(BF16) for x in pw]
            tinv = [t + x + _dot(t.astype(BF16), xb) for t, x, xb in zip(tinv, pw, pwb)]
        for p, t in zip(ps, tinv):
            advance_state(p, t)
        return carry

    lax.fori_loop(0, nchunk // group_size, group, 0)


def _gdn(q, k, v, z, gb, grow, ng, *, batch, seq):
    tm = ROW_TILE
    nchunk = tm // DN_CHUNK
    steps = seq // tm
    rows = lambda b, j: (b * steps + j, 0)
    return pl.pallas_call(
        functools.partial(_gdn_body, nchunk=nchunk, group_size=GDN_GROUP),
        grid=(batch, steps),
        in_specs=[pl.BlockSpec((tm, DN_KEY), rows), pl.BlockSpec((tm, DN_KEY), rows),
                  pl.BlockSpec((tm, DN_VAL), rows), pl.BlockSpec((tm, DN_VAL), rows),
                  pl.BlockSpec((tm, LANES), rows),
                  pl.BlockSpec((nchunk, DN_HEADS * DN_CHUNK), rows),
                  pl.BlockSpec((1, HEAD_DIM), lambda b, j: (0, 0))],
        out_specs=pl.BlockSpec((tm, DN_VAL), rows),
        out_shape=jax.ShapeDtypeStruct((batch * seq, DN_VAL), BF16),
        scratch_shapes=[pltpu.VMEM((DN_HEADS, DN_DK, HEAD_DIM), F32)],
        compiler_params=pltpu.CompilerParams(dimension_semantics=("arbitrary", "arbitrary"),
                                             vmem_limit_bytes=VMEM_LIMIT),
        name="gdn",
    )(q, k, v, z, gb, grow, ng)


def _outproj_body(oa_ref, ob_ref, x_ref, woa_ref, wob_ref, ng_ref, rw_ref, rb_ref, tri_ref,
                  h_ref, hp_ref, meta_ref, gate_ref, cnt_ref, run_ref, *, tm):
    i = pl.program_id(0)

    @pl.when(i == 0)
    def _():
        run_ref[...] = jnp.zeros(run_ref.shape, F32)

    h = x_ref[...] + _dot(oa_ref[...], woa_ref[...]) + _dot(ob_ref[...], wob_ref[...])
    h_ref[...] = h
    hb = _rms(h, ng_ref[...]).astype(BF16)
    half = D_MODEL // 2
    lo = pltpu.bitcast(hb[:, :half].astype(F32), U32) >> 16
    hi = pltpu.bitcast(hb[:, half:].astype(F32), U32) & jnp.uint32(0xFFFF0000)
    hp_ref[...] = lo | hi

    logits = _dot(hb, rw_ref[...]) + rb_ref[...]
    lane = lax.broadcasted_iota(I32, (tm, LANES), 1)
    lanef = lane.astype(F32)
    work = logits
    onehot = jnp.zeros((tm, LANES), F32)
    vals, sels = [], []
    for _ in range(TOP_K):
        m = jnp.max(work, axis=-1, keepdims=True)
        idx = jnp.min(jnp.where(work == m, lanef, float(LANES)), axis=-1, keepdims=True)
        sel = lanef == idx
        work = jnp.where(sel, -3e38, work)
        onehot = onehot + jnp.where(sel, 1.0, 0.0)
        vals.append(m)
        sels.append((sel, idx))
    ex = [jnp.exp(v - vals[0]) for v in vals]
    den = ex[0] + ex[1] + ex[2] + ex[3]
    pref = _dot(tri_ref[...], onehot.astype(BF16)) + run_ref[0:1, :]
    meta = jnp.zeros((tm, LANES), F32)
    gates = jnp.zeros((tm, LANES), F32)
    for kk in range(TOP_K):
        sel, idx = sels[kk]
        rank = jnp.sum(jnp.where(sel, pref, 0.0), axis=-1, keepdims=True)
        meta = meta + jnp.where(lane == kk, idx, 0.0) + jnp.where(lane == TOP_K + kk, rank, 0.0)
        gates = gates + jnp.where(lane == kk, ex[kk] / den, 0.0)
    meta_ref[...] = meta.astype(I32)
    gate_ref[...] = gates
    run = run_ref[...] + jnp.sum(onehot, axis=0, keepdims=True)
    run_ref[...] = run
    cnt_ref[...] = run.astype(I32)


def _outproj(oa, ob, x2, woa, wob, ng, rw, rb, tri):
    t = x2.shape[0]
    tm = ROW_TILE
    row = lambda i: (i, 0)
    full = lambda a: pl.BlockSpec(a.shape, (lambda i: (0,) * a.ndim))
    out_shapes = (
        jax.ShapeDtypeStruct((t, D_MODEL), F32),
        jax.ShapeDtypeStruct((t, D_MODEL // 2), U32),
        jax.ShapeDtypeStruct((t, LANES), I32),
        jax.ShapeDtypeStruct((t, LANES), F32),
        jax.ShapeDtypeStruct((8, LANES), I32),
    )
    return pl.pallas_call(
        functools.partial(_outproj_body, tm=tm),
        grid=(t // tm,),
        in_specs=[pl.BlockSpec((tm, GM_WIDTH), row), pl.BlockSpec((tm, DN_VAL), row),
                  pl.BlockSpec((tm, D_MODEL), row), full(woa), full(wob), full(ng), full(rw),
                  full(rb), full(tri)],
        out_specs=(pl.BlockSpec((tm, D_MODEL), row), pl.BlockSpec((tm, D_MODEL // 2), row),
                   pl.BlockSpec((tm, LANES), row), pl.BlockSpec((tm, LANES), row),
                   pl.BlockSpec((8, LANES), lambda i: (0, 0))),
        out_shape=out_shapes,
        scratch_shapes=[pltpu.VMEM((8, LANES), F32)],
        compiler_params=pltpu.CompilerParams(dimension_semantics=("arbitrary",),
                                             vmem_limit_bytes=VMEM_LIMIT),
        name="outproj",
    )(oa, ob, x2, woa, wob, ng, rw, rb, tri)


def _dispatch_body(pe_ref, dest_ref, hp_ref, xin_ref, zero_ref, sem, zsem, *, tm, bm):
    @pl.when(pl.program_id(0) == 0)
    def _():
        zero_ref[...] = jnp.zeros(zero_ref.shape, U32)

        def fill(e, carry):
            end = pe_ref[e]
            prev = jnp.where(e == 0, 0, pe_ref[jnp.maximum(e - 1, 0)])

            @pl.when(end > prev)
            def _():
                cp = pltpu.make_async_copy(
                    zero_ref, xin_ref.at[pl.ds(pl.multiple_of(end - bm, bm), bm)], zsem)
                cp.start()
                cp.wait()
            return carry

        lax.fori_loop(0, N_EXPERTS, fill, 0)

        def fill_unused(b, carry):
            cp = pltpu.make_async_copy(
                zero_ref, xin_ref.at[pl.ds(pl.multiple_of(b * bm, bm), bm)], zsem)
            cp.start()
            cp.wait()
            return carry

        lax.fori_loop(pe_ref[N_EXPERTS - 1] // bm, xin_ref.shape[0] // bm, fill_unused, 0)

    def start(i, carry):
        for u in range(DMA_UNROLL):
            t = i * DMA_UNROLL + u
            for kk in range(TOP_K):
                pltpu.make_async_copy(hp_ref.at[pl.ds(t, 1)],
                                      xin_ref.at[pl.ds(dest_ref[t * TOP_K + kk], 1)],
                                      sem).start(priority=kk % 2)
        return carry

    lax.fori_loop(0, tm // DMA_UNROLL, start, 0)
    for _ in range(TOP_K):
        pltpu.make_async_copy(hp_ref, xin_ref.at[pl.ds(0, tm)], sem).wait()


def _dispatch(pad_end, dest_flat, hp, n_rows):
    t = hp.shape[0]
    tm = ROW_TILE
    return pl.pallas_call(
        functools.partial(_dispatch_body, tm=tm, bm=FFN_BLOCK),
        grid_spec=pltpu.PrefetchScalarGridSpec(
            num_scalar_prefetch=1,
            grid=(t // tm,),
            in_specs=[pl.BlockSpec((tm * TOP_K,), lambda i, pe: (i,), memory_space=pltpu.SMEM),
                      pl.BlockSpec((tm, D_MODEL // 2), lambda i, pe: (i, 0))],
            out_specs=pl.BlockSpec(memory_space=pl.ANY),
            scratch_shapes=[pltpu.VMEM((FFN_BLOCK, D_MODEL // 2), U32),
                            pltpu.SemaphoreType.DMA(()), pltpu.SemaphoreType.DMA(())],
        ),
        out_shape=jax.ShapeDtypeStruct((n_rows, D_MODEL // 2), U32),
        compiler_params=pltpu.CompilerParams(dimension_semantics=("arbitrary",)),
        name="dispatch",
    )(pad_end, dest_flat, hp)


def _ffn_body(be_ref, nv_ref, x_ref, wgu_ref, bgu_ref, wd_ref, bd_ref, y_ref, wgub_ref, wdb_ref):
    i = pl.program_id(0)

    @pl.when(i >= nv_ref[0])
    def _():
        y_ref[...] = jnp.zeros(y_ref.shape, F32)

    @pl.when((i < nv_ref[0]) & ((i == 0) | (be_ref[i] != be_ref[jnp.maximum(i - 1, 0)])))
    def _():
        wgub_ref[...] = wgu_ref[0].astype(BF16)
        wdb_ref[...] = wd_ref[0].astype(BF16)

    @pl.when(i < nv_ref[0])
    def _():
        half = D_MODEL // 2
        xp = x_ref[...]
        lo = pltpu.bitcast(xp << 16, F32).astype(BF16)
        hi = pltpu.bitcast(xp & jnp.uint32(0xFFFF0000), F32).astype(BF16)
        gu = _dot(lo, wgub_ref[:half, :]) + _dot(hi, wgub_ref[half:, :]) + bgu_ref[0]
        gate = jnp.minimum(gu[:, :D_FF], SWIGLU_LIMIT)
        up = jnp.clip(gu[:, D_FF:], -SWIGLU_LIMIT, SWIGLU_LIMIT)
        act = (up + 1.0) * (gate * _sigmoid(SWIGLU_ALPHA * gate))
        y_ref[...] = _dot(act.astype(BF16), wdb_ref[...]) + bd_ref[0]


def _ffn(blk_e, nvalid, xin, wgu, bgu, wd, bd):
    p = xin.shape[0]
    bm = FFN_BLOCK
    rows = lambda i, be, nv: (jnp.minimum(i, nv[0] - 1), 0)
    wsel = lambda i, be, nv: (be[i], 0, 0)
    return pl.pallas_call(
        _ffn_body,
        grid_spec=pltpu.PrefetchScalarGridSpec(
            num_scalar_prefetch=2,
            grid=(p // bm,),
            in_specs=[pl.BlockSpec((bm, D_MODEL // 2), rows),
                      pl.BlockSpec((1, D_MODEL, 2 * D_FF), wsel),
                      pl.BlockSpec((1, 1, 2 * D_FF), wsel),
                      pl.BlockSpec((1, D_FF, D_MODEL), wsel),
                      pl.BlockSpec((1, 1, D_MODEL), wsel)],
            out_specs=pl.BlockSpec((bm, D_MODEL), lambda i, be, nv: (i, 0)),
            scratch_shapes=[pltpu.VMEM((D_MODEL, 2 * D_FF), BF16), pltpu.VMEM((D_FF, D_MODEL), BF16)],
        ),
        out_shape=jax.ShapeDtypeStruct((p, D_MODEL), F32),
        compiler_params=pltpu.CompilerParams(dimension_semantics=("arbitrary",),
                                             vmem_limit_bytes=VMEM_LIMIT),
        name="ffn",
    )(blk_e, nvalid, xin, wgu, bgu, wd, bd)


def _combine_body(dest_ref, h_ref, gate_ref, fg_ref, y_ref, o_ref, ybuf_ref, sem, *, tm):
    def start(i, carry):
        for u in range(DMA_UNROLL):
            t = i * DMA_UNROLL + u
            for kk in range(TOP_K):
                pltpu.make_async_copy(y_ref.at[pl.ds(dest_ref[t * TOP_K + kk], 1)],
                                      ybuf_ref.at[kk, pl.ds(t, 1)], sem).start(priority=kk % 2)
        return carry

    lax.fori_loop(0, tm // DMA_UNROLL, start, 0)
    for kk in range(TOP_K):
        pltpu.make_async_copy(y_ref.at[pl.ds(0, tm)], ybuf_ref.at[kk], sem).wait()
    gates = gate_ref[...]
    out = h_ref[...]
    for kk in range(TOP_K):
        out = out + gates[:, kk:kk + 1] * ybuf_ref[kk]
    o_ref[...] = _rms(out, fg_ref[...])


def _combine(dest_flat, h, gates, fg, y):
    t = h.shape[0]
    tm = COMBINE_TILE
    return pl.pallas_call(
        functools.partial(_combine_body, tm=tm),
        grid=(t // tm,),
        in_specs=[pl.BlockSpec((tm * TOP_K,), lambda i: (i,), memory_space=pltpu.SMEM),
                  pl.BlockSpec((tm, D_MODEL), lambda i: (i, 0)),
                  pl.BlockSpec((tm, LANES), lambda i: (i, 0)),
                  pl.BlockSpec((1, D_MODEL), lambda i: (0, 0)),
                  pl.BlockSpec(memory_space=pl.ANY)],
        out_specs=pl.BlockSpec((tm, D_MODEL), lambda i: (i, 0)),
        scratch_shapes=[pltpu.VMEM((TOP_K, tm, D_MODEL), F32), pltpu.SemaphoreType.DMA(())],
        out_shape=jax.ShapeDtypeStruct((t, D_MODEL), F32),
        compiler_params=pltpu.CompilerParams(dimension_semantics=("arbitrary",),
                                             vmem_limit_bytes=VMEM_LIMIT),
        name="combine",
    )(dest_flat, h, gates, fg, y)


def _block_tril(n, chunk, strict):
    r = jnp.arange(n)[:, None]
    c = jnp.arange(n)[None, :]
    keep = ((r // chunk) == (c // chunk)) & ((r > c) if strict else (r >= c))
    return keep.astype(BF16)


def _pad_lanes(a, fill=0.0):
    a = a.reshape(1, -1).astype(F32)
    return jnp.pad(a, ((0, 0), (0, LANES - a.shape[1])), constant_values=fill)


def _layer(h, norm_mix_g, w_in, gm_norm_g, gm_ws, gm_bs, dn_conv_w, dn_a_log, dn_dt_bias,
           dn_norm_g, w_out, norm_ffn_g, router_w, router_b, exp_w_gu, exp_b_gu, exp_w_down,
           exp_b_down, out_g):
    batch, seq, d = h.shape
    t = batch * seq
    x2 = h.reshape(t, d)

    c0, c1, c2 = GM_WIDTH, 2 * GM_WIDTH, 2 * GM_WIDTH + 3 * DN_KEY
    c3 = c2 + DN_VAL
    wb = w_in.astype(BF16)
    wu, wv, wqkv, wz = wb[:, :c0], wb[:, c0:c1], wb[:, c1:c2], wb[:, c2:c3]
    wab = jnp.pad(wb[:, c3:], ((0, 0), (0, LANES - 2 * DN_HEADS)))
    gmg = gm_norm_g.reshape(1, GM_WIDTH).astype(F32)
    bsb = jnp.repeat(gm_bs.T, HEAD_DIM, axis=1).astype(F32)
    alog = _pad_lanes(dn_a_log)
    dtb = _pad_lanes(dn_dt_bias)
    tri_incl = _block_tril(ROW_TILE, DN_CHUNK, strict=False)

    oa, q, k, v, z, gb, gt = _inproj(
        x2, norm_mix_g.reshape(1, d), wu, wv, wqkv, wz, wab, gmg, gm_ws.astype(F32), bsb,
        dn_conv_w.astype(F32), alog, dtb, tri_incl, seq=seq)

    grow = gt[:DN_HEADS].reshape(DN_HEADS, t // DN_CHUNK, DN_CHUNK).transpose(1, 0, 2)
    grow = grow.reshape(t // DN_CHUNK, DN_HEADS * DN_CHUNK)
    ob = _gdn(q, k, v, z, gb, grow, dn_norm_g.reshape(1, HEAD_DIM).astype(F32), batch=batch, seq=seq)

    wo = w_out.astype(BF16)
    rw = jnp.pad(router_w.astype(BF16), ((0, 0), (0, LANES - N_EXPERTS)))
    rb = _pad_lanes(router_b, fill=NEG_BIG)
    tri_strict = _block_tril(ROW_TILE, ROW_TILE, strict=True)
    hres, hp, meta, gates, cnt = _outproj(oa, ob, x2, wo[:GM_WIDTH], wo[GM_WIDTH:],
                                          norm_ffn_g.reshape(1, d), rw, rb, tri_strict)

    bm = FFN_BLOCK
    counts = cnt[0, :N_EXPERTS]
    padded = (counts + bm - 1) // bm * bm
    pad_end = jnp.cumsum(padded)
    pad_start = (pad_end - padded).astype(I32)
    n_blocks = (t * TOP_K + N_EXPERTS * bm) // bm
    nvalid = (pad_end[-1] // bm).astype(I32).reshape(1)
    blk = jnp.minimum(jnp.arange(n_blocks, dtype=I32), nvalid[0] - 1)
    blk_e = jnp.minimum(jnp.sum(pad_end[None, :] <= (blk * bm)[:, None], axis=1), N_EXPERTS - 1).astype(I32)
    eid = meta[:, :TOP_K]
    start_of = jnp.sum(jnp.where(eid[..., None] == jnp.arange(N_EXPERTS, dtype=I32), pad_start, 0), axis=-1)
    dest_flat = (start_of + meta[:, TOP_K:2 * TOP_K]).reshape(-1)

    xin = _dispatch(pad_end.astype(I32), dest_flat, hp, n_blocks * bm)
    y = _ffn(blk_e, nvalid, xin, exp_w_gu, exp_b_gu[:, None, :].astype(F32),
             exp_w_down, exp_b_down[:, None, :].astype(F32))
    out = _combine(dest_flat, hres, gates, out_g.reshape(1, d).astype(F32), y)
    return out.reshape(batch, seq, d)


def kernel(x, norm_mix_g, w_in, gm_norm_g, gm_ws, gm_bs, dn_conv_w, dn_a_log, dn_dt_bias, dn_norm_g, w_out, norm_ffn_g, router_w, router_b, exp_w_gu, exp_b_gu, exp_w_down, exp_b_down, final_norm_g):
    depth = norm_mix_g.shape[0]
    assert depth == 1, "single-layer problem"
    return _layer(x, norm_mix_g[0], w_in[0], gm_norm_g[0], gm_ws[0], gm_bs[0], dn_conv_w[0],
                  dn_a_log[0], dn_dt_bias[0], dn_norm_g[0], w_out[0], norm_ffn_g[0], router_w[0],
                  router_b[0], exp_w_gu[0], exp_b_gu[0], exp_w_down[0], exp_b_down[0], final_norm_g)
```

```python
import functools

import jax
import jax.numpy as jnp
from jax import lax
from jax.experimental import pallas as pl
from jax.experimental.pallas import tpu as pltpu
from jax.experimental.pallas import tpu_sc as plsc

F32 = jnp.float32
BF16 = jnp.bfloat16
I32 = jnp.int32
U32 = jnp.uint32

D_MODEL = 1024
HEAD_DIM = 128
GM_HEADS = 4
GM_WIDTH = GM_HEADS * HEAD_DIM
GM_CHUNK = 128
DN_HEADS = 4
DN_DK = 128
DN_KEY = DN_HEADS * DN_DK
DN_VAL = DN_HEADS * HEAD_DIM
DN_CHUNK = 64
CONV_W = 4
N_EXPERTS = 32
TOP_K = 4
D_FF = D_MODEL
SWIGLU_LIMIT = 7.0
SWIGLU_ALPHA = 1.702
EPS = 1e-6

LANES = 128
ROW_TILE = 512
FFN_BLOCK = 512
SC_GATHER_ROWS = 64
COMBINE_TILE = 256
VMEM_LIMIT = 56 * 1024 * 1024
NEG_BIG = -1e30
DMA_UNROLL = 4
GDN_GROUP = 4


def _dot(a, b):
    return jnp.dot(a, b, preferred_element_type=F32)


def _dot_nt(a, b):
    return lax.dot_general(a, b, (((1,), (1,)), ((), ())), preferred_element_type=F32)


def _dot_tn(a, b):
    return lax.dot_general(a, b, (((0,), (0,)), ((), ())), preferred_element_type=F32)


def _rms(x, g):
    return x * lax.rsqrt(jnp.mean(x * x, axis=-1, keepdims=True) + EPS) * g


def _gelu(x):
    return 0.5 * x * (1.0 + lax.erf(x * (2.0 ** -0.5)))


def _sigmoid(x):
    return 1.0 / (1.0 + jnp.exp(-x))


def _softplus(x):
    return jnp.maximum(x, 0.0) + jnp.log1p(jnp.exp(-jnp.abs(x)))


def _inproj_body(x_ref, ng_ref, wu_ref, wv_ref, wqkv_ref, wz_ref, wab_ref, gmg_ref, ws_ref,
                 bs_ref, cw_ref, alog_ref, dtb_ref, tri_ref,
                 oa_ref, q_ref, k_ref, v_ref, z_ref, gb_ref, gt_ref, cbuf_ref,
                 *, tm, tiles_per_seq):
    i = pl.program_id(0)
    y = _rms(x_ref[...], ng_ref[...]).astype(BF16)

    u = _gelu(_dot(y, wu_ref[...]))
    vg = _gelu(_dot(y, wv_ref[...]))
    ri = lax.broadcasted_iota(I32, (GM_CHUNK, GM_CHUNK), 0)
    ci = lax.broadcasted_iota(I32, (GM_CHUNK, GM_CHUNK), 1)
    causal = ri >= ci
    for h in range(GM_HEADS):
        sl = slice(h * HEAD_DIM, (h + 1) * HEAD_DIM)
        vh = _rms(vg[:, sl], gmg_ref[:, sl]).astype(BF16)
        wm = jnp.where(causal, ws_ref[h], 0.0).astype(BF16)
        for c in range(tm // GM_CHUNK):
            rs = slice(c * GM_CHUNK, (c + 1) * GM_CHUNK)
            gate = _dot(wm, vh[rs]) + bs_ref[:, sl]
            oa_ref[rs, sl] = (u[rs, sl] * gate).astype(BF16)

    pq = _dot(y, wqkv_ref[...])

    @pl.when(i % tiles_per_seq == 0)
    def _():
        cbuf_ref[0:8, :] = jnp.zeros((8, 3 * DN_KEY), F32)

    cbuf_ref[8:8 + tm, :] = pq
    acc = cw_ref[CONV_W - 1:CONV_W, :] * pq
    for j in range(CONV_W - 1):
        off = 8 - (CONV_W - 1) + j
        acc = acc + cw_ref[j:j + 1, :] * cbuf_ref[off:off + tm, :]
    cbuf_ref[0:8, :] = pq[tm - 8:tm, :]
    s = acc * _sigmoid(acc)
    for h in range(DN_HEADS):
        sl = slice(h * DN_DK, (h + 1) * DN_DK)
        qh = s[:, sl]
        q_ref[:, sl] = (qh * lax.rsqrt(jnp.sum(qh * qh, axis=-1, keepdims=True) + EPS)
                        * (DN_DK ** -0.5)).astype(BF16)
        kh = s[:, DN_KEY + h * DN_DK:DN_KEY + (h + 1) * DN_DK]
        k_ref[:, sl] = (kh * lax.rsqrt(jnp.sum(kh * kh, axis=-1, keepdims=True) + EPS)).astype(BF16)
    v_ref[...] = s[:, 2 * DN_KEY:].astype(BF16)
    z_ref[...] = _dot(y, wz_ref[...]).astype(BF16)

    ab = _dot(y, wab_ref[...])
    g = -jnp.exp(alog_ref[...]) * _softplus(ab + dtb_ref[...])
    beta = _sigmoid(ab)
    g_hi = g.astype(BF16)
    r1 = g - g_hi.astype(F32)
    g_mid = r1.astype(BF16)
    g_lo = (r1 - g_mid.astype(F32)).astype(BF16)
    tri = tri_ref[...]
    gc = _dot(tri, g_hi) + _dot(tri, g_mid) + _dot(tri, g_lo)
    lane = lax.broadcasted_iota(I32, (tm, LANES), 1)
    gbv = jnp.where(lane < DN_HEADS, gc, beta)
    gb_ref[...] = gbv
    gt_ref[...] = gbv.T[0:8, :]


def _inproj(x2, ng, wu, wv, wqkv, wz, wab, gmg, ws, bsb, cw, alog, dtb, tri, *, seq):
    t = x2.shape[0]
    tm = ROW_TILE
    const2 = lambda i: (0, 0)
    row = lambda i: (i, 0)
    full = lambda a: pl.BlockSpec(a.shape, (lambda i: (0,) * a.ndim))
    out_shapes = (
        jax.ShapeDtypeStruct((t, GM_WIDTH), BF16),
        jax.ShapeDtypeStruct((t, DN_KEY), BF16),
        jax.ShapeDtypeStruct((t, DN_KEY), BF16),
        jax.ShapeDtypeStruct((t, DN_VAL), BF16),
        jax.ShapeDtypeStruct((t, DN_VAL), BF16),
        jax.ShapeDtypeStruct((t, LANES), F32),
        jax.ShapeDtypeStruct((8, t), F32),
    )
    return pl.pallas_call(
        functools.partial(_inproj_body, tm=tm, tiles_per_seq=seq // tm),
        grid=(t // tm,),
        in_specs=[pl.BlockSpec((tm, D_MODEL), row), full(ng), full(wu), full(wv), full(wqkv),
                  full(wz), full(wab), full(gmg), full(ws), full(bsb), full(cw), full(alog),
                  full(dtb), full(tri)],
        out_specs=(pl.BlockSpec((tm, GM_WIDTH), row), pl.BlockSpec((tm, DN_KEY), row),
                   pl.BlockSpec((tm, DN_KEY), row), pl.BlockSpec((tm, DN_VAL), row),
                   pl.BlockSpec((tm, DN_VAL), row), pl.BlockSpec((tm, LANES), row),
                   pl.BlockSpec((8, tm), lambda i: (0, i))),
        out_shape=out_shapes,
        scratch_shapes=[pltpu.VMEM((tm + 8, 3 * DN_KEY), F32)],
        compiler_params=pltpu.CompilerParams(dimension_semantics=("arbitrary",),
                                             vmem_limit_bytes=VMEM_LIMIT),
        name="inproj",
    )(x2, ng, wu, wv, wqkv, wz, wab, gmg, ws, bsb, cw, alog, dtb, tri)


def _gdn_body(q_ref, k_ref, v_ref, z_ref, gb_ref, gr_ref, ng_ref, ob_ref, s_ref, *, nchunk,
              group_size):
    j = pl.program_id(1)

    @pl.when(j == 0)
    def _():
        s_ref[...] = jnp.zeros(s_ref.shape, F32)

    c = DN_CHUNK
    n = DN_HEADS * c
    ri = lax.broadcasted_iota(I32, (n, n), 0)
    ci = lax.broadcasted_iota(I32, (n, n), 1)
    same = (ri // c) == (ci // c)
    incl = same & ((ri % c) >= (ci % c))
    strict = same & ((ri % c) > (ci % c))
    ng = ng_ref[...]

    def stack(a):
        return jnp.concatenate([a[:, h * HEAD_DIM:(h + 1) * HEAD_DIM] for h in range(DN_HEADS)], axis=0)

    def prepare(ic):
        r0 = pl.multiple_of(ic * c, c)
        kst = stack(k_ref[pl.ds(r0, c), :])
        qst = stack(q_ref[pl.ds(r0, c), :])
        vst = stack(v_ref[pl.ds(r0, c), :])
        gbc = gb_ref[pl.ds(r0, c), :]
        grow = gr_ref[pl.ds(ic, 1), :]
        gcol = jnp.concatenate([gbc[:, h:h + 1] for h in range(DN_HEADS)], axis=0)
        bcol = jnp.concatenate([gbc[:, DN_HEADS + h:DN_HEADS + h + 1] for h in range(DN_HEADS)], axis=0)
        glast = jnp.concatenate(
            [jnp.broadcast_to(gbc[c - 1:c, h:h + 1], (c, 1)) for h in range(DN_HEADS)], axis=0)
        decay = jnp.where(incl, jnp.exp(jnp.where(incl, gcol - grow, 0.0)), 0.0)
        kf = kst.astype(F32)
        kb = kf * bcol
        lmat = jnp.where(strict, _dot_nt(kb.astype(BF16), kst) * decay, 0.0)
        eg = jnp.exp(gcol)
        rhs = jnp.concatenate([vst.astype(F32) * bcol, kb * eg], axis=1)
        attn = jnp.where(incl, _dot_nt(qst, kst) * decay, 0.0).astype(BF16)
        qd = (qst.astype(F32) * eg).astype(BF16)
        ke = (kf * jnp.exp(glast - gcol)).astype(BF16)
        return dict(r0=r0, gbc=gbc, lmat=lmat, rhs=rhs, attn=attn, qd=qd, ke=ke)

    def advance_state(p, tinv):
        sol = p["rhs"] + _dot(tinv.astype(BF16), p["rhs"].astype(BF16))
        u = sol[:, :HEAD_DIM]
        wb = sol[:, HEAD_DIM:].astype(BF16)
        vn, qs = [], []
        for h in range(DN_HEADS):
            rs = slice(h * c, (h + 1) * c)
            sb = s_ref[h].astype(BF16)
            vn.append(u[rs] - _dot(wb[rs], sb))
            qs.append(_dot(p["qd"][rs], sb))
        vnb = jnp.concatenate(vn, axis=0).astype(BF16)
        o = jnp.concatenate(qs, axis=0) + _dot(p["attn"], vnb)
        zc = z_ref[pl.ds(p["r0"], c), :]
        for h in range(DN_HEADS):
            rs = slice(h * c, (h + 1) * c)
            sl = slice(h * HEAD_DIM, (h + 1) * HEAD_DIM)
            s_ref[h] = (s_ref[h] * jnp.exp(p["gbc"][c - 1:c, h:h + 1])
                        + _dot_tn(p["ke"][rs], vnb[rs]))
            zf = zc[:, sl].astype(F32)
            ob_ref[pl.ds(p["r0"], c), sl] = (_rms(o[rs], ng) * (zf * _sigmoid(zf))).astype(BF16)

    def group(ig, carry):
        ps = [prepare(ig * group_size + b) for b in range(group_size)]
        tinv = [-p["lmat"] for p in ps]
        pw = [p["lmat"] for p in ps]
        for _ in range(5):
            pb = [x.astype(BF16) for x in pw]
            pw = [_dot(x, x) for x in pb]
            pwb = [x.astype(BF16) for x in pw]
            tinv = [t + x + _dot(t.astype(BF16), xb) for t, x, xb in zip(tinv, pw, pwb)]
        for p, t in zip(ps, tinv):
            advance_state(p, t)
        return carry

    lax.fori_loop(0, nchunk // group_size, group, 0)


def _gdn(q, k, v, z, gb, grow, ng, *, batch, seq):
    tm = ROW_TILE
    nchunk = tm // DN_CHUNK
    steps = seq // tm
    rows = lambda b, j: (b * steps + j, 0)
    return pl.pallas_call(
        functools.partial(_gdn_body, nchunk=nchunk, group_size=GDN_GROUP),
        grid=(batch, steps),
        in_specs=[pl.BlockSpec((tm, DN_KEY), rows), pl.BlockSpec((tm, DN_KEY), rows),
                  pl.BlockSpec((tm, DN_VAL), rows), pl.BlockSpec((tm, DN_VAL), rows),
                  pl.BlockSpec((tm, LANES), rows),
                  pl.BlockSpec((nchunk, DN_HEADS * DN_CHUNK), rows),
                  pl.BlockSpec((1, HEAD_DIM), lambda b, j: (0, 0))],
        out_specs=pl.BlockSpec((tm, DN_VAL), rows),
        out_shape=jax.ShapeDtypeStruct((batch * seq, DN_VAL), BF16),
        scratch_shapes=[pltpu.VMEM((DN_HEADS, DN_DK, HEAD_DIM), F32)],
        compiler_params=pltpu.CompilerParams(dimension_semantics=("arbitrary", "arbitrary"),
                                             vmem_limit_bytes=VMEM_LIMIT),
        name="gdn",
    )(q, k, v, z, gb, grow, ng)


def _outproj_body(oa_ref, ob_ref, x_ref, woa_ref, wob_ref, ng_ref, rw_ref, rb_ref, tri_ref,
                  h_ref, hp_ref, meta_ref, gate_ref, cnt_ref, run_ref, *, tm):
    i = pl.program_id(0)

    @pl.when(i == 0)
    def _():
        run_ref[...] = jnp.zeros(run_ref.shape, F32)

    h = x_ref[...] + _dot(oa_ref[...], woa_ref[...]) + _dot(ob_ref[...], wob_ref[...])
    h_ref[...] = h
    hb = _rms(h, ng_ref[...]).astype(BF16)
    half = D_MODEL // 2
    lo = pltpu.bitcast(hb[:, :half].astype(F32), U32) >> 16
    hi = pltpu.bitcast(hb[:, half:].astype(F32), U32) & jnp.uint32(0xFFFF0000)
    hp_ref[...] = lo | hi

    logits = _dot(hb, rw_ref[...]) + rb_ref[...]
    lane = lax.broadcasted_iota(I32, (tm, LANES), 1)
    lanef = lane.astype(F32)
    work = logits
    onehot = jnp.zeros((tm, LANES), F32)
    vals, sels = [], []
    for _ in range(TOP_K):
        m = jnp.max(work, axis=-1, keepdims=True)
        idx = jnp.min(jnp.where(work == m, lanef, float(LANES)), axis=-1, keepdims=True)
        sel = lanef == idx
        work = jnp.where(sel, -3e38, work)
        onehot = onehot + jnp.where(sel, 1.0, 0.0)
        vals.append(m)
        sels.append((sel, idx))
    ex = [jnp.exp(v - vals[0]) for v in vals]
    den = ex[0] + ex[1] + ex[2] + ex[3]
    pref = _dot(tri_ref[...], onehot.astype(BF16)) + run_ref[0:1, :]
    meta = jnp.zeros((tm, LANES), F32)
    gates = jnp.zeros((tm, LANES), F32)
    for kk in range(TOP_K):
        sel, idx = sels[kk]
        rank = jnp.sum(jnp.where(sel, pref, 0.0), axis=-1, keepdims=True)
        meta = meta + jnp.where(lane == kk, idx, 0.0) + jnp.where(lane == TOP_K + kk, rank, 0.0)
        gates = gates + jnp.where(lane == kk, ex[kk] / den, 0.0)
    meta_ref[...] = meta.astype(I32)
    gate_ref[...] = gates
    run = run_ref[...] + jnp.sum(onehot, axis=0, keepdims=True)
    run_ref[...] = run
    cnt_ref[...] = run.astype(I32)


def _outproj(oa, ob, x2, woa, wob, ng, rw, rb, tri):
    t = x2.shape[0]
    tm = ROW_TILE
    row = lambda i: (i, 0)
    full = lambda a: pl.BlockSpec(a.shape, (lambda i: (0,) * a.ndim))
    out_shapes = (
        jax.ShapeDtypeStruct((t, D_MODEL), F32),
        jax.ShapeDtypeStruct((t, D_MODEL // 2), U32),
        jax.ShapeDtypeStruct((t, LANES), I32),
        jax.ShapeDtypeStruct((t, LANES), F32),
        jax.ShapeDtypeStruct((8, LANES), I32),
    )
    return pl.pallas_call(
        functools.partial(_outproj_body, tm=tm),
        grid=(t // tm,),
        in_specs=[pl.BlockSpec((tm, GM_WIDTH), row), pl.BlockSpec((tm, DN_VAL), row),
                  pl.BlockSpec((tm, D_MODEL), row), full(woa), full(wob), full(ng), full(rw),
                  full(rb), full(tri)],
        out_specs=(pl.BlockSpec((tm, D_MODEL), row), pl.BlockSpec((tm, D_MODEL // 2), row),
                   pl.BlockSpec((tm, LANES), row), pl.BlockSpec((tm, LANES), row),
                   pl.BlockSpec((8, LANES), lambda i: (0, 0))),
        out_shape=out_shapes,
        scratch_shapes=[pltpu.VMEM((8, LANES), F32)],
        compiler_params=pltpu.CompilerParams(dimension_semantics=("arbitrary",),
                                             vmem_limit_bytes=VMEM_LIMIT),
        name="outproj",
    )(oa, ob, x2, woa, wob, ng, rw, rb, tri)


def _dispatch_body(pe_ref, dest_ref, hp_ref, xin_ref, zero_ref, sem, zsem, *, tm, bm):
    @pl.when(pl.program_id(0) == 0)
    def _():
        zero_ref[...] = jnp.zeros(zero_ref.shape, U32)

        def fill(e, carry):
            end = pe_ref[e]
            prev = jnp.where(e == 0, 0, pe_ref[jnp.maximum(e - 1, 0)])

            @pl.when(end > prev)
            def _():
                cp = pltpu.make_async_copy(
                    zero_ref, xin_ref.at[pl.ds(pl.multiple_of(end - bm, bm), bm)], zsem)
                cp.start()
                cp.wait()
            return carry

        lax.fori_loop(0, N_EXPERTS, fill, 0)

        def fill_unused(b, carry):
            cp = pltpu.make_async_copy(
                zero_ref, xin_ref.at[pl.ds(pl.multiple_of(b * bm, bm), bm)], zsem)
            cp.start()
            cp.wait()
            return carry

        lax.fori_loop(pe_ref[N_EXPERTS - 1] // bm, xin_ref.shape[0] // bm, fill_unused, 0)

    def start(i, carry):
        for u in range(DMA_UNROLL):
            t = i * DMA_UNROLL + u
            for kk in range(TOP_K):
                pltpu.make_async_copy(hp_ref.at[pl.ds(t, 1)],
                                      xin_ref.at[pl.ds(dest_ref[t * TOP_K + kk], 1)],
                                      sem).start(priority=kk % 2)
        return carry

    lax.fori_loop(0, tm // DMA_UNROLL, start, 0)
    for _ in range(TOP_K):
        pltpu.make_async_copy(hp_ref, xin_ref.at[pl.ds(0, tm)], sem).wait()


def _dispatch(pad_end, dest_flat, hp, n_rows):
    t = hp.shape[0]
    tm = ROW_TILE
    return pl.pallas_call(
        functools.partial(_dispatch_body, tm=tm, bm=FFN_BLOCK),
        grid_spec=pltpu.PrefetchScalarGridSpec(
            num_scalar_prefetch=1,
            grid=(t // tm,),
            in_specs=[pl.BlockSpec((tm * TOP_K,), lambda i, pe: (i,), memory_space=pltpu.SMEM),
                      pl.BlockSpec((tm, D_MODEL // 2), lambda i, pe: (i, 0))],
            out_specs=pl.BlockSpec(memory_space=pl.ANY),
            scratch_shapes=[pltpu.VMEM((FFN_BLOCK, D_MODEL // 2), U32),
                            pltpu.SemaphoreType.DMA(()), pltpu.SemaphoreType.DMA(())],
        ),
        out_shape=jax.ShapeDtypeStruct((n_rows, D_MODEL // 2), U32),
        compiler_params=pltpu.CompilerParams(dimension_semantics=("arbitrary",)),
        name="dispatch",
    )(pad_end, dest_flat, hp)


def _ffn_body(be_ref, nv_ref, x_ref, wgu_ref, bgu_ref, wd_ref, bd_ref, y_ref, wgub_ref, wdb_ref):
    i = pl.program_id(0)

    @pl.when(i >= nv_ref[0])
    def _():
        y_ref[...] = jnp.zeros(y_ref.shape, U32)

    @pl.when((i < nv_ref[0]) & ((i == 0) | (be_ref[i] != be_ref[jnp.maximum(i - 1, 0)])))
    def _():
        wgub_ref[...] = wgu_ref[0].astype(BF16)
        wdb_ref[...] = wd_ref[0].astype(BF16)

    @pl.when(i < nv_ref[0])
    def _():
        half = D_MODEL // 2
        xp = x_ref[...]
        lo = pltpu.bitcast(xp << 16, F32).astype(BF16)
        hi = pltpu.bitcast(xp & jnp.uint32(0xFFFF0000), F32).astype(BF16)
        gu = _dot(lo, wgub_ref[:half, :]) + _dot(hi, wgub_ref[half:, :]) + bgu_ref[0]
        gate = jnp.minimum(gu[:, :D_FF], SWIGLU_LIMIT)
        up = jnp.clip(gu[:, D_FF:], -SWIGLU_LIMIT, SWIGLU_LIMIT)
        act = (up + 1.0) * (gate * _sigmoid(SWIGLU_ALPHA * gate))
        y = _dot(act.astype(BF16), wdb_ref[...]) + bd_ref[0]
        ylo = pltpu.bitcast(y[:, :half].astype(BF16).astype(F32), U32) >> 16
        yhi = pltpu.bitcast(y[:, half:].astype(BF16).astype(F32), U32) & jnp.uint32(0xFFFF0000)
        y_ref[...] = ylo | yhi


def _ffn(blk_e, nvalid, xin, wgu, bgu, wd, bd):
    p = xin.shape[0]
    bm = FFN_BLOCK
    rows = lambda i, be, nv: (jnp.minimum(i, nv[0] - 1), 0)
    wsel = lambda i, be, nv: (be[i], 0, 0)
    return pl.pallas_call(
        _ffn_body,
        grid_spec=pltpu.PrefetchScalarGridSpec(
            num_scalar_prefetch=2,
            grid=(p // bm,),
            in_specs=[pl.BlockSpec((bm, D_MODEL // 2), rows),
                      pl.BlockSpec((1, D_MODEL, 2 * D_FF), wsel),
                      pl.BlockSpec((1, 1, 2 * D_FF), wsel),
                      pl.BlockSpec((1, D_FF, D_MODEL), wsel),
                      pl.BlockSpec((1, 1, D_MODEL), wsel)],
            out_specs=pl.BlockSpec((bm, D_MODEL // 2), lambda i, be, nv: (i, 0)),
            scratch_shapes=[pltpu.VMEM((D_MODEL, 2 * D_FF), BF16), pltpu.VMEM((D_FF, D_MODEL), BF16)],
        ),
        out_shape=jax.ShapeDtypeStruct((p, D_MODEL // 2), U32),
        compiler_params=pltpu.CompilerParams(dimension_semantics=("arbitrary",),
                                             vmem_limit_bytes=VMEM_LIMIT),
        name="ffn",
    )(blk_e, nvalid, xin, wgu, bgu, wd, bd)


def _sc_gather(table, idx):
    r, d = idx.shape[0], table.shape[1]
    mesh = plsc.VectorSubcoreMesh(core_axis_name="c", subcore_axis_name="s")
    nc, workers = mesh.num_cores, mesh.num_cores * mesh.num_subcores
    chunk = SC_GATHER_ROWS
    per_w = r // workers
    pairs = per_w // (2 * chunk)
    assert per_w * workers == r and pairs * 2 * chunk == per_w

    @functools.partial(
        pl.kernel, mesh=mesh, out_type=jax.ShapeDtypeStruct((r, d), table.dtype),
        scratch_types=[pltpu.VMEM((2, chunk), I32), pltpu.VMEM((2, chunk, d), table.dtype),
                       pltpu.SemaphoreType.DMA((2,)), pltpu.SemaphoreType.DMA((2,))],
        name="sc_gather")
    def gather(table_hbm, idx_hbm, out_hbm, idx_v, rows_v, gsem, wsem):
        base_w = (lax.axis_index("s") * nc + lax.axis_index("c")) * per_w

        def fetch(b):
            return pltpu.make_async_copy(table_hbm.at[idx_v.at[b]], rows_v.at[b], gsem.at[b])

        def flush(b, base):
            return pltpu.make_async_copy(rows_v.at[b], out_hbm.at[pl.ds(base, chunk)], wsem.at[b])

        def body(j, carry):
            for b in range(2):
                base = base_w + (2 * j + b) * chunk

                @pl.when(j > 0)
                def _():
                    flush(b, base).wait()

                pltpu.sync_copy(idx_hbm.at[pl.ds(base, chunk)], idx_v.at[b])
                fetch(b).start()
            for b in range(2):
                fetch(b).wait()
                flush(b, base_w + (2 * j + b) * chunk).start()
            return carry

        lax.fori_loop(0, pairs, body, 0)
        for b in range(2):
            flush(b, base_w).wait()

    return gather(table, idx)


def _combine_body(h_ref, gate_ref, fg_ref, yg_ref, o_ref):
    half = D_MODEL // 2
    gates = gate_ref[...]
    h = h_ref[...]
    lo, hi = h[:, :half], h[:, half:]
    for kk in range(TOP_K):
        yp = yg_ref[:, kk * half:(kk + 1) * half]
        g = gates[:, kk:kk + 1]
        lo = lo + g * pltpu.bitcast(yp << 16, F32)
        hi = hi + g * pltpu.bitcast(yp & jnp.uint32(0xFFFF0000), F32)
    out = jnp.concatenate([lo, hi], axis=1)
    o_ref[...] = _rms(out, fg_ref[...])


def _combine(h, gates, fg, ygath):
    t = h.shape[0]
    tm = COMBINE_TILE
    row = lambda i: (i, 0)
    return pl.pallas_call(
        _combine_body,
        grid=(t // tm,),
        in_specs=[pl.BlockSpec((tm, D_MODEL), row), pl.BlockSpec((tm, LANES), row),
                  pl.BlockSpec((1, D_MODEL), lambda i: (0, 0)),
                  pl.BlockSpec((tm, TOP_K * D_MODEL // 2), row)],
        out_specs=pl.BlockSpec((tm, D_MODEL), row),
        out_shape=jax.ShapeDtypeStruct((t, D_MODEL), F32),
        compiler_params=pltpu.CompilerParams(dimension_semantics=("arbitrary",),
                                             vmem_limit_bytes=VMEM_LIMIT),
        name="combine",
    )(h, gates, fg, ygath)


def _block_tril(n, chunk, strict):
    r = jnp.arange(n)[:, None]
    c = jnp.arange(n)[None, :]
    keep = ((r // chunk) == (c // chunk)) & ((r > c) if strict else (r >= c))
    return keep.astype(BF16)


def _pad_lanes(a, fill=0.0):
    a = a.reshape(1, -1).astype(F32)
    return jnp.pad(a, ((0, 0), (0, LANES - a.shape[1])), constant_values=fill)


def _layer(h, norm_mix_g, w_in, gm_norm_g, gm_ws, gm_bs, dn_conv_w, dn_a_log, dn_dt_bias,
           dn_norm_g, w_out, norm_ffn_g, router_w, router_b, exp_w_gu, exp_b_gu, exp_w_down,
           exp_b_down, out_g):
    batch, seq, d = h.shape
    t = batch * seq
    x2 = h.reshape(t, d)

    c0, c1, c2 = GM_WIDTH, 2 * GM_WIDTH, 2 * GM_WIDTH + 3 * DN_KEY
    c3 = c2 + DN_VAL
    wb = w_in.astype(BF16)
    wu, wv, wqkv, wz = wb[:, :c0], wb[:, c0:c1], wb[:, c1:c2], wb[:, c2:c3]
    wab = jnp.pad(wb[:, c3:], ((0, 0), (0, LANES - 2 * DN_HEADS)))
    gmg = gm_norm_g.reshape(1, GM_WIDTH).astype(F32)
    bsb = jnp.repeat(gm_bs.T, HEAD_DIM, axis=1).astype(F32)
    alog = _pad_lanes(dn_a_log)
    dtb = _pad_lanes(dn_dt_bias)
    tri_incl = _block_tril(ROW_TILE, DN_CHUNK, strict=False)

    oa, q, k, v, z, gb, gt = _inproj(
        x2, norm_mix_g.reshape(1, d), wu, wv, wqkv, wz, wab, gmg, gm_ws.astype(F32), bsb,
        dn_conv_w.astype(F32), alog, dtb, tri_incl, seq=seq)

    grow = gt[:DN_HEADS].reshape(DN_HEADS, t // DN_CHUNK, DN_CHUNK).transpose(1, 0, 2)
    grow = grow.reshape(t // DN_CHUNK, DN_HEADS * DN_CHUNK)
    ob = _gdn(q, k, v, z, gb, grow, dn_norm_g.reshape(1, HEAD_DIM).astype(F32), batch=batch, seq=seq)

    wo = w_out.astype(BF16)
    rw = jnp.pad(router_w.astype(BF16), ((0, 0), (0, LANES - N_EXPERTS)))
    rb = _pad_lanes(router_b, fill=NEG_BIG)
    tri_strict = _block_tril(ROW_TILE, ROW_TILE, strict=True)
    hres, hp, meta, gates, cnt = _outproj(oa, ob, x2, wo[:GM_WIDTH], wo[GM_WIDTH:],
                                          norm_ffn_g.reshape(1, d), rw, rb, tri_strict)

    bm = FFN_BLOCK
    counts = cnt[0, :N_EXPERTS]
    padded = (counts + bm - 1) // bm * bm
    pad_end = jnp.cumsum(padded)
    pad_start = (pad_end - padded).astype(I32)
    n_blocks = (t * TOP_K + N_EXPERTS * bm) // bm
    nvalid = (pad_end[-1] // bm).astype(I32).reshape(1)
    blk = jnp.minimum(jnp.arange(n_blocks, dtype=I32), nvalid[0] - 1)
    blk_e = jnp.minimum(jnp.sum(pad_end[None, :] <= (blk * bm)[:, None], axis=1), N_EXPERTS - 1).astype(I32)
    eid = meta[:, :TOP_K]
    start_of = jnp.sum(jnp.where(eid[..., None] == jnp.arange(N_EXPERTS, dtype=I32), pad_start, 0), axis=-1)
    dest_flat = (start_of + meta[:, TOP_K:2 * TOP_K]).reshape(-1)

    xin = _dispatch(pad_end.astype(I32), dest_flat, hp, n_blocks * bm)
    y = _ffn(blk_e, nvalid, xin, exp_w_gu, exp_b_gu[:, None, :].astype(F32),
             exp_w_down, exp_b_down[:, None, :].astype(F32))
    ygath = _sc_gather(y, dest_flat).reshape(t, TOP_K * d // 2)
    out = _combine(hres, gates, out_g.reshape(1, d).astype(F32), ygath)
    return out.reshape(batch, seq, d)


def kernel(x, norm_mix_g, w_in, gm_norm_g, gm_ws, gm_bs, dn_conv_w, dn_a_log, dn_dt_bias, dn_norm_g, w_out, norm_ffn_g, router_w, router_b, exp_w_gu, exp_b_gu, exp_w_down, exp_b_down, final_norm_g):
    depth = norm_mix_g.shape[0]
    assert depth == 1, "single-layer problem"
    return _layer(x, norm_mix_g[0], w_in[0], gm_norm_g[0], gm_ws[0], gm_bs[0], dn_conv_w[0],
                  dn_a_log[0], dn_dt_bias[0], dn_norm_g[0], w_out[0], norm_ffn_g[0], router_w[0],
                  router_b[0], exp_w_gu[0], exp_b_gu[0], exp_w_down[0], exp_b_down[0], final_norm_g)
```

```python
import functools

import jax
import jax.numpy as jnp
from jax import lax
from jax.experimental import pallas as pl
from jax.experimental.pallas import tpu as pltpu
from jax.experimental.pallas import tpu_sc as plsc

F32 = jnp.float32
BF16 = jnp.bfloat16
I32 = jnp.int32
U32 = jnp.uint32

D_MODEL = 1024
HEAD_DIM = 128
GM_HEADS = 4
GM_WIDTH = GM_HEADS * HEAD_DIM
GM_CHUNK = 128
DN_HEADS = 4
DN_DK = 128
DN_KEY = DN_HEADS * DN_DK
DN_VAL = DN_HEADS * HEAD_DIM
DN_CHUNK = 64
CONV_W = 4
N_EXPERTS = 32
TOP_K = 4
D_FF = D_MODEL
SWIGLU_LIMIT = 7.0
SWIGLU_ALPHA = 1.702
EPS = 1e-6

LANES = 128
SUBLANES = 8
ROW_TILE = 512
FFN_BLOCK = 512
SC_GATHER_ROWS = 64
COMBINE_TILE = 256
VMEM_LIMIT = 56 * 1024 * 1024
NEG_BIG = -1e30
GDN_GROUP = 4


def _dot(a, b):
    return jnp.dot(a, b, preferred_element_type=F32)


def _dot_nt(a, b):
    return lax.dot_general(a, b, (((1,), (1,)), ((), ())), preferred_element_type=F32)


def _dot_tn(a, b):
    return lax.dot_general(a, b, (((0,), (0,)), ((), ())), preferred_element_type=F32)


def _rms(x, g):
    return x * lax.rsqrt(jnp.mean(x * x, axis=-1, keepdims=True) + EPS) * g


def _gelu(x):
    return 0.5 * x * (1.0 + lax.erf(x * (2.0 ** -0.5)))


def _sigmoid(x):
    return 1.0 / (1.0 + jnp.exp(-x))


def _softplus(x):
    return jnp.maximum(x, 0.0) + jnp.log1p(jnp.exp(-jnp.abs(x)))


def _inproj_body(x_ref, ng_ref, wu_ref, wv_ref, wqkv_ref, wz_ref, wab_ref, gmg_ref, ws_ref,
                 bs_ref, cw_ref, alog_ref, dtb_ref, tri_ref,
                 oa_ref, q_ref, k_ref, v_ref, z_ref, gb_ref, gt_ref, cbuf_ref,
                 *, tm, tiles_per_seq):
    i = pl.program_id(0)
    y = _rms(x_ref[...], ng_ref[...]).astype(BF16)

    u = _gelu(_dot(y, wu_ref[...]))
    vg = _gelu(_dot(y, wv_ref[...]))
    ri = lax.broadcasted_iota(I32, (GM_CHUNK, GM_CHUNK), 0)
    ci = lax.broadcasted_iota(I32, (GM_CHUNK, GM_CHUNK), 1)
    causal = ri >= ci
    for h in range(GM_HEADS):
        sl = slice(h * HEAD_DIM, (h + 1) * HEAD_DIM)
        vh = _rms(vg[:, sl], gmg_ref[:, sl]).astype(BF16)
        wm = jnp.where(causal, ws_ref[h], 0.0).astype(BF16)
        for c in range(tm // GM_CHUNK):
            rs = slice(c * GM_CHUNK, (c + 1) * GM_CHUNK)
            gate = _dot(wm, vh[rs]) + bs_ref[:, sl]
            oa_ref[rs, sl] = (u[rs, sl] * gate).astype(BF16)

    pq = _dot(y, wqkv_ref[...])

    @pl.when(i % tiles_per_seq == 0)
    def _():
        cbuf_ref[0:8, :] = jnp.zeros((8, 3 * DN_KEY), F32)

    cbuf_ref[8:8 + tm, :] = pq
    acc = cw_ref[CONV_W - 1:CONV_W, :] * pq
    for j in range(CONV_W - 1):
        off = 8 - (CONV_W - 1) + j
        acc = acc + cw_ref[j:j + 1, :] * cbuf_ref[off:off + tm, :]
    cbuf_ref[0:8, :] = pq[tm - 8:tm, :]
    s = acc * _sigmoid(acc)
    for h in range(DN_HEADS):
        sl = slice(h * DN_DK, (h + 1) * DN_DK)
        qh = s[:, sl]
        q_ref[:, sl] = (qh * lax.rsqrt(jnp.sum(qh * qh, axis=-1, keepdims=True) + EPS)
                        * (DN_DK ** -0.5)).astype(BF16)
        kh = s[:, DN_KEY + h * DN_DK:DN_KEY + (h + 1) * DN_DK]
        k_ref[:, sl] = (kh * lax.rsqrt(jnp.sum(kh * kh, axis=-1, keepdims=True) + EPS)).astype(BF16)
    v_ref[...] = s[:, 2 * DN_KEY:].astype(BF16)
    z_ref[...] = _dot(y, wz_ref[...]).astype(BF16)

    ab = _dot(y, wab_ref[...])
    g = -jnp.exp(alog_ref[...]) * _softplus(ab + dtb_ref[...])
    beta = _sigmoid(ab)
    g_hi = g.astype(BF16)
    r1 = g - g_hi.astype(F32)
    g_mid = r1.astype(BF16)
    g_lo = (r1 - g_mid.astype(F32)).astype(BF16)
    tri = tri_ref[...]
    gc = _dot(tri, g_hi) + _dot(tri, g_mid) + _dot(tri, g_lo)
    lane = lax.broadcasted_iota(I32, (tm, LANES), 1)
    gbv = jnp.where(lane < DN_HEADS, gc, beta)
    gb_ref[...] = gbv
    gt_ref[...] = gbv.T[0:8, :]


def _inproj(x2, ng, wu, wv, wqkv, wz, wab, gmg, ws, bsb, cw, alog, dtb, tri, *, seq):
    t = x2.shape[0]
    tm = ROW_TILE
    const2 = lambda i: (0, 0)
    row = lambda i: (i, 0)
    full = lambda a: pl.BlockSpec(a.shape, (lambda i: (0,) * a.ndim))
    out_shapes = (
        jax.ShapeDtypeStruct((t, GM_WIDTH), BF16),
        jax.ShapeDtypeStruct((t, DN_KEY), BF16),
        jax.ShapeDtypeStruct((t, DN_KEY), BF16),
        jax.ShapeDtypeStruct((t, DN_VAL), BF16),
        jax.ShapeDtypeStruct((t, DN_VAL), BF16),
        jax.ShapeDtypeStruct((t, LANES), F32),
        jax.ShapeDtypeStruct((8, t), F32),
    )
    return pl.pallas_call(
        functools.partial(_inproj_body, tm=tm, tiles_per_seq=seq // tm),
        grid=(t // tm,),
        in_specs=[pl.BlockSpec((tm, D_MODEL), row), full(ng), full(wu), full(wv), full(wqkv),
                  full(wz), full(wab), full(gmg), full(ws), full(bsb), full(cw), full(alog),
                  full(dtb), full(tri)],
        out_specs=(pl.BlockSpec((tm, GM_WIDTH), row), pl.BlockSpec((tm, DN_KEY), row),
                   pl.BlockSpec((tm, DN_KEY), row), pl.BlockSpec((tm, DN_VAL), row),
                   pl.BlockSpec((tm, DN_VAL), row), pl.BlockSpec((tm, LANES), row),
                   pl.BlockSpec((8, tm), lambda i: (0, i))),
        out_shape=out_shapes,
        scratch_shapes=[pltpu.VMEM((tm + 8, 3 * DN_KEY), F32)],
        compiler_params=pltpu.CompilerParams(dimension_semantics=("arbitrary",),
                                             vmem_limit_bytes=VMEM_LIMIT),
        name="inproj",
    )(x2, ng, wu, wv, wqkv, wz, wab, gmg, ws, bsb, cw, alog, dtb, tri)


def _gdn_body(q_ref, k_ref, v_ref, z_ref, gb_ref, gr_ref, ng_ref, ob_ref, s_ref, *, nchunk,
              group_size):
    j = pl.program_id(1)

    @pl.when(j == 0)
    def _():
        s_ref[...] = jnp.zeros(s_ref.shape, F32)

    c = DN_CHUNK
    n = DN_HEADS * c
    ri = lax.broadcasted_iota(I32, (n, n), 0)
    ci = lax.broadcasted_iota(I32, (n, n), 1)
    same = (ri // c) == (ci // c)
    incl = same & ((ri % c) >= (ci % c))
    strict = same & ((ri % c) > (ci % c))
    ng = ng_ref[...]

    def stack(a):
        return jnp.concatenate([a[:, h * HEAD_DIM:(h + 1) * HEAD_DIM] for h in range(DN_HEADS)], axis=0)

    def prepare(ic):
        r0 = pl.multiple_of(ic * c, c)
        kst = stack(k_ref[pl.ds(r0, c), :])
        qst = stack(q_ref[pl.ds(r0, c), :])
        vst = stack(v_ref[pl.ds(r0, c), :])
        gbc = gb_ref[pl.ds(r0, c), :]
        grow = gr_ref[pl.ds(ic, 1), :]
        gcol = jnp.concatenate([gbc[:, h:h + 1] for h in range(DN_HEADS)], axis=0)
        bcol = jnp.concatenate([gbc[:, DN_HEADS + h:DN_HEADS + h + 1] for h in range(DN_HEADS)], axis=0)
        glast = jnp.concatenate(
            [jnp.broadcast_to(gbc[c - 1:c, h:h + 1], (c, 1)) for h in range(DN_HEADS)], axis=0)
        decay = jnp.where(incl, jnp.exp(jnp.where(incl, gcol - grow, 0.0)), 0.0)
        kf = kst.astype(F32)
        kb = kf * bcol
        lmat = jnp.where(strict, _dot_nt(kb.astype(BF16), kst) * decay, 0.0)
        eg = jnp.exp(gcol)
        rhs = jnp.concatenate([vst.astype(F32) * bcol, kb * eg], axis=1)
        attn = jnp.where(incl, _dot_nt(qst, kst) * decay, 0.0).astype(BF16)
        qd = (qst.astype(F32) * eg).astype(BF16)
        ke = (kf * jnp.exp(glast - gcol)).astype(BF16)
        return dict(r0=r0, gbc=gbc, lmat=lmat, rhs=rhs, attn=attn, qd=qd, ke=ke)

    def advance_state(p, tinv):
        sol = p["rhs"] + _dot(tinv.astype(BF16), p["rhs"].astype(BF16))
        u = sol[:, :HEAD_DIM]
        wb = sol[:, HEAD_DIM:].astype(BF16)
        vn, qs = [], []
        for h in range(DN_HEADS):
            rs = slice(h * c, (h + 1) * c)
            sb = s_ref[h].astype(BF16)
            vn.append(u[rs] - _dot(wb[rs], sb))
            qs.append(_dot(p["qd"][rs], sb))
        vnb = jnp.concatenate(vn, axis=0).astype(BF16)
        o = jnp.concatenate(qs, axis=0) + _dot(p["attn"], vnb)
        zc = z_ref[pl.ds(p["r0"], c), :]
        for h in range(DN_HEADS):
            rs = slice(h * c, (h + 1) * c)
            sl = slice(h * HEAD_DIM, (h + 1) * HEAD_DIM)
            s_ref[h] = (s_ref[h] * jnp.exp(p["gbc"][c - 1:c, h:h + 1])
                        + _dot_tn(p["ke"][rs], vnb[rs]))
            zf = zc[:, sl].astype(F32)
            ob_ref[pl.ds(p["r0"], c), sl] = (_rms(o[rs], ng) * (zf * _sigmoid(zf))).astype(BF16)

    def group(ig, carry):
        ps = [prepare(ig * group_size + b) for b in range(group_size)]
        tinv = [-p["lmat"] for p in ps]
        pw = [p["lmat"] for p in ps]
        for _ in range(5):
            pb = [x.astype(BF16) for x in pw]
            pw = [_dot(x, x) for x in pb]
            pwb = [x.astype(BF16) for x in pw]
            tinv = [t + x + _dot(t.astype(BF16), xb) for t, x, xb in zip(tinv, pw, pwb)]
        for p, t in zip(ps, tinv):
            advance_state(p, t)
        return carry

    lax.fori_loop(0, nchunk // group_size, group, 0)


def _gdn(q, k, v, z, gb, grow, ng, *, batch, seq):
    tm = ROW_TILE
    nchunk = tm // DN_CHUNK
    steps = seq // tm
    rows = lambda b, j: (b * steps + j, 0)
    return pl.pallas_call(
        functools.partial(_gdn_body, nchunk=nchunk, group_size=GDN_GROUP),
        grid=(batch, steps),
        in_specs=[pl.BlockSpec((tm, DN_KEY), rows), pl.BlockSpec((tm, DN_KEY), rows),
                  pl.BlockSpec((tm, DN_VAL), rows), pl.BlockSpec((tm, DN_VAL), rows),
                  pl.BlockSpec((tm, LANES), rows),
                  pl.BlockSpec((nchunk, DN_HEADS * DN_CHUNK), rows),
                  pl.BlockSpec((1, HEAD_DIM), lambda b, j: (0, 0))],
        out_specs=pl.BlockSpec((tm, DN_VAL), rows),
        out_shape=jax.ShapeDtypeStruct((batch * seq, DN_VAL), BF16),
        scratch_shapes=[pltpu.VMEM((DN_HEADS, DN_DK, HEAD_DIM), F32)],
        compiler_params=pltpu.CompilerParams(dimension_semantics=("arbitrary", "arbitrary"),
                                             vmem_limit_bytes=VMEM_LIMIT),
        name="gdn",
    )(q, k, v, z, gb, grow, ng)


def _outproj_body(oa_ref, ob_ref, x_ref, woa_ref, wob_ref, ng_ref, rw_ref, rb_ref, tri_ref,
                  h_ref, hp_ref, meta_ref, gate_ref, cnt_ref, run_ref, *, tm):
    i = pl.program_id(0)

    @pl.when(i == 0)
    def _():
        run_ref[...] = jnp.zeros(run_ref.shape, F32)

    h = x_ref[...] + _dot(oa_ref[...], woa_ref[...]) + _dot(ob_ref[...], wob_ref[...])
    h_ref[...] = h
    hb = _rms(h, ng_ref[...]).astype(BF16)
    half = D_MODEL // 2
    lo = pltpu.bitcast(hb[:, :half].astype(F32), U32) >> 16
    hi = pltpu.bitcast(hb[:, half:].astype(F32), U32) & jnp.uint32(0xFFFF0000)
    hp_ref[...] = lo | hi

    logits = _dot(hb, rw_ref[...]) + rb_ref[...]
    lane = lax.broadcasted_iota(I32, (tm, LANES), 1)
    lanef = lane.astype(F32)
    work = logits
    onehot = jnp.zeros((tm, LANES), F32)
    vals, sels = [], []
    for _ in range(TOP_K):
        m = jnp.max(work, axis=-1, keepdims=True)
        idx = jnp.min(jnp.where(work == m, lanef, float(LANES)), axis=-1, keepdims=True)
        sel = lanef == idx
        work = jnp.where(sel, -3e38, work)
        onehot = onehot + jnp.where(sel, 1.0, 0.0)
        vals.append(m)
        sels.append((sel, idx))
    ex = [jnp.exp(v - vals[0]) for v in vals]
    den = ex[0] + ex[1] + ex[2] + ex[3]
    pref = _dot(tri_ref[...], onehot.astype(BF16)) + run_ref[0:1, :]
    meta = jnp.zeros((tm, LANES), F32)
    gates = jnp.zeros((tm, LANES), F32)
    for kk in range(TOP_K):
        sel, idx = sels[kk]
        rank = jnp.sum(jnp.where(sel, pref, 0.0), axis=-1, keepdims=True)
        meta = meta + jnp.where(lane == kk, idx, 0.0) + jnp.where(lane == TOP_K + kk, rank, 0.0)
        gates = gates + jnp.where(lane == kk, ex[kk] / den, 0.0)
    meta_ref[...] = meta.astype(I32)
    gate_ref[...] = gates
    run = run_ref[...] + jnp.sum(onehot, axis=0, keepdims=True)
    run_ref[...] = run
    cnt_ref[...] = run.astype(I32)


def _outproj(oa, ob, x2, woa, wob, ng, rw, rb, tri):
    t = x2.shape[0]
    tm = ROW_TILE
    row = lambda i: (i, 0)
    full = lambda a: pl.BlockSpec(a.shape, (lambda i: (0,) * a.ndim))
    out_shapes = (
        jax.ShapeDtypeStruct((t, D_MODEL), F32),
        jax.ShapeDtypeStruct((t, D_MODEL // 2), U32),
        jax.ShapeDtypeStruct((t, LANES), I32),
        jax.ShapeDtypeStruct((t, LANES), F32),
        jax.ShapeDtypeStruct((8, LANES), I32),
    )
    return pl.pallas_call(
        functools.partial(_outproj_body, tm=tm),
        grid=(t // tm,),
        in_specs=[pl.BlockSpec((tm, GM_WIDTH), row), pl.BlockSpec((tm, DN_VAL), row),
                  pl.BlockSpec((tm, D_MODEL), row), full(woa), full(wob), full(ng), full(rw),
                  full(rb), full(tri)],
        out_specs=(pl.BlockSpec((tm, D_MODEL), row), pl.BlockSpec((tm, D_MODEL // 2), row),
                   pl.BlockSpec((tm, LANES), row), pl.BlockSpec((tm, LANES), row),
                   pl.BlockSpec((8, LANES), lambda i: (0, 0))),
        out_shape=out_shapes,
        scratch_shapes=[pltpu.VMEM((8, LANES), F32)],
        compiler_params=pltpu.CompilerParams(dimension_semantics=("arbitrary",),
                                             vmem_limit_bytes=VMEM_LIMIT),
        name="outproj",
    )(oa, ob, x2, woa, wob, ng, rw, rb, tri)


def _sc_dispatch(hp, dest_k, n_rows):
    t, d = hp.shape
    mesh = plsc.VectorSubcoreMesh(core_axis_name="c", subcore_axis_name="s")
    nc, workers = mesh.num_cores, mesh.num_cores * mesh.num_subcores
    chunk = SC_GATHER_ROWS
    per_w = t // workers
    pairs = per_w // (2 * chunk)
    assert per_w * workers == t and pairs * 2 * chunk == per_w

    @functools.partial(
        pl.kernel, mesh=mesh, out_type=jax.ShapeDtypeStruct((n_rows, d), hp.dtype),
        scratch_types=[pltpu.VMEM((2, TOP_K, chunk), I32), pltpu.VMEM((2, chunk, d), hp.dtype),
                       pltpu.SemaphoreType.DMA((2,)), pltpu.SemaphoreType.DMA((2,))],
        name="sc_dispatch")
    def scatter(hp_hbm, idx_hbm, xin_hbm, idx_v, rows_v, lsem, ssem):
        base_w = (lax.axis_index("s") * nc + lax.axis_index("c")) * per_w

        def load(b, base):
            return pltpu.make_async_copy(hp_hbm.at[pl.ds(base, chunk)], rows_v.at[b], lsem.at[b])

        def put(b, kk):
            return pltpu.make_async_copy(rows_v.at[b], xin_hbm.at[idx_v.at[b, kk]], ssem.at[b])

        def body(j, carry):
            for b in range(2):
                base = base_w + (2 * j + b) * chunk

                @pl.when(j > 0)
                def _():
                    for kk in range(TOP_K):
                        put(b, kk).wait()

                for kk in range(TOP_K):
                    pltpu.sync_copy(idx_hbm.at[pl.ds(kk * t + base, chunk)], idx_v.at[b, kk])
                load(b, base).start()
            for b in range(2):
                load(b, base_w).wait()
                for kk in range(TOP_K):
                    put(b, kk).start()
            return carry

        lax.fori_loop(0, pairs, body, 0)
        for b in range(2):
            for kk in range(TOP_K):
                put(b, kk).wait()

    return scatter(hp, dest_k)


def _padfill_body(cnt_ref, ps_ref, pe_ref, xin_in_ref, xin_ref, zero_ref, sem, *, bm):
    del xin_in_ref
    zero_ref[...] = jnp.zeros(zero_ref.shape, U32)

    def zero_rows(pos, n):
        cp = pltpu.make_async_copy(zero_ref.at[pl.ds(0, n)], xin_ref.at[pl.ds(pos, n)], sem)
        cp.start()
        cp.wait()

    def fill(e, carry):
        pos = ps_ref[e] + cnt_ref[e]
        head = (-pos) & (SUBLANES - 1)

        def one_row(r, c):
            zero_rows(pos + r, 1)
            return c

        lax.fori_loop(0, head, one_row, 0)
        pos = pos + head
        npad = pe_ref[e] - pos
        bit = bm // 2
        while bit >= SUBLANES:
            pl.when((npad & bit) != 0)(
                functools.partial(zero_rows, pl.multiple_of(pos, SUBLANES), bit))
            pos = pos + (npad & bit)
            bit //= 2
        return carry

    lax.fori_loop(0, N_EXPERTS, fill, 0)

    def fill_unused(b, carry):
        zero_rows(pl.multiple_of(b * bm, bm), bm)
        return carry

    lax.fori_loop(pe_ref[N_EXPERTS - 1] // bm, xin_ref.shape[0] // bm, fill_unused, 0)


def _padfill(counts, pad_start, pad_end, xin):
    return pl.pallas_call(
        functools.partial(_padfill_body, bm=FFN_BLOCK),
        grid_spec=pltpu.PrefetchScalarGridSpec(
            num_scalar_prefetch=3,
            grid=(1,),
            in_specs=[pl.BlockSpec(memory_space=pl.ANY)],
            out_specs=pl.BlockSpec(memory_space=pl.ANY),
            scratch_shapes=[pltpu.VMEM((FFN_BLOCK, D_MODEL // 2), U32), pltpu.SemaphoreType.DMA(())],
        ),
        out_shape=jax.ShapeDtypeStruct(xin.shape, xin.dtype),
        input_output_aliases={3: 0},
        compiler_params=pltpu.CompilerParams(dimension_semantics=("arbitrary",)),
        name="padfill",
    )(counts, pad_start, pad_end, xin)


def _ffn_body(be_ref, nv_ref, x_ref, wgu_ref, bgu_ref, wd_ref, bd_ref, y_ref, wgub_ref, wdb_ref):
    i = pl.program_id(0)

    @pl.when(i >= nv_ref[0])
    def _():
        y_ref[...] = jnp.zeros(y_ref.shape, U32)

    @pl.when((i < nv_ref[0]) & ((i == 0) | (be_ref[i] != be_ref[jnp.maximum(i - 1, 0)])))
    def _():
        wgub_ref[...] = wgu_ref[0].astype(BF16)
        wdb_ref[...] = wd_ref[0].astype(BF16)

    @pl.when(i < nv_ref[0])
    def _():
        half = D_MODEL // 2
        xp = x_ref[...]
        lo = pltpu.bitcast(xp << 16, F32).astype(BF16)
        hi = pltpu.bitcast(xp & jnp.uint32(0xFFFF0000), F32).astype(BF16)
        gu = _dot(lo, wgub_ref[:half, :]) + _dot(hi, wgub_ref[half:, :]) + bgu_ref[0]
        gate = jnp.minimum(gu[:, :D_FF], SWIGLU_LIMIT)
        up = jnp.clip(gu[:, D_FF:], -SWIGLU_LIMIT, SWIGLU_LIMIT)
        act = (up + 1.0) * (gate * _sigmoid(SWIGLU_ALPHA * gate))
        y = _dot(act.astype(BF16), wdb_ref[...]) + bd_ref[0]
        ylo = pltpu.bitcast(y[:, :half].astype(BF16).astype(F32), U32) >> 16
        yhi = pltpu.bitcast(y[:, half:].astype(BF16).astype(F32), U32) & jnp.uint32(0xFFFF0000)
        y_ref[...] = ylo | yhi


def _ffn(blk_e, nvalid, xin, wgu, bgu, wd, bd):
    p = xin.shape[0]
    bm = FFN_BLOCK
    rows = lambda i, be, nv: (jnp.minimum(i, nv[0] - 1), 0)
    wsel = lambda i, be, nv: (be[i], 0, 0)
    return pl.pallas_call(
        _ffn_body,
        grid_spec=pltpu.PrefetchScalarGridSpec(
            num_scalar_prefetch=2,
            grid=(p // bm,),
            in_specs=[pl.BlockSpec((bm, D_MODEL // 2), rows),
                      pl.BlockSpec((1, D_MODEL, 2 * D_FF), wsel),
                      pl.BlockSpec((1, 1, 2 * D_FF), wsel),
                      pl.BlockSpec((1, D_FF, D_MODEL), wsel),
                      pl.BlockSpec((1, 1, D_MODEL), wsel)],
            out_specs=pl.BlockSpec((bm, D_MODEL // 2), lambda i, be, nv: (i, 0)),
            scratch_shapes=[pltpu.VMEM((D_MODEL, 2 * D_FF), BF16), pltpu.VMEM((D_FF, D_MODEL), BF16)],
        ),
        out_shape=jax.ShapeDtypeStruct((p, D_MODEL // 2), U32),
        compiler_params=pltpu.CompilerParams(dimension_semantics=("arbitrary",),
                                             vmem_limit_bytes=VMEM_LIMIT),
        name="ffn",
    )(blk_e, nvalid, xin, wgu, bgu, wd, bd)


def _sc_gather(table, idx):
    r, d = idx.shape[0], table.shape[1]
    mesh = plsc.VectorSubcoreMesh(core_axis_name="c", subcore_axis_name="s")
    nc, workers = mesh.num_cores, mesh.num_cores * mesh.num_subcores
    chunk = SC_GATHER_ROWS
    per_w = r // workers
    pairs = per_w // (2 * chunk)
    assert per_w * workers == r and pairs * 2 * chunk == per_w

    @functools.partial(
        pl.kernel, mesh=mesh, out_type=jax.ShapeDtypeStruct((r, d), table.dtype),
        scratch_types=[pltpu.VMEM((2, chunk), I32), pltpu.VMEM((2, chunk, d), table.dtype),
                       pltpu.SemaphoreType.DMA((2,)), pltpu.SemaphoreType.DMA((2,))],
        name="sc_gather")
    def gather(table_hbm, idx_hbm, out_hbm, idx_v, rows_v, gsem, wsem):
        base_w = (lax.axis_index("s") * nc + lax.axis_index("c")) * per_w

        def fetch(b):
            return pltpu.make_async_copy(table_hbm.at[idx_v.at[b]], rows_v.at[b], gsem.at[b])

        def flush(b, base):
            return pltpu.make_async_copy(rows_v.at[b], out_hbm.at[pl.ds(base, chunk)], wsem.at[b])

        def body(j, carry):
            for b in range(2):
                base = base_w + (2 * j + b) * chunk

                @pl.when(j > 0)
                def _():
                    flush(b, base).wait()

                pltpu.sync_copy(idx_hbm.at[pl.ds(base, chunk)], idx_v.at[b])
                fetch(b).start()
            for b in range(2):
                fetch(b).wait()
                flush(b, base_w + (2 * j + b) * chunk).start()
            return carry

        lax.fori_loop(0, pairs, body, 0)
        for b in range(2):
            flush(b, base_w).wait()

    return gather(table, idx)


def _combine_body(h_ref, gate_ref, fg_ref, *refs):
    yg_refs, o_ref = refs[:TOP_K], refs[TOP_K]
    half = D_MODEL // 2
    gates = gate_ref[...]
    h = h_ref[...]
    lo, hi = h[:, :half], h[:, half:]
    for kk in range(TOP_K):
        yp = yg_refs[kk][...]
        g = gates[:, kk:kk + 1]
        lo = lo + g * pltpu.bitcast(yp << 16, F32)
        hi = hi + g * pltpu.bitcast(yp & jnp.uint32(0xFFFF0000), F32)
    out = jnp.concatenate([lo, hi], axis=1)
    o_ref[...] = _rms(out, fg_ref[...])


def _combine(h, gates, fg, ygath):
    t = h.shape[0]
    tm = COMBINE_TILE
    nt = t // tm
    row = lambda i: (i, 0)
    slot = lambda kk: pl.BlockSpec((tm, D_MODEL // 2), lambda i: (kk * nt + i, 0))
    return pl.pallas_call(
        _combine_body,
        grid=(nt,),
        in_specs=[pl.BlockSpec((tm, D_MODEL), row), pl.BlockSpec((tm, LANES), row),
                  pl.BlockSpec((1, D_MODEL), lambda i: (0, 0))] + [slot(kk) for kk in range(TOP_K)],
        out_specs=pl.BlockSpec((tm, D_MODEL), row),
        out_shape=jax.ShapeDtypeStruct((t, D_MODEL), F32),
        compiler_params=pltpu.CompilerParams(dimension_semantics=("arbitrary",),
                                             vmem_limit_bytes=VMEM_LIMIT),
        name="combine",
    )(h, gates, fg, *([ygath] * TOP_K))


def _block_tril(n, chunk, strict):
    r = jnp.arange(n)[:, None]
    c = jnp.arange(n)[None, :]
    keep = ((r // chunk) == (c // chunk)) & ((r > c) if strict else (r >= c))
    return keep.astype(BF16)


def _pad_lanes(a, fill=0.0):
    a = a.reshape(1, -1).astype(F32)
    return jnp.pad(a, ((0, 0), (0, LANES - a.shape[1])), constant_values=fill)


def _layer(h, norm_mix_g, w_in, gm_norm_g, gm_ws, gm_bs, dn_conv_w, dn_a_log, dn_dt_bias,
           dn_norm_g, w_out, norm_ffn_g, router_w, router_b, exp_w_gu, exp_b_gu, exp_w_down,
           exp_b_down, out_g):
    batch, seq, d = h.shape
    t = batch * seq
    x2 = h.reshape(t, d)

    c0, c1, c2 = GM_WIDTH, 2 * GM_WIDTH, 2 * GM_WIDTH + 3 * DN_KEY
    c3 = c2 + DN_VAL
    wb = w_in.astype(BF16)
    wu, wv, wqkv, wz = wb[:, :c0], wb[:, c0:c1], wb[:, c1:c2], wb[:, c2:c3]
    wab = jnp.pad(wb[:, c3:], ((0, 0), (0, LANES - 2 * DN_HEADS)))
    gmg = gm_norm_g.reshape(1, GM_WIDTH).astype(F32)
    bsb = jnp.repeat(gm_bs.T, HEAD_DIM, axis=1).astype(F32)
    alog = _pad_lanes(dn_a_log)
    dtb = _pad_lanes(dn_dt_bias)
    tri_incl = _block_tril(ROW_TILE, DN_CHUNK, strict=False)

    oa, q, k, v, z, gb, gt = _inproj(
        x2, norm_mix_g.reshape(1, d), wu, wv, wqkv, wz, wab, gmg, gm_ws.astype(F32), bsb,
        dn_conv_w.astype(F32), alog, dtb, tri_incl, seq=seq)

    grow = gt[:DN_HEADS].reshape(DN_HEADS, t // DN_CHUNK, DN_CHUNK).transpose(1, 0, 2)
    grow = grow.reshape(t // DN_CHUNK, DN_HEADS * DN_CHUNK)
    ob = _gdn(q, k, v, z, gb, grow, dn_norm_g.reshape(1, HEAD_DIM).astype(F32), batch=batch, seq=seq)

    wo = w_out.astype(BF16)
    rw = jnp.pad(router_w.astype(BF16), ((0, 0), (0, LANES - N_EXPERTS)))
    rb = _pad_lanes(router_b, fill=NEG_BIG)
    tri_strict = _block_tril(ROW_TILE, ROW_TILE, strict=True)
    hres, hp, meta, gates, cnt = _outproj(oa, ob, x2, wo[:GM_WIDTH], wo[GM_WIDTH:],
                                          norm_ffn_g.reshape(1, d), rw, rb, tri_strict)

    bm = FFN_BLOCK
    counts = cnt[0, :N_EXPERTS]
    padded = (counts + bm - 1) // bm * bm
    pad_end = jnp.cumsum(padded)
    pad_start = (pad_end - padded).astype(I32)
    n_blocks = (t * TOP_K + N_EXPERTS * bm) // bm
    nvalid = (pad_end[-1] // bm).astype(I32).reshape(1)
    blk = jnp.minimum(jnp.arange(n_blocks, dtype=I32), nvalid[0] - 1)
    blk_e = jnp.minimum(jnp.sum(pad_end[None, :] <= (blk * bm)[:, None], axis=1), N_EXPERTS - 1).astype(I32)
    eid = meta[:, :TOP_K]
    start_of = jnp.sum(jnp.where(eid[..., None] == jnp.arange(N_EXPERTS, dtype=I32), pad_start, 0), axis=-1)
    dest_k = (start_of + meta[:, TOP_K:2 * TOP_K]).T.reshape(-1)

    xin = _sc_dispatch(hp, dest_k, n_blocks * bm)
    xin = _padfill(counts.astype(I32), pad_start, pad_end.astype(I32), xin)
    y = _ffn(blk_e, nvalid, xin, exp_w_gu, exp_b_gu[:, None, :].astype(F32),
             exp_w_down, exp_b_down[:, None, :].astype(F32))
    ygath = _sc_gather(y, dest_k)
    out = _combine(hres, gates, out_g.reshape(1, d).astype(F32), ygath)
    return out.reshape(batch, seq, d)


def kernel(x, norm_mix_g, w_in, gm_norm_g, gm_ws, gm_bs, dn_conv_w, dn_a_log, dn_dt_bias, dn_norm_g, w_out, norm_ffn_g, router_w, router_b, exp_w_gu, exp_b_gu, exp_w_down, exp_b_down, final_norm_g):
    depth = norm_mix_g.shape[0]
    assert depth == 1, "single-layer problem"
    return _layer(x, norm_mix_g[0], w_in[0], gm_norm_g[0], gm_ws[0], gm_bs[0], dn_conv_w[0],
                  dn_a_log[0], dn_dt_bias[0], dn_norm_g[0], w_out[0], norm_ffn_g[0], router_w[0],
                  router_b[0], exp_w_gu[0], exp_b_gu[0], exp_w_down[0], exp_b_down[0], final_norm_g)
```

```python
import functools

import jax
import jax.numpy as jnp
from jax import lax
from jax.experimental import pallas as pl
from jax.experimental.pallas import tpu as pltpu
from jax.experimental.pallas import tpu_sc as plsc

F32 = jnp.float32
BF16 = jnp.bfloat16
I32 = jnp.int32
U32 = jnp.uint32

D_MODEL = 1024
HEAD_DIM = 128
GM_HEADS = 4
GM_WIDTH = GM_HEADS * HEAD_DIM
GM_CHUNK = 128
DN_HEADS = 4
DN_DK = 128
DN_KEY = DN_HEADS * DN_DK
DN_VAL = DN_HEADS * HEAD_DIM
DN_CHUNK = 64
CONV_W = 4
N_EXPERTS = 32
TOP_K = 4
D_FF = D_MODEL
SWIGLU_LIMIT = 7.0
SWIGLU_ALPHA = 1.702
EPS = 1e-6

LANES = 128
INPROJ_COLS = 512
SUBLANES = 8
ROW_TILE = 512
FFN_BLOCK = 512
SC_GATHER_ROWS = 64
COMBINE_TILE = 256
VMEM_LIMIT = 56 * 1024 * 1024
NEG_BIG = -1e30
GDN_GROUP = 4


def _dot(a, b):
    return jnp.dot(a, b, preferred_element_type=F32)


def _dot_nt(a, b):
    return lax.dot_general(a, b, (((1,), (1,)), ((), ())), preferred_element_type=F32)


def _dot_tn(a, b):
    return lax.dot_general(a, b, (((0,), (0,)), ((), ())), preferred_element_type=F32)


def _rms(x, g):
    return x * lax.rsqrt(jnp.mean(x * x, axis=-1, keepdims=True) + EPS) * g


def _gelu(x):
    return 0.5 * x * (1.0 + lax.erf(x * (2.0 ** -0.5)))


def _sigmoid(x):
    return 1.0 / (1.0 + jnp.exp(-x))


def _softplus(x):
    return jnp.maximum(x, 0.0) + jnp.log1p(jnp.exp(-jnp.abs(x)))


def _inproj_body(x_ref, ng_ref, wu_ref, wv_ref, wqkv_ref, wz_ref, wab_ref, gmg_ref, ws_ref,
                 bs_ref, cw_ref, alog_ref, dtb_ref, tri_ref,
                 oa_ref, q_ref, k_ref, v_ref, z_ref, gb_ref, gt_ref, cbuf_ref, ybuf_ref,
                 *, tm, tiles_per_seq):
    i = pl.program_id(0)
    ybuf_ref[...] = _rms(x_ref[...], ng_ref[...]).astype(BF16)
    nb = INPROJ_COLS
    heads_per_block = nb // HEAD_DIM


    ri = lax.broadcasted_iota(I32, (GM_CHUNK, GM_CHUNK), 0)
    ci = lax.broadcasted_iota(I32, (GM_CHUNK, GM_CHUNK), 1)
    causal = ri >= ci
    for blk in range(GM_WIDTH // nb):
        cs = slice(blk * nb, (blk + 1) * nb)
        u = _gelu(_dot(ybuf_ref[...], wu_ref[:, cs]))
        vg = _gelu(_dot(ybuf_ref[...], wv_ref[:, cs]))
        for hh in range(heads_per_block):
            h = blk * heads_per_block + hh
            sl = slice(h * HEAD_DIM, (h + 1) * HEAD_DIM)
            ls = slice(hh * HEAD_DIM, (hh + 1) * HEAD_DIM)
            vh = _rms(vg[:, ls], gmg_ref[:, sl]).astype(BF16)
            wm = jnp.where(causal, ws_ref[h], 0.0).astype(BF16)
            for c in range(tm // GM_CHUNK):
                rs = slice(c * GM_CHUNK, (c + 1) * GM_CHUNK)
                gate = _dot(wm, vh[rs]) + bs_ref[:, sl]
                oa_ref[rs, sl] = (u[rs, ls] * gate).astype(BF16)

    @pl.when(i % tiles_per_seq == 0)
    def _():
        cbuf_ref[0:8, :] = jnp.zeros((8, 3 * DN_KEY), F32)

    for blk in range(3 * DN_KEY // nb):
        cs = slice(blk * nb, (blk + 1) * nb)
        pq = _dot(ybuf_ref[...], wqkv_ref[:, cs])
        cbuf_ref[8:8 + tm, cs] = pq
        acc = cw_ref[CONV_W - 1:CONV_W, cs] * pq
        for j in range(CONV_W - 1):
            off = 8 - (CONV_W - 1) + j
            acc = acc + cw_ref[j:j + 1, cs] * cbuf_ref[off:off + tm, cs]
        cbuf_ref[0:8, cs] = pq[tm - 8:tm, :]
        s = acc * _sigmoid(acc)
        for hh in range(heads_per_block):
            c0 = blk * nb + hh * HEAD_DIM
            sh = s[:, hh * HEAD_DIM:(hh + 1) * HEAD_DIM]
            if c0 < DN_KEY:
                q_ref[:, c0:c0 + HEAD_DIM] = (
                    sh * lax.rsqrt(jnp.sum(sh * sh, axis=-1, keepdims=True) + EPS)
                    * (DN_DK ** -0.5)).astype(BF16)
            elif c0 < 2 * DN_KEY:
                k_ref[:, c0 - DN_KEY:c0 - DN_KEY + HEAD_DIM] = (
                    sh * lax.rsqrt(jnp.sum(sh * sh, axis=-1, keepdims=True) + EPS)).astype(BF16)
            else:
                v_ref[:, c0 - 2 * DN_KEY:c0 - 2 * DN_KEY + HEAD_DIM] = sh.astype(BF16)
    for blk in range(DN_VAL // nb):
        cs = slice(blk * nb, (blk + 1) * nb)
        z_ref[:, cs] = _dot(ybuf_ref[...], wz_ref[:, cs]).astype(BF16)

    ab = _dot(ybuf_ref[...], wab_ref[...])
    g = -jnp.exp(alog_ref[...]) * _softplus(ab + dtb_ref[...])
    beta = _sigmoid(ab)
    g_hi = g.astype(BF16)
    r1 = g - g_hi.astype(F32)
    g_mid = r1.astype(BF16)
    g_lo = (r1 - g_mid.astype(F32)).astype(BF16)
    tri = tri_ref[...]
    gc = _dot(tri, g_hi) + _dot(tri, g_mid) + _dot(tri, g_lo)
    lane = lax.broadcasted_iota(I32, (tm, LANES), 1)
    gbv = jnp.where(lane < DN_HEADS, gc, beta)
    gb_ref[...] = gbv
    gt_ref[...] = gbv.T[0:8, :]


def _inproj(x2, ng, wu, wv, wqkv, wz, wab, gmg, ws, bsb, cw, alog, dtb, tri, *, seq):
    t = x2.shape[0]
    tm = ROW_TILE
    const2 = lambda i: (0, 0)
    row = lambda i: (i, 0)
    full = lambda a: pl.BlockSpec(a.shape, (lambda i: (0,) * a.ndim))
    out_shapes = (
        jax.ShapeDtypeStruct((t, GM_WIDTH), BF16),
        jax.ShapeDtypeStruct((t, DN_KEY), BF16),
        jax.ShapeDtypeStruct((t, DN_KEY), BF16),
        jax.ShapeDtypeStruct((t, DN_VAL), BF16),
        jax.ShapeDtypeStruct((t, DN_VAL), BF16),
        jax.ShapeDtypeStruct((t, LANES), F32),
        jax.ShapeDtypeStruct((8, t), F32),
    )
    return pl.pallas_call(
        functools.partial(_inproj_body, tm=tm, tiles_per_seq=seq // tm),
        grid=(t // tm,),
        in_specs=[pl.BlockSpec((tm, D_MODEL), row), full(ng), full(wu), full(wv), full(wqkv),
                  full(wz), full(wab), full(gmg), full(ws), full(bsb), full(cw), full(alog),
                  full(dtb), full(tri)],
        out_specs=(pl.BlockSpec((tm, GM_WIDTH), row), pl.BlockSpec((tm, DN_KEY), row),
                   pl.BlockSpec((tm, DN_KEY), row), pl.BlockSpec((tm, DN_VAL), row),
                   pl.BlockSpec((tm, DN_VAL), row), pl.BlockSpec((tm, LANES), row),
                   pl.BlockSpec((8, tm), lambda i: (0, i))),
        out_shape=out_shapes,
        scratch_shapes=[pltpu.VMEM((tm + 8, 3 * DN_KEY), F32), pltpu.VMEM((tm, D_MODEL), BF16)],
        compiler_params=pltpu.CompilerParams(dimension_semantics=("arbitrary",),
                                             vmem_limit_bytes=VMEM_LIMIT),
        name="inproj",
    )(x2, ng, wu, wv, wqkv, wz, wab, gmg, ws, bsb, cw, alog, dtb, tri)


def _gdn_body(q_ref, k_ref, v_ref, z_ref, gb_ref, gr_ref, ng_ref, ob_ref, s_ref, *, nchunk,
              group_size):
    j = pl.program_id(1)

    @pl.when(j == 0)
    def _():
        s_ref[...] = jnp.zeros(s_ref.shape, F32)

    c = DN_CHUNK
    n = DN_HEADS * c
    ri = lax.broadcasted_iota(I32, (n, n), 0)
    ci = lax.broadcasted_iota(I32, (n, n), 1)
    same = (ri // c) == (ci // c)
    incl = same & ((ri % c) >= (ci % c))
    strict = same & ((ri % c) > (ci % c))
    ng = ng_ref[...]

    def stack(a):
        return jnp.concatenate([a[:, h * HEAD_DIM:(h + 1) * HEAD_DIM] for h in range(DN_HEADS)], axis=0)

    def prepare(ic):
        r0 = pl.multiple_of(ic * c, c)
        kst = stack(k_ref[pl.ds(r0, c), :])
        qst = stack(q_ref[pl.ds(r0, c), :])
        vst = stack(v_ref[pl.ds(r0, c), :])
        gbc = gb_ref[pl.ds(r0, c), :]
        grow = gr_ref[pl.ds(ic, 1), :]
        gcol = jnp.concatenate([gbc[:, h:h + 1] for h in range(DN_HEADS)], axis=0)
        bcol = jnp.concatenate([gbc[:, DN_HEADS + h:DN_HEADS + h + 1] for h in range(DN_HEADS)], axis=0)
        glast = jnp.concatenate(
            [jnp.broadcast_to(gbc[c - 1:c, h:h + 1], (c, 1)) for h in range(DN_HEADS)], axis=0)
        decay = jnp.where(incl, jnp.exp(jnp.where(incl, gcol - grow, 0.0)), 0.0)
        kf = kst.astype(F32)
        kb = kf * bcol
        lmat = jnp.where(strict, _dot_nt(kb.astype(BF16), kst) * decay, 0.0)
        eg = jnp.exp(gcol)
        rhs = jnp.concatenate([vst.astype(F32) * bcol, kb * eg], axis=1)
        attn = jnp.where(incl, _dot_nt(qst, kst) * decay, 0.0).astype(BF16)
        qd = (qst.astype(F32) * eg).astype(BF16)
        ke = (kf * jnp.exp(glast - gcol)).astype(BF16)
        return dict(r0=r0, gbc=gbc, lmat=lmat, rhs=rhs, attn=attn, qd=qd, ke=ke)

    def advance_state(p, tinv):
        sol = p["rhs"] + _dot(tinv.astype(BF16), p["rhs"].astype(BF16))
        u = sol[:, :HEAD_DIM]
        wb = sol[:, HEAD_DIM:].astype(BF16)
        vn, qs = [], []
        for h in range(DN_HEADS):
            rs = slice(h * c, (h + 1) * c)
            sb = s_ref[h].astype(BF16)
            vn.append(u[rs] - _dot(wb[rs], sb))
            qs.append(_dot(p["qd"][rs], sb))
        vnb = jnp.concatenate(vn, axis=0).astype(BF16)
        o = jnp.concatenate(qs, axis=0) + _dot(p["attn"], vnb)
        zc = z_ref[pl.ds(p["r0"], c), :]
        for h in range(DN_HEADS):
            rs = slice(h * c, (h + 1) * c)
            sl = slice(h * HEAD_DIM, (h + 1) * HEAD_DIM)
            s_ref[h] = (s_ref[h] * jnp.exp(p["gbc"][c - 1:c, h:h + 1])
                        + _dot_tn(p["ke"][rs], vnb[rs]))
            zf = zc[:, sl].astype(F32)
            ob_ref[pl.ds(p["r0"], c), sl] = (_rms(o[rs], ng) * (zf * _sigmoid(zf))).astype(BF16)

    def group(ig, carry):
        ps = [prepare(ig * group_size + b) for b in range(group_size)]
        tinv = [-p["lmat"] for p in ps]
        pw = [p["lmat"] for p in ps]
        for _ in range(5):
            pb = [x.astype(BF16) for x in pw]
            pw = [_dot(x, x) for x in pb]
            pwb = [x.astype(BF16) for x in pw]
            tinv = [t + x + _dot(t.astype(BF16), xb) for t, x, xb in zip(tinv, pw, pwb)]
        for p, t in zip(ps, tinv):
            advance_state(p, t)
        return carry

    lax.fori_loop(0, nchunk // group_size, group, 0)


def _gdn(q, k, v, z, gb, grow, ng, *, batch, seq):
    tm = ROW_TILE
    nchunk = tm // DN_CHUNK
    steps = seq // tm
    rows = lambda b, j: (b * steps + j, 0)
    return pl.pallas_call(
        functools.partial(_gdn_body, nchunk=nchunk, group_size=GDN_GROUP),
        grid=(batch, steps),
        in_specs=[pl.BlockSpec((tm, DN_KEY), rows), pl.BlockSpec((tm, DN_KEY), rows),
                  pl.BlockSpec((tm, DN_VAL), rows), pl.BlockSpec((tm, DN_VAL), rows),
                  pl.BlockSpec((tm, LANES), rows),
                  pl.BlockSpec((nchunk, DN_HEADS * DN_CHUNK), rows),
                  pl.BlockSpec((1, HEAD_DIM), lambda b, j: (0, 0))],
        out_specs=pl.BlockSpec((tm, DN_VAL), rows),
        out_shape=jax.ShapeDtypeStruct((batch * seq, DN_VAL), BF16),
        scratch_shapes=[pltpu.VMEM((DN_HEADS, DN_DK, HEAD_DIM), F32)],
        compiler_params=pltpu.CompilerParams(dimension_semantics=("arbitrary", "arbitrary"),
                                             vmem_limit_bytes=VMEM_LIMIT),
        name="gdn",
    )(q, k, v, z, gb, grow, ng)


def _outproj_body(oa_ref, ob_ref, x_ref, woa_ref, wob_ref, ng_ref, rw_ref, rb_ref, tri_ref,
                  h_ref, hp_ref, meta_ref, gate_ref, cnt_ref, run_ref, *, tm):
    i = pl.program_id(0)

    @pl.when(i == 0)
    def _():
        run_ref[...] = jnp.zeros(run_ref.shape, F32)

    h = x_ref[...] + _dot(oa_ref[...], woa_ref[...]) + _dot(ob_ref[...], wob_ref[...])
    h_ref[...] = h
    hb = _rms(h, ng_ref[...]).astype(BF16)
    half = D_MODEL // 2
    lo = pltpu.bitcast(hb[:, :half].astype(F32), U32) >> 16
    hi = pltpu.bitcast(hb[:, half:].astype(F32), U32) & jnp.uint32(0xFFFF0000)
    hp_ref[...] = lo | hi

    logits = _dot(hb, rw_ref[...]) + rb_ref[...]
    lane = lax.broadcasted_iota(I32, (tm, LANES), 1)
    lanef = lane.astype(F32)
    work = logits
    onehot = jnp.zeros((tm, LANES), F32)
    vals, sels = [], []
    for _ in range(TOP_K):
        m = jnp.max(work, axis=-1, keepdims=True)
        idx = jnp.min(jnp.where(work == m, lanef, float(LANES)), axis=-1, keepdims=True)
        sel = lanef == idx
        work = jnp.where(sel, -3e38, work)
        onehot = onehot + jnp.where(sel, 1.0, 0.0)
        vals.append(m)
        sels.append((sel, idx))
    ex = [jnp.exp(v - vals[0]) for v in vals]
    den = ex[0] + ex[1] + ex[2] + ex[3]
    pref = _dot(tri_ref[...], onehot.astype(BF16)) + run_ref[0:1, :]
    meta = jnp.zeros((tm, LANES), F32)
    gates = jnp.zeros((tm, LANES), F32)
    for kk in range(TOP_K):
        sel, idx = sels[kk]
        rank = jnp.sum(jnp.where(sel, pref, 0.0), axis=-1, keepdims=True)
        meta = meta + jnp.where(lane == kk, idx, 0.0) + jnp.where(lane == TOP_K + kk, rank, 0.0)
        gates = gates + jnp.where(lane == kk, ex[kk] / den, 0.0)
    meta_ref[...] = meta.astype(I32)
    gate_ref[...] = gates
    run = run_ref[...] + jnp.sum(onehot, axis=0, keepdims=True)
    run_ref[...] = run
    cnt_ref[...] = run.astype(I32)


def _outproj(oa, ob, x2, woa, wob, ng, rw, rb, tri):
    t = x2.shape[0]
    tm = ROW_TILE
    row = lambda i: (i, 0)
    full = lambda a: pl.BlockSpec(a.shape, (lambda i: (0,) * a.ndim))
    out_shapes = (
        jax.ShapeDtypeStruct((t, D_MODEL), F32),
        jax.ShapeDtypeStruct((t, D_MODEL // 2), U32),
        jax.ShapeDtypeStruct((t, LANES), I32),
        jax.ShapeDtypeStruct((t, LANES), F32),
        jax.ShapeDtypeStruct((8, LANES), I32),
    )
    return pl.pallas_call(
        functools.partial(_outproj_body, tm=tm),
        grid=(t // tm,),
        in_specs=[pl.BlockSpec((tm, GM_WIDTH), row), pl.BlockSpec((tm, DN_VAL), row),
                  pl.BlockSpec((tm, D_MODEL), row), full(woa), full(wob), full(ng), full(rw),
                  full(rb), full(tri)],
        out_specs=(pl.BlockSpec((tm, D_MODEL), row), pl.BlockSpec((tm, D_MODEL // 2), row),
                   pl.BlockSpec((tm, LANES), row), pl.BlockSpec((tm, LANES), row),
                   pl.BlockSpec((8, LANES), lambda i: (0, 0))),
        out_shape=out_shapes,
        scratch_shapes=[pltpu.VMEM((8, LANES), F32)],
        compiler_params=pltpu.CompilerParams(dimension_semantics=("arbitrary",),
                                             vmem_limit_bytes=VMEM_LIMIT),
        name="outproj",
    )(oa, ob, x2, woa, wob, ng, rw, rb, tri)


def _sc_dispatch(hp, dest_k, n_rows):
    t, d = hp.shape
    mesh = plsc.VectorSubcoreMesh(core_axis_name="c", subcore_axis_name="s")
    nc, workers = mesh.num_cores, mesh.num_cores * mesh.num_subcores
    chunk = SC_GATHER_ROWS
    per_w = t // workers
    pairs = per_w // (2 * chunk)
    assert per_w * workers == t and pairs * 2 * chunk == per_w

    @functools.partial(
        pl.kernel, mesh=mesh, out_type=jax.ShapeDtypeStruct((n_rows, d), hp.dtype),
        scratch_types=[pltpu.VMEM((2, TOP_K, chunk), I32), pltpu.VMEM((2, chunk, d), hp.dtype),
                       pltpu.SemaphoreType.DMA((2,)), pltpu.SemaphoreType.DMA((2,))],
        name="sc_dispatch")
    def scatter(hp_hbm, idx_hbm, xin_hbm, idx_v, rows_v, lsem, ssem):
        base_w = (lax.axis_index("s") * nc + lax.axis_index("c")) * per_w

        def load(b, base):
            return pltpu.make_async_copy(hp_hbm.at[pl.ds(base, chunk)], rows_v.at[b], lsem.at[b])

        def put(b, kk):
            return pltpu.make_async_copy(rows_v.at[b], xin_hbm.at[idx_v.at[b, kk]], ssem.at[b])

        def body(j, carry):
            for b in range(2):
                base = base_w + (2 * j + b) * chunk

                @pl.when(j > 0)
                def _():
                    for kk in range(TOP_K):
                        put(b, kk).wait()

                for kk in range(TOP_K):
                    pltpu.sync_copy(idx_hbm.at[pl.ds(kk * t + base, chunk)], idx_v.at[b, kk])
                load(b, base).start()
            for b in range(2):
                load(b, base_w).wait()
                for kk in range(TOP_K):
                    put(b, kk).start()
            return carry

        lax.fori_loop(0, pairs, body, 0)
        for b in range(2):
            for kk in range(TOP_K):
                put(b, kk).wait()

    return scatter(hp, dest_k)


def _ffn_body(be_ref, nv_ref, br_ref, x_ref, wgu_ref, bgu_ref, wd_ref, bd_ref, y_ref, wgub_ref,
              wdb_ref):
    i = pl.program_id(0)

    @pl.when(i >= nv_ref[0])
    def _():
        y_ref[...] = jnp.zeros(y_ref.shape, U32)

    @pl.when((i < nv_ref[0]) & ((i == 0) | (be_ref[i] != be_ref[jnp.maximum(i - 1, 0)])))
    def _():
        wgub_ref[...] = wgu_ref[0].astype(BF16)
        wdb_ref[...] = wd_ref[0].astype(BF16)

    @pl.when(i < nv_ref[0])
    def _():
        half = D_MODEL // 2
        row = lax.broadcasted_iota(I32, (x_ref.shape[0], 1), 0)
        xp = jnp.where(row < br_ref[i], x_ref[...], jnp.uint32(0))
        lo = pltpu.bitcast(xp << 16, F32).astype(BF16)
        hi = pltpu.bitcast(xp & jnp.uint32(0xFFFF0000), F32).astype(BF16)
        gu = _dot(lo, wgub_ref[:half, :]) + _dot(hi, wgub_ref[half:, :]) + bgu_ref[0]
        gate = jnp.minimum(gu[:, :D_FF], SWIGLU_LIMIT)
        up = jnp.clip(gu[:, D_FF:], -SWIGLU_LIMIT, SWIGLU_LIMIT)
        act = (up + 1.0) * (gate * _sigmoid(SWIGLU_ALPHA * gate))
        y = _dot(act.astype(BF16), wdb_ref[...]) + bd_ref[0]
        ylo = pltpu.bitcast(y[:, :half].astype(BF16).astype(F32), U32) >> 16
        yhi = pltpu.bitcast(y[:, half:].astype(BF16).astype(F32), U32) & jnp.uint32(0xFFFF0000)
        y_ref[...] = ylo | yhi


def _ffn(blk_e, nvalid, blk_rows, xin, wgu, bgu, wd, bd):
    p = xin.shape[0]
    bm = FFN_BLOCK
    rows = lambda i, be, nv, br: (jnp.minimum(i, nv[0] - 1), 0)
    wsel = lambda i, be, nv, br: (be[i], 0, 0)
    return pl.pallas_call(
        _ffn_body,
        grid_spec=pltpu.PrefetchScalarGridSpec(
            num_scalar_prefetch=3,
            grid=(p // bm,),
            in_specs=[pl.BlockSpec((bm, D_MODEL // 2), rows),
                      pl.BlockSpec((1, D_MODEL, 2 * D_FF), wsel),
                      pl.BlockSpec((1, 1, 2 * D_FF), wsel),
                      pl.BlockSpec((1, D_FF, D_MODEL), wsel),
                      pl.BlockSpec((1, 1, D_MODEL), wsel)],
            out_specs=pl.BlockSpec((bm, D_MODEL // 2), lambda i, be, nv, br: (i, 0)),
            scratch_shapes=[pltpu.VMEM((D_MODEL, 2 * D_FF), BF16), pltpu.VMEM((D_FF, D_MODEL), BF16)],
        ),
        out_shape=jax.ShapeDtypeStruct((p, D_MODEL // 2), U32),
        compiler_params=pltpu.CompilerParams(dimension_semantics=("arbitrary",),
                                             vmem_limit_bytes=VMEM_LIMIT),
        name="ffn",
    )(blk_e, nvalid, blk_rows, xin, wgu, bgu, wd, bd)


def _sc_gather(table, idx):
    r, d = idx.shape[0], table.shape[1]
    mesh = plsc.VectorSubcoreMesh(core_axis_name="c", subcore_axis_name="s")
    nc, workers = mesh.num_cores, mesh.num_cores * mesh.num_subcores
    chunk = SC_GATHER_ROWS
    per_w = r // workers
    pairs = per_w // (2 * chunk)
    assert per_w * workers == r and pairs * 2 * chunk == per_w

    @functools.partial(
        pl.kernel, mesh=mesh, out_type=jax.ShapeDtypeStruct((r, d), table.dtype),
        scratch_types=[pltpu.VMEM((2, chunk), I32), pltpu.VMEM((2, chunk, d), table.dtype),
                       pltpu.SemaphoreType.DMA((2,)), pltpu.SemaphoreType.DMA((2,))],
        name="sc_gather")
    def gather(table_hbm, idx_hbm, out_hbm, idx_v, rows_v, gsem, wsem):
        base_w = (lax.axis_index("s") * nc + lax.axis_index("c")) * per_w

        def fetch(b):
            return pltpu.make_async_copy(table_hbm.at[idx_v.at[b]], rows_v.at[b], gsem.at[b])

        def flush(b, base):
            return pltpu.make_async_copy(rows_v.at[b], out_hbm.at[pl.ds(base, chunk)], wsem.at[b])

        def body(j, carry):
            for b in range(2):
                base = base_w + (2 * j + b) * chunk

                @pl.when(j > 0)
                def _():
                    flush(b, base).wait()

                pltpu.sync_copy(idx_hbm.at[pl.ds(base, chunk)], idx_v.at[b])
                fetch(b).start()
            for b in range(2):
                fetch(b).wait()
                flush(b, base_w + (2 * j + b) * chunk).start()
            return carry

        lax.fori_loop(0, pairs, body, 0)
        for b in range(2):
            flush(b, base_w).wait()

    return gather(table, idx)


def _combine_body(h_ref, gate_ref, fg_ref, *refs):
    yg_refs, o_ref = refs[:TOP_K], refs[TOP_K]
    half = D_MODEL // 2
    gates = gate_ref[...]
    h = h_ref[...]
    lo, hi = h[:, :half], h[:, half:]
    for kk in range(TOP_K):
        yp = yg_refs[kk][...]
        g = gates[:, kk:kk + 1]
        lo = lo + g * pltpu.bitcast(yp << 16, F32)
        hi = hi + g * pltpu.bitcast(yp & jnp.uint32(0xFFFF0000), F32)
    out = jnp.concatenate([lo, hi], axis=1)
    o_ref[...] = _rms(out, fg_ref[...])


def _combine(h, gates, fg, ygath):
    t = h.shape[0]
    tm = COMBINE_TILE
    nt = t // tm
    row = lambda i: (i, 0)
    slot = lambda kk: pl.BlockSpec((tm, D_MODEL // 2), lambda i: (kk * nt + i, 0))
    return pl.pallas_call(
        _combine_body,
        grid=(nt,),
        in_specs=[pl.BlockSpec((tm, D_MODEL), row), pl.BlockSpec((tm, LANES), row),
                  pl.BlockSpec((1, D_MODEL), lambda i: (0, 0))] + [slot(kk) for kk in range(TOP_K)],
        out_specs=pl.BlockSpec((tm, D_MODEL), row),
        out_shape=jax.ShapeDtypeStruct((t, D_MODEL), F32),
        compiler_params=pltpu.CompilerParams(dimension_semantics=("arbitrary",),
                                             vmem_limit_bytes=VMEM_LIMIT),
        name="combine",
    )(h, gates, fg, *([ygath] * TOP_K))


def _block_tril(n, chunk, strict):
    r = jnp.arange(n)[:, None]
    c = jnp.arange(n)[None, :]
    keep = ((r // chunk) == (c // chunk)) & ((r > c) if strict else (r >= c))
    return keep.astype(BF16)


def _pad_lanes(a, fill=0.0):
    a = a.reshape(1, -1).astype(F32)
    return jnp.pad(a, ((0, 0), (0, LANES - a.shape[1])), constant_values=fill)


def _layer(h, norm_mix_g, w_in, gm_norm_g, gm_ws, gm_bs, dn_conv_w, dn_a_log, dn_dt_bias,
           dn_norm_g, w_out, norm_ffn_g, router_w, router_b, exp_w_gu, exp_b_gu, exp_w_down,
           exp_b_down, out_g):
    batch, seq, d = h.shape
    t = batch * seq
    x2 = h.reshape(t, d)

    c0, c1, c2 = GM_WIDTH, 2 * GM_WIDTH, 2 * GM_WIDTH + 3 * DN_KEY
    c3 = c2 + DN_VAL
    wb = w_in.astype(BF16)
    wu, wv, wqkv, wz = wb[:, :c0], wb[:, c0:c1], wb[:, c1:c2], wb[:, c2:c3]
    wab = jnp.pad(wb[:, c3:], ((0, 0), (0, LANES - 2 * DN_HEADS)))
    gmg = gm_norm_g.reshape(1, GM_WIDTH).astype(F32)
    bsb = jnp.repeat(gm_bs.T, HEAD_DIM, axis=1).astype(F32)
    alog = _pad_lanes(dn_a_log)
    dtb = _pad_lanes(dn_dt_bias)
    tri_incl = _block_tril(ROW_TILE, DN_CHUNK, strict=False)

    oa, q, k, v, z, gb, gt = _inproj(
        x2, norm_mix_g.reshape(1, d), wu, wv, wqkv, wz, wab, gmg, gm_ws.astype(F32), bsb,
        dn_conv_w.astype(F32), alog, dtb, tri_incl, seq=seq)

    grow = gt[:DN_HEADS].reshape(DN_HEADS, t // DN_CHUNK, DN_CHUNK).transpose(1, 0, 2)
    grow = grow.reshape(t // DN_CHUNK, DN_HEADS * DN_CHUNK)
    ob = _gdn(q, k, v, z, gb, grow, dn_norm_g.reshape(1, HEAD_DIM).astype(F32), batch=batch, seq=seq)

    wo = w_out.astype(BF16)
    rw = jnp.pad(router_w.astype(BF16), ((0, 0), (0, LANES - N_EXPERTS)))
    rb = _pad_lanes(router_b, fill=NEG_BIG)
    tri_strict = _block_tril(ROW_TILE, ROW_TILE, strict=True)
    hres, hp, meta, gates, cnt = _outproj(oa, ob, x2, wo[:GM_WIDTH], wo[GM_WIDTH:],
                                          norm_ffn_g.reshape(1, d), rw, rb, tri_strict)

    bm = FFN_BLOCK
    counts = cnt[0, :N_EXPERTS]
    padded = (counts + bm - 1) // bm * bm
    pad_end = jnp.cumsum(padded)
    pad_start = (pad_end - padded).astype(I32)
    n_blocks = (t * TOP_K + N_EXPERTS * bm) // bm
    nvalid = (pad_end[-1] // bm).astype(I32).reshape(1)
    blk = jnp.minimum(jnp.arange(n_blocks, dtype=I32), nvalid[0] - 1)
    blk_e = jnp.minimum(jnp.sum(pad_end[None, :] <= (blk * bm)[:, None], axis=1), N_EXPERTS - 1).astype(I32)
    eid = meta[:, :TOP_K]
    start_of = jnp.sum(jnp.where(eid[..., None] == jnp.arange(N_EXPERTS, dtype=I32), pad_start, 0), axis=-1)
    dest_k = (start_of + meta[:, TOP_K:2 * TOP_K]).T.reshape(-1)

    xin = _sc_dispatch(hp, dest_k, n_blocks * bm)
    blk_rows = jnp.clip(jnp.take(counts + pad_start, blk_e) - blk * bm, 0, bm).astype(I32)
    y = _ffn(blk_e, nvalid, blk_rows, xin, exp_w_gu, exp_b_gu[:, None, :].astype(F32),
             exp_w_down, exp_b_down[:, None, :].astype(F32))
    ygath = _sc_gather(y, dest_k)
    out = _combine(hres, gates, out_g.reshape(1, d).astype(F32), ygath)
    return out.reshape(batch, seq, d)


def kernel(x, norm_mix_g, w_in, gm_norm_g, gm_ws, gm_bs, dn_conv_w, dn_a_log, dn_dt_bias, dn_norm_g, w_out, norm_ffn_g, router_w, router_b, exp_w_gu, exp_b_gu, exp_w_down, exp_b_down, final_norm_g):
    depth = norm_mix_g.shape[0]
    assert depth == 1, "single-layer problem"
    return _layer(x, norm_mix_g[0], w_in[0], gm_norm_g[0], gm_ws[0], gm_bs[0], dn_conv_w[0],
                  dn_a_log[0], dn_dt_bias[0], dn_norm_g[0], w_out[0], norm_ffn_g[0], router_w[0],
                  router_b[0], exp_w_gu[0], exp_b_gu[0], exp_w_down[0], exp_b_down[0], final_norm_g)
```

```python
import functools

import jax
import jax.numpy as jnp
from jax import lax
from jax.experimental import pallas as pl
from jax.experimental.pallas import tpu as pltpu
from jax.experimental.pallas import tpu_sc as plsc

F32 = jnp.float32
BF16 = jnp.bfloat16
I32 = jnp.int32
U32 = jnp.uint32

D_MODEL = 1024
HEAD_DIM = 128
GM_HEADS = 4
GM_WIDTH = GM_HEADS * HEAD_DIM
GM_CHUNK = 128
DN_HEADS = 4
DN_DK = 128
DN_KEY = DN_HEADS * DN_DK
DN_VAL = DN_HEADS * HEAD_DIM
DN_CHUNK = 64
CONV_W = 4
N_EXPERTS = 32
TOP_K = 4
D_FF = D_MODEL
SWIGLU_LIMIT = 7.0
SWIGLU_ALPHA = 1.702
EPS = 1e-6

LANES = 128
INPROJ_COLS = 512
SUBLANES = 8
ROW_TILE = 512
FFN_BLOCK = 512
FFN_SUBBLOCKS = 2
SC_GATHER_ROWS = 64
COMBINE_PARTS = 4
COMBINE_TILE = 256
VMEM_LIMIT = 56 * 1024 * 1024
NEG_BIG = -1e30
GDN_GROUP = 4


def _dot(a, b):
    return jnp.dot(a, b, preferred_element_type=F32)


def _dot_nt(a, b):
    return lax.dot_general(a, b, (((1,), (1,)), ((), ())), preferred_element_type=F32)


def _dot_tn(a, b):
    return lax.dot_general(a, b, (((0,), (0,)), ((), ())), preferred_element_type=F32)


def _rms(x, g):
    return x * lax.rsqrt(jnp.mean(x * x, axis=-1, keepdims=True) + EPS) * g


def _gelu(x):
    return 0.5 * x * (1.0 + lax.erf(x * (2.0 ** -0.5)))


def _sigmoid(x):
    return 1.0 / (1.0 + jnp.exp(-x))


def _softplus(x):
    return jnp.maximum(x, 0.0) + jnp.log1p(jnp.exp(-jnp.abs(x)))


def _inproj_body(x_ref, ng_ref, wu_ref, wv_ref, wqkv_ref, wz_ref, wab_ref, gmg_ref, ws_ref,
                 bs_ref, cw_ref, alog_ref, dtb_ref, tri_ref,
                 oa_ref, q_ref, k_ref, v_ref, z_ref, gb_ref, gt_ref, cbuf_ref, ybuf_ref,
                 *, tm, tiles_per_seq):
    i = pl.program_id(0)
    ybuf_ref[...] = _rms(x_ref[...], ng_ref[...]).astype(BF16)
    nb = INPROJ_COLS
    heads_per_block = nb // HEAD_DIM


    ri = lax.broadcasted_iota(I32, (GM_CHUNK, GM_CHUNK), 0)
    ci = lax.broadcasted_iota(I32, (GM_CHUNK, GM_CHUNK), 1)
    causal = ri >= ci
    for blk in range(GM_WIDTH // nb):
        cs = slice(blk * nb, (blk + 1) * nb)
        u = _gelu(_dot(ybuf_ref[...], wu_ref[:, cs]))
        vg = _gelu(_dot(ybuf_ref[...], wv_ref[:, cs]))
        for hh in range(heads_per_block):
            h = blk * heads_per_block + hh
            sl = slice(h * HEAD_DIM, (h + 1) * HEAD_DIM)
            ls = slice(hh * HEAD_DIM, (hh + 1) * HEAD_DIM)
            vh = _rms(vg[:, ls], gmg_ref[:, sl]).astype(BF16)
            wm = jnp.where(causal, ws_ref[h], 0.0).astype(BF16)
            for c in range(tm // GM_CHUNK):
                rs = slice(c * GM_CHUNK, (c + 1) * GM_CHUNK)
                gate = _dot(wm, vh[rs]) + bs_ref[:, sl]
                oa_ref[rs, sl] = (u[rs, ls] * gate).astype(BF16)

    @pl.when(i % tiles_per_seq == 0)
    def _():
        cbuf_ref[0:8, :] = jnp.zeros((8, 3 * DN_KEY), F32)

    for blk in range(3 * DN_KEY // nb):
        cs = slice(blk * nb, (blk + 1) * nb)
        pq = _dot(ybuf_ref[...], wqkv_ref[:, cs])
        cbuf_ref[8:8 + tm, cs] = pq
        acc = cw_ref[CONV_W - 1:CONV_W, cs] * pq
        for j in range(CONV_W - 1):
            off = 8 - (CONV_W - 1) + j
            acc = acc + cw_ref[j:j + 1, cs] * cbuf_ref[off:off + tm, cs]
        cbuf_ref[0:8, cs] = pq[tm - 8:tm, :]
        s = acc * _sigmoid(acc)
        for hh in range(heads_per_block):
            c0 = blk * nb + hh * HEAD_DIM
            sh = s[:, hh * HEAD_DIM:(hh + 1) * HEAD_DIM]
            if c0 < DN_KEY:
                q_ref[:, c0:c0 + HEAD_DIM] = (
                    sh * lax.rsqrt(jnp.sum(sh * sh, axis=-1, keepdims=True) + EPS)
                    * (DN_DK ** -0.5)).astype(BF16)
            elif c0 < 2 * DN_KEY:
                k_ref[:, c0 - DN_KEY:c0 - DN_KEY + HEAD_DIM] = (
                    sh * lax.rsqrt(jnp.sum(sh * sh, axis=-1, keepdims=True) + EPS)).astype(BF16)
            else:
                v_ref[:, c0 - 2 * DN_KEY:c0 - 2 * DN_KEY + HEAD_DIM] = sh.astype(BF16)
    for blk in range(DN_VAL // nb):
        cs = slice(blk * nb, (blk + 1) * nb)
        z_ref[:, cs] = _dot(ybuf_ref[...], wz_ref[:, cs]).astype(BF16)

    ab = _dot(ybuf_ref[...], wab_ref[...])
    g = -jnp.exp(alog_ref[...]) * _softplus(ab + dtb_ref[...])
    beta = _sigmoid(ab)
    g_hi = g.astype(BF16)
    r1 = g - g_hi.astype(F32)
    g_mid = r1.astype(BF16)
    g_lo = (r1 - g_mid.astype(F32)).astype(BF16)
    tri = tri_ref[...]
    gc = _dot(tri, g_hi) + _dot(tri, g_mid) + _dot(tri, g_lo)
    lane = lax.broadcasted_iota(I32, (tm, LANES), 1)
    gbv = jnp.where(lane < DN_HEADS, gc, beta)
    gb_ref[...] = gbv
    gt_ref[...] = gbv.T[0:8, :]


def _inproj(x2, ng, wu, wv, wqkv, wz, wab, gmg, ws, bsb, cw, alog, dtb, tri, *, seq):
    t = x2.shape[0]
    tm = ROW_TILE
    const2 = lambda i: (0, 0)
    row = lambda i: (i, 0)
    full = lambda a: pl.BlockSpec(a.shape, (lambda i: (0,) * a.ndim))
    out_shapes = (
        jax.ShapeDtypeStruct((t, GM_WIDTH), BF16),
        jax.ShapeDtypeStruct((t, DN_KEY), BF16),
        jax.ShapeDtypeStruct((t, DN_KEY), BF16),
        jax.ShapeDtypeStruct((t, DN_VAL), BF16),
        jax.ShapeDtypeStruct((t, DN_VAL), BF16),
        jax.ShapeDtypeStruct((t, LANES), F32),
        jax.ShapeDtypeStruct((8, t), F32),
    )
    return pl.pallas_call(
        functools.partial(_inproj_body, tm=tm, tiles_per_seq=seq // tm),
        grid=(t // tm,),
        in_specs=[pl.BlockSpec((tm, D_MODEL), row), full(ng), full(wu), full(wv), full(wqkv),
                  full(wz), full(wab), full(gmg), full(ws), full(bsb), full(cw), full(alog),
                  full(dtb), full(tri)],
        out_specs=(pl.BlockSpec((tm, GM_WIDTH), row), pl.BlockSpec((tm, DN_KEY), row),
                   pl.BlockSpec((tm, DN_KEY), row), pl.BlockSpec((tm, DN_VAL), row),
                   pl.BlockSpec((tm, DN_VAL), row), pl.BlockSpec((tm, LANES), row),
                   pl.BlockSpec((8, tm), lambda i: (0, i))),
        out_shape=out_shapes,
        scratch_shapes=[pltpu.VMEM((tm + 8, 3 * DN_KEY), F32), pltpu.VMEM((tm, D_MODEL), BF16)],
        compiler_params=pltpu.CompilerParams(dimension_semantics=("arbitrary",),
                                             vmem_limit_bytes=VMEM_LIMIT),
        name="inproj",
    )(x2, ng, wu, wv, wqkv, wz, wab, gmg, ws, bsb, cw, alog, dtb, tri)


def _gdn_body(q_ref, k_ref, v_ref, z_ref, gb_ref, gr_ref, ng_ref, ob_ref, s_ref, *, nchunk,
              group_size):
    j = pl.program_id(1)

    @pl.when(j == 0)
    def _():
        s_ref[...] = jnp.zeros(s_ref.shape, F32)

    c = DN_CHUNK
    n = DN_HEADS * c
    ri = lax.broadcasted_iota(I32, (n, n), 0)
    ci = lax.broadcasted_iota(I32, (n, n), 1)
    same = (ri // c) == (ci // c)
    incl = same & ((ri % c) >= (ci % c))
    strict = same & ((ri % c) > (ci % c))
    ng = ng_ref[...]

    def stack(a):
        return jnp.concatenate([a[:, h * HEAD_DIM:(h + 1) * HEAD_DIM] for h in range(DN_HEADS)], axis=0)

    def prepare(ic):
        r0 = pl.multiple_of(ic * c, c)
        kst = stack(k_ref[pl.ds(r0, c), :])
        qst = stack(q_ref[pl.ds(r0, c), :])
        vst = stack(v_ref[pl.ds(r0, c), :])
        gbc = gb_ref[pl.ds(r0, c), :]
        grow = gr_ref[pl.ds(ic, 1), :]
        gcol = jnp.concatenate([gbc[:, h:h + 1] for h in range(DN_HEADS)], axis=0)
        bcol = jnp.concatenate([gbc[:, DN_HEADS + h:DN_HEADS + h + 1] for h in range(DN_HEADS)], axis=0)
        glast = jnp.concatenate(
            [jnp.broadcast_to(gbc[c - 1:c, h:h + 1], (c, 1)) for h in range(DN_HEADS)], axis=0)
        decay = jnp.where(incl, jnp.exp(jnp.where(incl, gcol - grow, 0.0)), 0.0)
        kf = kst.astype(F32)
        kb = kf * bcol
        lmat = jnp.where(strict, _dot_nt(kb.astype(BF16), kst) * decay, 0.0)
        eg = jnp.exp(gcol)
        rhs = jnp.concatenate([vst.astype(F32) * bcol, kb * eg], axis=1)
        attn = jnp.where(incl, _dot_nt(qst, kst) * decay, 0.0).astype(BF16)
        qd = (qst.astype(F32) * eg).astype(BF16)
        ke = (kf * jnp.exp(glast - gcol)).astype(BF16)
        return dict(r0=r0, gbc=gbc, lmat=lmat, rhs=rhs, attn=attn, qd=qd, ke=ke)

    def advance_state(p, sol):
        u = sol[:, :HEAD_DIM]
        wb = sol[:, HEAD_DIM:].astype(BF16)
        vn, qs = [], []
        for h in range(DN_HEADS):
            rs = slice(h * c, (h + 1) * c)
            sb = s_ref[h].astype(BF16)
            vn.append(u[rs] - _dot(wb[rs], sb))
            qs.append(_dot(p["qd"][rs], sb))
        vnb = jnp.concatenate(vn, axis=0).astype(BF16)
        o = jnp.concatenate(qs, axis=0) + _dot(p["attn"], vnb)
        zc = z_ref[pl.ds(p["r0"], c), :]
        for h in range(DN_HEADS):
            rs = slice(h * c, (h + 1) * c)
            sl = slice(h * HEAD_DIM, (h + 1) * HEAD_DIM)
            s_ref[h] = (s_ref[h] * jnp.exp(p["gbc"][c - 1:c, h:h + 1])
                        + _dot_tn(p["ke"][rs], vnb[rs]))
            zf = zc[:, sl].astype(F32)
            ob_ref[pl.ds(p["r0"], c), sl] = (_rms(o[rs], ng) * (zf * _sigmoid(zf))).astype(BF16)

    def group(ig, carry):
        ps = [prepare(ig * group_size + b) for b in range(group_size)]
        sol = [p["rhs"] for p in ps]
        pw = [-p["lmat"] for p in ps]
        for step in range(6):
            pb = [x.astype(BF16) for x in pw]
            sol = [s + _dot(xb, s.astype(BF16)) for s, xb in zip(sol, pb)]
            if step < 5:
                pw = [_dot(xb, xb) for xb in pb]
        for p, s in zip(ps, sol):
            advance_state(p, s)
        return carry

    lax.fori_loop(0, nchunk // group_size, group, 0)


def _gdn(q, k, v, z, gb, grow, ng, *, batch, seq):
    tm = ROW_TILE
    nchunk = tm // DN_CHUNK
    steps = seq // tm
    rows = lambda b, j: (b * steps + j, 0)
    return pl.pallas_call(
        functools.partial(_gdn_body, nchunk=nchunk, group_size=GDN_GROUP),
        grid=(batch, steps),
        in_specs=[pl.BlockSpec((tm, DN_KEY), rows), pl.BlockSpec((tm, DN_KEY), rows),
                  pl.BlockSpec((tm, DN_VAL), rows), pl.BlockSpec((tm, DN_VAL), rows),
                  pl.BlockSpec((tm, LANES), rows),
                  pl.BlockSpec((nchunk, DN_HEADS * DN_CHUNK), rows),
                  pl.BlockSpec((1, HEAD_DIM), lambda b, j: (0, 0))],
        out_specs=pl.BlockSpec((tm, DN_VAL), rows),
        out_shape=jax.ShapeDtypeStruct((batch * seq, DN_VAL), BF16),
        scratch_shapes=[pltpu.VMEM((DN_HEADS, DN_DK, HEAD_DIM), F32)],
        compiler_params=pltpu.CompilerParams(dimension_semantics=("arbitrary", "arbitrary"),
                                             vmem_limit_bytes=VMEM_LIMIT),
        name="gdn",
    )(q, k, v, z, gb, grow, ng)


def _outproj_body(oa_ref, ob_ref, x_ref, woa_ref, wob_ref, ng_ref, rw_ref, rb_ref, tri_ref,
                  h_ref, hp_ref, meta_ref, gate_ref, cnt_ref, run_ref, *, tm):
    i = pl.program_id(0)

    @pl.when(i == 0)
    def _():
        run_ref[...] = jnp.zeros(run_ref.shape, F32)

    h = x_ref[...] + _dot(oa_ref[...], woa_ref[...]) + _dot(ob_ref[...], wob_ref[...])
    h_ref[...] = h
    hb = _rms(h, ng_ref[...]).astype(BF16)
    half = D_MODEL // 2
    lo = pltpu.bitcast(hb[:, :half].astype(F32), U32) >> 16
    hi = pltpu.bitcast(hb[:, half:].astype(F32), U32) & jnp.uint32(0xFFFF0000)
    hp_ref[...] = lo | hi

    logits = _dot(hb, rw_ref[...]) + rb_ref[...]
    lane = lax.broadcasted_iota(I32, (tm, LANES), 1)
    lanef = lane.astype(F32)
    work = logits
    onehot = jnp.zeros((tm, LANES), F32)
    vals, sels = [], []
    for _ in range(TOP_K):
        m = jnp.max(work, axis=-1, keepdims=True)
        idx = jnp.min(jnp.where(work == m, lanef, float(LANES)), axis=-1, keepdims=True)
        sel = lanef == idx
        work = jnp.where(sel, -3e38, work)
        onehot = onehot + jnp.where(sel, 1.0, 0.0)
        vals.append(m)
        sels.append((sel, idx))
    ex = [jnp.exp(v - vals[0]) for v in vals]
    den = ex[0] + ex[1] + ex[2] + ex[3]
    pref = _dot(tri_ref[...], onehot.astype(BF16)) + run_ref[0:1, :]
    meta = jnp.zeros((tm, LANES), F32)
    gates = jnp.zeros((tm, LANES), F32)
    for kk in range(TOP_K):
        sel, idx = sels[kk]
        rank = jnp.sum(jnp.where(sel, pref, 0.0), axis=-1, keepdims=True)
        meta = meta + jnp.where(lane == kk, idx, 0.0) + jnp.where(lane == TOP_K + kk, rank, 0.0)
        gates = gates + jnp.where(lane == kk, ex[kk] / den, 0.0)
    meta_ref[...] = meta.astype(I32)
    gate_ref[...] = gates
    run = run_ref[...] + jnp.sum(onehot, axis=0, keepdims=True)
    run_ref[...] = run
    cnt_ref[...] = run.astype(I32)


def _outproj(oa, ob, x2, woa, wob, ng, rw, rb, tri):
    t = x2.shape[0]
    tm = ROW_TILE
    row = lambda i: (i, 0)
    full = lambda a: pl.BlockSpec(a.shape, (lambda i: (0,) * a.ndim))
    out_shapes = (
        jax.ShapeDtypeStruct((t, D_MODEL), F32),
        jax.ShapeDtypeStruct((t, D_MODEL // 2), U32),
        jax.ShapeDtypeStruct((t, LANES), I32),
        jax.ShapeDtypeStruct((t, LANES), F32),
        jax.ShapeDtypeStruct((8, LANES), I32),
    )
    return pl.pallas_call(
        functools.partial(_outproj_body, tm=tm),
        grid=(t // tm,),
        in_specs=[pl.BlockSpec((tm, GM_WIDTH), row), pl.BlockSpec((tm, DN_VAL), row),
                  pl.BlockSpec((tm, D_MODEL), row), full(woa), full(wob), full(ng), full(rw),
                  full(rb), full(tri)],
        out_specs=(pl.BlockSpec((tm, D_MODEL), row), pl.BlockSpec((tm, D_MODEL // 2), row),
                   pl.BlockSpec((tm, LANES), row), pl.BlockSpec((tm, LANES), row),
                   pl.BlockSpec((8, LANES), lambda i: (0, 0))),
        out_shape=out_shapes,
        scratch_shapes=[pltpu.VMEM((8, LANES), F32)],
        compiler_params=pltpu.CompilerParams(dimension_semantics=("arbitrary",),
                                             vmem_limit_bytes=VMEM_LIMIT),
        name="outproj",
    )(oa, ob, x2, woa, wob, ng, rw, rb, tri)


def _sc_dispatch(hp, dest_k, n_rows):
    t, d = hp.shape
    mesh = plsc.VectorSubcoreMesh(core_axis_name="c", subcore_axis_name="s")
    nc, workers = mesh.num_cores, mesh.num_cores * mesh.num_subcores
    chunk = SC_GATHER_ROWS
    per_w = t // workers
    pairs = per_w // (2 * chunk)
    assert per_w * workers == t and pairs * 2 * chunk == per_w

    @functools.partial(
        pl.kernel, mesh=mesh, out_type=jax.ShapeDtypeStruct((n_rows, d), hp.dtype),
        scratch_types=[pltpu.VMEM((2, TOP_K, chunk), I32), pltpu.VMEM((2, chunk, d), hp.dtype),
                       pltpu.SemaphoreType.DMA((2,)), pltpu.SemaphoreType.DMA((2,))],
        name="sc_dispatch")
    def scatter(hp_hbm, idx_hbm, xin_hbm, idx_v, rows_v, lsem, ssem):
        base_w = (lax.axis_index("s") * nc + lax.axis_index("c")) * per_w

        def load(b, base):
            return pltpu.make_async_copy(hp_hbm.at[pl.ds(base, chunk)], rows_v.at[b], lsem.at[b])

        def put(b, kk):
            return pltpu.make_async_copy(rows_v.at[b], xin_hbm.at[idx_v.at[b, kk]], ssem.at[b])

        def body(j, carry):
            for b in range(2):
                base = base_w + (2 * j + b) * chunk

                @pl.when(j > 0)
                def _():
                    for kk in range(TOP_K):
                        put(b, kk).wait()

                for kk in range(TOP_K):
                    pltpu.sync_copy(idx_hbm.at[pl.ds(kk * t + base, chunk)], idx_v.at[b, kk])
                load(b, base).start()
            for b in range(2):
                load(b, base_w).wait()
                for kk in range(TOP_K):
                    put(b, kk).start()
            return carry

        lax.fori_loop(0, pairs, body, 0)
        for b in range(2):
            for kk in range(TOP_K):
                put(b, kk).wait()

    return scatter(hp, dest_k)


def _ffn_body(be_ref, nv_ref, br_ref, x_ref, wgu_ref, bgu_ref, wd_ref, bd_ref, y_ref, wgub_ref,
              wdb_ref):
    i = pl.program_id(0)

    @pl.when(i >= nv_ref[0])
    def _():
        y_ref[...] = jnp.zeros(y_ref.shape, U32)

    @pl.when((i < nv_ref[0]) & ((i == 0) | (be_ref[i] != be_ref[jnp.maximum(i - 1, 0)])))
    def _():
        wgub_ref[...] = wgu_ref[0].astype(BF16)
        wdb_ref[...] = wd_ref[0].astype(BF16)

    @pl.when(i < nv_ref[0])
    def _():
        half = D_MODEL // 2
        nsub = FFN_SUBBLOCKS
        rsub = x_ref.shape[0] // nsub
        row = lax.broadcasted_iota(I32, (rsub, 1), 0)
        gus = []
        for sb in range(nsub):
            rs = slice(sb * rsub, (sb + 1) * rsub)
            xp = jnp.where(row + sb * rsub < br_ref[i], x_ref[rs, :], jnp.uint32(0))
            lo = pltpu.bitcast(xp << 16, F32).astype(BF16)
            hi = pltpu.bitcast(xp & jnp.uint32(0xFFFF0000), F32).astype(BF16)
            gus.append(_dot(lo, wgub_ref[:half, :]) + _dot(hi, wgub_ref[half:, :]) + bgu_ref[0])
        ys = []
        for gu in gus:
            gate = jnp.minimum(gu[:, :D_FF], SWIGLU_LIMIT)
            up = jnp.clip(gu[:, D_FF:], -SWIGLU_LIMIT, SWIGLU_LIMIT)
            act = (up + 1.0) * (gate * _sigmoid(SWIGLU_ALPHA * gate))
            ys.append(_dot(act.astype(BF16), wdb_ref[...]) + bd_ref[0])
        for sb, y in enumerate(ys):
            ylo = pltpu.bitcast(y[:, :half].astype(BF16).astype(F32), U32) >> 16
            yhi = pltpu.bitcast(y[:, half:].astype(BF16).astype(F32), U32) & jnp.uint32(0xFFFF0000)
            y_ref[sb * rsub:(sb + 1) * rsub, :] = ylo | yhi


def _ffn(blk_e, nvalid, blk_rows, xin, wgu, bgu, wd, bd):
    p = xin.shape[0]
    bm = FFN_BLOCK
    rows = lambda i, be, nv, br: (jnp.minimum(i, nv[0] - 1), 0)
    wsel = lambda i, be, nv, br: (be[i], 0, 0)
    return pl.pallas_call(
        _ffn_body,
        grid_spec=pltpu.PrefetchScalarGridSpec(
            num_scalar_prefetch=3,
            grid=(p // bm,),
            in_specs=[pl.BlockSpec((bm, D_MODEL // 2), rows),
                      pl.BlockSpec((1, D_MODEL, 2 * D_FF), wsel),
                      pl.BlockSpec((1, 1, 2 * D_FF), wsel),
                      pl.BlockSpec((1, D_FF, D_MODEL), wsel),
                      pl.BlockSpec((1, 1, D_MODEL), wsel)],
            out_specs=pl.BlockSpec((bm, D_MODEL // 2), lambda i, be, nv, br: (i, 0)),
            scratch_shapes=[pltpu.VMEM((D_MODEL, 2 * D_FF), BF16), pltpu.VMEM((D_FF, D_MODEL), BF16)],
        ),
        out_shape=jax.ShapeDtypeStruct((p, D_MODEL // 2), U32),
        compiler_params=pltpu.CompilerParams(dimension_semantics=("arbitrary",),
                                             vmem_limit_bytes=VMEM_LIMIT),
        name="ffn",
    )(blk_e, nvalid, blk_rows, xin, wgu, bgu, wd, bd)


def _sc_gather(table, idx):
    r, d = idx.shape[0], table.shape[1]
    mesh = plsc.VectorSubcoreMesh(core_axis_name="c", subcore_axis_name="s")
    nc, workers = mesh.num_cores, mesh.num_cores * mesh.num_subcores
    chunk = SC_GATHER_ROWS
    per_w = r // workers
    pairs = per_w // (2 * chunk)
    assert per_w * workers == r and pairs * 2 * chunk == per_w

    @functools.partial(
        pl.kernel, mesh=mesh, out_type=jax.ShapeDtypeStruct((r, d), table.dtype),
        scratch_types=[pltpu.VMEM((2, chunk), I32), pltpu.VMEM((2, chunk, d), table.dtype),
                       pltpu.SemaphoreType.DMA((2,)), pltpu.SemaphoreType.DMA((2,))],
        name="sc_gather")
    def gather(table_hbm, idx_hbm, out_hbm, idx_v, rows_v, gsem, wsem):
        base_w = (lax.axis_index("s") * nc + lax.axis_index("c")) * per_w

        def fetch(b):
            return pltpu.make_async_copy(table_hbm.at[idx_v.at[b]], rows_v.at[b], gsem.at[b])

        def flush(b, base):
            return pltpu.make_async_copy(rows_v.at[b], out_hbm.at[pl.ds(base, chunk)], wsem.at[b])

        def body(j, carry):
            for b in range(2):
                base = base_w + (2 * j + b) * chunk

                @pl.when(j > 0)
                def _():
                    flush(b, base).wait()

                pltpu.sync_copy(idx_hbm.at[pl.ds(base, chunk)], idx_v.at[b])
                fetch(b).start()
            for b in range(2):
                fetch(b).wait()
                flush(b, base_w + (2 * j + b) * chunk).start()
            return carry

        lax.fori_loop(0, pairs, body, 0)
        for b in range(2):
            flush(b, base_w).wait()

    return gather(table, idx)


def _combine_body(h_ref, gate_ref, fg_ref, *refs):
    yg_refs, o_ref = refs[:TOP_K], refs[-1]
    half = D_MODEL // 2
    gates = gate_ref[...]
    h = h_ref[...]
    lo, hi = h[:, :half], h[:, half:]
    for kk in range(TOP_K):
        yp = yg_refs[kk][...]
        g = gates[:, kk:kk + 1]
        lo = lo + g * pltpu.bitcast(yp << 16, F32)
        hi = hi + g * pltpu.bitcast(yp & jnp.uint32(0xFFFF0000), F32)
    out = jnp.concatenate([lo, hi], axis=1)
    o_ref[...] = _rms(out, fg_ref[...])


def _combine(h, gates, fg, ygath, out_prev, *, part, parts):
    t = h.shape[0]
    tm = COMBINE_TILE
    nt = t // parts // tm
    row = lambda i: (part * nt + i, 0)
    slot = lambda kk: pl.BlockSpec((tm, D_MODEL // 2), lambda i: (kk * nt + i, 0))
    in_specs = [pl.BlockSpec((tm, D_MODEL), row), pl.BlockSpec((tm, LANES), row),
                pl.BlockSpec((1, D_MODEL), lambda i: (0, 0))] + [slot(kk) for kk in range(TOP_K)]
    args = [h, gates, fg] + [ygath] * TOP_K
    aliases = {}
    if out_prev is not None:
        aliases = {len(args): 0}
        in_specs.append(pl.BlockSpec(memory_space=pl.ANY))
        args.append(out_prev)
    return pl.pallas_call(
        _combine_body,
        grid=(nt,),
        in_specs=in_specs,
        out_specs=pl.BlockSpec((tm, D_MODEL), row),
        out_shape=jax.ShapeDtypeStruct((t, D_MODEL), F32),
        input_output_aliases=aliases,
        compiler_params=pltpu.CompilerParams(dimension_semantics=("arbitrary",),
                                             vmem_limit_bytes=VMEM_LIMIT),
        name="combine",
    )(*args)


def _block_tril(n, chunk, strict):
    r = jnp.arange(n)[:, None]
    c = jnp.arange(n)[None, :]
    keep = ((r // chunk) == (c // chunk)) & ((r > c) if strict else (r >= c))
    return keep.astype(BF16)


def _pad_lanes(a, fill=0.0):
    a = a.reshape(1, -1).astype(F32)
    return jnp.pad(a, ((0, 0), (0, LANES - a.shape[1])), constant_values=fill)


def _layer(h, norm_mix_g, w_in, gm_norm_g, gm_ws, gm_bs, dn_conv_w, dn_a_log, dn_dt_bias,
           dn_norm_g, w_out, norm_ffn_g, router_w, router_b, exp_w_gu, exp_b_gu, exp_w_down,
           exp_b_down, out_g):
    batch, seq, d = h.shape
    t = batch * seq
    x2 = h.reshape(t, d)

    c0, c1, c2 = GM_WIDTH, 2 * GM_WIDTH, 2 * GM_WIDTH + 3 * DN_KEY
    c3 = c2 + DN_VAL
    wb = w_in.astype(BF16)
    wu, wv, wqkv, wz = wb[:, :c0], wb[:, c0:c1], wb[:, c1:c2], wb[:, c2:c3]
    wab = jnp.pad(wb[:, c3:], ((0, 0), (0, LANES - 2 * DN_HEADS)))
    gmg = gm_norm_g.reshape(1, GM_WIDTH).astype(F32)
    bsb = jnp.repeat(gm_bs.T, HEAD_DIM, axis=1).astype(F32)
    alog = _pad_lanes(dn_a_log)
    dtb = _pad_lanes(dn_dt_bias)
    tri_incl = _block_tril(ROW_TILE, DN_CHUNK, strict=False)

    oa, q, k, v, z, gb, gt = _inproj(
        x2, norm_mix_g.reshape(1, d), wu, wv, wqkv, wz, wab, gmg, gm_ws.astype(F32), bsb,
        dn_conv_w.astype(F32), alog, dtb, tri_incl, seq=seq)

    grow = gt[:DN_HEADS].reshape(DN_HEADS, t // DN_CHUNK, DN_CHUNK).transpose(1, 0, 2)
    grow = grow.reshape(t // DN_CHUNK, DN_HEADS * DN_CHUNK)
    ob = _gdn(q, k, v, z, gb, grow, dn_norm_g.reshape(1, HEAD_DIM).astype(F32), batch=batch, seq=seq)

    wo = w_out.astype(BF16)
    rw = jnp.pad(router_w.astype(BF16), ((0, 0), (0, LANES - N_EXPERTS)))
    rb = _pad_lanes(router_b, fill=NEG_BIG)
    tri_strict = _block_tril(ROW_TILE, ROW_TILE, strict=True)
    hres, hp, meta, gates, cnt = _outproj(oa, ob, x2, wo[:GM_WIDTH], wo[GM_WIDTH:],
                                          norm_ffn_g.reshape(1, d), rw, rb, tri_strict)

    bm = FFN_BLOCK
    counts = cnt[0, :N_EXPERTS]
    padded = (counts + bm - 1) // bm * bm
    pad_end = jnp.cumsum(padded)
    pad_start = (pad_end - padded).astype(I32)
    n_blocks = (t * TOP_K + N_EXPERTS * bm) // bm
    nvalid = (pad_end[-1] // bm).astype(I32).reshape(1)
    blk = jnp.minimum(jnp.arange(n_blocks, dtype=I32), nvalid[0] - 1)
    blk_e = jnp.minimum(jnp.sum(pad_end[None, :] <= (blk * bm)[:, None], axis=1), N_EXPERTS - 1).astype(I32)
    eid = meta[:, :TOP_K]
    start_of = jnp.sum(jnp.where(eid[..., None] == jnp.arange(N_EXPERTS, dtype=I32), pad_start, 0), axis=-1)
    dest = start_of + meta[:, TOP_K:2 * TOP_K]
    dest_k = dest.T.reshape(-1)

    xin = _sc_dispatch(hp, dest_k, n_blocks * bm)
    blk_rows = jnp.clip(jnp.take(counts + pad_start, blk_e) - blk * bm, 0, bm).astype(I32)
    y = _ffn(blk_e, nvalid, blk_rows, xin, exp_w_gu, exp_b_gu[:, None, :].astype(F32),
             exp_w_down, exp_b_down[:, None, :].astype(F32))
    out = None
    tp = t // COMBINE_PARTS
    for part in range(COMBINE_PARTS):
        ygath = _sc_gather(y, dest[part * tp:(part + 1) * tp].T.reshape(-1))
        out = _combine(hres, gates, out_g.reshape(1, d).astype(F32), ygath, out,
                       part=part, parts=COMBINE_PARTS)
    return out.reshape(batch, seq, d)


def kernel(x, norm_mix_g, w_in, gm_norm_g, gm_ws, gm_bs, dn_conv_w, dn_a_log, dn_dt_bias, dn_norm_g, w_out, norm_ffn_g, router_w, router_b, exp_w_gu, exp_b_gu, exp_w_down, exp_b_down, final_norm_g):
    depth = norm_mix_g.shape[0]
    assert depth == 1, "single-layer problem"
    return _layer(x, norm_mix_g[0], w_in[0], gm_norm_g[0], gm_ws[0], gm_bs[0], dn_conv_w[0],
                  dn_a_log[0], dn_dt_bias[0], dn_norm_g[0], w_out[0], norm_ffn_g[0], router_w[0],
                  router_b[0], exp_w_gu[0], exp_b_gu[0], exp_w_down[0], exp_b_down[0], final_norm_g)
```

```python
import functools

import jax
import jax.numpy as jnp
from jax import lax
from jax.experimental import pallas as pl
from jax.experimental.pallas import tpu as pltpu
from jax.experimental.pallas import tpu_sc as plsc

F32 = jnp.float32
BF16 = jnp.bfloat16
I32 = jnp.int32
U32 = jnp.uint32

D_MODEL = 1024
HEAD_DIM = 128
GM_HEADS = 4
GM_WIDTH = GM_HEADS * HEAD_DIM
GM_CHUNK = 128
DN_HEADS = 4
DN_DK = 128
DN_KEY = DN_HEADS * DN_DK
DN_VAL = DN_HEADS * HEAD_DIM
DN_CHUNK = 64
CONV_W = 4
N_EXPERTS = 32
TOP_K = 4
D_FF = D_MODEL
SWIGLU_LIMIT = 7.0
SWIGLU_ALPHA = 1.702
EPS = 1e-6

LANES = 128
INPROJ_COLS = 512
SUBLANES = 8
ROW_TILE = 512
FFN_BLOCK = 512
FFN_SUBBLOCKS = 2
SC_GATHER_ROWS = 64
COMBINE_PARTS = 4
COMBINE_TILE = 256
VMEM_LIMIT = 56 * 1024 * 1024
NEG_BIG = -1e30
GDN_GROUP = 4


def _dot(a, b):
    return jnp.dot(a, b, preferred_element_type=F32)


def _dot_nt(a, b):
    return lax.dot_general(a, b, (((1,), (1,)), ((), ())), preferred_element_type=F32)


def _dot_tn(a, b):
    return lax.dot_general(a, b, (((0,), (0,)), ((), ())), preferred_element_type=F32)


def _rms(x, g):
    return x * lax.rsqrt(jnp.mean(x * x, axis=-1, keepdims=True) + EPS) * g


def _gelu(x):
    return 0.5 * x * (1.0 + lax.erf(x * (2.0 ** -0.5)))


def _sigmoid(x):
    return 1.0 / (1.0 + jnp.exp(-x))


def _softplus(x):
    return jnp.maximum(x, 0.0) + jnp.log1p(jnp.exp(-jnp.abs(x)))


def _inproj_body(x_ref, ng_ref, wu_ref, wv_ref, wqkv_ref, wz_ref, wab_ref, gmg_ref, ws_ref,
                 bs_ref, cw_ref, alog_ref, dtb_ref, tri_ref,
                 oa_ref, q_ref, k_ref, v_ref, z_ref, gb_ref, gt_ref, cbuf_ref, ybuf_ref,
                 *, tm, tiles_per_seq):
    i = pl.program_id(0)
    ybuf_ref[...] = _rms(x_ref[...], ng_ref[...]).astype(BF16)
    nb = INPROJ_COLS
    heads_per_block = nb // HEAD_DIM

    ri = lax.broadcasted_iota(I32, (GM_CHUNK, GM_CHUNK), 0)
    ci = lax.broadcasted_iota(I32, (GM_CHUNK, GM_CHUNK), 1)
    causal = ri >= ci

    def gmlp_matmul(blk):
        cs = slice(blk * nb, (blk + 1) * nb)
        return _dot(ybuf_ref[...], wu_ref[:, cs]), _dot(ybuf_ref[...], wv_ref[:, cs])

    def gmlp_epilogue(blk, uv):
        u, vg = _gelu(uv[0]), _gelu(uv[1])
        for hh in range(heads_per_block):
            h = blk * heads_per_block + hh
            sl = slice(h * HEAD_DIM, (h + 1) * HEAD_DIM)
            ls = slice(hh * HEAD_DIM, (hh + 1) * HEAD_DIM)
            vh = _rms(vg[:, ls], gmg_ref[:, sl]).astype(BF16)
            wm = jnp.where(causal, ws_ref[h], 0.0).astype(BF16)
            for c in range(tm // GM_CHUNK):
                rs = slice(c * GM_CHUNK, (c + 1) * GM_CHUNK)
                gate = _dot(wm, vh[rs]) + bs_ref[:, sl]
                oa_ref[rs, sl] = (u[rs, ls] * gate).astype(BF16)

    def qkv_matmul(blk):
        return _dot(ybuf_ref[...], wqkv_ref[:, blk * nb:(blk + 1) * nb])

    def qkv_epilogue(blk, pq):
        cs = slice(blk * nb, (blk + 1) * nb)
        cbuf_ref[8:8 + tm, cs] = pq
        acc = cw_ref[CONV_W - 1:CONV_W, cs] * pq
        for j in range(CONV_W - 1):
            off = 8 - (CONV_W - 1) + j
            acc = acc + cw_ref[j:j + 1, cs] * cbuf_ref[off:off + tm, cs]
        cbuf_ref[0:8, cs] = pq[tm - 8:tm, :]
        s = acc * _sigmoid(acc)
        for hh in range(heads_per_block):
            c0 = blk * nb + hh * HEAD_DIM
            sh = s[:, hh * HEAD_DIM:(hh + 1) * HEAD_DIM]
            if c0 < DN_KEY:
                q_ref[:, c0:c0 + HEAD_DIM] = (
                    sh * lax.rsqrt(jnp.sum(sh * sh, axis=-1, keepdims=True) + EPS)
                    * (DN_DK ** -0.5)).astype(BF16)
            elif c0 < 2 * DN_KEY:
                k_ref[:, c0 - DN_KEY:c0 - DN_KEY + HEAD_DIM] = (
                    sh * lax.rsqrt(jnp.sum(sh * sh, axis=-1, keepdims=True) + EPS)).astype(BF16)
            else:
                v_ref[:, c0 - 2 * DN_KEY:c0 - 2 * DN_KEY + HEAD_DIM] = sh.astype(BF16)

    def z_matmul(blk):
        return _dot(ybuf_ref[...], wz_ref[:, blk * nb:(blk + 1) * nb])

    def z_epilogue(blk, zz):
        z_ref[:, blk * nb:(blk + 1) * nb] = zz.astype(BF16)

    @pl.when(i % tiles_per_seq == 0)
    def _():
        cbuf_ref[0:8, :] = jnp.zeros((8, 3 * DN_KEY), F32)

    stages = ([(gmlp_matmul, gmlp_epilogue, blk) for blk in range(GM_WIDTH // nb)]
              + [(qkv_matmul, qkv_epilogue, blk) for blk in range(3 * DN_KEY // nb)]
              + [(z_matmul, z_epilogue, blk) for blk in range(DN_VAL // nb)])
    for matmul, epilogue, blk in stages:
        epilogue(blk, matmul(blk))

    ab = _dot(ybuf_ref[...], wab_ref[...])
    g = -jnp.exp(alog_ref[...]) * _softplus(ab + dtb_ref[...])
    beta = _sigmoid(ab)
    g_hi = g.astype(BF16)
    r1 = g - g_hi.astype(F32)
    g_mid = r1.astype(BF16)
    g_lo = (r1 - g_mid.astype(F32)).astype(BF16)
    tri = tri_ref[...]
    gc = _dot(tri, g_hi) + _dot(tri, g_mid) + _dot(tri, g_lo)
    lane = lax.broadcasted_iota(I32, (tm, LANES), 1)
    gbv = jnp.where(lane < DN_HEADS, gc, beta)
    gb_ref[...] = gbv
    gt_ref[...] = gbv.T[0:8, :]


def _inproj(x2, ng, wu, wv, wqkv, wz, wab, gmg, ws, bsb, cw, alog, dtb, tri, *, seq):
    t = x2.shape[0]
    tm = ROW_TILE
    const2 = lambda i: (0, 0)
    row = lambda i: (i, 0)
    full = lambda a: pl.BlockSpec(a.shape, (lambda i: (0,) * a.ndim))
    out_shapes = (
        jax.ShapeDtypeStruct((t, GM_WIDTH), BF16),
        jax.ShapeDtypeStruct((t, DN_KEY), BF16),
        jax.ShapeDtypeStruct((t, DN_KEY), BF16),
        jax.ShapeDtypeStruct((t, DN_VAL), BF16),
        jax.ShapeDtypeStruct((t, DN_VAL), BF16),
        jax.ShapeDtypeStruct((t, LANES), F32),
        jax.ShapeDtypeStruct((8, t), F32),
    )
    return pl.pallas_call(
        functools.partial(_inproj_body, tm=tm, tiles_per_seq=seq // tm),
        grid=(t // tm,),
        in_specs=[pl.BlockSpec((tm, D_MODEL), row), full(ng), full(wu), full(wv), full(wqkv),
                  full(wz), full(wab), full(gmg), full(ws), full(bsb), full(cw), full(alog),
                  full(dtb), full(tri)],
        out_specs=(pl.BlockSpec((tm, GM_WIDTH), row), pl.BlockSpec((tm, DN_KEY), row),
                   pl.BlockSpec((tm, DN_KEY), row), pl.BlockSpec((tm, DN_VAL), row),
                   pl.BlockSpec((tm, DN_VAL), row), pl.BlockSpec((tm, LANES), row),
                   pl.BlockSpec((8, tm), lambda i: (0, i))),
        out_shape=out_shapes,
        scratch_shapes=[pltpu.VMEM((tm + 8, 3 * DN_KEY), F32), pltpu.VMEM((tm, D_MODEL), BF16)],
        compiler_params=pltpu.CompilerParams(dimension_semantics=("arbitrary",),
                                             vmem_limit_bytes=VMEM_LIMIT),
        name="inproj",
    )(x2, ng, wu, wv, wqkv, wz, wab, gmg, ws, bsb, cw, alog, dtb, tri)


def _gdn_body(q_ref, k_ref, v_ref, z_ref, gb_ref, gr_ref, ng_ref, ob_ref, s_ref, *, nchunk,
              group_size):
    j = pl.program_id(1)

    @pl.when(j == 0)
    def _():
        s_ref[...] = jnp.zeros(s_ref.shape, F32)

    c = DN_CHUNK
    n = DN_HEADS * c
    ri = lax.broadcasted_iota(I32, (n, n), 0)
    ci = lax.broadcasted_iota(I32, (n, n), 1)
    same = (ri // c) == (ci // c)
    incl = same & ((ri % c) >= (ci % c))
    strict = same & ((ri % c) > (ci % c))
    ng = ng_ref[...]

    def stack(a):
        return jnp.concatenate([a[:, h * HEAD_DIM:(h + 1) * HEAD_DIM] for h in range(DN_HEADS)], axis=0)

    def prepare(ic):
        r0 = pl.multiple_of(ic * c, c)
        kst = stack(k_ref[pl.ds(r0, c), :])
        qst = stack(q_ref[pl.ds(r0, c), :])
        vst = stack(v_ref[pl.ds(r0, c), :])
        gbc = gb_ref[pl.ds(r0, c), :]
        grow = gr_ref[pl.ds(ic, 1), :]
        gcol = jnp.concatenate([gbc[:, h:h + 1] for h in range(DN_HEADS)], axis=0)
        bcol = jnp.concatenate([gbc[:, DN_HEADS + h:DN_HEADS + h + 1] for h in range(DN_HEADS)], axis=0)
        glast = jnp.concatenate(
            [jnp.broadcast_to(gbc[c - 1:c, h:h + 1], (c, 1)) for h in range(DN_HEADS)], axis=0)
        decay = jnp.where(incl, jnp.exp(jnp.where(incl, gcol - grow, 0.0)), 0.0)
        kf = kst.astype(F32)
        kb = kf * bcol
        lmat = jnp.where(strict, _dot_nt(kb.astype(BF16), kst) * decay, 0.0)
        eg = jnp.exp(gcol)
        rhs = jnp.concatenate([vst.astype(F32) * bcol, kb * eg], axis=1)
        attn = jnp.where(incl, _dot_nt(qst, kst) * decay, 0.0).astype(BF16)
        qd = (qst.astype(F32) * eg).astype(BF16)
        ke = (kf * jnp.exp(glast - gcol)).astype(BF16)
        return dict(r0=r0, gbc=gbc, lmat=lmat, rhs=rhs, attn=attn, qd=qd, ke=ke)

    def advance_state(p, sol):
        u = sol[:, :HEAD_DIM]
        wb = sol[:, HEAD_DIM:].astype(BF16)
        vn, qs = [], []
        for h in range(DN_HEADS):
            rs = slice(h * c, (h + 1) * c)
            sb = s_ref[h].astype(BF16)
            vn.append(u[rs] - _dot(wb[rs], sb))
            qs.append(_dot(p["qd"][rs], sb))
        vnb = jnp.concatenate(vn, axis=0).astype(BF16)
        o = jnp.concatenate(qs, axis=0) + _dot(p["attn"], vnb)
        zc = z_ref[pl.ds(p["r0"], c), :]
        for h in range(DN_HEADS):
            rs = slice(h * c, (h + 1) * c)
            sl = slice(h * HEAD_DIM, (h + 1) * HEAD_DIM)
            s_ref[h] = (s_ref[h] * jnp.exp(p["gbc"][c - 1:c, h:h + 1])
                        + _dot_tn(p["ke"][rs], vnb[rs]))
            zf = zc[:, sl].astype(F32)
            ob_ref[pl.ds(p["r0"], c), sl] = (_rms(o[rs], ng) * (zf * _sigmoid(zf))).astype(BF16)

    def group(ig, carry):
        ps = [prepare(ig * group_size + b) for b in range(group_size)]
        sol = [p["rhs"] for p in ps]
        pw = [-p["lmat"] for p in ps]
        for step in range(6):
            pb = [x.astype(BF16) for x in pw]
            sol = [s + _dot(xb, s.astype(BF16)) for s, xb in zip(sol, pb)]
            if step < 5:
                pw = [_dot(xb, xb) for xb in pb]
        for p, s in zip(ps, sol):
            advance_state(p, s)
        return carry

    lax.fori_loop(0, nchunk // group_size, group, 0)


def _gdn(q, k, v, z, gb, grow, ng, *, batch, seq):
    tm = ROW_TILE
    nchunk = tm // DN_CHUNK
    steps = seq // tm
    rows = lambda b, j: (b * steps + j, 0)
    return pl.pallas_call(
        functools.partial(_gdn_body, nchunk=nchunk, group_size=GDN_GROUP),
        grid=(batch, steps),
        in_specs=[pl.BlockSpec((tm, DN_KEY), rows), pl.BlockSpec((tm, DN_KEY), rows),
                  pl.BlockSpec((tm, DN_VAL), rows), pl.BlockSpec((tm, DN_VAL), rows),
                  pl.BlockSpec((tm, LANES), rows),
                  pl.BlockSpec((nchunk, DN_HEADS * DN_CHUNK), rows),
                  pl.BlockSpec((1, HEAD_DIM), lambda b, j: (0, 0))],
        out_specs=pl.BlockSpec((tm, DN_VAL), rows),
        out_shape=jax.ShapeDtypeStruct((batch * seq, DN_VAL), BF16),
        scratch_shapes=[pltpu.VMEM((DN_HEADS, DN_DK, HEAD_DIM), F32)],
        compiler_params=pltpu.CompilerParams(dimension_semantics=("arbitrary", "arbitrary"),
                                             vmem_limit_bytes=VMEM_LIMIT),
        name="gdn",
    )(q, k, v, z, gb, grow, ng)


def _outproj_body(oa_ref, ob_ref, x_ref, woa_ref, wob_ref, ng_ref, rw_ref, rb_ref, tri_ref,
                  h_ref, hp_ref, meta_ref, gate_ref, cnt_ref, run_ref, *, tm):
    i = pl.program_id(0)

    @pl.when(i == 0)
    def _():
        run_ref[...] = jnp.zeros(run_ref.shape, F32)

    h = x_ref[...] + _dot(oa_ref[...], woa_ref[...]) + _dot(ob_ref[...], wob_ref[...])
    h_ref[...] = h
    hb = _rms(h, ng_ref[...]).astype(BF16)
    half = D_MODEL // 2
    lo = pltpu.bitcast(hb[:, :half].astype(F32), U32) >> 16
    hi = pltpu.bitcast(hb[:, half:].astype(F32), U32) & jnp.uint32(0xFFFF0000)
    hp_ref[...] = lo | hi

    logits = _dot(hb, rw_ref[...]) + rb_ref[...]
    lane = lax.broadcasted_iota(I32, (tm, LANES), 1)
    lanef = lane.astype(F32)
    work = logits
    onehot = jnp.zeros((tm, LANES), F32)
    vals, sels = [], []
    for _ in range(TOP_K):
        m = jnp.max(work, axis=-1, keepdims=True)
        idx = jnp.min(jnp.where(work == m, lanef, float(LANES)), axis=-1, keepdims=True)
        sel = lanef == idx
        work = jnp.where(sel, -3e38, work)
        onehot = onehot + jnp.where(sel, 1.0, 0.0)
        vals.append(m)
        sels.append((sel, idx))
    ex = [jnp.exp(v - vals[0]) for v in vals]
    den = ex[0] + ex[1] + ex[2] + ex[3]
    pref = _dot(tri_ref[...], onehot.astype(BF16)) + run_ref[0:1, :]
    meta = jnp.zeros((tm, LANES), F32)
    gates = jnp.zeros((tm, LANES), F32)
    for kk in range(TOP_K):
        sel, idx = sels[kk]
        rank = jnp.sum(jnp.where(sel, pref, 0.0), axis=-1, keepdims=True)
        meta = meta + jnp.where(lane == kk, idx, 0.0) + jnp.where(lane == TOP_K + kk, rank, 0.0)
        gates = gates + jnp.where(lane == kk, ex[kk] / den, 0.0)
    meta_ref[...] = meta.astype(I32)
    gate_ref[...] = gates
    run = run_ref[...] + jnp.sum(onehot, axis=0, keepdims=True)
    run_ref[...] = run
    cnt_ref[...] = run.astype(I32)


def _outproj(oa, ob, x2, woa, wob, ng, rw, rb, tri):
    t = x2.shape[0]
    tm = ROW_TILE
    row = lambda i: (i, 0)
    full = lambda a: pl.BlockSpec(a.shape, (lambda i: (0,) * a.ndim))
    out_shapes = (
        jax.ShapeDtypeStruct((t, D_MODEL), F32),
        jax.ShapeDtypeStruct((t, D_MODEL // 2), U32),
        jax.ShapeDtypeStruct((t, LANES), I32),
        jax.ShapeDtypeStruct((t, LANES), F32),
        jax.ShapeDtypeStruct((8, LANES), I32),
    )
    return pl.pallas_call(
        functools.partial(_outproj_body, tm=tm),
        grid=(t // tm,),
        in_specs=[pl.BlockSpec((tm, GM_WIDTH), row), pl.BlockSpec((tm, DN_VAL), row),
                  pl.BlockSpec((tm, D_MODEL), row), full(woa), full(wob), full(ng), full(rw),
                  full(rb), full(tri)],
        out_specs=(pl.BlockSpec((tm, D_MODEL), row), pl.BlockSpec((tm, D_MODEL // 2), row),
                   pl.BlockSpec((tm, LANES), row), pl.BlockSpec((tm, LANES), row),
                   pl.BlockSpec((8, LANES), lambda i: (0, 0))),
        out_shape=out_shapes,
        scratch_shapes=[pltpu.VMEM((8, LANES), F32)],
        compiler_params=pltpu.CompilerParams(dimension_semantics=("arbitrary",),
                                             vmem_limit_bytes=VMEM_LIMIT),
        name="outproj",
    )(oa, ob, x2, woa, wob, ng, rw, rb, tri)


def _sc_dispatch(hp, dest_k, n_rows):
    t, d = hp.shape
    mesh = plsc.VectorSubcoreMesh(core_axis_name="c", subcore_axis_name="s")
    nc, workers = mesh.num_cores, mesh.num_cores * mesh.num_subcores
    chunk = SC_GATHER_ROWS
    per_w = t // workers
    pairs = per_w // (2 * chunk)
    assert per_w * workers == t and pairs * 2 * chunk == per_w

    @functools.partial(
        pl.kernel, mesh=mesh, out_type=jax.ShapeDtypeStruct((n_rows, d), hp.dtype),
        scratch_types=[pltpu.VMEM((2, TOP_K, chunk), I32), pltpu.VMEM((2, chunk, d), hp.dtype),
                       pltpu.SemaphoreType.DMA((2,)), pltpu.SemaphoreType.DMA((2,))],
        name="sc_dispatch")
    def scatter(hp_hbm, idx_hbm, xin_hbm, idx_v, rows_v, lsem, ssem):
        base_w = (lax.axis_index("s") * nc + lax.axis_index("c")) * per_w

        def load(b, base):
            return pltpu.make_async_copy(hp_hbm.at[pl.ds(base, chunk)], rows_v.at[b], lsem.at[b])

        def put(b, kk):
            return pltpu.make_async_copy(rows_v.at[b], xin_hbm.at[idx_v.at[b, kk]], ssem.at[b])

        def body(j, carry):
            for b in range(2):
                base = base_w + (2 * j + b) * chunk

                @pl.when(j > 0)
                def _():
                    for kk in range(TOP_K):
                        put(b, kk).wait()

                for kk in range(TOP_K):
                    pltpu.sync_copy(idx_hbm.at[pl.ds(kk * t + base, chunk)], idx_v.at[b, kk])
                load(b, base).start()
            for b in range(2):
                load(b, base_w).wait()
                for kk in range(TOP_K):
                    put(b, kk).start()
            return carry

        lax.fori_loop(0, pairs, body, 0)
        for b in range(2):
            for kk in range(TOP_K):
                put(b, kk).wait()

    return scatter(hp, dest_k)


def _ffn_body(be_ref, nv_ref, br_ref, ep_ref, es_ref, x_ref, wgu_hbm, bgu_ref, wd_hbm, bd_ref, y_ref,
              wgub_ref, wdb_ref, wguf_ref, wdf_ref, gsem, dsem):
    i = pl.program_id(0)

    @pl.when(i >= nv_ref[0])
    def _():
        y_ref[...] = jnp.zeros(y_ref.shape, U32)

    def fetch(pos):
        e, slot = es_ref[pos], pos % 2
        return (pltpu.make_async_copy(wgu_hbm.at[e], wguf_ref.at[slot], gsem.at[slot]),
                pltpu.make_async_copy(wd_hbm.at[e], wdf_ref.at[slot], dsem.at[slot]))

    @pl.when((i < nv_ref[0]) & ((i == 0) | (be_ref[i] != be_ref[jnp.maximum(i - 1, 0)])))
    def _():
        pos = ep_ref[i]

        @pl.when(i == 0)
        def _():
            for cp in fetch(pos):
                cp.start()

        for cp in fetch(pos):
            cp.wait()

        @pl.when(pos + 1 < es_ref[N_EXPERTS])
        def _():
            for cp in fetch(pos + 1):
                cp.start()

        slot = pos % 2
        wgub_ref[...] = wguf_ref[slot].astype(BF16)
        wdb_ref[...] = wdf_ref[slot].astype(BF16)

    @pl.when(i < nv_ref[0])
    def _():
        half = D_MODEL // 2
        nsub = FFN_SUBBLOCKS
        rsub = x_ref.shape[0] // nsub
        row = lax.broadcasted_iota(I32, (rsub, 1), 0)
        gus = []
        for sb in range(nsub):
            rs = slice(sb * rsub, (sb + 1) * rsub)
            xp = jnp.where(row + sb * rsub < br_ref[i], x_ref[rs, :], jnp.uint32(0))
            lo = pltpu.bitcast(xp << 16, F32).astype(BF16)
            hi = pltpu.bitcast(xp & jnp.uint32(0xFFFF0000), F32).astype(BF16)
            gus.append(_dot(lo, wgub_ref[:half, :]) + _dot(hi, wgub_ref[half:, :]) + bgu_ref[0])
        ys = []
        for gu in gus:
            gate = jnp.minimum(gu[:, :D_FF], SWIGLU_LIMIT)
            up = jnp.clip(gu[:, D_FF:], -SWIGLU_LIMIT, SWIGLU_LIMIT)
            act = (up + 1.0) * (gate * _sigmoid(SWIGLU_ALPHA * gate))
            ys.append(_dot(act.astype(BF16), wdb_ref[...]) + bd_ref[0])
        for sb, y in enumerate(ys):
            ylo = pltpu.bitcast(y[:, :half].astype(BF16).astype(F32), U32) >> 16
            yhi = pltpu.bitcast(y[:, half:].astype(BF16).astype(F32), U32) & jnp.uint32(0xFFFF0000)
            y_ref[sb * rsub:(sb + 1) * rsub, :] = ylo | yhi


def _ffn(blk_e, nvalid, blk_rows, blk_pos, used, xin, wgu, bgu, wd, bd):
    p = xin.shape[0]
    bm = FFN_BLOCK
    rows = lambda i, be, nv, br, ep, es: (jnp.minimum(i, nv[0] - 1), 0)
    wsel = lambda i, be, nv, br, ep, es: (be[i], 0, 0)
    return pl.pallas_call(
        _ffn_body,
        grid_spec=pltpu.PrefetchScalarGridSpec(
            num_scalar_prefetch=5,
            grid=(p // bm,),
            in_specs=[pl.BlockSpec((bm, D_MODEL // 2), rows),
                      pl.BlockSpec(memory_space=pl.ANY),
                      pl.BlockSpec((1, 1, 2 * D_FF), wsel),
                      pl.BlockSpec(memory_space=pl.ANY),
                      pl.BlockSpec((1, 1, D_MODEL), wsel)],
            out_specs=pl.BlockSpec((bm, D_MODEL // 2), lambda i, be, nv, br, ep, es: (i, 0)),
            scratch_shapes=[pltpu.VMEM((D_MODEL, 2 * D_FF), BF16), pltpu.VMEM((D_FF, D_MODEL), BF16),
                            pltpu.VMEM((2, D_MODEL, 2 * D_FF), F32), pltpu.VMEM((2, D_FF, D_MODEL), F32),
                            pltpu.SemaphoreType.DMA((2,)), pltpu.SemaphoreType.DMA((2,))],
        ),
        out_shape=jax.ShapeDtypeStruct((p, D_MODEL // 2), U32),
        compiler_params=pltpu.CompilerParams(dimension_semantics=("arbitrary",),
                                             vmem_limit_bytes=VMEM_LIMIT),
        name="ffn",
    )(blk_e, nvalid, blk_rows, blk_pos, used, xin, wgu, bgu, wd, bd)


def _sc_gather(table, idx):
    r, d = idx.shape[0], table.shape[1]
    mesh = plsc.VectorSubcoreMesh(core_axis_name="c", subcore_axis_name="s")
    nc, workers = mesh.num_cores, mesh.num_cores * mesh.num_subcores
    chunk = SC_GATHER_ROWS
    per_w = r // workers
    pairs = per_w // (2 * chunk)
    assert per_w * workers == r and pairs * 2 * chunk == per_w

    @functools.partial(
        pl.kernel, mesh=mesh, out_type=jax.ShapeDtypeStruct((r, d), table.dtype),
        scratch_types=[pltpu.VMEM((2, chunk), I32), pltpu.VMEM((2, chunk, d), table.dtype),
                       pltpu.SemaphoreType.DMA((2,)), pltpu.SemaphoreType.DMA((2,))],
        name="sc_gather")
    def gather(table_hbm, idx_hbm, out_hbm, idx_v, rows_v, gsem, wsem):
        base_w = (lax.axis_index("s") * nc + lax.axis_index("c")) * per_w

        def fetch(b):
            return pltpu.make_async_copy(table_hbm.at[idx_v.at[b]], rows_v.at[b], gsem.at[b])

        def flush(b, base):
            return pltpu.make_async_copy(rows_v.at[b], out_hbm.at[pl.ds(base, chunk)], wsem.at[b])

        def body(j, carry):
            for b in range(2):
                base = base_w + (2 * j + b) * chunk

                @pl.when(j > 0)
                def _():
                    flush(b, base).wait()

                pltpu.sync_copy(idx_hbm.at[pl.ds(base, chunk)], idx_v.at[b])
                fetch(b).start()
            for b in range(2):
                fetch(b).wait()
                flush(b, base_w + (2 * j + b) * chunk).start()
            return carry

        lax.fori_loop(0, pairs, body, 0)
        for b in range(2):
            flush(b, base_w).wait()

    return gather(table, idx)


def _combine_body(h_ref, gate_ref, fg_ref, *refs):
    yg_refs, o_ref = refs[:TOP_K], refs[-1]
    half = D_MODEL // 2
    gates = gate_ref[...]
    h = h_ref[...]
    lo, hi = h[:, :half], h[:, half:]
    for kk in range(TOP_K):
        yp = yg_refs[kk][...]
        g = gates[:, kk:kk + 1]
        lo = lo + g * pltpu.bitcast(yp << 16, F32)
        hi = hi + g * pltpu.bitcast(yp & jnp.uint32(0xFFFF0000), F32)
    out = jnp.concatenate([lo, hi], axis=1)
    o_ref[...] = _rms(out, fg_ref[...])


def _combine(h, gates, fg, ygath, out_prev, *, part, parts):
    t = h.shape[0]
    tm = COMBINE_TILE
    nt = t // parts // tm
    row = lambda i: (part * nt + i, 0)
    slot = lambda kk: pl.BlockSpec((tm, D_MODEL // 2), lambda i: (kk * nt + i, 0))
    in_specs = [pl.BlockSpec((tm, D_MODEL), row), pl.BlockSpec((tm, LANES), row),
                pl.BlockSpec((1, D_MODEL), lambda i: (0, 0))] + [slot(kk) for kk in range(TOP_K)]
    args = [h, gates, fg] + [ygath] * TOP_K
    aliases = {}
    if out_prev is not None:
        aliases = {len(args): 0}
        in_specs.append(pl.BlockSpec(memory_space=pl.ANY))
        args.append(out_prev)
    return pl.pallas_call(
        _combine_body,
        grid=(nt,),
        in_specs=in_specs,
        out_specs=pl.BlockSpec((tm, D_MODEL), row),
        out_shape=jax.ShapeDtypeStruct((t, D_MODEL), F32),
        input_output_aliases=aliases,
        compiler_params=pltpu.CompilerParams(dimension_semantics=("arbitrary",),
                                             vmem_limit_bytes=VMEM_LIMIT),
        name="combine",
    )(*args)


def _block_tril(n, chunk, strict):
    r = jnp.arange(n)[:, None]
    c = jnp.arange(n)[None, :]
    keep = ((r // chunk) == (c // chunk)) & ((r > c) if strict else (r >= c))
    return keep.astype(BF16)


def _pad_lanes(a, fill=0.0):
    a = a.reshape(1, -1).astype(F32)
    return jnp.pad(a, ((0, 0), (0, LANES - a.shape[1])), constant_values=fill)


def _layer(h, norm_mix_g, w_in, gm_norm_g, gm_ws, gm_bs, dn_conv_w, dn_a_log, dn_dt_bias,
           dn_norm_g, w_out, norm_ffn_g, router_w, router_b, exp_w_gu, exp_b_gu, exp_w_down,
           exp_b_down, out_g):
    batch, seq, d = h.shape
    t = batch * seq
    x2 = h.reshape(t, d)

    c0, c1, c2 = GM_WIDTH, 2 * GM_WIDTH, 2 * GM_WIDTH + 3 * DN_KEY
    c3 = c2 + DN_VAL
    wb = w_in.astype(BF16)
    wu, wv, wqkv, wz = wb[:, :c0], wb[:, c0:c1], wb[:, c1:c2], wb[:, c2:c3]
    wab = jnp.pad(wb[:, c3:], ((0, 0), (0, LANES - 2 * DN_HEADS)))
    gmg = gm_norm_g.reshape(1, GM_WIDTH).astype(F32)
    bsb = jnp.repeat(gm_bs.T, HEAD_DIM, axis=1).astype(F32)
    alog = _pad_lanes(dn_a_log)
    dtb = _pad_lanes(dn_dt_bias)
    tri_incl = _block_tril(ROW_TILE, DN_CHUNK, strict=False)

    oa, q, k, v, z, gb, gt = _inproj(
        x2, norm_mix_g.reshape(1, d), wu, wv, wqkv, wz, wab, gmg, gm_ws.astype(F32), bsb,
        dn_conv_w.astype(F32), alog, dtb, tri_incl, seq=seq)

    grow = gt[:DN_HEADS].reshape(DN_HEADS, t // DN_CHUNK, DN_CHUNK).transpose(1, 0, 2)
    grow = grow.reshape(t // DN_CHUNK, DN_HEADS * DN_CHUNK)
    ob = _gdn(q, k, v, z, gb, grow, dn_norm_g.reshape(1, HEAD_DIM).astype(F32), batch=batch, seq=seq)

    wo = w_out.astype(BF16)
    rw = jnp.pad(router_w.astype(BF16), ((0, 0), (0, LANES - N_EXPERTS)))
    rb = _pad_lanes(router_b, fill=NEG_BIG)
    tri_strict = _block_tril(ROW_TILE, ROW_TILE, strict=True)
    hres, hp, meta, gates, cnt = _outproj(oa, ob, x2, wo[:GM_WIDTH], wo[GM_WIDTH:],
                                          norm_ffn_g.reshape(1, d), rw, rb, tri_strict)

    bm = FFN_BLOCK
    counts = cnt[0, :N_EXPERTS]
    padded = (counts + bm - 1) // bm * bm
    pad_end = jnp.cumsum(padded)
    pad_start = (pad_end - padded).astype(I32)
    n_blocks = (t * TOP_K + N_EXPERTS * bm) // bm
    nvalid = (pad_end[-1] // bm).astype(I32).reshape(1)
    blk = jnp.minimum(jnp.arange(n_blocks, dtype=I32), nvalid[0] - 1)
    blk_e = jnp.minimum(jnp.sum(pad_end[None, :] <= (blk * bm)[:, None], axis=1), N_EXPERTS - 1).astype(I32)
    eid = meta[:, :TOP_K]
    start_of = jnp.sum(jnp.where(eid[..., None] == jnp.arange(N_EXPERTS, dtype=I32), pad_start, 0), axis=-1)
    dest = start_of + meta[:, TOP_K:2 * TOP_K]
    dest_k = dest.T.reshape(-1)

    xin = _sc_dispatch(hp, dest_k, n_blocks * bm)
    blk_rows = jnp.clip(jnp.take(counts + pad_start, blk_e) - blk * bm, 0, bm).astype(I32)
    has_rows = counts > 0
    eids = jnp.arange(N_EXPERTS, dtype=I32)
    used = jnp.minimum(jnp.sort(jnp.where(has_rows, eids, N_EXPERTS + eids)), N_EXPERTS - 1)
    used = jnp.concatenate([used, jnp.sum(has_rows).reshape(1)]).astype(I32)
    blk_pos = jnp.take(jnp.cumsum(has_rows) - 1, blk_e).astype(I32)
    y = _ffn(blk_e, nvalid, blk_rows, blk_pos, used, xin, exp_w_gu, exp_b_gu[:, None, :].astype(F32),
             exp_w_down, exp_b_down[:, None, :].astype(F32))
    out = None
    tp = t // COMBINE_PARTS
    for part in range(COMBINE_PARTS):
        ygath = _sc_gather(y, dest[part * tp:(part + 1) * tp].T.reshape(-1))
        out = _combine(hres, gates, out_g.reshape(1, d).astype(F32), ygath, out,
                       part=part, parts=COMBINE_PARTS)
    return out.reshape(batch, seq, d)


def kernel(x, norm_mix_g, w_in, gm_norm_g, gm_ws, gm_bs, dn_conv_w, dn_a_log, dn_dt_bias, dn_norm_g, w_out, norm_ffn_g, router_w, router_b, exp_w_gu, exp_b_gu, exp_w_down, exp_b_down, final_norm_g):
    depth = norm_mix_g.shape[0]
    assert depth == 1, "single-layer problem"
    return _layer(x, norm_mix_g[0], w_in[0], gm_norm_g[0], gm_ws[0], gm_bs[0], dn_conv_w[0],
                  dn_a_log[0], dn_dt_bias[0], dn_norm_g[0], w_out[0], norm_ffn_g[0], router_w[0],
                  router_b[0], exp_w_gu[0], exp_b_gu[0], exp_w_down[0], exp_b_down[0], final_norm_g)
```

```python
import functools

import jax
import jax.numpy as jnp
from jax import lax
from jax.experimental import pallas as pl
from jax.experimental.pallas import tpu as pltpu
from jax.experimental.pallas import tpu_sc as plsc

F32 = jnp.float32
BF16 = jnp.bfloat16
I32 = jnp.int32
U32 = jnp.uint32

D_MODEL = 1024
HEAD_DIM = 128
GM_HEADS = 4
GM_WIDTH = GM_HEADS * HEAD_DIM
GM_CHUNK = 128
DN_HEADS = 4
DN_DK = 128
DN_KEY = DN_HEADS * DN_DK
DN_VAL = DN_HEADS * HEAD_DIM
DN_CHUNK = 64
CONV_W = 4
N_EXPERTS = 32
TOP_K = 4
D_FF = D_MODEL
SWIGLU_LIMIT = 7.0
SWIGLU_ALPHA = 1.702
EPS = 1e-6

LANES = 128
INPROJ_COLS = 512
SUBLANES = 8
ROW_TILE = 512
FFN_BLOCK = 512
FFN_SUBBLOCKS = 2
SC_GATHER_ROWS = 64
COMBINE_PARTS = 4
COMBINE_TILE = 512
VMEM_LIMIT = 56 * 1024 * 1024
NEG_BIG = -1e30
GDN_GROUP = 4


def _dot(a, b):
    return jnp.dot(a, b, preferred_element_type=F32)


def _dot_nt(a, b):
    return lax.dot_general(a, b, (((1,), (1,)), ((), ())), preferred_element_type=F32)


def _dot_tn(a, b):
    return lax.dot_general(a, b, (((0,), (0,)), ((), ())), preferred_element_type=F32)


def _rms(x, g):
    return x * lax.rsqrt(jnp.mean(x * x, axis=-1, keepdims=True) + EPS) * g


def _gelu(x):
    return 0.5 * x * (1.0 + lax.erf(x * (2.0 ** -0.5)))


def _sigmoid(x):
    return 1.0 / (1.0 + jnp.exp(-x))


def _softplus(x):
    return jnp.maximum(x, 0.0) + jnp.log1p(jnp.exp(-jnp.abs(x)))


def _inproj_body(x_ref, ng_ref, wu_ref, wv_ref, wqkv_ref, wz_ref, wab_ref, gmg_ref, ws_ref,
                 bs_ref, cw_ref, alog_ref, dtb_ref, tri_ref,
                 oa_ref, q_ref, k_ref, v_ref, z_ref, gb_ref, gt_ref, cbuf_ref, ybuf_ref,
                 *, tm, tiles_per_seq):
    i = pl.program_id(0)
    ybuf_ref[...] = _rms(x_ref[...], ng_ref[...]).astype(BF16)
    nb = INPROJ_COLS
    heads_per_block = nb // HEAD_DIM

    ri = lax.broadcasted_iota(I32, (GM_CHUNK, GM_CHUNK), 0)
    ci = lax.broadcasted_iota(I32, (GM_CHUNK, GM_CHUNK), 1)
    causal = ri >= ci

    def gmlp_matmul(blk):
        cs = slice(blk * nb, (blk + 1) * nb)
        return _dot(ybuf_ref[...], wu_ref[:, cs]), _dot(ybuf_ref[...], wv_ref[:, cs])

    def gmlp_epilogue(blk, uv):
        u, vg = _gelu(uv[0]), _gelu(uv[1])
        for hh in range(heads_per_block):
            h = blk * heads_per_block + hh
            sl = slice(h * HEAD_DIM, (h + 1) * HEAD_DIM)
            ls = slice(hh * HEAD_DIM, (hh + 1) * HEAD_DIM)
            vh = _rms(vg[:, ls], gmg_ref[:, sl]).astype(BF16)
            wm = jnp.where(causal, ws_ref[h], 0.0).astype(BF16)
            for c in range(tm // GM_CHUNK):
                rs = slice(c * GM_CHUNK, (c + 1) * GM_CHUNK)
                gate = _dot(wm, vh[rs]) + bs_ref[:, sl]
                oa_ref[rs, sl] = (u[rs, ls] * gate).astype(BF16)

    def qkv_matmul(blk):
        return _dot(ybuf_ref[...], wqkv_ref[:, blk * nb:(blk + 1) * nb])

    def qkv_epilogue(blk, pq):
        cs = slice(blk * nb, (blk + 1) * nb)
        cbuf_ref[8:8 + tm, cs] = pq
        acc = cw_ref[CONV_W - 1:CONV_W, cs] * pq
        for j in range(CONV_W - 1):
            off = 8 - (CONV_W - 1) + j
            acc = acc + cw_ref[j:j + 1, cs] * cbuf_ref[off:off + tm, cs]
        cbuf_ref[0:8, cs] = pq[tm - 8:tm, :]
        s = acc * _sigmoid(acc)
        for hh in range(heads_per_block):
            c0 = blk * nb + hh * HEAD_DIM
            sh = s[:, hh * HEAD_DIM:(hh + 1) * HEAD_DIM]
            if c0 < DN_KEY:
                q_ref[:, c0:c0 + HEAD_DIM] = (
                    sh * lax.rsqrt(jnp.sum(sh * sh, axis=-1, keepdims=True) + EPS)
                    * (DN_DK ** -0.5)).astype(BF16)
            elif c0 < 2 * DN_KEY:
                k_ref[:, c0 - DN_KEY:c0 - DN_KEY + HEAD_DIM] = (
                    sh * lax.rsqrt(jnp.sum(sh * sh, axis=-1, keepdims=True) + EPS)).astype(BF16)
            else:
                v_ref[:, c0 - 2 * DN_KEY:c0 - 2 * DN_KEY + HEAD_DIM] = sh.astype(BF16)

    def z_matmul(blk):
        return _dot(ybuf_ref[...], wz_ref[:, blk * nb:(blk + 1) * nb])

    def z_epilogue(blk, zz):
        z_ref[:, blk * nb:(blk + 1) * nb] = zz.astype(BF16)

    @pl.when(i % tiles_per_seq == 0)
    def _():
        cbuf_ref[0:8, :] = jnp.zeros((8, 3 * DN_KEY), F32)

    stages = ([(gmlp_matmul, gmlp_epilogue, blk) for blk in range(GM_WIDTH // nb)]
              + [(qkv_matmul, qkv_epilogue, blk) for blk in range(3 * DN_KEY // nb)]
              + [(z_matmul, z_epilogue, blk) for blk in range(DN_VAL // nb)])
    for matmul, epilogue, blk in stages:
        epilogue(blk, matmul(blk))

    ab = _dot(ybuf_ref[...], wab_ref[...])
    g = -jnp.exp(alog_ref[...]) * _softplus(ab + dtb_ref[...])
    beta = _sigmoid(ab)
    g_hi = g.astype(BF16)
    r1 = g - g_hi.astype(F32)
    g_mid = r1.astype(BF16)
    g_lo = (r1 - g_mid.astype(F32)).astype(BF16)
    tri = tri_ref[...]
    gc = _dot(tri, g_hi) + _dot(tri, g_mid) + _dot(tri, g_lo)
    lane = lax.broadcasted_iota(I32, (tm, LANES), 1)
    gbv = jnp.where(lane < DN_HEADS, gc, beta)
    gb_ref[...] = gbv
    gt_ref[...] = gbv.T[0:8, :]


def _inproj(x2, ng, wu, wv, wqkv, wz, wab, gmg, ws, bsb, cw, alog, dtb, tri, *, seq):
    t = x2.shape[0]
    tm = ROW_TILE
    const2 = lambda i: (0, 0)
    row = lambda i: (i, 0)
    full = lambda a: pl.BlockSpec(a.shape, (lambda i: (0,) * a.ndim))
    out_shapes = (
        jax.ShapeDtypeStruct((t, GM_WIDTH), BF16),
        jax.ShapeDtypeStruct((t, DN_KEY), BF16),
        jax.ShapeDtypeStruct((t, DN_KEY), BF16),
        jax.ShapeDtypeStruct((t, DN_VAL), BF16),
        jax.ShapeDtypeStruct((t, DN_VAL), BF16),
        jax.ShapeDtypeStruct((t, LANES), F32),
        jax.ShapeDtypeStruct((8, t), F32),
    )
    return pl.pallas_call(
        functools.partial(_inproj_body, tm=tm, tiles_per_seq=seq // tm),
        grid=(t // tm,),
        in_specs=[pl.BlockSpec((tm, D_MODEL), row), full(ng), full(wu), full(wv), full(wqkv),
                  full(wz), full(wab), full(gmg), full(ws), full(bsb), full(cw), full(alog),
                  full(dtb), full(tri)],
        out_specs=(pl.BlockSpec((tm, GM_WIDTH), row), pl.BlockSpec((tm, DN_KEY), row),
                   pl.BlockSpec((tm, DN_KEY), row), pl.BlockSpec((tm, DN_VAL), row),
                   pl.BlockSpec((tm, DN_VAL), row), pl.BlockSpec((tm, LANES), row),
                   pl.BlockSpec((8, tm), lambda i: (0, i))),
        out_shape=out_shapes,
        scratch_shapes=[pltpu.VMEM((tm + 8, 3 * DN_KEY), F32), pltpu.VMEM((tm, D_MODEL), BF16)],
        compiler_params=pltpu.CompilerParams(dimension_semantics=("arbitrary",),
                                             vmem_limit_bytes=VMEM_LIMIT),
        name="inproj",
    )(x2, ng, wu, wv, wqkv, wz, wab, gmg, ws, bsb, cw, alog, dtb, tri)


def _gdn_body(q_ref, k_ref, v_ref, z_ref, gb_ref, gr_ref, ng_ref, ob_ref, s_ref, *, nchunk,
              group_size):
    j = pl.program_id(1)

    @pl.when(j == 0)
    def _():
        s_ref[...] = jnp.zeros(s_ref.shape, F32)

    c = DN_CHUNK
    n = DN_HEADS * c
    ri = lax.broadcasted_iota(I32, (n, n), 0)
    ci = lax.broadcasted_iota(I32, (n, n), 1)
    same = (ri // c) == (ci // c)
    incl = same & ((ri % c) >= (ci % c))
    strict = same & ((ri % c) > (ci % c))
    ng = ng_ref[...]

    def stack(a):
        return jnp.concatenate([a[:, h * HEAD_DIM:(h + 1) * HEAD_DIM] for h in range(DN_HEADS)], axis=0)

    def prepare(ic):
        r0 = pl.multiple_of(ic * c, c)
        kst = stack(k_ref[pl.ds(r0, c), :])
        qst = stack(q_ref[pl.ds(r0, c), :])
        vst = stack(v_ref[pl.ds(r0, c), :])
        gbc = gb_ref[pl.ds(r0, c), :]
        grow = gr_ref[pl.ds(ic, 1), :]
        gcol = jnp.concatenate([gbc[:, h:h + 1] for h in range(DN_HEADS)], axis=0)
        bcol = jnp.concatenate([gbc[:, DN_HEADS + h:DN_HEADS + h + 1] for h in range(DN_HEADS)], axis=0)
        glast = jnp.concatenate(
            [jnp.broadcast_to(gbc[c - 1:c, h:h + 1], (c, 1)) for h in range(DN_HEADS)], axis=0)
        decay = jnp.where(incl, jnp.exp(jnp.where(incl, gcol - grow, 0.0)), 0.0)
        kf = kst.astype(F32)
        kb = kf * bcol
        lmat = jnp.where(strict, _dot_nt(kb.astype(BF16), kst) * decay, 0.0)
        eg = jnp.exp(gcol)
        rhs = jnp.concatenate([vst.astype(F32) * bcol, kb * eg], axis=1)
        attn = jnp.where(incl, _dot_nt(qst, kst) * decay, 0.0).astype(BF16)
        qd = (qst.astype(F32) * eg).astype(BF16)
        ke = (kf * jnp.exp(glast - gcol)).astype(BF16)
        return dict(r0=r0, gbc=gbc, lmat=lmat, rhs=rhs, attn=attn, qd=qd, ke=ke)

    def advance_state(p, sol):
        u = sol[:, :HEAD_DIM]
        wb = sol[:, HEAD_DIM:].astype(BF16)
        vn, qs = [], []
        for h in range(DN_HEADS):
            rs = slice(h * c, (h + 1) * c)
            sb = s_ref[h].astype(BF16)
            vn.append(u[rs] - _dot(wb[rs], sb))
            qs.append(_dot(p["qd"][rs], sb))
        vnb = jnp.concatenate(vn, axis=0).astype(BF16)
        o = jnp.concatenate(qs, axis=0) + _dot(p["attn"], vnb)
        zc = z_ref[pl.ds(p["r0"], c), :]
        for h in range(DN_HEADS):
            rs = slice(h * c, (h + 1) * c)
            sl = slice(h * HEAD_DIM, (h + 1) * HEAD_DIM)
            s_ref[h] = (s_ref[h] * jnp.exp(p["gbc"][c - 1:c, h:h + 1])
                        + _dot_tn(p["ke"][rs], vnb[rs]))
            zf = zc[:, sl].astype(F32)
            ob_ref[pl.ds(p["r0"], c), sl] = (_rms(o[rs], ng) * (zf * _sigmoid(zf))).astype(BF16)

    def group(ig, carry):
        ps = [prepare(ig * group_size + b) for b in range(group_size)]
        sol = [p["rhs"] for p in ps]
        pw = [-p["lmat"] for p in ps]
        for step in range(6):
            pb = [x.astype(BF16) for x in pw]
            sol = [s + _dot(xb, s.astype(BF16)) for s, xb in zip(sol, pb)]
            if step < 5:
                pw = [_dot(xb, xb) for xb in pb]
        for p, s in zip(ps, sol):
            advance_state(p, s)
        return carry

    lax.fori_loop(0, nchunk // group_size, group, 0)


def _gdn(q, k, v, z, gb, grow, ng, *, batch, seq):
    tm = ROW_TILE
    nchunk = tm // DN_CHUNK
    steps = seq // tm
    rows = lambda b, j: (b * steps + j, 0)
    return pl.pallas_call(
        functools.partial(_gdn_body, nchunk=nchunk, group_size=GDN_GROUP),
        grid=(batch, steps),
        in_specs=[pl.BlockSpec((tm, DN_KEY), rows), pl.BlockSpec((tm, DN_KEY), rows),
                  pl.BlockSpec((tm, DN_VAL), rows), pl.BlockSpec((tm, DN_VAL), rows),
                  pl.BlockSpec((tm, LANES), rows),
                  pl.BlockSpec((nchunk, DN_HEADS * DN_CHUNK), rows),
                  pl.BlockSpec((1, HEAD_DIM), lambda b, j: (0, 0))],
        out_specs=pl.BlockSpec((tm, DN_VAL), rows),
        out_shape=jax.ShapeDtypeStruct((batch * seq, DN_VAL), BF16),
        scratch_shapes=[pltpu.VMEM((DN_HEADS, DN_DK, HEAD_DIM), F32)],
        compiler_params=pltpu.CompilerParams(dimension_semantics=("arbitrary", "arbitrary"),
                                             vmem_limit_bytes=VMEM_LIMIT),
        name="gdn",
    )(q, k, v, z, gb, grow, ng)


def _outproj_body(oa_ref, ob_ref, x_ref, woa_ref, wob_ref, ng_ref, rw_ref, rb_ref, tri_ref,
                  h_ref, hp_ref, meta_ref, gate_ref, cnt_ref, run_ref, *, tm):
    i = pl.program_id(0)

    @pl.when(i == 0)
    def _():
        run_ref[...] = jnp.zeros(run_ref.shape, F32)

    h = x_ref[...] + _dot(oa_ref[...], woa_ref[...]) + _dot(ob_ref[...], wob_ref[...])
    h_ref[...] = h
    hb = _rms(h, ng_ref[...]).astype(BF16)
    half = D_MODEL // 2
    lo = pltpu.bitcast(hb[:, :half].astype(F32), U32) >> 16
    hi = pltpu.bitcast(hb[:, half:].astype(F32), U32) & jnp.uint32(0xFFFF0000)
    hp_ref[...] = lo | hi

    logits = _dot(hb, rw_ref[...]) + rb_ref[...]
    lane = lax.broadcasted_iota(I32, (tm, LANES), 1)
    lanef = lane.astype(F32)
    work = logits
    onehot = jnp.zeros((tm, LANES), F32)
    vals, sels = [], []
    for _ in range(TOP_K):
        m = jnp.max(work, axis=-1, keepdims=True)
        idx = jnp.min(jnp.where(work == m, lanef, float(LANES)), axis=-1, keepdims=True)
        sel = lanef == idx
        work = jnp.where(sel, -3e38, work)
        onehot = onehot + jnp.where(sel, 1.0, 0.0)
        vals.append(m)
        sels.append((sel, idx))
    ex = [jnp.exp(v - vals[0]) for v in vals]
    den = ex[0] + ex[1] + ex[2] + ex[3]
    pref = _dot(tri_ref[...], onehot.astype(BF16)) + run_ref[0:1, :]
    meta = jnp.zeros((tm, LANES), F32)
    gates = jnp.zeros((tm, LANES), F32)
    for kk in range(TOP_K):
        sel, idx = sels[kk]
        rank = jnp.sum(jnp.where(sel, pref, 0.0), axis=-1, keepdims=True)
        meta = meta + jnp.where(lane == kk, idx, 0.0) + jnp.where(lane == TOP_K + kk, rank, 0.0)
        gates = gates + jnp.where(lane == kk, ex[kk] / den, 0.0)
    meta_ref[...] = meta.T[0:8, :].astype(I32)
    gate_ref[...] = gates
    run = run_ref[...] + jnp.sum(onehot, axis=0, keepdims=True)
    run_ref[...] = run
    cnt_ref[...] = run.astype(I32)


def _outproj(oa, ob, x2, woa, wob, ng, rw, rb, tri):
    t = x2.shape[0]
    tm = ROW_TILE
    row = lambda i: (i, 0)
    full = lambda a: pl.BlockSpec(a.shape, (lambda i: (0,) * a.ndim))
    out_shapes = (
        jax.ShapeDtypeStruct((t, D_MODEL), F32),
        jax.ShapeDtypeStruct((t, D_MODEL // 2), U32),
        jax.ShapeDtypeStruct((8, t), I32),
        jax.ShapeDtypeStruct((t, LANES), F32),
        jax.ShapeDtypeStruct((8, LANES), I32),
    )
    return pl.pallas_call(
        functools.partial(_outproj_body, tm=tm),
        grid=(t // tm,),
        in_specs=[pl.BlockSpec((tm, GM_WIDTH), row), pl.BlockSpec((tm, DN_VAL), row),
                  pl.BlockSpec((tm, D_MODEL), row), full(woa), full(wob), full(ng), full(rw),
                  full(rb), full(tri)],
        out_specs=(pl.BlockSpec((tm, D_MODEL), row), pl.BlockSpec((tm, D_MODEL // 2), row),
                   pl.BlockSpec((8, tm), lambda i: (0, i)), pl.BlockSpec((tm, LANES), row),
                   pl.BlockSpec((8, LANES), lambda i: (0, 0))),
        out_shape=out_shapes,
        scratch_shapes=[pltpu.VMEM((8, LANES), F32)],
        compiler_params=pltpu.CompilerParams(dimension_semantics=("arbitrary",),
                                             vmem_limit_bytes=VMEM_LIMIT),
        name="outproj",
    )(oa, ob, x2, woa, wob, ng, rw, rb, tri)


def _sc_dispatch(hp, dest_k, n_rows):
    t, d = hp.shape
    mesh = plsc.VectorSubcoreMesh(core_axis_name="c", subcore_axis_name="s")
    nc, workers = mesh.num_cores, mesh.num_cores * mesh.num_subcores
    chunk = SC_GATHER_ROWS
    per_w = t // workers
    pairs = per_w // (2 * chunk)
    assert per_w * workers == t and pairs * 2 * chunk == per_w

    @functools.partial(
        pl.kernel, mesh=mesh, out_type=jax.ShapeDtypeStruct((n_rows, d), hp.dtype),
        scratch_types=[pltpu.VMEM((2, TOP_K, chunk), I32), pltpu.VMEM((2, chunk, d), hp.dtype),
                       pltpu.SemaphoreType.DMA((2,)), pltpu.SemaphoreType.DMA((2,))],
        name="sc_dispatch")
    def scatter(hp_hbm, idx_hbm, xin_hbm, idx_v, rows_v, lsem, ssem):
        base_w = (lax.axis_index("s") * nc + lax.axis_index("c")) * per_w

        def load(b, base):
            return pltpu.make_async_copy(hp_hbm.at[pl.ds(base, chunk)], rows_v.at[b], lsem.at[b])

        def put(b, kk):
            return pltpu.make_async_copy(rows_v.at[b], xin_hbm.at[idx_v.at[b, kk]], ssem.at[b])

        def body(j, carry):
            for b in range(2):
                base = base_w + (2 * j + b) * chunk

                @pl.when(j > 0)
                def _():
                    for kk in range(TOP_K):
                        put(b, kk).wait()

                for kk in range(TOP_K):
                    pltpu.sync_copy(idx_hbm.at[pl.ds(kk * t + base, chunk)], idx_v.at[b, kk])
                load(b, base).start()
            for b in range(2):
                load(b, base_w).wait()
                for kk in range(TOP_K):
                    put(b, kk).start()
            return carry

        lax.fori_loop(0, pairs, body, 0)
        for b in range(2):
            for kk in range(TOP_K):
                put(b, kk).wait()

    return scatter(hp, dest_k)


def _ffn_body(be_ref, nv_ref, br_ref, ep_ref, es_ref, x_ref, wgu_hbm, bgu_ref, wd_hbm, bd_ref, y_ref,
              wgub_ref, wdb_ref, wguf_ref, wdf_ref, gsem, dsem):
    i = pl.program_id(0)

    @pl.when(i >= nv_ref[0])
    def _():
        y_ref[...] = jnp.zeros(y_ref.shape, U32)

    def fetch(pos):
        e, slot = es_ref[pos], pos % 2
        return (pltpu.make_async_copy(wgu_hbm.at[e], wguf_ref.at[slot], gsem.at[slot]),
                pltpu.make_async_copy(wd_hbm.at[e], wdf_ref.at[slot], dsem.at[slot]))

    @pl.when((i < nv_ref[0]) & ((i == 0) | (be_ref[i] != be_ref[jnp.maximum(i - 1, 0)])))
    def _():
        pos = ep_ref[i]

        @pl.when(i == 0)
        def _():
            for cp in fetch(pos):
                cp.start()

        for cp in fetch(pos):
            cp.wait()

        @pl.when(pos + 1 < es_ref[N_EXPERTS])
        def _():
            for cp in fetch(pos + 1):
                cp.start()

        slot = pos % 2
        wgub_ref[...] = wguf_ref[slot].astype(BF16)
        wdb_ref[...] = wdf_ref[slot].astype(BF16)

    nsub = FFN_SUBBLOCKS
    rsub = x_ref.shape[0] // nsub

    def compute(active):
        half = D_MODEL // 2
        row = lax.broadcasted_iota(I32, (rsub, 1), 0)
        gus = []
        for sb in range(active):
            rs = slice(sb * rsub, (sb + 1) * rsub)
            xp = jnp.where(row + sb * rsub < br_ref[i], x_ref[rs, :], jnp.uint32(0))
            lo = pltpu.bitcast(xp << 16, F32).astype(BF16)
            hi = pltpu.bitcast(xp & jnp.uint32(0xFFFF0000), F32).astype(BF16)
            gus.append(_dot(lo, wgub_ref[:half, :]) + _dot(hi, wgub_ref[half:, :]) + bgu_ref[0])
        ys = []
        for gu in gus:
            gate = jnp.minimum(gu[:, :D_FF], SWIGLU_LIMIT)
            up = jnp.clip(gu[:, D_FF:], -SWIGLU_LIMIT, SWIGLU_LIMIT)
            act = (up + 1.0) * (gate * _sigmoid(SWIGLU_ALPHA * gate))
            ys.append(_dot(act.astype(BF16), wdb_ref[...]) + bd_ref[0])
        for sb, y in enumerate(ys):
            ylo = pltpu.bitcast(y[:, :half].astype(BF16).astype(F32), U32) >> 16
            yhi = pltpu.bitcast(y[:, half:].astype(BF16).astype(F32), U32) & jnp.uint32(0xFFFF0000)
            y_ref[sb * rsub:(sb + 1) * rsub, :] = ylo | yhi
        if active < nsub:
            y_ref[active * rsub:, :] = jnp.zeros(((nsub - active) * rsub, half), U32)

    pl.when((i < nv_ref[0]) & (br_ref[i] > rsub))(functools.partial(compute, nsub))
    pl.when((i < nv_ref[0]) & (br_ref[i] <= rsub))(functools.partial(compute, 1))


def _ffn(blk_e, nvalid, blk_rows, blk_pos, used, xin, wgu, bgu, wd, bd):
    p = xin.shape[0]
    bm = FFN_BLOCK
    rows = lambda i, be, nv, br, ep, es: (jnp.minimum(i, nv[0] - 1), 0)
    wsel = lambda i, be, nv, br, ep, es: (be[i], 0, 0)
    return pl.pallas_call(
        _ffn_body,
        grid_spec=pltpu.PrefetchScalarGridSpec(
            num_scalar_prefetch=5,
            grid=(p // bm,),
            in_specs=[pl.BlockSpec((bm, D_MODEL // 2), rows),
                      pl.BlockSpec(memory_space=pl.ANY),
                      pl.BlockSpec((1, 1, 2 * D_FF), wsel),
                      pl.BlockSpec(memory_space=pl.ANY),
                      pl.BlockSpec((1, 1, D_MODEL), wsel)],
            out_specs=pl.BlockSpec((bm, D_MODEL // 2), lambda i, be, nv, br, ep, es: (i, 0)),
            scratch_shapes=[pltpu.VMEM((D_MODEL, 2 * D_FF), BF16), pltpu.VMEM((D_FF, D_MODEL), BF16),
                            pltpu.VMEM((2, D_MODEL, 2 * D_FF), F32), pltpu.VMEM((2, D_FF, D_MODEL), F32),
                            pltpu.SemaphoreType.DMA((2,)), pltpu.SemaphoreType.DMA((2,))],
        ),
        out_shape=jax.ShapeDtypeStruct((p, D_MODEL // 2), U32),
        compiler_params=pltpu.CompilerParams(dimension_semantics=("arbitrary",),
                                             vmem_limit_bytes=VMEM_LIMIT),
        name="ffn",
    )(blk_e, nvalid, blk_rows, blk_pos, used, xin, wgu, bgu, wd, bd)


def _sc_gather(table, idx):
    r, d = idx.shape[0], table.shape[1]
    mesh = plsc.VectorSubcoreMesh(core_axis_name="c", subcore_axis_name="s")
    nc, workers = mesh.num_cores, mesh.num_cores * mesh.num_subcores
    chunk = SC_GATHER_ROWS
    per_w = r // workers
    pairs = per_w // (2 * chunk)
    assert per_w * workers == r and pairs * 2 * chunk == per_w

    @functools.partial(
        pl.kernel, mesh=mesh, out_type=jax.ShapeDtypeStruct((r, d), table.dtype),
        scratch_types=[pltpu.VMEM((2, chunk), I32), pltpu.VMEM((2, chunk, d), table.dtype),
                       pltpu.SemaphoreType.DMA((2,)), pltpu.SemaphoreType.DMA((2,))],
        name="sc_gather")
    def gather(table_hbm, idx_hbm, out_hbm, idx_v, rows_v, gsem, wsem):
        base_w = (lax.axis_index("s") * nc + lax.axis_index("c")) * per_w

        def fetch(b):
            return pltpu.make_async_copy(table_hbm.at[idx_v.at[b]], rows_v.at[b], gsem.at[b])

        def flush(b, base):
            return pltpu.make_async_copy(rows_v.at[b], out_hbm.at[pl.ds(base, chunk)], wsem.at[b])

        def body(j, carry):
            for b in range(2):
                base = base_w + (2 * j + b) * chunk

                @pl.when(j > 0)
                def _():
                    flush(b, base).wait()

                pltpu.sync_copy(idx_hbm.at[pl.ds(base, chunk)], idx_v.at[b])
                fetch(b).start()
            for b in range(2):
                fetch(b).wait()
                flush(b, base_w + (2 * j + b) * chunk).start()
            return carry

        lax.fori_loop(0, pairs, body, 0)
        for b in range(2):
            flush(b, base_w).wait()

    return gather(table, idx)


def _combine_body(h_ref, gate_ref, fg_ref, *refs):
    yg_refs, o_ref = refs[:TOP_K], refs[-1]
    half = D_MODEL // 2
    gates = gate_ref[...]
    h = h_ref[...]
    lo, hi = h[:, :half], h[:, half:]
    for kk in range(TOP_K):
        yp = yg_refs[kk][...]
        g = gates[:, kk:kk + 1]
        lo = lo + g * pltpu.bitcast(yp << 16, F32)
        hi = hi + g * pltpu.bitcast(yp & jnp.uint32(0xFFFF0000), F32)
    out = jnp.concatenate([lo, hi], axis=1)
    o_ref[...] = _rms(out, fg_ref[...])


def _combine(h, gates, fg, ygath, out_prev, *, part, parts):
    t = h.shape[0]
    tm = COMBINE_TILE
    nt = t // parts // tm
    row = lambda i: (part * nt + i, 0)
    slot = lambda kk: pl.BlockSpec((tm, D_MODEL // 2), lambda i: (kk * nt + i, 0))
    in_specs = [pl.BlockSpec((tm, D_MODEL), row), pl.BlockSpec((tm, LANES), row),
                pl.BlockSpec((1, D_MODEL), lambda i: (0, 0))] + [slot(kk) for kk in range(TOP_K)]
    args = [h, gates, fg] + [ygath] * TOP_K
    aliases = {}
    if out_prev is not None:
        aliases = {len(args): 0}
        in_specs.append(pl.BlockSpec(memory_space=pl.ANY))
        args.append(out_prev)
    return pl.pallas_call(
        _combine_body,
        grid=(nt,),
        in_specs=in_specs,
        out_specs=pl.BlockSpec((tm, D_MODEL), row),
        out_shape=jax.ShapeDtypeStruct((t, D_MODEL), F32),
        input_output_aliases=aliases,
        compiler_params=pltpu.CompilerParams(dimension_semantics=("arbitrary",),
                                             vmem_limit_bytes=VMEM_LIMIT),
        name="combine",
    )(*args)


def _block_tril(n, chunk, strict):
    r = jnp.arange(n)[:, None]
    c = jnp.arange(n)[None, :]
    keep = ((r // chunk) == (c // chunk)) & ((r > c) if strict else (r >= c))
    return keep.astype(BF16)


def _pad_lanes(a, fill=0.0):
    a = a.reshape(1, -1).astype(F32)
    return jnp.pad(a, ((0, 0), (0, LANES - a.shape[1])), constant_values=fill)


def _layer(h, norm_mix_g, w_in, gm_norm_g, gm_ws, gm_bs, dn_conv_w, dn_a_log, dn_dt_bias,
           dn_norm_g, w_out, norm_ffn_g, router_w, router_b, exp_w_gu, exp_b_gu, exp_w_down,
           exp_b_down, out_g):
    batch, seq, d = h.shape
    t = batch * seq
    x2 = h.reshape(t, d)

    c0, c1, c2 = GM_WIDTH, 2 * GM_WIDTH, 2 * GM_WIDTH + 3 * DN_KEY
    c3 = c2 + DN_VAL
    wb = w_in.astype(BF16)
    wu, wv, wqkv, wz = wb[:, :c0], wb[:, c0:c1], wb[:, c1:c2], wb[:, c2:c3]
    wab = jnp.pad(wb[:, c3:], ((0, 0), (0, LANES - 2 * DN_HEADS)))
    gmg = gm_norm_g.reshape(1, GM_WIDTH).astype(F32)
    bsb = jnp.repeat(gm_bs.T, HEAD_DIM, axis=1).astype(F32)
    alog = _pad_lanes(dn_a_log)
    dtb = _pad_lanes(dn_dt_bias)
    tri_incl = _block_tril(ROW_TILE, DN_CHUNK, strict=False)

    oa, q, k, v, z, gb, gt = _inproj(
        x2, norm_mix_g.reshape(1, d), wu, wv, wqkv, wz, wab, gmg, gm_ws.astype(F32), bsb,
        dn_conv_w.astype(F32), alog, dtb, tri_incl, seq=seq)

    grow = gt[:DN_HEADS].reshape(DN_HEADS, t // DN_CHUNK, DN_CHUNK).transpose(1, 0, 2)
    grow = grow.reshape(t // DN_CHUNK, DN_HEADS * DN_CHUNK)
    ob = _gdn(q, k, v, z, gb, grow, dn_norm_g.reshape(1, HEAD_DIM).astype(F32), batch=batch, seq=seq)

    wo = w_out.astype(BF16)
    rw = jnp.pad(router_w.astype(BF16), ((0, 0), (0, LANES - N_EXPERTS)))
    rb = _pad_lanes(router_b, fill=NEG_BIG)
    tri_strict = _block_tril(ROW_TILE, ROW_TILE, strict=True)
    hres, hp, meta, gates, cnt = _outproj(oa, ob, x2, wo[:GM_WIDTH], wo[GM_WIDTH:],
                                          norm_ffn_g.reshape(1, d), rw, rb, tri_strict)

    bm = FFN_BLOCK
    counts = cnt[0, :N_EXPERTS]
    padded = (counts + bm - 1) // bm * bm
    pad_end = jnp.cumsum(padded)
    pad_start = (pad_end - padded).astype(I32)
    n_blocks = (t * TOP_K + N_EXPERTS * bm) // bm
    nvalid = (pad_end[-1] // bm).astype(I32).reshape(1)
    blk = jnp.minimum(jnp.arange(n_blocks, dtype=I32), nvalid[0] - 1)
    blk_e = jnp.minimum(jnp.sum(pad_end[None, :] <= (blk * bm)[:, None], axis=1), N_EXPERTS - 1).astype(I32)
    eid = meta[:TOP_K]
    start_of = jnp.sum(jnp.where(eid[..., None] == jnp.arange(N_EXPERTS, dtype=I32), pad_start, 0), axis=-1)
    dest = start_of + meta[TOP_K:2 * TOP_K]

    xin = _sc_dispatch(hp, dest.reshape(-1), n_blocks * bm)
    blk_rows = jnp.clip(jnp.take(counts + pad_start, blk_e) - blk * bm, 0, bm).astype(I32)
    has_rows = counts > 0
    eids = jnp.arange(N_EXPERTS, dtype=I32)
    used = jnp.minimum(jnp.sort(jnp.where(has_rows, eids, N_EXPERTS + eids)), N_EXPERTS - 1)
    used = jnp.concatenate([used, jnp.sum(has_rows).reshape(1)]).astype(I32)
    blk_pos = jnp.take(jnp.cumsum(has_rows) - 1, blk_e).astype(I32)
    y = _ffn(blk_e, nvalid, blk_rows, blk_pos, used, xin, exp_w_gu, exp_b_gu[:, None, :].astype(F32),
             exp_w_down, exp_b_down[:, None, :].astype(F32))
    out = None
    tp = t // COMBINE_PARTS
    for part in range(COMBINE_PARTS):
        ygath = _sc_gather(y, dest[:, part * tp:(part + 1) * tp].reshape(-1))
        out = _combine(hres, gates, out_g.reshape(1, d).astype(F32), ygath, out,
                       part=part, parts=COMBINE_PARTS)
    return out.reshape(batch, seq, d)


def kernel(x, norm_mix_g, w_in, gm_norm_g, gm_ws, gm_bs, dn_conv_w, dn_a_log, dn_dt_bias, dn_norm_g, w_out, norm_ffn_g, router_w, router_b, exp_w_gu, exp_b_gu, exp_w_down, exp_b_down, final_norm_g):
    depth = norm_mix_g.shape[0]
    assert depth == 1, "single-layer problem"
    return _layer(x, norm_mix_g[0], w_in[0], gm_norm_g[0], gm_ws[0], gm_bs[0], dn_conv_w[0],
                  dn_a_log[0], dn_dt_bias[0], dn_norm_g[0], w_out[0], norm_ffn_g[0], router_w[0],
                  router_b[0], exp_w_gu[0], exp_b_gu[0], exp_w_down[0], exp_b_down[0], final_norm_g)
```

```python
import functools

import jax
import jax.numpy as jnp
from jax import lax
from jax.experimental import pallas as pl
from jax.experimental.pallas import tpu as pltpu
from jax.experimental.pallas import tpu_sc as plsc

F32 = jnp.float32
BF16 = jnp.bfloat16
I32 = jnp.int32
U32 = jnp.uint32

D_MODEL = 1024
HEAD_DIM = 128
GM_HEADS = 4
GM_WIDTH = GM_HEADS * HEAD_DIM
GM_CHUNK = 128
DN_HEADS = 4
DN_DK = 128
DN_KEY = DN_HEADS * DN_DK
DN_VAL = DN_HEADS * HEAD_DIM
DN_CHUNK = 64
CONV_W = 4
N_EXPERTS = 32
TOP_K = 4
D_FF = D_MODEL
SWIGLU_LIMIT = 7.0
SWIGLU_ALPHA = 1.702
EPS = 1e-6

LANES = 128
INPROJ_COLS = 512
SUBLANES = 8
ROW_TILE = 512
FFN_BLOCK = 1024
FFN_SUBBLOCKS = 4
SC_GATHER_ROWS = 64
COMBINE_PARTS = 4
COMBINE_TILE = 512
VMEM_LIMIT = 56 * 1024 * 1024
NEG_BIG = -1e30
GDN_GROUP = 4


def _dot(a, b):
    return jnp.dot(a, b, preferred_element_type=F32)


def _dot_nt(a, b):
    return lax.dot_general(a, b, (((1,), (1,)), ((), ())), preferred_element_type=F32)


def _dot_tn(a, b):
    return lax.dot_general(a, b, (((0,), (0,)), ((), ())), preferred_element_type=F32)


def _rms(x, g):
    return x * lax.rsqrt(jnp.mean(x * x, axis=-1, keepdims=True) + EPS) * g


def _gelu(x):
    return 0.5 * x * (1.0 + lax.erf(x * (2.0 ** -0.5)))


def _sigmoid(x):
    return 1.0 / (1.0 + jnp.exp(-x))


def _softplus(x):
    return jnp.maximum(x, 0.0) + jnp.log1p(jnp.exp(-jnp.abs(x)))


def _inproj_body(x_ref, ng_ref, wu_ref, wv_ref, wqkv_ref, wz_ref, wab_ref, gmg_ref, ws_ref,
                 bs_ref, cw_ref, alog_ref, dtb_ref, tri_ref,
                 oa_ref, q_ref, k_ref, v_ref, z_ref, gb_ref, gt_ref, cbuf_ref, ybuf_ref,
                 *, tm, tiles_per_seq):
    i = pl.program_id(0)
    ybuf_ref[...] = _rms(x_ref[...], ng_ref[...]).astype(BF16)
    nb = INPROJ_COLS
    heads_per_block = nb // HEAD_DIM

    ri = lax.broadcasted_iota(I32, (GM_CHUNK, GM_CHUNK), 0)
    ci = lax.broadcasted_iota(I32, (GM_CHUNK, GM_CHUNK), 1)
    causal = ri >= ci

    def gmlp_matmul(blk):
        cs = slice(blk * nb, (blk + 1) * nb)
        return _dot(ybuf_ref[...], wu_ref[:, cs]), _dot(ybuf_ref[...], wv_ref[:, cs])

    def gmlp_epilogue(blk, uv):
        u, vg = _gelu(uv[0]), _gelu(uv[1])
        for hh in range(heads_per_block):
            h = blk * heads_per_block + hh
            sl = slice(h * HEAD_DIM, (h + 1) * HEAD_DIM)
            ls = slice(hh * HEAD_DIM, (hh + 1) * HEAD_DIM)
            vh = _rms(vg[:, ls], gmg_ref[:, sl]).astype(BF16)
            wm = jnp.where(causal, ws_ref[h], 0.0).astype(BF16)
            for c in range(tm // GM_CHUNK):
                rs = slice(c * GM_CHUNK, (c + 1) * GM_CHUNK)
                gate = _dot(wm, vh[rs]) + bs_ref[:, sl]
                oa_ref[rs, sl] = (u[rs, ls] * gate).astype(BF16)

    def qkv_matmul(blk):
        return _dot(ybuf_ref[...], wqkv_ref[:, blk * nb:(blk + 1) * nb])

    def qkv_epilogue(blk, pq):
        cs = slice(blk * nb, (blk + 1) * nb)
        cbuf_ref[8:8 + tm, cs] = pq
        acc = cw_ref[CONV_W - 1:CONV_W, cs] * pq
        for j in range(CONV_W - 1):
            off = 8 - (CONV_W - 1) + j
            acc = acc + cw_ref[j:j + 1, cs] * cbuf_ref[off:off + tm, cs]
        cbuf_ref[0:8, cs] = pq[tm - 8:tm, :]
        s = acc * _sigmoid(acc)
        for hh in range(heads_per_block):
            c0 = blk * nb + hh * HEAD_DIM
            sh = s[:, hh * HEAD_DIM:(hh + 1) * HEAD_DIM]
            if c0 < DN_KEY:
                q_ref[:, c0:c0 + HEAD_DIM] = (
                    sh * lax.rsqrt(jnp.sum(sh * sh, axis=-1, keepdims=True) + EPS)
                    * (DN_DK ** -0.5)).astype(BF16)
            elif c0 < 2 * DN_KEY:
                k_ref[:, c0 - DN_KEY:c0 - DN_KEY + HEAD_DIM] = (
                    sh * lax.rsqrt(jnp.sum(sh * sh, axis=-1, keepdims=True) + EPS)).astype(BF16)
            else:
                v_ref[:, c0 - 2 * DN_KEY:c0 - 2 * DN_KEY + HEAD_DIM] = sh.astype(BF16)

    def z_matmul(blk):
        return _dot(ybuf_ref[...], wz_ref[:, blk * nb:(blk + 1) * nb])

    def z_epilogue(blk, zz):
        z_ref[:, blk * nb:(blk + 1) * nb] = zz.astype(BF16)

    @pl.when(i % tiles_per_seq == 0)
    def _():
        cbuf_ref[0:8, :] = jnp.zeros((8, 3 * DN_KEY), F32)

    stages = ([(gmlp_matmul, gmlp_epilogue, blk) for blk in range(GM_WIDTH // nb)]
              + [(qkv_matmul, qkv_epilogue, blk) for blk in range(3 * DN_KEY // nb)]
              + [(z_matmul, z_epilogue, blk) for blk in range(DN_VAL // nb)])
    for matmul, epilogue, blk in stages:
        epilogue(blk, matmul(blk))

    ab = _dot(ybuf_ref[...], wab_ref[...])
    g = -jnp.exp(alog_ref[...]) * _softplus(ab + dtb_ref[...])
    beta = _sigmoid(ab)
    g_hi = g.astype(BF16)
    r1 = g - g_hi.astype(F32)
    g_mid = r1.astype(BF16)
    g_lo = (r1 - g_mid.astype(F32)).astype(BF16)
    tri = tri_ref[...]
    gc = _dot(tri, g_hi) + _dot(tri, g_mid) + _dot(tri, g_lo)
    lane = lax.broadcasted_iota(I32, (tm, LANES), 1)
    gbv = jnp.where(lane < DN_HEADS, gc, beta)
    gb_ref[...] = gbv
    gt_ref[...] = gbv.T[0:8, :]


def _inproj(x2, ng, wu, wv, wqkv, wz, wab, gmg, ws, bsb, cw, alog, dtb, tri, *, seq):
    t = x2.shape[0]
    tm = ROW_TILE
    const2 = lambda i: (0, 0)
    row = lambda i: (i, 0)
    full = lambda a: pl.BlockSpec(a.shape, (lambda i: (0,) * a.ndim))
    out_shapes = (
        jax.ShapeDtypeStruct((t, GM_WIDTH), BF16),
        jax.ShapeDtypeStruct((t, DN_KEY), BF16),
        jax.ShapeDtypeStruct((t, DN_KEY), BF16),
        jax.ShapeDtypeStruct((t, DN_VAL), BF16),
        jax.ShapeDtypeStruct((t, DN_VAL), BF16),
        jax.ShapeDtypeStruct((t, LANES), F32),
        jax.ShapeDtypeStruct((8, t), F32),
    )
    return pl.pallas_call(
        functools.partial(_inproj_body, tm=tm, tiles_per_seq=seq // tm),
        grid=(t // tm,),
        in_specs=[pl.BlockSpec((tm, D_MODEL), row), full(ng), full(wu), full(wv), full(wqkv),
                  full(wz), full(wab), full(gmg), full(ws), full(bsb), full(cw), full(alog),
                  full(dtb), full(tri)],
        out_specs=(pl.BlockSpec((tm, GM_WIDTH), row), pl.BlockSpec((tm, DN_KEY), row),
                   pl.BlockSpec((tm, DN_KEY), row), pl.BlockSpec((tm, DN_VAL), row),
                   pl.BlockSpec((tm, DN_VAL), row), pl.BlockSpec((tm, LANES), row),
                   pl.BlockSpec((8, tm), lambda i: (0, i))),
        out_shape=out_shapes,
        scratch_shapes=[pltpu.VMEM((tm + 8, 3 * DN_KEY), F32), pltpu.VMEM((tm, D_MODEL), BF16)],
        compiler_params=pltpu.CompilerParams(dimension_semantics=("arbitrary",),
                                             vmem_limit_bytes=VMEM_LIMIT),
        name="inproj",
    )(x2, ng, wu, wv, wqkv, wz, wab, gmg, ws, bsb, cw, alog, dtb, tri)


def _gdn_body(q_ref, k_ref, v_ref, z_ref, gb_ref, gr_ref, ng_ref, ob_ref, s_ref, *, nchunk,
              group_size):
    j = pl.program_id(1)

    @pl.when(j == 0)
    def _():
        s_ref[...] = jnp.zeros(s_ref.shape, F32)

    c = DN_CHUNK
    n = DN_HEADS * c
    ri = lax.broadcasted_iota(I32, (n, n), 0)
    ci = lax.broadcasted_iota(I32, (n, n), 1)
    same = (ri // c) == (ci // c)
    incl = same & ((ri % c) >= (ci % c))
    strict = same & ((ri % c) > (ci % c))
    ng = ng_ref[...]

    def stack(a):
        return jnp.concatenate([a[:, h * HEAD_DIM:(h + 1) * HEAD_DIM] for h in range(DN_HEADS)], axis=0)

    def prepare(ic):
        r0 = pl.multiple_of(ic * c, c)
        kst = stack(k_ref[pl.ds(r0, c), :])
        qst = stack(q_ref[pl.ds(r0, c), :])
        vst = stack(v_ref[pl.ds(r0, c), :])
        gbc = gb_ref[pl.ds(r0, c), :]
        grow = gr_ref[pl.ds(ic, 1), :]
        gcol = jnp.concatenate([gbc[:, h:h + 1] for h in range(DN_HEADS)], axis=0)
        bcol = jnp.concatenate([gbc[:, DN_HEADS + h:DN_HEADS + h + 1] for h in range(DN_HEADS)], axis=0)
        glast = jnp.concatenate(
            [jnp.broadcast_to(gbc[c - 1:c, h:h + 1], (c, 1)) for h in range(DN_HEADS)], axis=0)
        decay = jnp.where(incl, jnp.exp(jnp.where(incl, gcol - grow, 0.0)), 0.0)
        kf = kst.astype(F32)
        kb = kf * bcol
        lmat = jnp.where(strict, _dot_nt(kb.astype(BF16), kst) * decay, 0.0)
        eg = jnp.exp(gcol)
        rhs = jnp.concatenate([vst.astype(F32) * bcol, kb * eg], axis=1)
        attn = jnp.where(incl, _dot_nt(qst, kst) * decay, 0.0).astype(BF16)
        qd = (qst.astype(F32) * eg).astype(BF16)
        ke = (kf * jnp.exp(glast - gcol)).astype(BF16)
        return dict(r0=r0, gbc=gbc, lmat=lmat, rhs=rhs, attn=attn, qd=qd, ke=ke)

    def advance_state(p, sol):
        u = sol[:, :HEAD_DIM]
        wb = sol[:, HEAD_DIM:].astype(BF16)
        vn, qs = [], []
        for h in range(DN_HEADS):
            rs = slice(h * c, (h + 1) * c)
            sb = s_ref[h].astype(BF16)
            vn.append(u[rs] - _dot(wb[rs], sb))
            qs.append(_dot(p["qd"][rs], sb))
        vnb = jnp.concatenate(vn, axis=0).astype(BF16)
        o = jnp.concatenate(qs, axis=0) + _dot(p["attn"], vnb)
        zc = z_ref[pl.ds(p["r0"], c), :]
        for h in range(DN_HEADS):
            rs = slice(h * c, (h + 1) * c)
            sl = slice(h * HEAD_DIM, (h + 1) * HEAD_DIM)
            s_ref[h] = (s_ref[h] * jnp.exp(p["gbc"][c - 1:c, h:h + 1])
                        + _dot_tn(p["ke"][rs], vnb[rs]))
            zf = zc[:, sl].astype(F32)
            ob_ref[pl.ds(p["r0"], c), sl] = (_rms(o[rs], ng) * (zf * _sigmoid(zf))).astype(BF16)

    def group(ig, carry):
        ps = [prepare(ig * group_size + b) for b in range(group_size)]
        sol = [p["rhs"] for p in ps]
        pw = [-p["lmat"] for p in ps]
        for step in range(6):
            pb = [x.astype(BF16) for x in pw]
            sol = [s + _dot(xb, s.astype(BF16)) for s, xb in zip(sol, pb)]
            if step < 5:
                pw = [_dot(xb, xb) for xb in pb]
        for p, s in zip(ps, sol):
            advance_state(p, s)
        return carry

    lax.fori_loop(0, nchunk // group_size, group, 0)


def _gdn(q, k, v, z, gb, grow, ng, *, batch, seq):
    tm = ROW_TILE
    nchunk = tm // DN_CHUNK
    steps = seq // tm
    rows = lambda b, j: (b * steps + j, 0)
    return pl.pallas_call(
        functools.partial(_gdn_body, nchunk=nchunk, group_size=GDN_GROUP),
        grid=(batch, steps),
        in_specs=[pl.BlockSpec((tm, DN_KEY), rows), pl.BlockSpec((tm, DN_KEY), rows),
                  pl.BlockSpec((tm, DN_VAL), rows), pl.BlockSpec((tm, DN_VAL), rows),
                  pl.BlockSpec((tm, LANES), rows),
                  pl.BlockSpec((nchunk, DN_HEADS * DN_CHUNK), rows),
                  pl.BlockSpec((1, HEAD_DIM), lambda b, j: (0, 0))],
        out_specs=pl.BlockSpec((tm, DN_VAL), rows),
        out_shape=jax.ShapeDtypeStruct((batch * seq, DN_VAL), BF16),
        scratch_shapes=[pltpu.VMEM((DN_HEADS, DN_DK, HEAD_DIM), F32)],
        compiler_params=pltpu.CompilerParams(dimension_semantics=("arbitrary", "arbitrary"),
                                             vmem_limit_bytes=VMEM_LIMIT),
        name="gdn",
    )(q, k, v, z, gb, grow, ng)


def _outproj_body(oa_ref, ob_ref, x_ref, woa_ref, wob_ref, ng_ref, rw_ref, rb_ref, tri_ref,
                  h_ref, hp_ref, meta_ref, gate_ref, cnt_ref, run_ref, *, tm):
    i = pl.program_id(0)

    @pl.when(i == 0)
    def _():
        run_ref[...] = jnp.zeros(run_ref.shape, F32)

    h = x_ref[...] + _dot(oa_ref[...], woa_ref[...]) + _dot(ob_ref[...], wob_ref[...])
    h_ref[...] = h
    hb = _rms(h, ng_ref[...]).astype(BF16)
    half = D_MODEL // 2
    lo = pltpu.bitcast(hb[:, :half].astype(F32), U32) >> 16
    hi = pltpu.bitcast(hb[:, half:].astype(F32), U32) & jnp.uint32(0xFFFF0000)
    hp_ref[...] = lo | hi

    logits = _dot(hb, rw_ref[...]) + rb_ref[...]
    lane = lax.broadcasted_iota(I32, (tm, LANES), 1)
    lanef = lane.astype(F32)
    work = logits
    onehot = jnp.zeros((tm, LANES), F32)
    vals, sels = [], []
    for _ in range(TOP_K):
        m = jnp.max(work, axis=-1, keepdims=True)
        idx = jnp.min(jnp.where(work == m, lanef, float(LANES)), axis=-1, keepdims=True)
        sel = lanef == idx
        work = jnp.where(sel, -3e38, work)
        onehot = onehot + jnp.where(sel, 1.0, 0.0)
        vals.append(m)
        sels.append((sel, idx))
    ex = [jnp.exp(v - vals[0]) for v in vals]
    den = ex[0] + ex[1] + ex[2] + ex[3]
    pref = _dot(tri_ref[...], onehot.astype(BF16)) + run_ref[0:1, :]
    meta = jnp.zeros((tm, LANES), F32)
    gates = jnp.zeros((tm, LANES), F32)
    for kk in range(TOP_K):
        sel, idx = sels[kk]
        rank = jnp.sum(jnp.where(sel, pref, 0.0), axis=-1, keepdims=True)
        meta = meta + jnp.where(lane == kk, idx, 0.0) + jnp.where(lane == TOP_K + kk, rank, 0.0)
        gates = gates + jnp.where(lane == kk, ex[kk] / den, 0.0)
    meta_ref[...] = meta.T[0:8, :].astype(I32)
    gate_ref[...] = gates
    run = run_ref[...] + jnp.sum(onehot, axis=0, keepdims=True)
    run_ref[...] = run
    cnt_ref[...] = run.astype(I32)


def _outproj(oa, ob, x2, woa, wob, ng, rw, rb, tri):
    t = x2.shape[0]
    tm = ROW_TILE
    row = lambda i: (i, 0)
    full = lambda a: pl.BlockSpec(a.shape, (lambda i: (0,) * a.ndim))
    out_shapes = (
        jax.ShapeDtypeStruct((t, D_MODEL), F32),
        jax.ShapeDtypeStruct((t, D_MODEL // 2), U32),
        jax.ShapeDtypeStruct((8, t), I32),
        jax.ShapeDtypeStruct((t, LANES), F32),
        jax.ShapeDtypeStruct((8, LANES), I32),
    )
    return pl.pallas_call(
        functools.partial(_outproj_body, tm=tm),
        grid=(t // tm,),
        in_specs=[pl.BlockSpec((tm, GM_WIDTH), row), pl.BlockSpec((tm, DN_VAL), row),
                  pl.BlockSpec((tm, D_MODEL), row), full(woa), full(wob), full(ng), full(rw),
                  full(rb), full(tri)],
        out_specs=(pl.BlockSpec((tm, D_MODEL), row), pl.BlockSpec((tm, D_MODEL // 2), row),
                   pl.BlockSpec((8, tm), lambda i: (0, i)), pl.BlockSpec((tm, LANES), row),
                   pl.BlockSpec((8, LANES), lambda i: (0, 0))),
        out_shape=out_shapes,
        scratch_shapes=[pltpu.VMEM((8, LANES), F32)],
        compiler_params=pltpu.CompilerParams(dimension_semantics=("arbitrary",),
                                             vmem_limit_bytes=VMEM_LIMIT),
        name="outproj",
    )(oa, ob, x2, woa, wob, ng, rw, rb, tri)


def _sc_dispatch(hp, dest_k, n_rows):
    t, d = hp.shape
    mesh = plsc.VectorSubcoreMesh(core_axis_name="c", subcore_axis_name="s")
    nc, workers = mesh.num_cores, mesh.num_cores * mesh.num_subcores
    chunk = SC_GATHER_ROWS
    per_w = t // workers
    pairs = per_w // (2 * chunk)
    assert per_w * workers == t and pairs * 2 * chunk == per_w

    @functools.partial(
        pl.kernel, mesh=mesh, out_type=jax.ShapeDtypeStruct((n_rows, d), hp.dtype),
        scratch_types=[pltpu.VMEM((2, TOP_K, chunk), I32), pltpu.VMEM((2, chunk, d), hp.dtype),
                       pltpu.SemaphoreType.DMA((2,)), pltpu.SemaphoreType.DMA((2,))],
        name="sc_dispatch")
    def scatter(hp_hbm, idx_hbm, xin_hbm, idx_v, rows_v, lsem, ssem):
        base_w = (lax.axis_index("s") * nc + lax.axis_index("c")) * per_w

        def load(b, base):
            return pltpu.make_async_copy(hp_hbm.at[pl.ds(base, chunk)], rows_v.at[b], lsem.at[b])

        def put(b, kk):
            return pltpu.make_async_copy(rows_v.at[b], xin_hbm.at[idx_v.at[b, kk]], ssem.at[b])

        def body(j, carry):
            for b in range(2):
                base = base_w + (2 * j + b) * chunk

                @pl.when(j > 0)
                def _():
                    for kk in range(TOP_K):
                        put(b, kk).wait()

                for kk in range(TOP_K):
                    pltpu.sync_copy(idx_hbm.at[pl.ds(kk * t + base, chunk)], idx_v.at[b, kk])
                load(b, base).start()
            for b in range(2):
                load(b, base_w).wait()
                for kk in range(TOP_K):
                    put(b, kk).start()
            return carry

        lax.fori_loop(0, pairs, body, 0)
        for b in range(2):
            for kk in range(TOP_K):
                put(b, kk).wait()

    return scatter(hp, dest_k)


def _ffn_body(be_ref, nv_ref, br_ref, ep_ref, es_ref, x_ref, wgu_hbm, bgu_ref, wd_hbm, bd_ref, y_ref,
              wgub_ref, wdb_ref, wguf_ref, wdf_ref, gsem, dsem):
    i = pl.program_id(0)

    @pl.when(i >= nv_ref[0])
    def _():
        y_ref[...] = jnp.zeros(y_ref.shape, U32)

    def fetch(pos):
        e, slot = es_ref[pos], pos % 2
        return (pltpu.make_async_copy(wgu_hbm.at[e], wguf_ref.at[slot], gsem.at[slot]),
                pltpu.make_async_copy(wd_hbm.at[e], wdf_ref.at[slot], dsem.at[slot]))

    @pl.when((i < nv_ref[0]) & ((i == 0) | (be_ref[i] != be_ref[jnp.maximum(i - 1, 0)])))
    def _():
        pos = ep_ref[i]

        @pl.when(i == 0)
        def _():
            for cp in fetch(pos):
                cp.start()

        for cp in fetch(pos):
            cp.wait()

        @pl.when(pos + 1 < es_ref[N_EXPERTS])
        def _():
            for cp in fetch(pos + 1):
                cp.start()

        slot = pos % 2
        wgub_ref[...] = wguf_ref[slot].astype(BF16)
        wdb_ref[...] = wdf_ref[slot].astype(BF16)

    nsub = FFN_SUBBLOCKS
    rsub = x_ref.shape[0] // nsub

    def compute(active):
        half = D_MODEL // 2
        row = lax.broadcasted_iota(I32, (rsub, 1), 0)
        gus = []
        for sb in range(active):
            rs = slice(sb * rsub, (sb + 1) * rsub)
            xp = jnp.where(row + sb * rsub < br_ref[i], x_ref[rs, :], jnp.uint32(0))
            lo = pltpu.bitcast(xp << 16, F32).astype(BF16)
            hi = pltpu.bitcast(xp & jnp.uint32(0xFFFF0000), F32).astype(BF16)
            gus.append(_dot(lo, wgub_ref[:half, :]) + _dot(hi, wgub_ref[half:, :]) + bgu_ref[0])
        ys = []
        for gu in gus:
            gate = jnp.minimum(gu[:, :D_FF], SWIGLU_LIMIT)
            up = jnp.clip(gu[:, D_FF:], -SWIGLU_LIMIT, SWIGLU_LIMIT)
            act = (up + 1.0) * (gate * _sigmoid(SWIGLU_ALPHA * gate))
            ys.append(_dot(act.astype(BF16), wdb_ref[...]) + bd_ref[0])
        for sb, y in enumerate(ys):
            ylo = pltpu.bitcast(y[:, :half].astype(BF16).astype(F32), U32) >> 16
            yhi = pltpu.bitcast(y[:, half:].astype(BF16).astype(F32), U32) & jnp.uint32(0xFFFF0000)
            y_ref[sb * rsub:(sb + 1) * rsub, :] = ylo | yhi
        if active < nsub:
            y_ref[active * rsub:, :] = jnp.zeros(((nsub - active) * rsub, half), U32)

    active = (br_ref[i] + rsub - 1) // rsub
    for n in range(1, nsub + 1):
        pl.when((i < nv_ref[0]) & (active == n))(functools.partial(compute, n))


def _ffn(blk_e, nvalid, blk_rows, blk_pos, used, xin, wgu, bgu, wd, bd):
    p = xin.shape[0]
    bm = FFN_BLOCK
    rows = lambda i, be, nv, br, ep, es: (jnp.minimum(i, nv[0] - 1), 0)
    wsel = lambda i, be, nv, br, ep, es: (be[i], 0, 0)
    return pl.pallas_call(
        _ffn_body,
        grid_spec=pltpu.PrefetchScalarGridSpec(
            num_scalar_prefetch=5,
            grid=(p // bm,),
            in_specs=[pl.BlockSpec((bm, D_MODEL // 2), rows),
                      pl.BlockSpec(memory_space=pl.ANY),
                      pl.BlockSpec((1, 1, 2 * D_FF), wsel),
                      pl.BlockSpec(memory_space=pl.ANY),
                      pl.BlockSpec((1, 1, D_MODEL), wsel)],
            out_specs=pl.BlockSpec((bm, D_MODEL // 2), lambda i, be, nv, br, ep, es: (i, 0)),
            scratch_shapes=[pltpu.VMEM((D_MODEL, 2 * D_FF), BF16), pltpu.VMEM((D_FF, D_MODEL), BF16),
                            pltpu.VMEM((2, D_MODEL, 2 * D_FF), F32), pltpu.VMEM((2, D_FF, D_MODEL), F32),
                            pltpu.SemaphoreType.DMA((2,)), pltpu.SemaphoreType.DMA((2,))],
        ),
        out_shape=jax.ShapeDtypeStruct((p, D_MODEL // 2), U32),
        compiler_params=pltpu.CompilerParams(dimension_semantics=("arbitrary",),
                                             vmem_limit_bytes=VMEM_LIMIT),
        name="ffn",
    )(blk_e, nvalid, blk_rows, blk_pos, used, xin, wgu, bgu, wd, bd)


def _sc_gather(table, idx):
    r, d = idx.shape[0], table.shape[1]
    mesh = plsc.VectorSubcoreMesh(core_axis_name="c", subcore_axis_name="s")
    nc, workers = mesh.num_cores, mesh.num_cores * mesh.num_subcores
    chunk = SC_GATHER_ROWS
    per_w = r // workers
    pairs = per_w // (2 * chunk)
    assert per_w * workers == r and pairs * 2 * chunk == per_w

    @functools.partial(
        pl.kernel, mesh=mesh, out_type=jax.ShapeDtypeStruct((r, d), table.dtype),
        scratch_types=[pltpu.VMEM((2, chunk), I32), pltpu.VMEM((2, chunk, d), table.dtype),
                       pltpu.SemaphoreType.DMA((2,)), pltpu.SemaphoreType.DMA((2,))],
        name="sc_gather")
    def gather(table_hbm, idx_hbm, out_hbm, idx_v, rows_v, gsem, wsem):
        base_w = (lax.axis_index("s") * nc + lax.axis_index("c")) * per_w

        def fetch(b):
            return pltpu.make_async_copy(table_hbm.at[idx_v.at[b]], rows_v.at[b], gsem.at[b])

        def flush(b, base):
            return pltpu.make_async_copy(rows_v.at[b], out_hbm.at[pl.ds(base, chunk)], wsem.at[b])

        def body(j, carry):
            for b in range(2):
                base = base_w + (2 * j + b) * chunk

                @pl.when(j > 0)
                def _():
                    flush(b, base).wait()

                pltpu.sync_copy(idx_hbm.at[pl.ds(base, chunk)], idx_v.at[b])
                fetch(b).start()
            for b in range(2):
                fetch(b).wait()
                flush(b, base_w + (2 * j + b) * chunk).start()
            return carry

        lax.fori_loop(0, pairs, body, 0)
        for b in range(2):
            flush(b, base_w).wait()

    return gather(table, idx)


def _combine_body(h_ref, gate_ref, fg_ref, *refs):
    yg_refs, o_ref = refs[:TOP_K], refs[-1]
    half = D_MODEL // 2
    gates = gate_ref[...]
    h = h_ref[...]
    lo, hi = h[:, :half], h[:, half:]
    for kk in range(TOP_K):
        yp = yg_refs[kk][...]
        g = gates[:, kk:kk + 1]
        lo = lo + g * pltpu.bitcast(yp << 16, F32)
        hi = hi + g * pltpu.bitcast(yp & jnp.uint32(0xFFFF0000), F32)
    out = jnp.concatenate([lo, hi], axis=1)
    o_ref[...] = _rms(out, fg_ref[...])


def _combine(h, gates, fg, ygath, out_prev, *, part, parts):
    t = h.shape[0]
    tm = COMBINE_TILE
    nt = t // parts // tm
    row = lambda i: (part * nt + i, 0)
    slot = lambda kk: pl.BlockSpec((tm, D_MODEL // 2), lambda i: (kk * nt + i, 0))
    in_specs = [pl.BlockSpec((tm, D_MODEL), row), pl.BlockSpec((tm, LANES), row),
                pl.BlockSpec((1, D_MODEL), lambda i: (0, 0))] + [slot(kk) for kk in range(TOP_K)]
    args = [h, gates, fg] + [ygath] * TOP_K
    aliases = {}
    if out_prev is not None:
        aliases = {len(args): 0}
        in_specs.append(pl.BlockSpec(memory_space=pl.ANY))
        args.append(out_prev)
    return pl.pallas_call(
        _combine_body,
        grid=(nt,),
        in_specs=in_specs,
        out_specs=pl.BlockSpec((tm, D_MODEL), row),
        out_shape=jax.ShapeDtypeStruct((t, D_MODEL), F32),
        input_output_aliases=aliases,
        compiler_params=pltpu.CompilerParams(dimension_semantics=("arbitrary",),
                                             vmem_limit_bytes=VMEM_LIMIT),
        name="combine",
    )(*args)


def _block_tril(n, chunk, strict):
    r = jnp.arange(n)[:, None]
    c = jnp.arange(n)[None, :]
    keep = ((r // chunk) == (c // chunk)) & ((r > c) if strict else (r >= c))
    return keep.astype(BF16)


def _pad_lanes(a, fill=0.0):
    a = a.reshape(1, -1).astype(F32)
    return jnp.pad(a, ((0, 0), (0, LANES - a.shape[1])), constant_values=fill)


def _layer(h, norm_mix_g, w_in, gm_norm_g, gm_ws, gm_bs, dn_conv_w, dn_a_log, dn_dt_bias,
           dn_norm_g, w_out, norm_ffn_g, router_w, router_b, exp_w_gu, exp_b_gu, exp_w_down,
           exp_b_down, out_g):
    batch, seq, d = h.shape
    t = batch * seq
    x2 = h.reshape(t, d)

    c0, c1, c2 = GM_WIDTH, 2 * GM_WIDTH, 2 * GM_WIDTH + 3 * DN_KEY
    c3 = c2 + DN_VAL
    wb = w_in.astype(BF16)
    wu, wv, wqkv, wz = wb[:, :c0], wb[:, c0:c1], wb[:, c1:c2], wb[:, c2:c3]
    wab = jnp.pad(wb[:, c3:], ((0, 0), (0, LANES - 2 * DN_HEADS)))
    gmg = gm_norm_g.reshape(1, GM_WIDTH).astype(F32)
    bsb = jnp.repeat(gm_bs.T, HEAD_DIM, axis=1).astype(F32)
    alog = _pad_lanes(dn_a_log)
    dtb = _pad_lanes(dn_dt_bias)
    tri_incl = _block_tril(ROW_TILE, DN_CHUNK, strict=False)

    oa, q, k, v, z, gb, gt = _inproj(
        x2, norm_mix_g.reshape(1, d), wu, wv, wqkv, wz, wab, gmg, gm_ws.astype(F32), bsb,
        dn_conv_w.astype(F32), alog, dtb, tri_incl, seq=seq)

    grow = gt[:DN_HEADS].reshape(DN_HEADS, t // DN_CHUNK, DN_CHUNK).transpose(1, 0, 2)
    grow = grow.reshape(t // DN_CHUNK, DN_HEADS * DN_CHUNK)
    ob = _gdn(q, k, v, z, gb, grow, dn_norm_g.reshape(1, HEAD_DIM).astype(F32), batch=batch, seq=seq)

    wo = w_out.astype(BF16)
    rw = jnp.pad(router_w.astype(BF16), ((0, 0), (0, LANES - N_EXPERTS)))
    rb = _pad_lanes(router_b, fill=NEG_BIG)
    tri_strict = _block_tril(ROW_TILE, ROW_TILE, strict=True)
    hres, hp, meta, gates, cnt = _outproj(oa, ob, x2, wo[:GM_WIDTH], wo[GM_WIDTH:],
                                          norm_ffn_g.reshape(1, d), rw, rb, tri_strict)

    bm = FFN_BLOCK
    counts = cnt[0, :N_EXPERTS]
    padded = (counts + bm - 1) // bm * bm
    pad_end = jnp.cumsum(padded)
    pad_start = (pad_end - padded).astype(I32)
    n_blocks = (t * TOP_K + N_EXPERTS * bm) // bm
    nvalid = (pad_end[-1] // bm).astype(I32).reshape(1)
    blk = jnp.minimum(jnp.arange(n_blocks, dtype=I32), nvalid[0] - 1)
    blk_e = jnp.minimum(jnp.sum(pad_end[None, :] <= (blk * bm)[:, None], axis=1), N_EXPERTS - 1).astype(I32)
    eid = meta[:TOP_K]
    start_of = jnp.sum(jnp.where(eid[..., None] == jnp.arange(N_EXPERTS, dtype=I32), pad_start, 0), axis=-1)
    dest = start_of + meta[TOP_K:2 * TOP_K]

    xin = _sc_dispatch(hp, dest.reshape(-1), n_blocks * bm)
    eids = jnp.arange(N_EXPERTS, dtype=I32)
    of_blk = blk_e[:, None] == eids
    blk_rows = jnp.clip(jnp.sum(jnp.where(of_blk, counts + pad_start, 0), axis=1) - blk * bm, 0, bm)
    has_rows = counts > 0
    used = jnp.minimum(jnp.sort(jnp.where(has_rows, eids, N_EXPERTS + eids)), N_EXPERTS - 1)
    used = jnp.concatenate([used, jnp.sum(has_rows).reshape(1)]).astype(I32)
    blk_pos = jnp.sum(jnp.where(of_blk, jnp.cumsum(has_rows) - 1, 0), axis=1).astype(I32)
    blk_rows = blk_rows.astype(I32)
    y = _ffn(blk_e, nvalid, blk_rows, blk_pos, used, xin, exp_w_gu, exp_b_gu[:, None, :].astype(F32),
             exp_w_down, exp_b_down[:, None, :].astype(F32))
    out = None
    tp = t // COMBINE_PARTS
    for part in range(COMBINE_PARTS):
        ygath = _sc_gather(y, dest[:, part * tp:(part + 1) * tp].reshape(-1))
        out = _combine(hres, gates, out_g.reshape(1, d).astype(F32), ygath, out,
                       part=part, parts=COMBINE_PARTS)
    return out.reshape(batch, seq, d)


def kernel(x, norm_mix_g, w_in, gm_norm_g, gm_ws, gm_bs, dn_conv_w, dn_a_log, dn_dt_bias, dn_norm_g, w_out, norm_ffn_g, router_w, router_b, exp_w_gu, exp_b_gu, exp_w_down, exp_b_down, final_norm_g):
    depth = norm_mix_g.shape[0]
    assert depth == 1, "single-layer problem"
    return _layer(x, norm_mix_g[0], w_in[0], gm_norm_g[0], gm_ws[0], gm_bs[0], dn_conv_w[0],
                  dn_a_log[0], dn_dt_bias[0], dn_norm_g[0], w_out[0], norm_ffn_g[0], router_w[0],
                  router_b[0], exp_w_gu[0], exp_b_gu[0], exp_w_down[0], exp_b_down[0], final_norm_g)
```

```python
import functools

import jax
import jax.numpy as jnp
from jax import lax
from jax.experimental import pallas as pl
from jax.experimental.pallas import tpu as pltpu
from jax.experimental.pallas import tpu_sc as plsc

F32 = jnp.float32
BF16 = jnp.bfloat16
I32 = jnp.int32
U32 = jnp.uint32

D_MODEL = 1024
HEAD_DIM = 128
GM_HEADS = 4
GM_WIDTH = GM_HEADS * HEAD_DIM
GM_CHUNK = 128
DN_HEADS = 4
DN_DK = 128
DN_KEY = DN_HEADS * DN_DK
DN_VAL = DN_HEADS * HEAD_DIM
DN_CHUNK = 64
CONV_W = 4
N_EXPERTS = 32
TOP_K = 4
D_FF = D_MODEL
SWIGLU_LIMIT = 7.0
SWIGLU_ALPHA = 1.702
EPS = 1e-6

LANES = 128
INPROJ_COLS = 512
INPROJ_SUBBLOCKS = 2
SUBLANES = 8
ROW_TILE = 512
FFN_BLOCK = 1024
FFN_SUBBLOCKS = 4
SC_GATHER_ROWS = 64
COMBINE_PARTS = 4
COMBINE_TILE = 512
VMEM_LIMIT = 56 * 1024 * 1024
NEG_BIG = -1e30
GDN_GROUP = 4


def _dot(a, b):
    return jnp.dot(a, b, preferred_element_type=F32)


def _dot_nt(a, b):
    return lax.dot_general(a, b, (((1,), (1,)), ((), ())), preferred_element_type=F32)


def _dot_tn(a, b):
    return lax.dot_general(a, b, (((0,), (0,)), ((), ())), preferred_element_type=F32)


def _rms(x, g):
    return x * lax.rsqrt(jnp.mean(x * x, axis=-1, keepdims=True) + EPS) * g


def _gelu(x):
    return 0.5 * x * (1.0 + lax.erf(x * (2.0 ** -0.5)))


def _sigmoid(x):
    return 1.0 / (1.0 + jnp.exp(-x))


def _softplus(x):
    return jnp.maximum(x, 0.0) + jnp.log1p(jnp.exp(-jnp.abs(x)))


def _inproj_body(x_ref, ng_ref, wu_ref, wv_ref, wqkv_ref, wz_ref, wab_ref, gmg_ref, ws_ref,
                 bs_ref, cw_ref, alog_ref, dtb_ref, tri_ref,
                 oa_ref, q_ref, k_ref, v_ref, z_ref, gb_ref, gt_ref, cbuf_ref, ybuf_ref,
                 *, tm, tiles_per_seq):
    i = pl.program_id(0)
    ybuf_ref[...] = _rms(x_ref[...], ng_ref[...]).astype(BF16)
    nb = INPROJ_COLS
    heads_per_block = nb // HEAD_DIM

    ri = lax.broadcasted_iota(I32, (GM_CHUNK, GM_CHUNK), 0)
    ci = lax.broadcasted_iota(I32, (GM_CHUNK, GM_CHUNK), 1)
    causal = ri >= ci

    def gmlp_matmul(blk, r0, r1):
        cs = slice(blk * nb, (blk + 1) * nb)
        return _dot(ybuf_ref[r0:r1, :], wu_ref[:, cs]), _dot(ybuf_ref[r0:r1, :], wv_ref[:, cs])

    def gmlp_epilogue(blk, r0, r1, uv):
        u, vg = _gelu(uv[0]), _gelu(uv[1])
        for hh in range(heads_per_block):
            h = blk * heads_per_block + hh
            sl = slice(h * HEAD_DIM, (h + 1) * HEAD_DIM)
            ls = slice(hh * HEAD_DIM, (hh + 1) * HEAD_DIM)
            vh = _rms(vg[:, ls], gmg_ref[:, sl]).astype(BF16)
            wm = jnp.where(causal, ws_ref[h], 0.0).astype(BF16)
            for c in range((r1 - r0) // GM_CHUNK):
                rs = slice(c * GM_CHUNK, (c + 1) * GM_CHUNK)
                gate = _dot(wm, vh[rs]) + bs_ref[:, sl]
                oa_ref[r0 + c * GM_CHUNK:r0 + (c + 1) * GM_CHUNK, sl] = (u[rs, ls] * gate).astype(BF16)

    def qkv_matmul(blk, r0, r1):
        return _dot(ybuf_ref[r0:r1, :], wqkv_ref[:, blk * nb:(blk + 1) * nb])

    def qkv_epilogue(blk, r0, r1, pq):
        cs = slice(blk * nb, (blk + 1) * nb)
        cbuf_ref[8 + r0:8 + r1, cs] = pq
        acc = cw_ref[CONV_W - 1:CONV_W, cs] * pq
        for j in range(CONV_W - 1):
            off = 8 - (CONV_W - 1) + j
            acc = acc + cw_ref[j:j + 1, cs] * cbuf_ref[off + r0:off + r1, cs]
        if r1 == tm:
            cbuf_ref[0:8, cs] = pq[r1 - r0 - 8:, :]
        s = acc * _sigmoid(acc)
        for hh in range(heads_per_block):
            c0 = blk * nb + hh * HEAD_DIM
            sh = s[:, hh * HEAD_DIM:(hh + 1) * HEAD_DIM]
            if c0 < DN_KEY:
                q_ref[r0:r1, c0:c0 + HEAD_DIM] = (
                    sh * lax.rsqrt(jnp.sum(sh * sh, axis=-1, keepdims=True) + EPS)
                    * (DN_DK ** -0.5)).astype(BF16)
            elif c0 < 2 * DN_KEY:
                k_ref[r0:r1, c0 - DN_KEY:c0 - DN_KEY + HEAD_DIM] = (
                    sh * lax.rsqrt(jnp.sum(sh * sh, axis=-1, keepdims=True) + EPS)).astype(BF16)
            else:
                v_ref[r0:r1, c0 - 2 * DN_KEY:c0 - 2 * DN_KEY + HEAD_DIM] = sh.astype(BF16)

    def z_matmul(blk, r0, r1):
        return _dot(ybuf_ref[r0:r1, :], wz_ref[:, blk * nb:(blk + 1) * nb])

    def z_epilogue(blk, r0, r1, zz):
        z_ref[r0:r1, blk * nb:(blk + 1) * nb] = zz.astype(BF16)

    @pl.when(i % tiles_per_seq == 0)
    def _():
        cbuf_ref[0:8, :] = jnp.zeros((8, 3 * DN_KEY), F32)

    stages = ([(gmlp_matmul, gmlp_epilogue, blk) for blk in range(GM_WIDTH // nb)]
              + [(qkv_matmul, qkv_epilogue, blk) for blk in range(3 * DN_KEY // nb)]
              + [(z_matmul, z_epilogue, blk) for blk in range(DN_VAL // nb)])
    rsub = tm // INPROJ_SUBBLOCKS
    spans = [(sb * rsub, (sb + 1) * rsub) for sb in range(INPROJ_SUBBLOCKS)]
    for matmul, epilogue, blk in stages:
        results = [matmul(blk, r0, r1) for r0, r1 in spans]
        for (r0, r1), res in zip(spans, results):
            epilogue(blk, r0, r1, res)

    ab = _dot(ybuf_ref[...], wab_ref[...])
    g = -jnp.exp(alog_ref[...]) * _softplus(ab + dtb_ref[...])
    beta = _sigmoid(ab)
    g_hi = g.astype(BF16)
    r1 = g - g_hi.astype(F32)
    g_mid = r1.astype(BF16)
    g_lo = (r1 - g_mid.astype(F32)).astype(BF16)
    tri = tri_ref[...]
    gc = _dot(tri, g_hi) + _dot(tri, g_mid) + _dot(tri, g_lo)
    lane = lax.broadcasted_iota(I32, (tm, LANES), 1)
    gbv = jnp.where(lane < DN_HEADS, gc, beta)
    gb_ref[...] = gbv
    gt_ref[...] = gbv.T[0:8, :]


def _inproj(x2, ng, wu, wv, wqkv, wz, wab, gmg, ws, bsb, cw, alog, dtb, tri, *, seq):
    t = x2.shape[0]
    tm = ROW_TILE
    const2 = lambda i: (0, 0)
    row = lambda i: (i, 0)
    full = lambda a: pl.BlockSpec(a.shape, (lambda i: (0,) * a.ndim))
    out_shapes = (
        jax.ShapeDtypeStruct((t, GM_WIDTH), BF16),
        jax.ShapeDtypeStruct((t, DN_KEY), BF16),
        jax.ShapeDtypeStruct((t, DN_KEY), BF16),
        jax.ShapeDtypeStruct((t, DN_VAL), BF16),
        jax.ShapeDtypeStruct((t, DN_VAL), BF16),
        jax.ShapeDtypeStruct((t, LANES), F32),
        jax.ShapeDtypeStruct((8, t), F32),
    )
    return pl.pallas_call(
        functools.partial(_inproj_body, tm=tm, tiles_per_seq=seq // tm),
        grid=(t // tm,),
        in_specs=[pl.BlockSpec((tm, D_MODEL), row), full(ng), full(wu), full(wv), full(wqkv),
                  full(wz), full(wab), full(gmg), full(ws), full(bsb), full(cw), full(alog),
                  full(dtb), full(tri)],
        out_specs=(pl.BlockSpec((tm, GM_WIDTH), row), pl.BlockSpec((tm, DN_KEY), row),
                   pl.BlockSpec((tm, DN_KEY), row), pl.BlockSpec((tm, DN_VAL), row),
                   pl.BlockSpec((tm, DN_VAL), row), pl.BlockSpec((tm, LANES), row),
                   pl.BlockSpec((8, tm), lambda i: (0, i))),
        out_shape=out_shapes,
        scratch_shapes=[pltpu.VMEM((tm + 8, 3 * DN_KEY), F32), pltpu.VMEM((tm, D_MODEL), BF16)],
        compiler_params=pltpu.CompilerParams(dimension_semantics=("arbitrary",),
                                             vmem_limit_bytes=VMEM_LIMIT),
        name="inproj",
    )(x2, ng, wu, wv, wqkv, wz, wab, gmg, ws, bsb, cw, alog, dtb, tri)


def _gdn_body(q_ref, k_ref, v_ref, z_ref, gb_ref, gr_ref, ng_ref, ob_ref, s_ref, *, nchunk,
              group_size):
    j = pl.program_id(1)

    @pl.when(j == 0)
    def _():
        s_ref[...] = jnp.zeros(s_ref.shape, F32)

    c = DN_CHUNK
    n = DN_HEADS * c
    ri = lax.broadcasted_iota(I32, (n, n), 0)
    ci = lax.broadcasted_iota(I32, (n, n), 1)
    same = (ri // c) == (ci // c)
    incl = same & ((ri % c) >= (ci % c))
    strict = same & ((ri % c) > (ci % c))
    ng = ng_ref[...]

    def stack(a):
        return jnp.concatenate([a[:, h * HEAD_DIM:(h + 1) * HEAD_DIM] for h in range(DN_HEADS)], axis=0)

    def prepare(ic):
        r0 = pl.multiple_of(ic * c, c)
        kst = stack(k_ref[pl.ds(r0, c), :])
        qst = stack(q_ref[pl.ds(r0, c), :])
        vst = stack(v_ref[pl.ds(r0, c), :])
        gbc = gb_ref[pl.ds(r0, c), :]
        grow = gr_ref[pl.ds(ic, 1), :]
        gcol = jnp.concatenate([gbc[:, h:h + 1] for h in range(DN_HEADS)], axis=0)
        bcol = jnp.concatenate([gbc[:, DN_HEADS + h:DN_HEADS + h + 1] for h in range(DN_HEADS)], axis=0)
        glast = jnp.concatenate(
            [jnp.broadcast_to(gbc[c - 1:c, h:h + 1], (c, 1)) for h in range(DN_HEADS)], axis=0)
        decay = jnp.where(incl, jnp.exp(jnp.where(incl, gcol - grow, 0.0)), 0.0)
        kf = kst.astype(F32)
        kb = kf * bcol
        lmat = jnp.where(strict, _dot_nt(kb.astype(BF16), kst) * decay, 0.0)
        eg = jnp.exp(gcol)
        rhs = jnp.concatenate([vst.astype(F32) * bcol, kb * eg], axis=1)
        attn = jnp.where(incl, _dot_nt(qst, kst) * decay, 0.0).astype(BF16)
        qd = (qst.astype(F32) * eg).astype(BF16)
        ke = (kf * jnp.exp(glast - gcol)).astype(BF16)
        return dict(r0=r0, gbc=gbc, lmat=lmat, rhs=rhs, attn=attn, qd=qd, ke=ke)

    def advance_state(p, sol):
        u = sol[:, :HEAD_DIM]
        wb = sol[:, HEAD_DIM:].astype(BF16)
        vn, qs = [], []
        for h in range(DN_HEADS):
            rs = slice(h * c, (h + 1) * c)
            sb = s_ref[h].astype(BF16)
            vn.append(u[rs] - _dot(wb[rs], sb))
            qs.append(_dot(p["qd"][rs], sb))
        vnb = jnp.concatenate(vn, axis=0).astype(BF16)
        o = jnp.concatenate(qs, axis=0) + _dot(p["attn"], vnb)
        zc = z_ref[pl.ds(p["r0"], c), :]
        for h in range(DN_HEADS):
            rs = slice(h * c, (h + 1) * c)
            sl = slice(h * HEAD_DIM, (h + 1) * HEAD_DIM)
            s_ref[h] = (s_ref[h] * jnp.exp(p["gbc"][c - 1:c, h:h + 1])
                        + _dot_tn(p["ke"][rs], vnb[rs]))
            zf = zc[:, sl].astype(F32)
            ob_ref[pl.ds(p["r0"], c), sl] = (_rms(o[rs], ng) * (zf * _sigmoid(zf))).astype(BF16)

    def group(ig, carry):
        ps = [prepare(ig * group_size + b) for b in range(group_size)]
        sol = [p["rhs"] for p in ps]
        pw = [-p["lmat"] for p in ps]
        for step in range(6):
            pb = [x.astype(BF16) for x in pw]
            sol = [s + _dot(xb, s.astype(BF16)) for s, xb in zip(sol, pb)]
            if step < 5:
                pw = [_dot(xb, xb) for xb in pb]
        for p, s in zip(ps, sol):
            advance_state(p, s)
        return carry

    lax.fori_loop(0, nchunk // group_size, group, 0)


def _gdn(q, k, v, z, gb, grow, ng, *, batch, seq):
    tm = ROW_TILE
    nchunk = tm // DN_CHUNK
    steps = seq // tm
    rows = lambda b, j: (b * steps + j, 0)
    return pl.pallas_call(
        functools.partial(_gdn_body, nchunk=nchunk, group_size=GDN_GROUP),
        grid=(batch, steps),
        in_specs=[pl.BlockSpec((tm, DN_KEY), rows), pl.BlockSpec((tm, DN_KEY), rows),
                  pl.BlockSpec((tm, DN_VAL), rows), pl.BlockSpec((tm, DN_VAL), rows),
                  pl.BlockSpec((tm, LANES), rows),
                  pl.BlockSpec((nchunk, DN_HEADS * DN_CHUNK), rows),
                  pl.BlockSpec((1, HEAD_DIM), lambda b, j: (0, 0))],
        out_specs=pl.BlockSpec((tm, DN_VAL), rows),
        out_shape=jax.ShapeDtypeStruct((batch * seq, DN_VAL), BF16),
        scratch_shapes=[pltpu.VMEM((DN_HEADS, DN_DK, HEAD_DIM), F32)],
        compiler_params=pltpu.CompilerParams(dimension_semantics=("arbitrary", "arbitrary"),
                                             vmem_limit_bytes=VMEM_LIMIT),
        name="gdn",
    )(q, k, v, z, gb, grow, ng)


def _outproj_body(oa_ref, ob_ref, x_ref, woa_ref, wob_ref, ng_ref, rw_ref, rb_ref, tri_ref,
                  h_ref, hp_ref, meta_ref, gate_ref, cnt_ref, run_ref, *, tm):
    i = pl.program_id(0)

    @pl.when(i == 0)
    def _():
        run_ref[...] = jnp.zeros(run_ref.shape, F32)

    h = x_ref[...] + _dot(oa_ref[...], woa_ref[...]) + _dot(ob_ref[...], wob_ref[...])
    h_ref[...] = h
    hb = _rms(h, ng_ref[...]).astype(BF16)
    half = D_MODEL // 2
    lo = pltpu.bitcast(hb[:, :half].astype(F32), U32) >> 16
    hi = pltpu.bitcast(hb[:, half:].astype(F32), U32) & jnp.uint32(0xFFFF0000)
    hp_ref[...] = lo | hi

    logits = _dot(hb, rw_ref[...]) + rb_ref[...]
    lane = lax.broadcasted_iota(I32, (tm, LANES), 1)
    lanef = lane.astype(F32)
    work = logits
    onehot = jnp.zeros((tm, LANES), F32)
    vals, sels = [], []
    for _ in range(TOP_K):
        m = jnp.max(work, axis=-1, keepdims=True)
        idx = jnp.min(jnp.where(work == m, lanef, float(LANES)), axis=-1, keepdims=True)
        sel = lanef == idx
        work = jnp.where(sel, -3e38, work)
        onehot = onehot + jnp.where(sel, 1.0, 0.0)
        vals.append(m)
        sels.append((sel, idx))
    ex = [jnp.exp(v - vals[0]) for v in vals]
    den = ex[0] + ex[1] + ex[2] + ex[3]
    pref = _dot(tri_ref[...], onehot.astype(BF16)) + run_ref[0:1, :]
    meta = jnp.zeros((tm, LANES), F32)
    gates = jnp.zeros((tm, LANES), F32)
    for kk in range(TOP_K):
        sel, idx = sels[kk]
        rank = jnp.sum(jnp.where(sel, pref, 0.0), axis=-1, keepdims=True)
        meta = meta + jnp.where(lane == kk, idx, 0.0) + jnp.where(lane == TOP_K + kk, rank, 0.0)
        gates = gates + jnp.where(lane == kk, ex[kk] / den, 0.0)
    meta_ref[...] = meta.T[0:8, :].astype(I32)
    gate_ref[...] = gates
    run = run_ref[...] + jnp.sum(onehot, axis=0, keepdims=True)
    run_ref[...] = run
    cnt_ref[...] = run.astype(I32)


def _outproj(oa, ob, x2, woa, wob, ng, rw, rb, tri):
    t = x2.shape[0]
    tm = ROW_TILE
    row = lambda i: (i, 0)
    full = lambda a: pl.BlockSpec(a.shape, (lambda i: (0,) * a.ndim))
    out_shapes = (
        jax.ShapeDtypeStruct((t, D_MODEL), F32),
        jax.ShapeDtypeStruct((t, D_MODEL // 2), U32),
        jax.ShapeDtypeStruct((8, t), I32),
        jax.ShapeDtypeStruct((t, LANES), F32),
        jax.ShapeDtypeStruct((8, LANES), I32),
    )
    return pl.pallas_call(
        functools.partial(_outproj_body, tm=tm),
        grid=(t // tm,),
        in_specs=[pl.BlockSpec((tm, GM_WIDTH), row), pl.BlockSpec((tm, DN_VAL), row),
                  pl.BlockSpec((tm, D_MODEL), row), full(woa), full(wob), full(ng), full(rw),
                  full(rb), full(tri)],
        out_specs=(pl.BlockSpec((tm, D_MODEL), row), pl.BlockSpec((tm, D_MODEL // 2), row),
                   pl.BlockSpec((8, tm), lambda i: (0, i)), pl.BlockSpec((tm, LANES), row),
                   pl.BlockSpec((8, LANES), lambda i: (0, 0))),
        out_shape=out_shapes,
        scratch_shapes=[pltpu.VMEM((8, LANES), F32)],
        compiler_params=pltpu.CompilerParams(dimension_semantics=("arbitrary",),
                                             vmem_limit_bytes=VMEM_LIMIT),
        name="outproj",
    )(oa, ob, x2, woa, wob, ng, rw, rb, tri)


def _sc_dispatch(hp, dest_k, n_rows):
    t, d = hp.shape
    mesh = plsc.VectorSubcoreMesh(core_axis_name="c", subcore_axis_name="s")
    nc, workers = mesh.num_cores, mesh.num_cores * mesh.num_subcores
    chunk = SC_GATHER_ROWS
    per_w = t // workers
    pairs = per_w // (2 * chunk)
    assert per_w * workers == t and pairs * 2 * chunk == per_w

    @functools.partial(
        pl.kernel, mesh=mesh, out_type=jax.ShapeDtypeStruct((n_rows, d), hp.dtype),
        scratch_types=[pltpu.VMEM((2, TOP_K, chunk), I32), pltpu.VMEM((2, chunk, d), hp.dtype),
                       pltpu.SemaphoreType.DMA((2,)), pltpu.SemaphoreType.DMA((2,)),
                       pltpu.SemaphoreType.DMA((2,))],
        name="sc_dispatch")
    def scatter(hp_hbm, idx_hbm, xin_hbm, idx_v, rows_v, lsem, ssem, isem):
        base_w = (lax.axis_index("s") * nc + lax.axis_index("c")) * per_w

        def index(b, kk, base):
            return pltpu.make_async_copy(idx_hbm.at[pl.ds(kk * t + base, chunk)], idx_v.at[b, kk],
                                         isem.at[b])

        def load(b, base):
            return pltpu.make_async_copy(hp_hbm.at[pl.ds(base, chunk)], rows_v.at[b], lsem.at[b])

        def put(b, kk):
            return pltpu.make_async_copy(rows_v.at[b], xin_hbm.at[idx_v.at[b, kk]], ssem.at[b])

        def body(j, carry):
            for b in range(2):
                base = base_w + (2 * j + b) * chunk

                @pl.when(j > 0)
                def _():
                    for kk in range(TOP_K):
                        put(b, kk).wait()

                for kk in range(TOP_K):
                    index(b, kk, base).start()
                load(b, base).start()
            for b in range(2):
                for kk in range(TOP_K):
                    index(b, kk, base_w).wait()
                load(b, base_w).wait()
                for kk in range(TOP_K):
                    put(b, kk).start()
            return carry

        lax.fori_loop(0, pairs, body, 0)
        for b in range(2):
            for kk in range(TOP_K):
                put(b, kk).wait()

    return scatter(hp, dest_k)


def _ffn_body(be_ref, nv_ref, br_ref, ep_ref, es_ref, x_ref, wgu_hbm, bgu_ref, wd_hbm, bd_ref, y_ref,
              wgub_ref, wdb_ref, wguf_ref, wdf_ref, gsem, dsem):
    i = pl.program_id(0)

    @pl.when(i >= nv_ref[0])
    def _():
        y_ref[...] = jnp.zeros(y_ref.shape, U32)

    def fetch(pos):
        e, slot = es_ref[pos], pos % 2
        return (pltpu.make_async_copy(wgu_hbm.at[e], wguf_ref.at[slot], gsem.at[slot]),
                pltpu.make_async_copy(wd_hbm.at[e], wdf_ref.at[slot], dsem.at[slot]))

    @pl.when((i < nv_ref[0]) & ((i == 0) | (be_ref[i] != be_ref[jnp.maximum(i - 1, 0)])))
    def _():
        pos = ep_ref[i]

        @pl.when(i == 0)
        def _():
            for cp in fetch(pos):
                cp.start()

        for cp in fetch(pos):
            cp.wait()

        @pl.when(pos + 1 < es_ref[N_EXPERTS])
        def _():
            for cp in fetch(pos + 1):
                cp.start()

        slot = pos % 2
        wgub_ref[...] = wguf_ref[slot].astype(BF16)
        wdb_ref[...] = wdf_ref[slot].astype(BF16)

    nsub = FFN_SUBBLOCKS
    rsub = x_ref.shape[0] // nsub

    def compute(active):
        half = D_MODEL // 2
        row = lax.broadcasted_iota(I32, (rsub, 1), 0)
        gus = []
        for sb in range(active):
            rs = slice(sb * rsub, (sb + 1) * rsub)
            xp = jnp.where(row + sb * rsub < br_ref[i], x_ref[rs, :], jnp.uint32(0))
            lo = pltpu.bitcast(xp << 16, F32).astype(BF16)
            hi = pltpu.bitcast(xp & jnp.uint32(0xFFFF0000), F32).astype(BF16)
            gus.append(_dot(lo, wgub_ref[:half, :]) + _dot(hi, wgub_ref[half:, :]) + bgu_ref[0])
        ys = []
        for gu in gus:
            gate = jnp.minimum(gu[:, :D_FF], SWIGLU_LIMIT)
            up = jnp.clip(gu[:, D_FF:], -SWIGLU_LIMIT, SWIGLU_LIMIT)
            act = (up + 1.0) * (gate * _sigmoid(SWIGLU_ALPHA * gate))
            ys.append(_dot(act.astype(BF16), wdb_ref[...]) + bd_ref[0])
        for sb, y in enumerate(ys):
            ylo = pltpu.bitcast(y[:, :half].astype(BF16).astype(F32), U32) >> 16
            yhi = pltpu.bitcast(y[:, half:].astype(BF16).astype(F32), U32) & jnp.uint32(0xFFFF0000)
            y_ref[sb * rsub:(sb + 1) * rsub, :] = ylo | yhi
        if active < nsub:
            y_ref[active * rsub:, :] = jnp.zeros(((nsub - active) * rsub, half), U32)

    active = (br_ref[i] + rsub - 1) // rsub
    for n in range(1, nsub + 1):
        pl.when((i < nv_ref[0]) & (active == n))(functools.partial(compute, n))


def _ffn(blk_e, nvalid, blk_rows, blk_pos, used, xin, wgu, bgu, wd, bd):
    p = xin.shape[0]
    bm = FFN_BLOCK
    rows = lambda i, be, nv, br, ep, es: (jnp.minimum(i, nv[0] - 1), 0)
    wsel = lambda i, be, nv, br, ep, es: (be[i], 0, 0)
    return pl.pallas_call(
        _ffn_body,
        grid_spec=pltpu.PrefetchScalarGridSpec(
            num_scalar_prefetch=5,
            grid=(p // bm,),
            in_specs=[pl.BlockSpec((bm, D_MODEL // 2), rows),
                      pl.BlockSpec(memory_space=pl.ANY),
                      pl.BlockSpec((1, 1, 2 * D_FF), wsel),
                      pl.BlockSpec(memory_space=pl.ANY),
                      pl.BlockSpec((1, 1, D_MODEL), wsel)],
            out_specs=pl.BlockSpec((bm, D_MODEL // 2), lambda i, be, nv, br, ep, es: (i, 0)),
            scratch_shapes=[pltpu.VMEM((D_MODEL, 2 * D_FF), BF16), pltpu.VMEM((D_FF, D_MODEL), BF16),
                            pltpu.VMEM((2, D_MODEL, 2 * D_FF), F32), pltpu.VMEM((2, D_FF, D_MODEL), F32),
                            pltpu.SemaphoreType.DMA((2,)), pltpu.SemaphoreType.DMA((2,))],
        ),
        out_shape=jax.ShapeDtypeStruct((p, D_MODEL // 2), U32),
        compiler_params=pltpu.CompilerParams(dimension_semantics=("arbitrary",),
                                             vmem_limit_bytes=VMEM_LIMIT),
        name="ffn",
    )(blk_e, nvalid, blk_rows, blk_pos, used, xin, wgu, bgu, wd, bd)


def _sc_gather(table, idx):
    r, d = idx.shape[0], table.shape[1]
    mesh = plsc.VectorSubcoreMesh(core_axis_name="c", subcore_axis_name="s")
    nc, workers = mesh.num_cores, mesh.num_cores * mesh.num_subcores
    chunk = SC_GATHER_ROWS
    per_w = r // workers
    pairs = per_w // (2 * chunk)
    assert per_w * workers == r and pairs * 2 * chunk == per_w

    @functools.partial(
        pl.kernel, mesh=mesh, out_type=jax.ShapeDtypeStruct((r, d), table.dtype),
        scratch_types=[pltpu.VMEM((per_w,), I32), pltpu.VMEM((2, chunk, d), table.dtype),
                       pltpu.SemaphoreType.DMA((2,)), pltpu.SemaphoreType.DMA((2,))],
        name="sc_gather")
    def gather(table_hbm, idx_hbm, out_hbm, idx_v, rows_v, gsem, wsem):
        base_w = (lax.axis_index("s") * nc + lax.axis_index("c")) * per_w
        pltpu.sync_copy(idx_hbm.at[pl.ds(base_w, per_w)], idx_v)

        def fetch(b, c):
            return pltpu.make_async_copy(table_hbm.at[idx_v.at[pl.ds(c * chunk, chunk)]],
                                         rows_v.at[b], gsem.at[b])

        def flush(b, base):
            return pltpu.make_async_copy(rows_v.at[b], out_hbm.at[pl.ds(base, chunk)], wsem.at[b])

        def body(j, carry):
            for b in range(2):
                @pl.when(j > 0)
                def _():
                    flush(b, base_w).wait()

                fetch(b, 2 * j + b).start()
            for b in range(2):
                fetch(b, 2 * j + b).wait()
                flush(b, base_w + (2 * j + b) * chunk).start()
            return carry

        lax.fori_loop(0, pairs, body, 0)
        for b in range(2):
            flush(b, base_w).wait()

    return gather(table, idx)


def _combine_body(h_ref, gate_ref, fg_ref, *refs):
    yg_refs, o_ref = refs[:TOP_K], refs[-1]
    half = D_MODEL // 2
    gates = gate_ref[...]
    h = h_ref[...]
    lo, hi = h[:, :half], h[:, half:]
    for kk in range(TOP_K):
        yp = yg_refs[kk][...]
        g = gates[:, kk:kk + 1]
        lo = lo + g * pltpu.bitcast(yp << 16, F32)
        hi = hi + g * pltpu.bitcast(yp & jnp.uint32(0xFFFF0000), F32)
    out = jnp.concatenate([lo, hi], axis=1)
    o_ref[...] = _rms(out, fg_ref[...])


def _combine(h, gates, fg, ygath, out_prev, *, part, parts):
    t = h.shape[0]
    tm = COMBINE_TILE
    nt = t // parts // tm
    row = lambda i: (part * nt + i, 0)
    slot = lambda kk: pl.BlockSpec((tm, D_MODEL // 2), lambda i: (kk * nt + i, 0))
    in_specs = [pl.BlockSpec((tm, D_MODEL), row), pl.BlockSpec((tm, LANES), row),
                pl.BlockSpec((1, D_MODEL), lambda i: (0, 0))] + [slot(kk) for kk in range(TOP_K)]
    args = [h, gates, fg] + [ygath] * TOP_K
    aliases = {}
    if out_prev is not None:
        aliases = {len(args): 0}
        in_specs.append(pl.BlockSpec(memory_space=pl.ANY))
        args.append(out_prev)
    return pl.pallas_call(
        _combine_body,
        grid=(nt,),
        in_specs=in_specs,
        out_specs=pl.BlockSpec((tm, D_MODEL), row),
        out_shape=jax.ShapeDtypeStruct((t, D_MODEL), F32),
        input_output_aliases=aliases,
        compiler_params=pltpu.CompilerParams(dimension_semantics=("arbitrary",),
                                             vmem_limit_bytes=VMEM_LIMIT),
        name="combine",
    )(*args)


def _block_tril(n, chunk, strict):
    r = jnp.arange(n)[:, None]
    c = jnp.arange(n)[None, :]
    keep = ((r // chunk) == (c // chunk)) & ((r > c) if strict else (r >= c))
    return keep.astype(BF16)


def _pad_lanes(a, fill=0.0):
    a = a.reshape(1, -1).astype(F32)
    return jnp.pad(a, ((0, 0), (0, LANES - a.shape[1])), constant_values=fill)


def _layer(h, norm_mix_g, w_in, gm_norm_g, gm_ws, gm_bs, dn_conv_w, dn_a_log, dn_dt_bias,
           dn_norm_g, w_out, norm_ffn_g, router_w, router_b, exp_w_gu, exp_b_gu, exp_w_down,
           exp_b_down, out_g):
    batch, seq, d = h.shape
    t = batch * seq
    x2 = h.reshape(t, d)

    c0, c1, c2 = GM_WIDTH, 2 * GM_WIDTH, 2 * GM_WIDTH + 3 * DN_KEY
    c3 = c2 + DN_VAL
    wb = w_in.astype(BF16)
    wu, wv, wqkv, wz = wb[:, :c0], wb[:, c0:c1], wb[:, c1:c2], wb[:, c2:c3]
    wab = jnp.pad(wb[:, c3:], ((0, 0), (0, LANES - 2 * DN_HEADS)))
    gmg = gm_norm_g.reshape(1, GM_WIDTH).astype(F32)
    bsb = jnp.repeat(gm_bs.T, HEAD_DIM, axis=1).astype(F32)
    alog = _pad_lanes(dn_a_log)
    dtb = _pad_lanes(dn_dt_bias)
    tri_incl = _block_tril(ROW_TILE, DN_CHUNK, strict=False)

    oa, q, k, v, z, gb, gt = _inproj(
        x2, norm_mix_g.reshape(1, d), wu, wv, wqkv, wz, wab, gmg, gm_ws.astype(F32), bsb,
        dn_conv_w.astype(F32), alog, dtb, tri_incl, seq=seq)

    grow = gt[:DN_HEADS].reshape(DN_HEADS, t // DN_CHUNK, DN_CHUNK).transpose(1, 0, 2)
    grow = grow.reshape(t // DN_CHUNK, DN_HEADS * DN_CHUNK)
    ob = _gdn(q, k, v, z, gb, grow, dn_norm_g.reshape(1, HEAD_DIM).astype(F32), batch=batch, seq=seq)

    wo = w_out.astype(BF16)
    rw = jnp.pad(router_w.astype(BF16), ((0, 0), (0, LANES - N_EXPERTS)))
    rb = _pad_lanes(router_b, fill=NEG_BIG)
    tri_strict = _block_tril(ROW_TILE, ROW_TILE, strict=True)
    hres, hp, meta, gates, cnt = _outproj(oa, ob, x2, wo[:GM_WIDTH], wo[GM_WIDTH:],
                                          norm_ffn_g.reshape(1, d), rw, rb, tri_strict)

    bm = FFN_BLOCK
    counts = cnt[0, :N_EXPERTS]
    padded = (counts + bm - 1) // bm * bm
    pad_end = jnp.cumsum(padded)
    pad_start = (pad_end - padded).astype(I32)
    n_blocks = (t * TOP_K + N_EXPERTS * bm) // bm
    nvalid = (pad_end[-1] // bm).astype(I32).reshape(1)
    blk = jnp.minimum(jnp.arange(n_blocks, dtype=I32), nvalid[0] - 1)
    blk_e = jnp.minimum(jnp.sum(pad_end[None, :] <= (blk * bm)[:, None], axis=1), N_EXPERTS - 1).astype(I32)
    eid = meta[:TOP_K]
    start_of = jnp.sum(jnp.where(eid[..., None] == jnp.arange(N_EXPERTS, dtype=I32), pad_start, 0), axis=-1)
    dest = start_of + meta[TOP_K:2 * TOP_K]

    xin = _sc_dispatch(hp, dest.reshape(-1), n_blocks * bm)
    eids = jnp.arange(N_EXPERTS, dtype=I32)
    of_blk = blk_e[:, None] == eids
    blk_rows = jnp.clip(jnp.sum(jnp.where(of_blk, counts + pad_start, 0), axis=1) - blk * bm, 0, bm)
    has_rows = counts > 0
    used = jnp.minimum(jnp.sort(jnp.where(has_rows, eids, N_EXPERTS + eids)), N_EXPERTS - 1)
    used = jnp.concatenate([used, jnp.sum(has_rows).reshape(1)]).astype(I32)
    blk_pos = jnp.sum(jnp.where(of_blk, jnp.cumsum(has_rows) - 1, 0), axis=1).astype(I32)
    blk_rows = blk_rows.astype(I32)
    y = _ffn(blk_e, nvalid, blk_rows, blk_pos, used, xin, exp_w_gu, exp_b_gu[:, None, :].astype(F32),
             exp_w_down, exp_b_down[:, None, :].astype(F32))
    out = None
    tp = t // COMBINE_PARTS
    for part in range(COMBINE_PARTS):
        ygath = _sc_gather(y, dest[:, part * tp:(part + 1) * tp].reshape(-1))
        out = _combine(hres, gates, out_g.reshape(1, d).astype(F32), ygath, out,
                       part=part, parts=COMBINE_PARTS)
    return out.reshape(batch, seq, d)


def kernel(x, norm_mix_g, w_in, gm_norm_g, gm_ws, gm_bs, dn_conv_w, dn_a_log, dn_dt_bias, dn_norm_g, w_out, norm_ffn_g, router_w, router_b, exp_w_gu, exp_b_gu, exp_w_down, exp_b_down, final_norm_g):
    depth = norm_mix_g.shape[0]
    assert depth == 1, "single-layer problem"
    return _layer(x, norm_mix_g[0], w_in[0], gm_norm_g[0], gm_ws[0], gm_bs[0], dn_conv_w[0],
                  dn_a_log[0], dn_dt_bias[0], dn_norm_g[0], w_out[0], norm_ffn_g[0], router_w[0],
                  router_b[0], exp_w_gu[0], exp_b_gu[0], exp_w_down[0], exp_b_down[0], final_norm_g)
```

```python
import functools

import jax
import jax.numpy as jnp
from jax import lax
from jax.experimental import pallas as pl
from jax.experimental.pallas import tpu as pltpu
from jax.experimental.pallas import tpu_sc as plsc

F32 = jnp.float32
BF16 = jnp.bfloat16
I32 = jnp.int32
U32 = jnp.uint32

D_MODEL = 1024
HEAD_DIM = 128
GM_HEADS = 4
GM_WIDTH = GM_HEADS * HEAD_DIM
GM_CHUNK = 128
DN_HEADS = 4
DN_DK = 128
DN_KEY = DN_HEADS * DN_DK
DN_VAL = DN_HEADS * HEAD_DIM
DN_CHUNK = 64
CONV_W = 4
N_EXPERTS = 32
TOP_K = 4
D_FF = D_MODEL
SWIGLU_LIMIT = 7.0
SWIGLU_ALPHA = 1.702
EPS = 1e-6

LANES = 128
INPROJ_COLS = 512
INPROJ_SUBBLOCKS = 2
ROW_TILE = 512
FFN_BLOCK = 1024
FFN_SUBBLOCKS = 4
SC_GATHER_ROWS = 64
COMBINE_PARTS = 4
COMBINE_TILE = 512
VMEM_LIMIT = 56 * 1024 * 1024
NEG_BIG = -1e30
GDN_GROUP = 4


def _dot(a, b):
    return jnp.dot(a, b, preferred_element_type=F32)


def _dot_nt(a, b):
    return lax.dot_general(a, b, (((1,), (1,)), ((), ())), preferred_element_type=F32)


def _dot_tn(a, b):
    return lax.dot_general(a, b, (((0,), (0,)), ((), ())), preferred_element_type=F32)


def _rms(x, g):
    return x * lax.rsqrt(jnp.mean(x * x, axis=-1, keepdims=True) + EPS) * g


def _gelu(x):
    return 0.5 * x * (1.0 + lax.erf(x * (2.0 ** -0.5)))


def _sigmoid(x):
    return 1.0 / (1.0 + jnp.exp(-x))


def _softplus(x):
    return jnp.maximum(x, 0.0) + jnp.log1p(jnp.exp(-jnp.abs(x)))


def _inproj_body(x_ref, ng_ref, wu_ref, wv_ref, wqkv_ref, wz_ref, wab_ref, gmg_ref, ws_ref,
                 bs_ref, cw_ref, alog_ref, dtb_ref, tri_ref,
                 oa_ref, q_ref, k_ref, v_ref, z_ref, gb_ref, gt_ref, cbuf_ref, ybuf_ref,
                 *, tm, tiles_per_seq):
    i = pl.program_id(0)
    ybuf_ref[...] = _rms(x_ref[...], ng_ref[...]).astype(BF16)
    nb = INPROJ_COLS
    heads_per_block = nb // HEAD_DIM

    ri = lax.broadcasted_iota(I32, (GM_CHUNK, GM_CHUNK), 0)
    ci = lax.broadcasted_iota(I32, (GM_CHUNK, GM_CHUNK), 1)
    causal = ri >= ci

    def gmlp_matmul(blk, r0, r1):
        cs = slice(blk * nb, (blk + 1) * nb)
        return _dot(ybuf_ref[r0:r1, :], wu_ref[:, cs]), _dot(ybuf_ref[r0:r1, :], wv_ref[:, cs])

    def gmlp_epilogue(blk, r0, r1, uv):
        u, vg = _gelu(uv[0]), _gelu(uv[1])
        for hh in range(heads_per_block):
            h = blk * heads_per_block + hh
            sl = slice(h * HEAD_DIM, (h + 1) * HEAD_DIM)
            ls = slice(hh * HEAD_DIM, (hh + 1) * HEAD_DIM)
            vh = _rms(vg[:, ls], gmg_ref[:, sl]).astype(BF16)
            wm = jnp.where(causal, ws_ref[h], 0.0).astype(BF16)
            for c in range((r1 - r0) // GM_CHUNK):
                rs = slice(c * GM_CHUNK, (c + 1) * GM_CHUNK)
                gate = _dot(wm, vh[rs]) + bs_ref[:, sl]
                oa_ref[r0 + c * GM_CHUNK:r0 + (c + 1) * GM_CHUNK, sl] = (u[rs, ls] * gate).astype(BF16)

    def qkv_matmul(blk, r0, r1):
        return _dot(ybuf_ref[r0:r1, :], wqkv_ref[:, blk * nb:(blk + 1) * nb])

    def qkv_epilogue(blk, r0, r1, pq):
        cs = slice(blk * nb, (blk + 1) * nb)
        cbuf_ref[8 + r0:8 + r1, cs] = pq
        acc = cw_ref[CONV_W - 1:CONV_W, cs] * pq
        for j in range(CONV_W - 1):
            off = 8 - (CONV_W - 1) + j
            acc = acc + cw_ref[j:j + 1, cs] * cbuf_ref[off + r0:off + r1, cs]
        if r1 == tm:
            cbuf_ref[0:8, cs] = pq[r1 - r0 - 8:, :]
        s = acc * _sigmoid(acc)
        for hh in range(heads_per_block):
            c0 = blk * nb + hh * HEAD_DIM
            sh = s[:, hh * HEAD_DIM:(hh + 1) * HEAD_DIM]
            if c0 < DN_KEY:
                q_ref[r0:r1, c0:c0 + HEAD_DIM] = (
                    sh * lax.rsqrt(jnp.sum(sh * sh, axis=-1, keepdims=True) + EPS)
                    * (DN_DK ** -0.5)).astype(BF16)
            elif c0 < 2 * DN_KEY:
                k_ref[r0:r1, c0 - DN_KEY:c0 - DN_KEY + HEAD_DIM] = (
                    sh * lax.rsqrt(jnp.sum(sh * sh, axis=-1, keepdims=True) + EPS)).astype(BF16)
            else:
                v_ref[r0:r1, c0 - 2 * DN_KEY:c0 - 2 * DN_KEY + HEAD_DIM] = sh.astype(BF16)

    def z_matmul(blk, r0, r1):
        return _dot(ybuf_ref[r0:r1, :], wz_ref[:, blk * nb:(blk + 1) * nb])

    def z_epilogue(blk, r0, r1, zz):
        z_ref[r0:r1, blk * nb:(blk + 1) * nb] = zz.astype(BF16)

    @pl.when(i % tiles_per_seq == 0)
    def _():
        cbuf_ref[0:8, :] = jnp.zeros((8, 3 * DN_KEY), F32)

    stages = ([(gmlp_matmul, gmlp_epilogue, blk) for blk in range(GM_WIDTH // nb)]
              + [(qkv_matmul, qkv_epilogue, blk) for blk in range(3 * DN_KEY // nb)]
              + [(z_matmul, z_epilogue, blk) for blk in range(DN_VAL // nb)])
    rsub = tm // INPROJ_SUBBLOCKS
    spans = [(sb * rsub, (sb + 1) * rsub) for sb in range(INPROJ_SUBBLOCKS)]
    for matmul, epilogue, blk in stages:
        results = [matmul(blk, r0, r1) for r0, r1 in spans]
        for (r0, r1), res in zip(spans, results):
            epilogue(blk, r0, r1, res)

    ab = _dot(ybuf_ref[...], wab_ref[...])
    g = -jnp.exp(alog_ref[...]) * _softplus(ab + dtb_ref[...])
    beta = _sigmoid(ab)
    g_hi = g.astype(BF16)
    r1 = g - g_hi.astype(F32)
    g_mid = r1.astype(BF16)
    g_lo = (r1 - g_mid.astype(F32)).astype(BF16)
    tri = tri_ref[...]
    gc = _dot(tri, g_hi) + _dot(tri, g_mid) + _dot(tri, g_lo)
    lane = lax.broadcasted_iota(I32, (tm, LANES), 1)
    gbv = jnp.where(lane < DN_HEADS, gc, beta)
    gb_ref[...] = gbv
    gt_ref[...] = gbv.T[0:8, :]


def _inproj(x2, ng, wu, wv, wqkv, wz, wab, gmg, ws, bsb, cw, alog, dtb, tri, *, seq):
    t = x2.shape[0]
    tm = ROW_TILE
    row = lambda i: (i, 0)
    full = lambda a: pl.BlockSpec(a.shape, (lambda i: (0,) * a.ndim))
    out_shapes = (
        jax.ShapeDtypeStruct((t, GM_WIDTH), BF16),
        jax.ShapeDtypeStruct((t, DN_KEY), BF16),
        jax.ShapeDtypeStruct((t, DN_KEY), BF16),
        jax.ShapeDtypeStruct((t, DN_VAL), BF16),
        jax.ShapeDtypeStruct((t, DN_VAL), BF16),
        jax.ShapeDtypeStruct((t, LANES), F32),
        jax.ShapeDtypeStruct((8, t), F32),
    )
    return pl.pallas_call(
        functools.partial(_inproj_body, tm=tm, tiles_per_seq=seq // tm),
        grid=(t // tm,),
        in_specs=[pl.BlockSpec((tm, D_MODEL), row), full(ng), full(wu), full(wv), full(wqkv),
                  full(wz), full(wab), full(gmg), full(ws), full(bsb), full(cw), full(alog),
                  full(dtb), full(tri)],
        out_specs=(pl.BlockSpec((tm, GM_WIDTH), row), pl.BlockSpec((tm, DN_KEY), row),
                   pl.BlockSpec((tm, DN_KEY), row), pl.BlockSpec((tm, DN_VAL), row),
                   pl.BlockSpec((tm, DN_VAL), row), pl.BlockSpec((tm, LANES), row),
                   pl.BlockSpec((8, tm), lambda i: (0, i))),
        out_shape=out_shapes,
        scratch_shapes=[pltpu.VMEM((tm + 8, 3 * DN_KEY), F32), pltpu.VMEM((tm, D_MODEL), BF16)],
        compiler_params=pltpu.CompilerParams(dimension_semantics=("arbitrary",),
                                             vmem_limit_bytes=VMEM_LIMIT),
        name="inproj",
    )(x2, ng, wu, wv, wqkv, wz, wab, gmg, ws, bsb, cw, alog, dtb, tri)


def _gdn_body(q_ref, k_ref, v_ref, z_ref, gb_ref, gr_ref, ng_ref, ob_ref, s_ref, *, nchunk,
              group_size):
    j = pl.program_id(1)

    @pl.when(j == 0)
    def _():
        s_ref[...] = jnp.zeros(s_ref.shape, F32)

    c = DN_CHUNK
    n = DN_HEADS * c
    ri = lax.broadcasted_iota(I32, (n, n), 0)
    ci = lax.broadcasted_iota(I32, (n, n), 1)
    same = (ri // c) == (ci // c)
    incl = same & ((ri % c) >= (ci % c))
    strict = same & ((ri % c) > (ci % c))
    ng = ng_ref[...]

    def stack(a):
        return jnp.concatenate([a[:, h * HEAD_DIM:(h + 1) * HEAD_DIM] for h in range(DN_HEADS)], axis=0)

    def prepare(ic):
        r0 = ic * c
        kst = stack(k_ref[pl.ds(r0, c), :])
        qst = stack(q_ref[pl.ds(r0, c), :])
        vst = stack(v_ref[pl.ds(r0, c), :])
        gbc = gb_ref[pl.ds(r0, c), :]
        grow = gr_ref[pl.ds(ic, 1), :]
        gcol = jnp.concatenate([gbc[:, h:h + 1] for h in range(DN_HEADS)], axis=0)
        bcol = jnp.concatenate([gbc[:, DN_HEADS + h:DN_HEADS + h + 1] for h in range(DN_HEADS)], axis=0)
        glast = jnp.concatenate(
            [jnp.broadcast_to(gbc[c - 1:c, h:h + 1], (c, 1)) for h in range(DN_HEADS)], axis=0)
        decay = jnp.where(incl, jnp.exp(jnp.where(incl, gcol - grow, 0.0)), 0.0)
        kf = kst.astype(F32)
        kb = kf * bcol
        lmat = jnp.where(strict, _dot_nt(kb.astype(BF16), kst) * decay, 0.0)
        eg = jnp.exp(gcol)
        rhs = jnp.concatenate([vst.astype(F32) * bcol, kb * eg], axis=1)
        attn = jnp.where(incl, _dot_nt(qst, kst) * decay, 0.0).astype(BF16)
        qd = (qst.astype(F32) * eg).astype(BF16)
        ke = (kf * jnp.exp(glast - gcol)).astype(BF16)
        return dict(r0=r0, gbc=gbc, lmat=lmat, rhs=rhs, attn=attn, qd=qd, ke=ke)

    def advance_state(p, sol):
        u = sol[:, :HEAD_DIM]
        wb = sol[:, HEAD_DIM:].astype(BF16)
        vn, qs = [], []
        for h in range(DN_HEADS):
            rs = slice(h * c, (h + 1) * c)
            sb = s_ref[h].astype(BF16)
            vn.append(u[rs] - _dot(wb[rs], sb))
            qs.append(_dot(p["qd"][rs], sb))
        vnb = jnp.concatenate(vn, axis=0).astype(BF16)
        o = jnp.concatenate(qs, axis=0) + _dot(p["attn"], vnb)
        zc = z_ref[pl.ds(p["r0"], c), :]
        for h in range(DN_HEADS):
            rs = slice(h * c, (h + 1) * c)
            sl = slice(h * HEAD_DIM, (h + 1) * HEAD_DIM)
            s_ref[h] = (s_ref[h] * jnp.exp(p["gbc"][c - 1:c, h:h + 1])
                        + _dot_tn(p["ke"][rs], vnb[rs]))
            zf = zc[:, sl].astype(F32)
            ob_ref[pl.ds(p["r0"], c), sl] = (_rms(o[rs], ng) * (zf * _sigmoid(zf))).astype(BF16)

    def solve_group(ig):
        ps = [prepare(ig * group_size + b) for b in range(group_size)]
        sol = [p["rhs"] for p in ps]
        pw = [-p["lmat"] for p in ps]
        for step in range(6):
            pb = [x.astype(BF16) for x in pw]
            sol = [s + _dot(xb, s.astype(BF16)) for s, xb in zip(sol, pb)]
            if step < 5:
                pw = [_dot(xb, xb) for xb in pb]
        return ps, sol

    ngroups = nchunk // group_size
    solved = solve_group(0)
    for ig in range(ngroups):
        ahead = solve_group(ig + 1) if ig + 1 < ngroups else None
        for p, s in zip(*solved):
            advance_state(p, s)
        solved = ahead


def _gdn(q, k, v, z, gb, grow, ng, *, batch, seq):
    tm = ROW_TILE
    nchunk = tm // DN_CHUNK
    steps = seq // tm
    rows = lambda b, j: (b * steps + j, 0)
    return pl.pallas_call(
        functools.partial(_gdn_body, nchunk=nchunk, group_size=GDN_GROUP),
        grid=(batch, steps),
        in_specs=[pl.BlockSpec((tm, DN_KEY), rows), pl.BlockSpec((tm, DN_KEY), rows),
                  pl.BlockSpec((tm, DN_VAL), rows), pl.BlockSpec((tm, DN_VAL), rows),
                  pl.BlockSpec((tm, LANES), rows),
                  pl.BlockSpec((nchunk, DN_HEADS * DN_CHUNK), rows),
                  pl.BlockSpec((1, HEAD_DIM), lambda b, j: (0, 0))],
        out_specs=pl.BlockSpec((tm, DN_VAL), rows),
        out_shape=jax.ShapeDtypeStruct((batch * seq, DN_VAL), BF16),
        scratch_shapes=[pltpu.VMEM((DN_HEADS, DN_DK, HEAD_DIM), F32)],
        compiler_params=pltpu.CompilerParams(dimension_semantics=("arbitrary", "arbitrary"),
                                             vmem_limit_bytes=VMEM_LIMIT),
        name="gdn",
    )(q, k, v, z, gb, grow, ng)


def _outproj_body(oa_ref, ob_ref, x_ref, woa_ref, wob_ref, ng_ref, rw_ref, rb_ref, tri_ref,
                  h_ref, hp_ref, meta_ref, gate_ref, cnt_ref, run_ref, *, tm):
    i = pl.program_id(0)

    @pl.when(i == 0)
    def _():
        run_ref[...] = jnp.zeros(run_ref.shape, F32)

    h = x_ref[...] + _dot(oa_ref[...], woa_ref[...]) + _dot(ob_ref[...], wob_ref[...])
    h_ref[...] = h
    hb = _rms(h, ng_ref[...]).astype(BF16)
    half = D_MODEL // 2
    lo = pltpu.bitcast(hb[:, :half].astype(F32), U32) >> 16
    hi = pltpu.bitcast(hb[:, half:].astype(F32), U32) & jnp.uint32(0xFFFF0000)
    hp_ref[...] = lo | hi

    logits = _dot(hb, rw_ref[...]) + rb_ref[...]
    lane = lax.broadcasted_iota(I32, (tm, LANES), 1)
    lanef = lane.astype(F32)
    work = logits
    onehot = jnp.zeros((tm, LANES), F32)
    vals, sels = [], []
    for _ in range(TOP_K):
        m = jnp.max(work, axis=-1, keepdims=True)
        idx = jnp.min(jnp.where(work == m, lanef, float(LANES)), axis=-1, keepdims=True)
        sel = lanef == idx
        work = jnp.where(sel, -3e38, work)
        onehot = onehot + jnp.where(sel, 1.0, 0.0)
        vals.append(m)
        sels.append((sel, idx))
    ex = [jnp.exp(v - vals[0]) for v in vals]
    den = ex[0] + ex[1] + ex[2] + ex[3]
    pref = _dot(tri_ref[...], onehot.astype(BF16)) + run_ref[0:1, :]
    meta = jnp.zeros((tm, LANES), F32)
    gates = jnp.zeros((tm, LANES), F32)
    for kk in range(TOP_K):
        sel, idx = sels[kk]
        rank = jnp.sum(jnp.where(sel, pref, 0.0), axis=-1, keepdims=True)
        meta = meta + jnp.where(lane == kk, idx, 0.0) + jnp.where(lane == TOP_K + kk, rank, 0.0)
        gates = gates + jnp.where(lane == kk, ex[kk] / den, 0.0)
    meta_ref[...] = meta.T[0:8, :].astype(I32)
    gate_ref[...] = gates
    run = run_ref[...] + jnp.sum(onehot, axis=0, keepdims=True)
    run_ref[...] = run
    cnt_ref[...] = run.astype(I32)


def _outproj(oa, ob, x2, woa, wob, ng, rw, rb, tri):
    t = x2.shape[0]
    tm = ROW_TILE
    row = lambda i: (i, 0)
    full = lambda a: pl.BlockSpec(a.shape, (lambda i: (0,) * a.ndim))
    out_shapes = (
        jax.ShapeDtypeStruct((t, D_MODEL), F32),
        jax.ShapeDtypeStruct((t, D_MODEL // 2), U32),
        jax.ShapeDtypeStruct((8, t), I32),
        jax.ShapeDtypeStruct((t, LANES), F32),
        jax.ShapeDtypeStruct((8, LANES), I32),
    )
    return pl.pallas_call(
        functools.partial(_outproj_body, tm=tm),
        grid=(t // tm,),
        in_specs=[pl.BlockSpec((tm, GM_WIDTH), row), pl.BlockSpec((tm, DN_VAL), row),
                  pl.BlockSpec((tm, D_MODEL), row), full(woa), full(wob), full(ng), full(rw),
                  full(rb), full(tri)],
        out_specs=(pl.BlockSpec((tm, D_MODEL), row), pl.BlockSpec((tm, D_MODEL // 2), row),
                   pl.BlockSpec((8, tm), lambda i: (0, i)), pl.BlockSpec((tm, LANES), row),
                   pl.BlockSpec((8, LANES), lambda i: (0, 0))),
        out_shape=out_shapes,
        scratch_shapes=[pltpu.VMEM((8, LANES), F32)],
        compiler_params=pltpu.CompilerParams(dimension_semantics=("arbitrary",),
                                             vmem_limit_bytes=VMEM_LIMIT),
        name="outproj",
    )(oa, ob, x2, woa, wob, ng, rw, rb, tri)


def _sc_dispatch(hp, dest_k, n_rows):
    t, d = hp.shape
    mesh = plsc.VectorSubcoreMesh(core_axis_name="c", subcore_axis_name="s")
    nc, workers = mesh.num_cores, mesh.num_cores * mesh.num_subcores
    chunk = SC_GATHER_ROWS
    per_w = t // workers
    pairs = per_w // (2 * chunk)
    assert per_w * workers == t and pairs * 2 * chunk == per_w

    @functools.partial(
        pl.kernel, mesh=mesh, out_type=jax.ShapeDtypeStruct((n_rows, d), hp.dtype),
        scratch_types=[pltpu.VMEM((2, TOP_K, chunk), I32), pltpu.VMEM((2, chunk, d), hp.dtype),
                       pltpu.SemaphoreType.DMA((2,)), pltpu.SemaphoreType.DMA((2,)),
                       pltpu.SemaphoreType.DMA((2,))],
        name="sc_dispatch")
    def scatter(hp_hbm, idx_hbm, xin_hbm, idx_v, rows_v, lsem, ssem, isem):
        base_w = (lax.axis_index("s") * nc + lax.axis_index("c")) * per_w

        def index(b, kk, base):
            return pltpu.make_async_copy(idx_hbm.at[pl.ds(kk * t + base, chunk)], idx_v.at[b, kk],
                                         isem.at[b])

        def load(b, base):
            return pltpu.make_async_copy(hp_hbm.at[pl.ds(base, chunk)], rows_v.at[b], lsem.at[b])

        def put(b, kk):
            return pltpu.make_async_copy(rows_v.at[b], xin_hbm.at[idx_v.at[b, kk]], ssem.at[b])

        def body(j, carry):
            for b in range(2):
                base = base_w + (2 * j + b) * chunk

                @pl.when(j > 0)
                def _():
                    for kk in range(TOP_K):
                        put(b, kk).wait()

                for kk in range(TOP_K):
                    index(b, kk, base).start()
                load(b, base).start()
            for b in range(2):
                for kk in range(TOP_K):
                    index(b, kk, base_w).wait()
                load(b, base_w).wait()
                for kk in range(TOP_K):
                    put(b, kk).start()
            return carry

        lax.fori_loop(0, pairs, body, 0)
        for b in range(2):
            for kk in range(TOP_K):
                put(b, kk).wait()

    return scatter(hp, dest_k)


def _ffn_body(be_ref, nv_ref, br_ref, ep_ref, es_ref, x_ref, wgu_hbm, bgu_ref, wd_hbm, bd_ref, y_ref,
              wgub_ref, wdb_ref, wguf_ref, wdf_ref, gsem, dsem):
    i = pl.program_id(0)

    @pl.when(i >= nv_ref[0])
    def _():
        y_ref[...] = jnp.zeros(y_ref.shape, U32)

    def fetch(pos):
        e, slot = es_ref[pos], pos % 2
        return (pltpu.make_async_copy(wgu_hbm.at[e], wguf_ref.at[slot], gsem.at[slot]),
                pltpu.make_async_copy(wd_hbm.at[e], wdf_ref.at[slot], dsem.at[slot]))

    @pl.when((i < nv_ref[0]) & ((i == 0) | (be_ref[i] != be_ref[jnp.maximum(i - 1, 0)])))
    def _():
        pos = ep_ref[i]

        @pl.when(i == 0)
        def _():
            for cp in fetch(pos):
                cp.start()

        for cp in fetch(pos):
            cp.wait()

        @pl.when(pos + 1 < es_ref[N_EXPERTS])
        def _():
            for cp in fetch(pos + 1):
                cp.start()

        slot = pos % 2
        wgub_ref[...] = wguf_ref[slot].astype(BF16)
        wdb_ref[...] = wdf_ref[slot].astype(BF16)

    nsub = FFN_SUBBLOCKS
    rsub = x_ref.shape[0] // nsub

    def compute(active):
        half = D_MODEL // 2
        row = lax.broadcasted_iota(I32, (rsub, 1), 0)
        gus = []
        for sb in range(active):
            rs = slice(sb * rsub, (sb + 1) * rsub)
            xp = jnp.where(row + sb * rsub < br_ref[i], x_ref[rs, :], jnp.uint32(0))
            lo = pltpu.bitcast(xp << 16, F32).astype(BF16)
            hi = pltpu.bitcast(xp & jnp.uint32(0xFFFF0000), F32).astype(BF16)
            gus.append(_dot(lo, wgub_ref[:half, :]) + _dot(hi, wgub_ref[half:, :]) + bgu_ref[0])
        ys = []
        for gu in gus:
            gate = jnp.minimum(gu[:, :D_FF], SWIGLU_LIMIT)
            up = jnp.clip(gu[:, D_FF:], -SWIGLU_LIMIT, SWIGLU_LIMIT)
            act = (up + 1.0) * (gate * _sigmoid(SWIGLU_ALPHA * gate))
            ys.append(_dot(act.astype(BF16), wdb_ref[...]) + bd_ref[0])
        for sb, y in enumerate(ys):
            ylo = pltpu.bitcast(y[:, :half].astype(BF16).astype(F32), U32) >> 16
            yhi = pltpu.bitcast(y[:, half:].astype(BF16).astype(F32), U32) & jnp.uint32(0xFFFF0000)
            y_ref[sb * rsub:(sb + 1) * rsub, :] = ylo | yhi
        if active < nsub:
            y_ref[active * rsub:, :] = jnp.zeros(((nsub - active) * rsub, half), U32)

    active = (br_ref[i] + rsub - 1) // rsub
    for n in range(1, nsub + 1):
        pl.when((i < nv_ref[0]) & (active == n))(functools.partial(compute, n))


def _ffn(blk_e, nvalid, blk_rows, blk_pos, used, xin, wgu, bgu, wd, bd):
    p = xin.shape[0]
    bm = FFN_BLOCK
    rows = lambda i, be, nv, br, ep, es: (jnp.minimum(i, nv[0] - 1), 0)
    wsel = lambda i, be, nv, br, ep, es: (be[i], 0, 0)
    return pl.pallas_call(
        _ffn_body,
        grid_spec=pltpu.PrefetchScalarGridSpec(
            num_scalar_prefetch=5,
            grid=(p // bm,),
            in_specs=[pl.BlockSpec((bm, D_MODEL // 2), rows),
                      pl.BlockSpec(memory_space=pl.ANY),
                      pl.BlockSpec((1, 1, 2 * D_FF), wsel),
                      pl.BlockSpec(memory_space=pl.ANY),
                      pl.BlockSpec((1, 1, D_MODEL), wsel)],
            out_specs=pl.BlockSpec((bm, D_MODEL // 2), lambda i, be, nv, br, ep, es: (i, 0)),
            scratch_shapes=[pltpu.VMEM((D_MODEL, 2 * D_FF), BF16), pltpu.VMEM((D_FF, D_MODEL), BF16),
                            pltpu.VMEM((2, D_MODEL, 2 * D_FF), F32), pltpu.VMEM((2, D_FF, D_MODEL), F32),
                            pltpu.SemaphoreType.DMA((2,)), pltpu.SemaphoreType.DMA((2,))],
        ),
        out_shape=jax.ShapeDtypeStruct((p, D_MODEL // 2), U32),
        compiler_params=pltpu.CompilerParams(dimension_semantics=("arbitrary",),
                                             vmem_limit_bytes=VMEM_LIMIT),
        name="ffn",
    )(blk_e, nvalid, blk_rows, blk_pos, used, xin, wgu, bgu, wd, bd)


def _sc_gather(table, idx):
    r, d = idx.shape[0], table.shape[1]
    mesh = plsc.VectorSubcoreMesh(core_axis_name="c", subcore_axis_name="s")
    nc, workers = mesh.num_cores, mesh.num_cores * mesh.num_subcores
    chunk = SC_GATHER_ROWS
    per_w = r // workers
    pairs = per_w // (2 * chunk)
    assert per_w * workers == r and pairs * 2 * chunk == per_w

    @functools.partial(
        pl.kernel, mesh=mesh, out_type=jax.ShapeDtypeStruct((r, d), table.dtype),
        scratch_types=[pltpu.VMEM((per_w,), I32), pltpu.VMEM((2, chunk, d), table.dtype),
                       pltpu.SemaphoreType.DMA((2,)), pltpu.SemaphoreType.DMA((2,))],
        name="sc_gather")
    def gather(table_hbm, idx_hbm, out_hbm, idx_v, rows_v, gsem, wsem):
        base_w = (lax.axis_index("s") * nc + lax.axis_index("c")) * per_w
        pltpu.sync_copy(idx_hbm.at[pl.ds(base_w, per_w)], idx_v)

        def fetch(b, c):
            return pltpu.make_async_copy(table_hbm.at[idx_v.at[pl.ds(c * chunk, chunk)]],
                                         rows_v.at[b], gsem.at[b])

        def flush(b, base):
            return pltpu.make_async_copy(rows_v.at[b], out_hbm.at[pl.ds(base, chunk)], wsem.at[b])

        def body(j, carry):
            for b in range(2):
                @pl.when(j > 0)
                def _():
                    flush(b, base_w).wait()

                fetch(b, 2 * j + b).start()
            for b in range(2):
                fetch(b, 2 * j + b).wait()
                flush(b, base_w + (2 * j + b) * chunk).start()
            return carry

        lax.fori_loop(0, pairs, body, 0)
        for b in range(2):
            flush(b, base_w).wait()

    return gather(table, idx)


def _combine_body(h_ref, gate_ref, fg_ref, *refs):
    yg_refs, o_ref = refs[:TOP_K], refs[-1]
    half = D_MODEL // 2
    gates = gate_ref[...]
    h = h_ref[...]
    lo, hi = h[:, :half], h[:, half:]
    for kk in range(TOP_K):
        yp = yg_refs[kk][...]
        g = gates[:, kk:kk + 1]
        lo = lo + g * pltpu.bitcast(yp << 16, F32)
        hi = hi + g * pltpu.bitcast(yp & jnp.uint32(0xFFFF0000), F32)
    out = jnp.concatenate([lo, hi], axis=1)
    o_ref[...] = _rms(out, fg_ref[...])


def _combine(h, gates, fg, ygath, out_prev, *, part, parts):
    t = h.shape[0]
    tm = COMBINE_TILE
    nt = t // parts // tm
    row = lambda i: (part * nt + i, 0)
    slot = lambda kk: pl.BlockSpec((tm, D_MODEL // 2), lambda i: (kk * nt + i, 0))
    in_specs = [pl.BlockSpec((tm, D_MODEL), row), pl.BlockSpec((tm, LANES), row),
                pl.BlockSpec((1, D_MODEL), lambda i: (0, 0))] + [slot(kk) for kk in range(TOP_K)]
    args = [h, gates, fg] + [ygath] * TOP_K
    aliases = {}
    if out_prev is not None:
        aliases = {len(args): 0}
        in_specs.append(pl.BlockSpec(memory_space=pl.ANY))
        args.append(out_prev)
    return pl.pallas_call(
        _combine_body,
        grid=(nt,),
        in_specs=in_specs,
        out_specs=pl.BlockSpec((tm, D_MODEL), row),
        out_shape=jax.ShapeDtypeStruct((t, D_MODEL), F32),
        input_output_aliases=aliases,
        compiler_params=pltpu.CompilerParams(dimension_semantics=("arbitrary",),
                                             vmem_limit_bytes=VMEM_LIMIT),
        name="combine",
    )(*args)


def _block_tril(n, chunk, strict):
    r = jnp.arange(n)[:, None]
    c = jnp.arange(n)[None, :]
    keep = ((r // chunk) == (c // chunk)) & ((r > c) if strict else (r >= c))
    return keep.astype(BF16)


def _pad_lanes(a, fill=0.0):
    a = a.reshape(1, -1).astype(F32)
    return jnp.pad(a, ((0, 0), (0, LANES - a.shape[1])), constant_values=fill)


def _layer(h, norm_mix_g, w_in, gm_norm_g, gm_ws, gm_bs, dn_conv_w, dn_a_log, dn_dt_bias,
           dn_norm_g, w_out, norm_ffn_g, router_w, router_b, exp_w_gu, exp_b_gu, exp_w_down,
           exp_b_down, out_g):
    batch, seq, d = h.shape
    t = batch * seq
    x2 = h.reshape(t, d)

    c0, c1, c2 = GM_WIDTH, 2 * GM_WIDTH, 2 * GM_WIDTH + 3 * DN_KEY
    c3 = c2 + DN_VAL
    wb = w_in.astype(BF16)
    wu, wv, wqkv, wz = wb[:, :c0], wb[:, c0:c1], wb[:, c1:c2], wb[:, c2:c3]
    wab = jnp.pad(wb[:, c3:], ((0, 0), (0, LANES - 2 * DN_HEADS)))
    gmg = gm_norm_g.reshape(1, GM_WIDTH).astype(F32)
    bsb = jnp.repeat(gm_bs.T, HEAD_DIM, axis=1).astype(F32)
    alog = _pad_lanes(dn_a_log)
    dtb = _pad_lanes(dn_dt_bias)
    tri_incl = _block_tril(ROW_TILE, DN_CHUNK, strict=False)

    oa, q, k, v, z, gb, gt = _inproj(
        x2, norm_mix_g.reshape(1, d), wu, wv, wqkv, wz, wab, gmg, gm_ws.astype(F32), bsb,
        dn_conv_w.astype(F32), alog, dtb, tri_incl, seq=seq)

    grow = gt[:DN_HEADS].reshape(DN_HEADS, t // DN_CHUNK, DN_CHUNK).transpose(1, 0, 2)
    grow = grow.reshape(t // DN_CHUNK, DN_HEADS * DN_CHUNK)
    ob = _gdn(q, k, v, z, gb, grow, dn_norm_g.reshape(1, HEAD_DIM).astype(F32), batch=batch, seq=seq)

    wo = w_out.astype(BF16)
    rw = jnp.pad(router_w.astype(BF16), ((0, 0), (0, LANES - N_EXPERTS)))
    rb = _pad_lanes(router_b, fill=NEG_BIG)
    tri_strict = _block_tril(ROW_TILE, ROW_TILE, strict=True)
    hres, hp, meta, gates, cnt = _outproj(oa, ob, x2, wo[:GM_WIDTH], wo[GM_WIDTH:],
                                          norm_ffn_g.reshape(1, d), rw, rb, tri_strict)

    bm = FFN_BLOCK
    counts = cnt[0, :N_EXPERTS]
    padded = (counts + bm - 1) // bm * bm
    pad_end = jnp.cumsum(padded)
    pad_start = (pad_end - padded).astype(I32)
    n_blocks = (t * TOP_K + N_EXPERTS * bm) // bm
    nvalid = (pad_end[-1] // bm).astype(I32).reshape(1)
    blk = jnp.minimum(jnp.arange(n_blocks, dtype=I32), nvalid[0] - 1)
    blk_e = jnp.minimum(jnp.sum(pad_end[None, :] <= (blk * bm)[:, None], axis=1), N_EXPERTS - 1).astype(I32)
    eid = meta[:TOP_K]
    start_of = jnp.sum(jnp.where(eid[..., None] == jnp.arange(N_EXPERTS, dtype=I32), pad_start, 0), axis=-1)
    dest = start_of + meta[TOP_K:2 * TOP_K]

    xin = _sc_dispatch(hp, dest.reshape(-1), n_blocks * bm)
    eids = jnp.arange(N_EXPERTS, dtype=I32)
    of_blk = blk_e[:, None] == eids
    blk_rows = jnp.clip(jnp.sum(jnp.where(of_blk, counts + pad_start, 0), axis=1) - blk * bm, 0, bm)
    has_rows = counts > 0
    used = jnp.minimum(jnp.sort(jnp.where(has_rows, eids, N_EXPERTS + eids)), N_EXPERTS - 1)
    used = jnp.concatenate([used, jnp.sum(has_rows).reshape(1)]).astype(I32)
    blk_pos = jnp.sum(jnp.where(of_blk, jnp.cumsum(has_rows) - 1, 0), axis=1).astype(I32)
    blk_rows = blk_rows.astype(I32)
    y = _ffn(blk_e, nvalid, blk_rows, blk_pos, used, xin, exp_w_gu, exp_b_gu[:, None, :].astype(F32),
             exp_w_down, exp_b_down[:, None, :].astype(F32))
    out = None
    tp = t // COMBINE_PARTS
    for part in range(COMBINE_PARTS):
        ygath = _sc_gather(y, dest[:, part * tp:(part + 1) * tp].reshape(-1))
        out = _combine(hres, gates, out_g.reshape(1, d).astype(F32), ygath, out,
                       part=part, parts=COMBINE_PARTS)
    return out.reshape(batch, seq, d)


def kernel(x, norm_mix_g, w_in, gm_norm_g, gm_ws, gm_bs, dn_conv_w, dn_a_log, dn_dt_bias, dn_norm_g, w_out, norm_ffn_g, router_w, router_b, exp_w_gu, exp_b_gu, exp_w_down, exp_b_down, final_norm_g):
    depth = norm_mix_g.shape[0]
    assert depth == 1, "single-layer problem"
    return _layer(x, norm_mix_g[0], w_in[0], gm_norm_g[0], gm_ws[0], gm_bs[0], dn_conv_w[0],
                  dn_a_log[0], dn_dt_bias[0], dn_norm_g[0], w_out[0], norm_ffn_g[0], router_w[0],
                  router_b[0], exp_w_gu[0], exp_b_gu[0], exp_w_down[0], exp_b_down[0], final_norm_g)
```

```python
import functools

import jax
import jax.numpy as jnp
from jax import lax
from jax.experimental import pallas as pl
from jax.experimental.pallas import tpu as pltpu
from jax.experimental.pallas import tpu_sc as plsc

F32 = jnp.float32
BF16 = jnp.bfloat16
I32 = jnp.int32
U32 = jnp.uint32

D_MODEL = 1024
HEAD_DIM = 128
GM_HEADS = 4
GM_WIDTH = GM_HEADS * HEAD_DIM
GM_CHUNK = 128
DN_HEADS = 4
DN_DK = 128
DN_KEY = DN_HEADS * DN_DK
DN_VAL = DN_HEADS * HEAD_DIM
DN_CHUNK = 64
CONV_W = 4
N_EXPERTS = 32
TOP_K = 4
D_FF = D_MODEL
SWIGLU_LIMIT = 7.0
SWIGLU_ALPHA = 1.702
EPS = 1e-6

LANES = 128
INPROJ_COLS = 512
INPROJ_SUBBLOCKS = 2
ROW_TILE = 512
GDN_TILE = 1024
FFN_BLOCK = 1024
FFN_SUBBLOCKS = 4
SC_GATHER_ROWS = 64
COMBINE_PARTS = 4
COMBINE_TILE = 512
VMEM_LIMIT = 56 * 1024 * 1024
NEG_BIG = -1e30
GDN_GROUP = 4


def _dot(a, b):
    return jnp.dot(a, b, preferred_element_type=F32)


def _dot_nt(a, b):
    return lax.dot_general(a, b, (((1,), (1,)), ((), ())), preferred_element_type=F32)


def _dot_tn(a, b):
    return lax.dot_general(a, b, (((0,), (0,)), ((), ())), preferred_element_type=F32)


def _rms(x, g):
    return x * lax.rsqrt(jnp.mean(x * x, axis=-1, keepdims=True) + EPS) * g


def _gelu(x):
    return 0.5 * x * (1.0 + lax.erf(x * (2.0 ** -0.5)))


def _sigmoid(x):
    return 1.0 / (1.0 + jnp.exp(-x))


def _softplus(x):
    return jnp.maximum(x, 0.0) + jnp.log1p(jnp.exp(-jnp.abs(x)))


def _inproj_body(x_ref, ng_ref, wu_ref, wv_ref, wqkv_ref, wz_ref, wab_ref, gmg_ref, ws_ref,
                 bs_ref, cw_ref, alog_ref, dtb_ref, tri_ref,
                 oa_ref, q_ref, k_ref, v_ref, z_ref, gb_ref, gt_ref, cbuf_ref, ybuf_ref,
                 *, tm, tiles_per_seq):
    i = pl.program_id(0)
    ybuf_ref[...] = _rms(x_ref[...], ng_ref[...]).astype(BF16)
    nb = INPROJ_COLS
    heads_per_block = nb // HEAD_DIM

    ri = lax.broadcasted_iota(I32, (GM_CHUNK, GM_CHUNK), 0)
    ci = lax.broadcasted_iota(I32, (GM_CHUNK, GM_CHUNK), 1)
    causal = ri >= ci

    def gmlp_matmul(blk, r0, r1):
        cs = slice(blk * nb, (blk + 1) * nb)
        return _dot(ybuf_ref[r0:r1, :], wu_ref[:, cs]), _dot(ybuf_ref[r0:r1, :], wv_ref[:, cs])

    def gmlp_epilogue(blk, r0, r1, uv):
        u, vg = _gelu(uv[0]), _gelu(uv[1])
        for hh in range(heads_per_block):
            h = blk * heads_per_block + hh
            sl = slice(h * HEAD_DIM, (h + 1) * HEAD_DIM)
            ls = slice(hh * HEAD_DIM, (hh + 1) * HEAD_DIM)
            vh = _rms(vg[:, ls], gmg_ref[:, sl]).astype(BF16)
            wm = jnp.where(causal, ws_ref[h], 0.0).astype(BF16)
            for c in range((r1 - r0) // GM_CHUNK):
                rs = slice(c * GM_CHUNK, (c + 1) * GM_CHUNK)
                gate = _dot(wm, vh[rs]) + bs_ref[:, sl]
                oa_ref[r0 + c * GM_CHUNK:r0 + (c + 1) * GM_CHUNK, sl] = (u[rs, ls] * gate).astype(BF16)

    def qkv_matmul(blk, r0, r1):
        return _dot(ybuf_ref[r0:r1, :], wqkv_ref[:, blk * nb:(blk + 1) * nb])

    def qkv_epilogue(blk, r0, r1, pq):
        cs = slice(blk * nb, (blk + 1) * nb)
        cbuf_ref[8 + r0:8 + r1, cs] = pq
        acc = cw_ref[CONV_W - 1:CONV_W, cs] * pq
        for j in range(CONV_W - 1):
            off = 8 - (CONV_W - 1) + j
            acc = acc + cw_ref[j:j + 1, cs] * cbuf_ref[off + r0:off + r1, cs]
        if r1 == tm:
            cbuf_ref[0:8, cs] = pq[r1 - r0 - 8:, :]
        s = acc * _sigmoid(acc)
        for hh in range(heads_per_block):
            c0 = blk * nb + hh * HEAD_DIM
            sh = s[:, hh * HEAD_DIM:(hh + 1) * HEAD_DIM]
            if c0 < DN_KEY:
                q_ref[r0:r1, c0:c0 + HEAD_DIM] = (
                    sh * lax.rsqrt(jnp.sum(sh * sh, axis=-1, keepdims=True) + EPS)
                    * (DN_DK ** -0.5)).astype(BF16)
            elif c0 < 2 * DN_KEY:
                k_ref[r0:r1, c0 - DN_KEY:c0 - DN_KEY + HEAD_DIM] = (
                    sh * lax.rsqrt(jnp.sum(sh * sh, axis=-1, keepdims=True) + EPS)).astype(BF16)
            else:
                v_ref[r0:r1, c0 - 2 * DN_KEY:c0 - 2 * DN_KEY + HEAD_DIM] = sh.astype(BF16)

    def z_matmul(blk, r0, r1):
        return _dot(ybuf_ref[r0:r1, :], wz_ref[:, blk * nb:(blk + 1) * nb])

    def z_epilogue(blk, r0, r1, zz):
        z_ref[r0:r1, blk * nb:(blk + 1) * nb] = zz.astype(BF16)

    @pl.when(i % tiles_per_seq == 0)
    def _():
        cbuf_ref[0:8, :] = jnp.zeros((8, 3 * DN_KEY), F32)

    stages = ([(gmlp_matmul, gmlp_epilogue, blk) for blk in range(GM_WIDTH // nb)]
              + [(qkv_matmul, qkv_epilogue, blk) for blk in range(3 * DN_KEY // nb)]
              + [(z_matmul, z_epilogue, blk) for blk in range(DN_VAL // nb)])
    rsub = tm // INPROJ_SUBBLOCKS
    spans = [(sb * rsub, (sb + 1) * rsub) for sb in range(INPROJ_SUBBLOCKS)]
    for matmul, epilogue, blk in stages:
        results = [matmul(blk, r0, r1) for r0, r1 in spans]
        for (r0, r1), res in zip(spans, results):
            epilogue(blk, r0, r1, res)

    ab = _dot(ybuf_ref[...], wab_ref[...])
    g = -jnp.exp(alog_ref[...]) * _softplus(ab + dtb_ref[...])
    beta = _sigmoid(ab)
    g_hi = g.astype(BF16)
    r1 = g - g_hi.astype(F32)
    g_mid = r1.astype(BF16)
    g_lo = (r1 - g_mid.astype(F32)).astype(BF16)
    tri = tri_ref[...]
    gc = _dot(tri, g_hi) + _dot(tri, g_mid) + _dot(tri, g_lo)
    lane = lax.broadcasted_iota(I32, (tm, LANES), 1)
    gbv = jnp.where(lane < DN_HEADS, gc, beta)
    gb_ref[...] = gbv
    gt_ref[...] = gbv.T[0:8, :]


def _inproj(x2, ng, wu, wv, wqkv, wz, wab, gmg, ws, bsb, cw, alog, dtb, tri, *, seq):
    t = x2.shape[0]
    tm = ROW_TILE
    row = lambda i: (i, 0)
    full = lambda a: pl.BlockSpec(a.shape, (lambda i: (0,) * a.ndim))
    out_shapes = (
        jax.ShapeDtypeStruct((t, GM_WIDTH), BF16),
        jax.ShapeDtypeStruct((t, DN_KEY), BF16),
        jax.ShapeDtypeStruct((t, DN_KEY), BF16),
        jax.ShapeDtypeStruct((t, DN_VAL), BF16),
        jax.ShapeDtypeStruct((t, DN_VAL), BF16),
        jax.ShapeDtypeStruct((t, LANES), F32),
        jax.ShapeDtypeStruct((8, t), F32),
    )
    return pl.pallas_call(
        functools.partial(_inproj_body, tm=tm, tiles_per_seq=seq // tm),
        grid=(t // tm,),
        in_specs=[pl.BlockSpec((tm, D_MODEL), row), full(ng), full(wu), full(wv), full(wqkv),
                  full(wz), full(wab), full(gmg), full(ws), full(bsb), full(cw), full(alog),
                  full(dtb), full(tri)],
        out_specs=(pl.BlockSpec((tm, GM_WIDTH), row), pl.BlockSpec((tm, DN_KEY), row),
                   pl.BlockSpec((tm, DN_KEY), row), pl.BlockSpec((tm, DN_VAL), row),
                   pl.BlockSpec((tm, DN_VAL), row), pl.BlockSpec((tm, LANES), row),
                   pl.BlockSpec((8, tm), lambda i: (0, i))),
        out_shape=out_shapes,
        scratch_shapes=[pltpu.VMEM((tm + 8, 3 * DN_KEY), F32), pltpu.VMEM((tm, D_MODEL), BF16)],
        compiler_params=pltpu.CompilerParams(dimension_semantics=("arbitrary",),
                                             vmem_limit_bytes=VMEM_LIMIT),
        name="inproj",
    )(x2, ng, wu, wv, wqkv, wz, wab, gmg, ws, bsb, cw, alog, dtb, tri)


def _gdn_body(q_ref, k_ref, v_ref, z_ref, gb_ref, gr_ref, ng_ref, ob_ref, s_ref, *, nchunk,
              group_size):
    j = pl.program_id(1)

    @pl.when(j == 0)
    def _():
        s_ref[...] = jnp.zeros(s_ref.shape, F32)

    c = DN_CHUNK
    n = DN_HEADS * c
    ri = lax.broadcasted_iota(I32, (n, n), 0)
    ci = lax.broadcasted_iota(I32, (n, n), 1)
    same = (ri // c) == (ci // c)
    incl = same & ((ri % c) >= (ci % c))
    strict = same & ((ri % c) > (ci % c))
    ng = ng_ref[...]

    def stack(a):
        return jnp.concatenate([a[:, h * HEAD_DIM:(h + 1) * HEAD_DIM] for h in range(DN_HEADS)], axis=0)

    def prepare(ic):
        r0 = ic * c
        kst = stack(k_ref[pl.ds(r0, c), :])
        qst = stack(q_ref[pl.ds(r0, c), :])
        vst = stack(v_ref[pl.ds(r0, c), :])
        gbc = gb_ref[pl.ds(r0, c), :]
        grow = gr_ref[pl.ds(ic, 1), :]
        gcol = jnp.concatenate([gbc[:, h:h + 1] for h in range(DN_HEADS)], axis=0)
        bcol = jnp.concatenate([gbc[:, DN_HEADS + h:DN_HEADS + h + 1] for h in range(DN_HEADS)], axis=0)
        glast = jnp.concatenate(
            [jnp.broadcast_to(gbc[c - 1:c, h:h + 1], (c, 1)) for h in range(DN_HEADS)], axis=0)
        decay = jnp.where(incl, jnp.exp(jnp.where(incl, gcol - grow, 0.0)), 0.0)
        kf = kst.astype(F32)
        kb = kf * bcol
        lmat = jnp.where(strict, _dot_nt(kb.astype(BF16), kst) * decay, 0.0)
        eg = jnp.exp(gcol)
        rhs = jnp.concatenate([vst.astype(F32) * bcol, kb * eg], axis=1)
        attn = jnp.where(incl, _dot_nt(qst, kst) * decay, 0.0).astype(BF16)
        qd = (qst.astype(F32) * eg).astype(BF16)
        ke = (kf * jnp.exp(glast - gcol)).astype(BF16)
        return dict(r0=r0, gbc=gbc, lmat=lmat, rhs=rhs, attn=attn, qd=qd, ke=ke)

    def advance_state(p, sol):
        u = sol[:, :HEAD_DIM]
        wb = sol[:, HEAD_DIM:].astype(BF16)
        vn, qs = [], []
        for h in range(DN_HEADS):
            rs = slice(h * c, (h + 1) * c)
            sb = s_ref[h].astype(BF16)
            vn.append(u[rs] - _dot(wb[rs], sb))
            qs.append(_dot(p["qd"][rs], sb))
        vnb = jnp.concatenate(vn, axis=0).astype(BF16)
        o = jnp.concatenate(qs, axis=0) + _dot(p["attn"], vnb)
        zc = z_ref[pl.ds(p["r0"], c), :]
        for h in range(DN_HEADS):
            rs = slice(h * c, (h + 1) * c)
            sl = slice(h * HEAD_DIM, (h + 1) * HEAD_DIM)
            s_ref[h] = (s_ref[h] * jnp.exp(p["gbc"][c - 1:c, h:h + 1])
                        + _dot_tn(p["ke"][rs], vnb[rs]))
            zf = zc[:, sl].astype(F32)
            ob_ref[pl.ds(p["r0"], c), sl] = (_rms(o[rs], ng) * (zf * _sigmoid(zf))).astype(BF16)

    def solve_group(ig):
        ps = [prepare(ig * group_size + b) for b in range(group_size)]
        sol = [p["rhs"] for p in ps]
        pw = [-p["lmat"] for p in ps]
        for step in range(6):
            pb = [x.astype(BF16) for x in pw]
            sol = [s + _dot(xb, s.astype(BF16)) for s, xb in zip(sol, pb)]
            if step < 5:
                pw = [_dot(xb, xb) for xb in pb]
        return ps, sol

    ngroups = nchunk // group_size
    solved = solve_group(0)
    for ig in range(ngroups):
        ahead = solve_group(ig + 1) if ig + 1 < ngroups else None
        for p, s in zip(*solved):
            advance_state(p, s)
        solved = ahead


def _gdn(q, k, v, z, gb, grow, ng, *, batch, seq):
    tm = GDN_TILE
    nchunk = tm // DN_CHUNK
    steps = seq // tm
    rows = lambda b, j: (b * steps + j, 0)
    return pl.pallas_call(
        functools.partial(_gdn_body, nchunk=nchunk, group_size=GDN_GROUP),
        grid=(batch, steps),
        in_specs=[pl.BlockSpec((tm, DN_KEY), rows), pl.BlockSpec((tm, DN_KEY), rows),
                  pl.BlockSpec((tm, DN_VAL), rows), pl.BlockSpec((tm, DN_VAL), rows),
                  pl.BlockSpec((tm, LANES), rows),
                  pl.BlockSpec((nchunk, DN_HEADS * DN_CHUNK), rows),
                  pl.BlockSpec((1, HEAD_DIM), lambda b, j: (0, 0))],
        out_specs=pl.BlockSpec((tm, DN_VAL), rows),
        out_shape=jax.ShapeDtypeStruct((batch * seq, DN_VAL), BF16),
        scratch_shapes=[pltpu.VMEM((DN_HEADS, DN_DK, HEAD_DIM), F32)],
        compiler_params=pltpu.CompilerParams(dimension_semantics=("arbitrary", "arbitrary"),
                                             vmem_limit_bytes=VMEM_LIMIT),
        name="gdn",
    )(q, k, v, z, gb, grow, ng)


def _outproj_body(oa_ref, ob_ref, x_ref, woa_ref, wob_ref, ng_ref, rw_ref, rb_ref, tri_ref,
                  h_ref, hp_ref, meta_ref, gate_ref, cnt_ref, run_ref, *, tm):
    i = pl.program_id(0)

    @pl.when(i == 0)
    def _():
        run_ref[...] = jnp.zeros(run_ref.shape, F32)

    h = x_ref[...] + _dot(oa_ref[...], woa_ref[...]) + _dot(ob_ref[...], wob_ref[...])
    h_ref[...] = h
    hb = _rms(h, ng_ref[...]).astype(BF16)
    half = D_MODEL // 2
    lo = pltpu.bitcast(hb[:, :half].astype(F32), U32) >> 16
    hi = pltpu.bitcast(hb[:, half:].astype(F32), U32) & jnp.uint32(0xFFFF0000)
    hp_ref[...] = lo | hi

    logits = _dot(hb, rw_ref[...]) + rb_ref[...]
    lane = lax.broadcasted_iota(I32, (tm, LANES), 1)
    lanef = lane.astype(F32)
    work = logits
    onehot = jnp.zeros((tm, LANES), F32)
    vals, sels = [], []
    for _ in range(TOP_K):
        m = jnp.max(work, axis=-1, keepdims=True)
        idx = jnp.min(jnp.where(work == m, lanef, float(LANES)), axis=-1, keepdims=True)
        sel = lanef == idx
        work = jnp.where(sel, -3e38, work)
        onehot = onehot + jnp.where(sel, 1.0, 0.0)
        vals.append(m)
        sels.append((sel, idx))
    ex = [jnp.exp(v - vals[0]) for v in vals]
    den = ex[0] + ex[1] + ex[2] + ex[3]
    pref = _dot(tri_ref[...], onehot.astype(BF16)) + run_ref[0:1, :]
    meta = jnp.zeros((tm, LANES), F32)
    gates = jnp.zeros((tm, LANES), F32)
    for kk in range(TOP_K):
        sel, idx = sels[kk]
        rank = jnp.sum(jnp.where(sel, pref, 0.0), axis=-1, keepdims=True)
        meta = meta + jnp.where(lane == kk, idx, 0.0) + jnp.where(lane == TOP_K + kk, rank, 0.0)
        gates = gates + jnp.where(lane == kk, ex[kk] / den, 0.0)
    meta_ref[...] = meta.T[0:8, :].astype(I32)
    gate_ref[...] = gates
    run = run_ref[...] + jnp.sum(onehot, axis=0, keepdims=True)
    run_ref[...] = run
    cnt_ref[...] = run.astype(I32)


def _outproj(oa, ob, x2, woa, wob, ng, rw, rb, tri):
    t = x2.shape[0]
    tm = ROW_TILE
    row = lambda i: (i, 0)
    full = lambda a: pl.BlockSpec(a.shape, (lambda i: (0,) * a.ndim))
    out_shapes = (
        jax.ShapeDtypeStruct((t, D_MODEL), F32),
        jax.ShapeDtypeStruct((t, D_MODEL // 2), U32),
        jax.ShapeDtypeStruct((8, t), I32),
        jax.ShapeDtypeStruct((t, LANES), F32),
        jax.ShapeDtypeStruct((8, LANES), I32),
    )
    return pl.pallas_call(
        functools.partial(_outproj_body, tm=tm),
        grid=(t // tm,),
        in_specs=[pl.BlockSpec((tm, GM_WIDTH), row), pl.BlockSpec((tm, DN_VAL), row),
                  pl.BlockSpec((tm, D_MODEL), row), full(woa), full(wob), full(ng), full(rw),
                  full(rb), full(tri)],
        out_specs=(pl.BlockSpec((tm, D_MODEL), row), pl.BlockSpec((tm, D_MODEL // 2), row),
                   pl.BlockSpec((8, tm), lambda i: (0, i)), pl.BlockSpec((tm, LANES), row),
                   pl.BlockSpec((8, LANES), lambda i: (0, 0))),
        out_shape=out_shapes,
        scratch_shapes=[pltpu.VMEM((8, LANES), F32)],
        compiler_params=pltpu.CompilerParams(dimension_semantics=("arbitrary",),
                                             vmem_limit_bytes=VMEM_LIMIT),
        name="outproj",
    )(oa, ob, x2, woa, wob, ng, rw, rb, tri)


def _sc_dispatch(hp, dest_k, n_rows):
    t, d = hp.shape
    mesh = plsc.VectorSubcoreMesh(core_axis_name="c", subcore_axis_name="s")
    nc, workers = mesh.num_cores, mesh.num_cores * mesh.num_subcores
    chunk = SC_GATHER_ROWS
    per_w = t // workers
    pairs = per_w // (2 * chunk)
    assert per_w * workers == t and pairs * 2 * chunk == per_w

    @functools.partial(
        pl.kernel, mesh=mesh, out_type=jax.ShapeDtypeStruct((n_rows, d), hp.dtype),
        scratch_types=[pltpu.VMEM((2, TOP_K, chunk), I32), pltpu.VMEM((2, chunk, d), hp.dtype),
                       pltpu.SemaphoreType.DMA((2,)), pltpu.SemaphoreType.DMA((2,)),
                       pltpu.SemaphoreType.DMA((2,))],
        name="sc_dispatch")
    def scatter(hp_hbm, idx_hbm, xin_hbm, idx_v, rows_v, lsem, ssem, isem):
        base_w = (lax.axis_index("s") * nc + lax.axis_index("c")) * per_w

        def index(b, kk, base):
            return pltpu.make_async_copy(idx_hbm.at[pl.ds(kk * t + base, chunk)], idx_v.at[b, kk],
                                         isem.at[b])

        def load(b, base):
            return pltpu.make_async_copy(hp_hbm.at[pl.ds(base, chunk)], rows_v.at[b], lsem.at[b])

        def put(b, kk):
            return pltpu.make_async_copy(rows_v.at[b], xin_hbm.at[idx_v.at[b, kk]], ssem.at[b])

        def body(j, carry):
            for b in range(2):
                base = base_w + (2 * j + b) * chunk

                @pl.when(j > 0)
                def _():
                    for kk in range(TOP_K):
                        put(b, kk).wait()

                for kk in range(TOP_K):
                    index(b, kk, base).start()
                load(b, base).start()
            for b in range(2):
                for kk in range(TOP_K):
                    index(b, kk, base_w).wait()
                load(b, base_w).wait()
                for kk in range(TOP_K):
                    put(b, kk).start()
            return carry

        lax.fori_loop(0, pairs, body, 0)
        for b in range(2):
            for kk in range(TOP_K):
                put(b, kk).wait()

    return scatter(hp, dest_k)


def _ffn_body(be_ref, nv_ref, br_ref, ep_ref, es_ref, x_ref, wgu_hbm, bgu_ref, wd_hbm, bd_ref, y_ref,
              wgub_ref, wdb_ref, wguf_ref, wdf_ref, gsem, dsem):
    i = pl.program_id(0)

    @pl.when(i >= nv_ref[0])
    def _():
        y_ref[...] = jnp.zeros(y_ref.shape, U32)

    def fetch(pos):
        e, slot = es_ref[pos], pos % 2
        return (pltpu.make_async_copy(wgu_hbm.at[e], wguf_ref.at[slot], gsem.at[slot]),
                pltpu.make_async_copy(wd_hbm.at[e], wdf_ref.at[slot], dsem.at[slot]))

    @pl.when((i < nv_ref[0]) & ((i == 0) | (be_ref[i] != be_ref[jnp.maximum(i - 1, 0)])))
    def _():
        pos = ep_ref[i]

        @pl.when(i == 0)
        def _():
            for cp in fetch(pos):
                cp.start()

        for cp in fetch(pos):
            cp.wait()

        @pl.when(pos + 1 < es_ref[N_EXPERTS])
        def _():
            for cp in fetch(pos + 1):
                cp.start()

        slot = pos % 2
        wgub_ref[...] = wguf_ref[slot].astype(BF16)
        wdb_ref[...] = wdf_ref[slot].astype(BF16)

    nsub = FFN_SUBBLOCKS
    rsub = x_ref.shape[0] // nsub

    def compute(active):
        half = D_MODEL // 2
        row = lax.broadcasted_iota(I32, (rsub, 1), 0)
        gus = []
        for sb in range(active):
            rs = slice(sb * rsub, (sb + 1) * rsub)
            xp = jnp.where(row + sb * rsub < br_ref[i], x_ref[rs, :], jnp.uint32(0))
            lo = pltpu.bitcast(xp << 16, F32).astype(BF16)
            hi = pltpu.bitcast(xp & jnp.uint32(0xFFFF0000), F32).astype(BF16)
            gus.append(_dot(lo, wgub_ref[:half, :]) + _dot(hi, wgub_ref[half:, :]) + bgu_ref[0])
        ys = []
        for gu in gus:
            gate = jnp.minimum(gu[:, :D_FF], SWIGLU_LIMIT)
            up = jnp.clip(gu[:, D_FF:], -SWIGLU_LIMIT, SWIGLU_LIMIT)
            act = (up + 1.0) * (gate * _sigmoid(SWIGLU_ALPHA * gate))
            ys.append(_dot(act.astype(BF16), wdb_ref[...]) + bd_ref[0])
        for sb, y in enumerate(ys):
            ylo = pltpu.bitcast(y[:, :half].astype(BF16).astype(F32), U32) >> 16
            yhi = pltpu.bitcast(y[:, half:].astype(BF16).astype(F32), U32) & jnp.uint32(0xFFFF0000)
            y_ref[sb * rsub:(sb + 1) * rsub, :] = ylo | yhi
        if active < nsub:
            y_ref[active * rsub:, :] = jnp.zeros(((nsub - active) * rsub, half), U32)

    active = (br_ref[i] + rsub - 1) // rsub
    for n in range(1, nsub + 1):
        pl.when((i < nv_ref[0]) & (active == n))(functools.partial(compute, n))


def _ffn(blk_e, nvalid, blk_rows, blk_pos, used, xin, wgu, bgu, wd, bd):
    p = xin.shape[0]
    bm = FFN_BLOCK
    rows = lambda i, be, nv, br, ep, es: (jnp.minimum(i, nv[0] - 1), 0)
    wsel = lambda i, be, nv, br, ep, es: (be[i], 0, 0)
    return pl.pallas_call(
        _ffn_body,
        grid_spec=pltpu.PrefetchScalarGridSpec(
            num_scalar_prefetch=5,
            grid=(p // bm,),
            in_specs=[pl.BlockSpec((bm, D_MODEL // 2), rows),
                      pl.BlockSpec(memory_space=pl.ANY),
                      pl.BlockSpec((1, 1, 2 * D_FF), wsel),
                      pl.BlockSpec(memory_space=pl.ANY),
                      pl.BlockSpec((1, 1, D_MODEL), wsel)],
            out_specs=pl.BlockSpec((bm, D_MODEL // 2), lambda i, be, nv, br, ep, es: (i, 0)),
            scratch_shapes=[pltpu.VMEM((D_MODEL, 2 * D_FF), BF16), pltpu.VMEM((D_FF, D_MODEL), BF16),
                            pltpu.VMEM((2, D_MODEL, 2 * D_FF), F32), pltpu.VMEM((2, D_FF, D_MODEL), F32),
                            pltpu.SemaphoreType.DMA((2,)), pltpu.SemaphoreType.DMA((2,))],
        ),
        out_shape=jax.ShapeDtypeStruct((p, D_MODEL // 2), U32),
        compiler_params=pltpu.CompilerParams(dimension_semantics=("arbitrary",),
                                             vmem_limit_bytes=VMEM_LIMIT),
        name="ffn",
    )(blk_e, nvalid, blk_rows, blk_pos, used, xin, wgu, bgu, wd, bd)


def _sc_gather(table, idx):
    r, d = idx.shape[0], table.shape[1]
    mesh = plsc.VectorSubcoreMesh(core_axis_name="c", subcore_axis_name="s")
    nc, workers = mesh.num_cores, mesh.num_cores * mesh.num_subcores
    chunk = SC_GATHER_ROWS
    per_w = r // workers
    pairs = per_w // (2 * chunk)
    assert per_w * workers == r and pairs * 2 * chunk == per_w

    @functools.partial(
        pl.kernel, mesh=mesh, out_type=jax.ShapeDtypeStruct((r, d), table.dtype),
        scratch_types=[pltpu.VMEM((per_w,), I32), pltpu.VMEM((2, chunk, d), table.dtype),
                       pltpu.SemaphoreType.DMA((2,)), pltpu.SemaphoreType.DMA((2,))],
        name="sc_gather")
    def gather(table_hbm, idx_hbm, out_hbm, idx_v, rows_v, gsem, wsem):
        base_w = (lax.axis_index("s") * nc + lax.axis_index("c")) * per_w
        pltpu.sync_copy(idx_hbm.at[pl.ds(base_w, per_w)], idx_v)

        def fetch(b, c):
            return pltpu.make_async_copy(table_hbm.at[idx_v.at[pl.ds(c * chunk, chunk)]],
                                         rows_v.at[b], gsem.at[b])

        def flush(b, base):
            return pltpu.make_async_copy(rows_v.at[b], out_hbm.at[pl.ds(base, chunk)], wsem.at[b])

        def body(j, carry):
            for b in range(2):
                @pl.when(j > 0)
                def _():
                    flush(b, base_w).wait()

                fetch(b, 2 * j + b).start()
            for b in range(2):
                fetch(b, 2 * j + b).wait()
                flush(b, base_w + (2 * j + b) * chunk).start()
            return carry

        lax.fori_loop(0, pairs, body, 0)
        for b in range(2):
            flush(b, base_w).wait()

    return gather(table, idx)


def _combine_body(h_ref, gate_ref, fg_ref, *refs):
    yg_refs, o_ref = refs[:TOP_K], refs[-1]
    half = D_MODEL // 2
    gates = gate_ref[...]
    h = h_ref[...]
    lo, hi = h[:, :half], h[:, half:]
    for kk in range(TOP_K):
        yp = yg_refs[kk][...]
        g = gates[:, kk:kk + 1]
        lo = lo + g * pltpu.bitcast(yp << 16, F32)
        hi = hi + g * pltpu.bitcast(yp & jnp.uint32(0xFFFF0000), F32)
    out = jnp.concatenate([lo, hi], axis=1)
    o_ref[...] = _rms(out, fg_ref[...])


def _combine(h, gates, fg, ygath, out_prev, *, part, parts):
    t = h.shape[0]
    tm = COMBINE_TILE
    nt = t // parts // tm
    row = lambda i: (part * nt + i, 0)
    slot = lambda kk: pl.BlockSpec((tm, D_MODEL // 2), lambda i: (kk * nt + i, 0))
    in_specs = [pl.BlockSpec((tm, D_MODEL), row), pl.BlockSpec((tm, LANES), row),
                pl.BlockSpec((1, D_MODEL), lambda i: (0, 0))] + [slot(kk) for kk in range(TOP_K)]
    args = [h, gates, fg] + [ygath] * TOP_K
    aliases = {}
    if out_prev is not None:
        aliases = {len(args): 0}
        in_specs.append(pl.BlockSpec(memory_space=pl.ANY))
        args.append(out_prev)
    return pl.pallas_call(
        _combine_body,
        grid=(nt,),
        in_specs=in_specs,
        out_specs=pl.BlockSpec((tm, D_MODEL), row),
        out_shape=jax.ShapeDtypeStruct((t, D_MODEL), F32),
        input_output_aliases=aliases,
        compiler_params=pltpu.CompilerParams(dimension_semantics=("arbitrary",),
                                             vmem_limit_bytes=VMEM_LIMIT),
        name="combine",
    )(*args)


def _block_tril(n, chunk, strict):
    r = jnp.arange(n)[:, None]
    c = jnp.arange(n)[None, :]
    keep = ((r // chunk) == (c // chunk)) & ((r > c) if strict else (r >= c))
    return keep.astype(BF16)


def _pad_lanes(a, fill=0.0):
    a = a.reshape(1, -1).astype(F32)
    return jnp.pad(a, ((0, 0), (0, LANES - a.shape[1])), constant_values=fill)


def _layer(h, norm_mix_g, w_in, gm_norm_g, gm_ws, gm_bs, dn_conv_w, dn_a_log, dn_dt_bias,
           dn_norm_g, w_out, norm_ffn_g, router_w, router_b, exp_w_gu, exp_b_gu, exp_w_down,
           exp_b_down, out_g):
    batch, seq, d = h.shape
    t = batch * seq
    x2 = h.reshape(t, d)

    c0, c1, c2 = GM_WIDTH, 2 * GM_WIDTH, 2 * GM_WIDTH + 3 * DN_KEY
    c3 = c2 + DN_VAL
    wb = w_in.astype(BF16)
    wu, wv, wqkv, wz = wb[:, :c0], wb[:, c0:c1], wb[:, c1:c2], wb[:, c2:c3]
    wab = jnp.pad(wb[:, c3:], ((0, 0), (0, LANES - 2 * DN_HEADS)))
    gmg = gm_norm_g.reshape(1, GM_WIDTH).astype(F32)
    bsb = jnp.repeat(gm_bs.T, HEAD_DIM, axis=1).astype(F32)
    alog = _pad_lanes(dn_a_log)
    dtb = _pad_lanes(dn_dt_bias)
    tri_incl = _block_tril(ROW_TILE, DN_CHUNK, strict=False)

    oa, q, k, v, z, gb, gt = _inproj(
        x2, norm_mix_g.reshape(1, d), wu, wv, wqkv, wz, wab, gmg, gm_ws.astype(F32), bsb,
        dn_conv_w.astype(F32), alog, dtb, tri_incl, seq=seq)

    grow = gt[:DN_HEADS].reshape(DN_HEADS, t // DN_CHUNK, DN_CHUNK).transpose(1, 0, 2)
    grow = grow.reshape(t // DN_CHUNK, DN_HEADS * DN_CHUNK)
    ob = _gdn(q, k, v, z, gb, grow, dn_norm_g.reshape(1, HEAD_DIM).astype(F32), batch=batch, seq=seq)

    wo = w_out.astype(BF16)
    rw = jnp.pad(router_w.astype(BF16), ((0, 0), (0, LANES - N_EXPERTS)))
    rb = _pad_lanes(router_b, fill=NEG_BIG)
    tri_strict = _block_tril(ROW_TILE, ROW_TILE, strict=True)
    hres, hp, meta, gates, cnt = _outproj(oa, ob, x2, wo[:GM_WIDTH], wo[GM_WIDTH:],
                                          norm_ffn_g.reshape(1, d), rw, rb, tri_strict)

    bm = FFN_BLOCK
    counts = cnt[0, :N_EXPERTS]
    padded = (counts + bm - 1) // bm * bm
    pad_end = jnp.cumsum(padded)
    pad_start = (pad_end - padded).astype(I32)
    n_blocks = (t * TOP_K + N_EXPERTS * bm) // bm
    nvalid = (pad_end[-1] // bm).astype(I32).reshape(1)
    blk = jnp.minimum(jnp.arange(n_blocks, dtype=I32), nvalid[0] - 1)
    blk_e = jnp.minimum(jnp.sum(pad_end[None, :] <= (blk * bm)[:, None], axis=1), N_EXPERTS - 1).astype(I32)
    eid = meta[:TOP_K]
    start_of = jnp.sum(jnp.where(eid[..., None] == jnp.arange(N_EXPERTS, dtype=I32), pad_start, 0), axis=-1)
    dest = start_of + meta[TOP_K:2 * TOP_K]

    xin = _sc_dispatch(hp, dest.reshape(-1), n_blocks * bm)
    eids = jnp.arange(N_EXPERTS, dtype=I32)
    of_blk = blk_e[:, None] == eids
    blk_rows = jnp.clip(jnp.sum(jnp.where(of_blk, counts + pad_start, 0), axis=1) - blk * bm, 0, bm)
    has_rows = counts > 0
    used = jnp.minimum(jnp.sort(jnp.where(has_rows, eids, N_EXPERTS + eids)), N_EXPERTS - 1)
    used = jnp.concatenate([used, jnp.sum(has_rows).reshape(1)]).astype(I32)
    blk_pos = jnp.sum(jnp.where(of_blk, jnp.cumsum(has_rows) - 1, 0), axis=1).astype(I32)
    blk_rows = blk_rows.astype(I32)
    y = _ffn(blk_e, nvalid, blk_rows, blk_pos, used, xin, exp_w_gu, exp_b_gu[:, None, :].astype(F32),
             exp_w_down, exp_b_down[:, None, :].astype(F32))
    out = None
    tp = t // COMBINE_PARTS
    for part in range(COMBINE_PARTS):
        ygath = _sc_gather(y, dest[:, part * tp:(part + 1) * tp].reshape(-1))
        out = _combine(hres, gates, out_g.reshape(1, d).astype(F32), ygath, out,
                       part=part, parts=COMBINE_PARTS)
    return out.reshape(batch, seq, d)


def kernel(x, norm_mix_g, w_in, gm_norm_g, gm_ws, gm_bs, dn_conv_w, dn_a_log, dn_dt_bias, dn_norm_g, w_out, norm_ffn_g, router_w, router_b, exp_w_gu, exp_b_gu, exp_w_down, exp_b_down, final_norm_g):
    depth = norm_mix_g.shape[0]
    assert depth == 1, "single-layer problem"
    return _layer(x, norm_mix_g[0], w_in[0], gm_norm_g[0], gm_ws[0], gm_bs[0], dn_conv_w[0],
                  dn_a_log[0], dn_dt_bias[0], dn_norm_g[0], w_out[0], norm_ffn_g[0], router_w[0],
                  router_b[0], exp_w_gu[0], exp_b_gu[0], exp_w_down[0], exp_b_down[0], final_norm_g)
```

```python
import functools

import jax
import jax.numpy as jnp
from jax import lax
from jax.experimental import pallas as pl
from jax.experimental.pallas import tpu as pltpu
from jax.experimental.pallas import tpu_sc as plsc

F32 = jnp.float32
BF16 = jnp.bfloat16
I32 = jnp.int32
U32 = jnp.uint32

D_MODEL = 1024
HEAD_DIM = 128
GM_HEADS = 4
GM_WIDTH = GM_HEADS * HEAD_DIM
GM_CHUNK = 128
DN_HEADS = 4
DN_DK = 128
DN_KEY = DN_HEADS * DN_DK
DN_VAL = DN_HEADS * HEAD_DIM
DN_CHUNK = 64
CONV_W = 4
N_EXPERTS = 32
TOP_K = 4
D_FF = D_MODEL
SWIGLU_LIMIT = 7.0
SWIGLU_ALPHA = 1.702
EPS = 1e-6

LANES = 128
INPROJ_COLS = 512
INPROJ_SUBBLOCKS = 2
ROW_TILE = 512
OUTPROJ_TILE = 1024
GDN_TILE = 1024
FFN_BLOCK = 1024
FFN_SUBBLOCKS = 4
SC_GATHER_ROWS = 64
COMBINE_PARTS = 8
COMBINE_TILE = 512
VMEM_LIMIT = 56 * 1024 * 1024
NEG_BIG = -1e30
GDN_GROUP = 4


def _dot(a, b):
    return jnp.dot(a, b, preferred_element_type=F32)


def _dot_nt(a, b):
    return lax.dot_general(a, b, (((1,), (1,)), ((), ())), preferred_element_type=F32)


def _dot_tn(a, b):
    return lax.dot_general(a, b, (((0,), (0,)), ((), ())), preferred_element_type=F32)


def _rms(x, g):
    return x * lax.rsqrt(jnp.mean(x * x, axis=-1, keepdims=True) + EPS) * g


def _gelu(x):
    return 0.5 * x * (1.0 + lax.erf(x * (2.0 ** -0.5)))


def _sigmoid(x):
    return 1.0 / (1.0 + jnp.exp(-x))


def _softplus(x):
    return jnp.maximum(x, 0.0) + jnp.log1p(jnp.exp(-jnp.abs(x)))


def _inproj_body(x_ref, ng_ref, wu_ref, wv_ref, wqkv_ref, wz_ref, wab_ref, gmg_ref, ws_ref,
                 bs_ref, cw_ref, alog_ref, dtb_ref, tri_ref,
                 oa_ref, q_ref, k_ref, v_ref, z_ref, gb_ref, gt_ref, cbuf_ref, ybuf_ref,
                 *, tm, tiles_per_seq):
    i = pl.program_id(0)
    ybuf_ref[...] = _rms(x_ref[...], ng_ref[...]).astype(BF16)
    nb = INPROJ_COLS
    heads_per_block = nb // HEAD_DIM

    ri = lax.broadcasted_iota(I32, (GM_CHUNK, GM_CHUNK), 0)
    ci = lax.broadcasted_iota(I32, (GM_CHUNK, GM_CHUNK), 1)
    causal = ri >= ci

    def gmlp_matmul(blk, r0, r1):
        cs = slice(blk * nb, (blk + 1) * nb)
        return _dot(ybuf_ref[r0:r1, :], wu_ref[:, cs]), _dot(ybuf_ref[r0:r1, :], wv_ref[:, cs])

    def gmlp_epilogue(blk, r0, r1, uv):
        u, vg = _gelu(uv[0]), _gelu(uv[1])
        for hh in range(heads_per_block):
            h = blk * heads_per_block + hh
            sl = slice(h * HEAD_DIM, (h + 1) * HEAD_DIM)
            ls = slice(hh * HEAD_DIM, (hh + 1) * HEAD_DIM)
            vh = _rms(vg[:, ls], gmg_ref[:, sl]).astype(BF16)
            wm = jnp.where(causal, ws_ref[h], 0.0).astype(BF16)
            for c in range((r1 - r0) // GM_CHUNK):
                rs = slice(c * GM_CHUNK, (c + 1) * GM_CHUNK)
                gate = _dot(wm, vh[rs]) + bs_ref[:, sl]
                oa_ref[r0 + c * GM_CHUNK:r0 + (c + 1) * GM_CHUNK, sl] = (u[rs, ls] * gate).astype(BF16)

    def qkv_matmul(blk, r0, r1):
        return _dot(ybuf_ref[r0:r1, :], wqkv_ref[:, blk * nb:(blk + 1) * nb])

    def qkv_epilogue(blk, r0, r1, pq):
        cs = slice(blk * nb, (blk + 1) * nb)
        cbuf_ref[8 + r0:8 + r1, cs] = pq
        acc = cw_ref[CONV_W - 1:CONV_W, cs] * pq
        for j in range(CONV_W - 1):
            off = 8 - (CONV_W - 1) + j
            acc = acc + cw_ref[j:j + 1, cs] * cbuf_ref[off + r0:off + r1, cs]
        if r1 == tm:
            cbuf_ref[0:8, cs] = pq[r1 - r0 - 8:, :]
        s = acc * _sigmoid(acc)
        for hh in range(heads_per_block):
            c0 = blk * nb + hh * HEAD_DIM
            sh = s[:, hh * HEAD_DIM:(hh + 1) * HEAD_DIM]
            if c0 < DN_KEY:
                q_ref[r0:r1, c0:c0 + HEAD_DIM] = (
                    sh * lax.rsqrt(jnp.sum(sh * sh, axis=-1, keepdims=True) + EPS)
                    * (DN_DK ** -0.5)).astype(BF16)
            elif c0 < 2 * DN_KEY:
                k_ref[r0:r1, c0 - DN_KEY:c0 - DN_KEY + HEAD_DIM] = (
                    sh * lax.rsqrt(jnp.sum(sh * sh, axis=-1, keepdims=True) + EPS)).astype(BF16)
            else:
                v_ref[r0:r1, c0 - 2 * DN_KEY:c0 - 2 * DN_KEY + HEAD_DIM] = sh.astype(BF16)

    def z_matmul(blk, r0, r1):
        return _dot(ybuf_ref[r0:r1, :], wz_ref[:, blk * nb:(blk + 1) * nb])

    def z_epilogue(blk, r0, r1, zz):
        z_ref[r0:r1, blk * nb:(blk + 1) * nb] = zz.astype(BF16)

    @pl.when(i % tiles_per_seq == 0)
    def _():
        cbuf_ref[0:8, :] = jnp.zeros((8, 3 * DN_KEY), F32)

    stages = ([(gmlp_matmul, gmlp_epilogue, blk) for blk in range(GM_WIDTH // nb)]
              + [(qkv_matmul, qkv_epilogue, blk) for blk in range(3 * DN_KEY // nb)]
              + [(z_matmul, z_epilogue, blk) for blk in range(DN_VAL // nb)])
    rsub = tm // INPROJ_SUBBLOCKS
    spans = [(sb * rsub, (sb + 1) * rsub) for sb in range(INPROJ_SUBBLOCKS)]
    for matmul, epilogue, blk in stages:
        results = [matmul(blk, r0, r1) for r0, r1 in spans]
        for (r0, r1), res in zip(spans, results):
            epilogue(blk, r0, r1, res)

    ab = _dot(ybuf_ref[...], wab_ref[...])
    g = -jnp.exp(alog_ref[...]) * _softplus(ab + dtb_ref[...])
    beta = _sigmoid(ab)
    g_hi = g.astype(BF16)
    r1 = g - g_hi.astype(F32)
    g_mid = r1.astype(BF16)
    g_lo = (r1 - g_mid.astype(F32)).astype(BF16)
    tri = tri_ref[...]
    gc = _dot(tri, g_hi) + _dot(tri, g_mid) + _dot(tri, g_lo)
    lane = lax.broadcasted_iota(I32, (tm, LANES), 1)
    gbv = jnp.where(lane < DN_HEADS, gc, beta)
    gb_ref[...] = gbv
    gt_ref[...] = gbv.T[0:8, :]


def _inproj(x2, ng, wu, wv, wqkv, wz, wab, gmg, ws, bsb, cw, alog, dtb, tri, *, seq):
    t = x2.shape[0]
    tm = ROW_TILE
    row = lambda i: (i, 0)
    full = lambda a: pl.BlockSpec(a.shape, (lambda i: (0,) * a.ndim))
    out_shapes = (
        jax.ShapeDtypeStruct((t, GM_WIDTH), BF16),
        jax.ShapeDtypeStruct((t, DN_KEY), BF16),
        jax.ShapeDtypeStruct((t, DN_KEY), BF16),
        jax.ShapeDtypeStruct((t, DN_VAL), BF16),
        jax.ShapeDtypeStruct((t, DN_VAL), BF16),
        jax.ShapeDtypeStruct((t, LANES), F32),
        jax.ShapeDtypeStruct((8, t), F32),
    )
    return pl.pallas_call(
        functools.partial(_inproj_body, tm=tm, tiles_per_seq=seq // tm),
        grid=(t // tm,),
        in_specs=[pl.BlockSpec((tm, D_MODEL), row), full(ng), full(wu), full(wv), full(wqkv),
                  full(wz), full(wab), full(gmg), full(ws), full(bsb), full(cw), full(alog),
                  full(dtb), full(tri)],
        out_specs=(pl.BlockSpec((tm, GM_WIDTH), row), pl.BlockSpec((tm, DN_KEY), row),
                   pl.BlockSpec((tm, DN_KEY), row), pl.BlockSpec((tm, DN_VAL), row),
                   pl.BlockSpec((tm, DN_VAL), row), pl.BlockSpec((tm, LANES), row),
                   pl.BlockSpec((8, tm), lambda i: (0, i))),
        out_shape=out_shapes,
        scratch_shapes=[pltpu.VMEM((tm + 8, 3 * DN_KEY), F32), pltpu.VMEM((tm, D_MODEL), BF16)],
        compiler_params=pltpu.CompilerParams(dimension_semantics=("arbitrary",),
                                             vmem_limit_bytes=VMEM_LIMIT),
        name="inproj",
    )(x2, ng, wu, wv, wqkv, wz, wab, gmg, ws, bsb, cw, alog, dtb, tri)


def _gdn_body(q_ref, k_ref, v_ref, z_ref, gb_ref, gr_ref, ng_ref, ob_ref, s_ref, *, nchunk,
              group_size):
    j = pl.program_id(1)

    @pl.when(j == 0)
    def _():
        s_ref[...] = jnp.zeros(s_ref.shape, F32)

    c = DN_CHUNK
    n = DN_HEADS * c
    ri = lax.broadcasted_iota(I32, (n, n), 0)
    ci = lax.broadcasted_iota(I32, (n, n), 1)
    same = (ri // c) == (ci // c)
    incl = same & ((ri % c) >= (ci % c))
    strict = same & ((ri % c) > (ci % c))
    ng = ng_ref[...]

    def stack(a):
        return jnp.concatenate([a[:, h * HEAD_DIM:(h + 1) * HEAD_DIM] for h in range(DN_HEADS)], axis=0)

    def prepare(ic):
        r0 = ic * c
        kst = stack(k_ref[pl.ds(r0, c), :])
        qst = stack(q_ref[pl.ds(r0, c), :])
        vst = stack(v_ref[pl.ds(r0, c), :])
        gbc = gb_ref[pl.ds(r0, c), :]
        grow = gr_ref[pl.ds(ic, 1), :]
        gcol = jnp.concatenate([gbc[:, h:h + 1] for h in range(DN_HEADS)], axis=0)
        bcol = jnp.concatenate([gbc[:, DN_HEADS + h:DN_HEADS + h + 1] for h in range(DN_HEADS)], axis=0)
        glast = jnp.concatenate(
            [jnp.broadcast_to(gbc[c - 1:c, h:h + 1], (c, 1)) for h in range(DN_HEADS)], axis=0)
        decay = jnp.where(incl, jnp.exp(jnp.where(incl, gcol - grow, 0.0)), 0.0)
        kf = kst.astype(F32)
        kb = kf * bcol
        lmat = jnp.where(strict, _dot_nt(kb.astype(BF16), kst) * decay, 0.0)
        eg = jnp.exp(gcol)
        rhs = jnp.concatenate([vst.astype(F32) * bcol, kb * eg], axis=1)
        attn = jnp.where(incl, _dot_nt(qst, kst) * decay, 0.0).astype(BF16)
        qd = (qst.astype(F32) * eg).astype(BF16)
        ke = (kf * jnp.exp(glast - gcol)).astype(BF16)
        return dict(r0=r0, gbc=gbc, lmat=lmat, rhs=rhs, attn=attn, qd=qd, ke=ke)

    def advance_state(p, sol):
        u = sol[:, :HEAD_DIM]
        wb = sol[:, HEAD_DIM:].astype(BF16)
        vn, qs = [], []
        for h in range(DN_HEADS):
            rs = slice(h * c, (h + 1) * c)
            sb = s_ref[h].astype(BF16)
            vn.append(u[rs] - _dot(wb[rs], sb))
            qs.append(_dot(p["qd"][rs], sb))
        vnb = jnp.concatenate(vn, axis=0).astype(BF16)
        o = jnp.concatenate(qs, axis=0) + _dot(p["attn"], vnb)
        zc = z_ref[pl.ds(p["r0"], c), :]
        for h in range(DN_HEADS):
            rs = slice(h * c, (h + 1) * c)
            sl = slice(h * HEAD_DIM, (h + 1) * HEAD_DIM)
            s_ref[h] = (s_ref[h] * jnp.exp(p["gbc"][c - 1:c, h:h + 1])
                        + _dot_tn(p["ke"][rs], vnb[rs]))
            zf = zc[:, sl].astype(F32)
            ob_ref[pl.ds(p["r0"], c), sl] = (_rms(o[rs], ng) * (zf * _sigmoid(zf))).astype(BF16)

    def solve_group(ig):
        ps = [prepare(ig * group_size + b) for b in range(group_size)]
        sol = [p["rhs"] for p in ps]
        pw = [-p["lmat"] for p in ps]
        for step in range(6):
            pb = [x.astype(BF16) for x in pw]
            sol = [s + _dot(xb, s.astype(BF16)) for s, xb in zip(sol, pb)]
            if step < 5:
                pw = [_dot(xb, xb) for xb in pb]
        return ps, sol

    ngroups = nchunk // group_size
    solved = solve_group(0)
    for ig in range(ngroups):
        ahead = solve_group(ig + 1) if ig + 1 < ngroups else None
        for p, s in zip(*solved):
            advance_state(p, s)
        solved = ahead


def _gdn(q, k, v, z, gb, grow, ng, *, batch, seq):
    tm = GDN_TILE
    nchunk = tm // DN_CHUNK
    steps = seq // tm
    rows = lambda b, j: (b * steps + j, 0)
    return pl.pallas_call(
        functools.partial(_gdn_body, nchunk=nchunk, group_size=GDN_GROUP),
        grid=(batch, steps),
        in_specs=[pl.BlockSpec((tm, DN_KEY), rows), pl.BlockSpec((tm, DN_KEY), rows),
                  pl.BlockSpec((tm, DN_VAL), rows), pl.BlockSpec((tm, DN_VAL), rows),
                  pl.BlockSpec((tm, LANES), rows),
                  pl.BlockSpec((nchunk, DN_HEADS * DN_CHUNK), rows),
                  pl.BlockSpec((1, HEAD_DIM), lambda b, j: (0, 0))],
        out_specs=pl.BlockSpec((tm, DN_VAL), rows),
        out_shape=jax.ShapeDtypeStruct((batch * seq, DN_VAL), BF16),
        scratch_shapes=[pltpu.VMEM((DN_HEADS, DN_DK, HEAD_DIM), F32)],
        compiler_params=pltpu.CompilerParams(dimension_semantics=("arbitrary", "arbitrary"),
                                             vmem_limit_bytes=VMEM_LIMIT),
        name="gdn",
    )(q, k, v, z, gb, grow, ng)


def _outproj_body(oa_ref, ob_ref, x_ref, woa_ref, wob_ref, ng_ref, rw_ref, rb_ref, tri_ref,
                  h_ref, hp_ref, meta_ref, gate_ref, cnt_ref, run_ref, *, tm):
    i = pl.program_id(0)

    @pl.when(i == 0)
    def _():
        run_ref[...] = jnp.zeros(run_ref.shape, F32)

    h = x_ref[...] + _dot(oa_ref[...], woa_ref[...]) + _dot(ob_ref[...], wob_ref[...])
    h_ref[...] = h
    hb = _rms(h, ng_ref[...]).astype(BF16)
    half = D_MODEL // 2
    lo = pltpu.bitcast(hb[:, :half].astype(F32), U32) >> 16
    hi = pltpu.bitcast(hb[:, half:].astype(F32), U32) & jnp.uint32(0xFFFF0000)
    hp_ref[...] = lo | hi

    logits = _dot(hb, rw_ref[...]) + rb_ref[...]
    lane = lax.broadcasted_iota(I32, (tm, LANES), 1)
    lanef = lane.astype(F32)
    work = logits
    onehot = jnp.zeros((tm, LANES), F32)
    vals, sels = [], []
    for _ in range(TOP_K):
        m = jnp.max(work, axis=-1, keepdims=True)
        idx = jnp.min(jnp.where(work == m, lanef, float(LANES)), axis=-1, keepdims=True)
        sel = lanef == idx
        work = jnp.where(sel, -3e38, work)
        onehot = onehot + jnp.where(sel, 1.0, 0.0)
        vals.append(m)
        sels.append((sel, idx))
    ex = [jnp.exp(v - vals[0]) for v in vals]
    den = ex[0] + ex[1] + ex[2] + ex[3]
    pref = _dot(tri_ref[...], onehot.astype(BF16)) + run_ref[0:1, :]
    meta = jnp.zeros((tm, LANES), F32)
    gates = jnp.zeros((tm, LANES), F32)
    for kk in range(TOP_K):
        sel, idx = sels[kk]
        rank = jnp.sum(jnp.where(sel, pref, 0.0), axis=-1, keepdims=True)
        meta = meta + jnp.where(lane == kk, idx, 0.0) + jnp.where(lane == TOP_K + kk, rank, 0.0)
        gates = gates + jnp.where(lane == kk, ex[kk] / den, 0.0)
    meta_ref[...] = meta.T[0:8, :].astype(I32)
    gate_ref[...] = gates
    run = run_ref[...] + jnp.sum(onehot, axis=0, keepdims=True)
    run_ref[...] = run
    cnt_ref[...] = run.astype(I32)


def _outproj(oa, ob, x2, woa, wob, ng, rw, rb, tri):
    t = x2.shape[0]
    tm = OUTPROJ_TILE
    row = lambda i: (i, 0)
    full = lambda a: pl.BlockSpec(a.shape, (lambda i: (0,) * a.ndim))
    out_shapes = (
        jax.ShapeDtypeStruct((t, D_MODEL), F32),
        jax.ShapeDtypeStruct((t, D_MODEL // 2), U32),
        jax.ShapeDtypeStruct((8, t), I32),
        jax.ShapeDtypeStruct((t, LANES), F32),
        jax.ShapeDtypeStruct((8, LANES), I32),
    )
    return pl.pallas_call(
        functools.partial(_outproj_body, tm=tm),
        grid=(t // tm,),
        in_specs=[pl.BlockSpec((tm, GM_WIDTH), row), pl.BlockSpec((tm, DN_VAL), row),
                  pl.BlockSpec((tm, D_MODEL), row), full(woa), full(wob), full(ng), full(rw),
                  full(rb), full(tri)],
        out_specs=(pl.BlockSpec((tm, D_MODEL), row), pl.BlockSpec((tm, D_MODEL // 2), row),
                   pl.BlockSpec((8, tm), lambda i: (0, i)), pl.BlockSpec((tm, LANES), row),
                   pl.BlockSpec((8, LANES), lambda i: (0, 0))),
        out_shape=out_shapes,
        scratch_shapes=[pltpu.VMEM((8, LANES), F32)],
        compiler_params=pltpu.CompilerParams(dimension_semantics=("arbitrary",),
                                             vmem_limit_bytes=VMEM_LIMIT),
        name="outproj",
    )(oa, ob, x2, woa, wob, ng, rw, rb, tri)


def _sc_dispatch(hp, dest_k, n_rows):
    t, d = hp.shape
    mesh = plsc.VectorSubcoreMesh(core_axis_name="c", subcore_axis_name="s")
    nc, workers = mesh.num_cores, mesh.num_cores * mesh.num_subcores
    chunk = SC_GATHER_ROWS
    per_w = t // workers
    pairs = per_w // (2 * chunk)
    assert per_w * workers == t and pairs * 2 * chunk == per_w

    @functools.partial(
        pl.kernel, mesh=mesh, out_type=jax.ShapeDtypeStruct((n_rows, d), hp.dtype),
        scratch_types=[pltpu.VMEM((2, TOP_K, chunk), I32), pltpu.VMEM((2, chunk, d), hp.dtype),
                       pltpu.SemaphoreType.DMA((2,)), pltpu.SemaphoreType.DMA((2,)),
                       pltpu.SemaphoreType.DMA((2,))],
        name="sc_dispatch")
    def scatter(hp_hbm, idx_hbm, xin_hbm, idx_v, rows_v, lsem, ssem, isem):
        base_w = (lax.axis_index("s") * nc + lax.axis_index("c")) * per_w

        def index(b, kk, base):
            return pltpu.make_async_copy(idx_hbm.at[pl.ds(kk * t + base, chunk)], idx_v.at[b, kk],
                                         isem.at[b])

        def load(b, base):
            return pltpu.make_async_copy(hp_hbm.at[pl.ds(base, chunk)], rows_v.at[b], lsem.at[b])

        def put(b, kk):
            return pltpu.make_async_copy(rows_v.at[b], xin_hbm.at[idx_v.at[b, kk]], ssem.at[b])

        def body(j, carry):
            for b in range(2):
                base = base_w + (2 * j + b) * chunk

                @pl.when(j > 0)
                def _():
                    for kk in range(TOP_K):
                        put(b, kk).wait()

                for kk in range(TOP_K):
                    index(b, kk, base).start()
                load(b, base).start()
            for b in range(2):
                for kk in range(TOP_K):
                    index(b, kk, base_w).wait()
                load(b, base_w).wait()
                for kk in range(TOP_K):
                    put(b, kk).start()
            return carry

        lax.fori_loop(0, pairs, body, 0)
        for b in range(2):
            for kk in range(TOP_K):
                put(b, kk).wait()

    return scatter(hp, dest_k)


def _ffn_body(be_ref, nv_ref, br_ref, ep_ref, es_ref, x_ref, wgu_hbm, bgu_ref, wd_hbm, bd_ref, y_ref,
              wgub_ref, wdb_ref, wguf_ref, wdf_ref, gsem, dsem):
    i = pl.program_id(0)

    @pl.when(i >= nv_ref[0])
    def _():
        y_ref[...] = jnp.zeros(y_ref.shape, U32)

    def fetch(pos):
        e, slot = es_ref[pos], pos % 2
        return (pltpu.make_async_copy(wgu_hbm.at[e], wguf_ref.at[slot], gsem.at[slot]),
                pltpu.make_async_copy(wd_hbm.at[e], wdf_ref.at[slot], dsem.at[slot]))

    @pl.when((i < nv_ref[0]) & ((i == 0) | (be_ref[i] != be_ref[jnp.maximum(i - 1, 0)])))
    def _():
        pos = ep_ref[i]

        @pl.when(i == 0)
        def _():
            for cp in fetch(pos):
                cp.start()

        for cp in fetch(pos):
            cp.wait()

        @pl.when(pos + 1 < es_ref[N_EXPERTS])
        def _():
            for cp in fetch(pos + 1):
                cp.start()

        slot = pos % 2
        wgub_ref[...] = wguf_ref[slot].astype(BF16)
        wdb_ref[...] = wdf_ref[slot].astype(BF16)

    nsub = FFN_SUBBLOCKS
    rsub = x_ref.shape[0] // nsub

    def compute(active):
        half = D_MODEL // 2
        row = lax.broadcasted_iota(I32, (rsub, 1), 0)
        gus = []
        for sb in range(active):
            rs = slice(sb * rsub, (sb + 1) * rsub)
            xp = jnp.where(row + sb * rsub < br_ref[i], x_ref[rs, :], jnp.uint32(0))
            lo = pltpu.bitcast(xp << 16, F32).astype(BF16)
            hi = pltpu.bitcast(xp & jnp.uint32(0xFFFF0000), F32).astype(BF16)
            gus.append(_dot(lo, wgub_ref[:half, :]) + _dot(hi, wgub_ref[half:, :]) + bgu_ref[0])
        ys = []
        for gu in gus:
            gate = jnp.minimum(gu[:, :D_FF], SWIGLU_LIMIT)
            up = jnp.clip(gu[:, D_FF:], -SWIGLU_LIMIT, SWIGLU_LIMIT)
            act = (up + 1.0) * (gate * _sigmoid(SWIGLU_ALPHA * gate))
            ys.append(_dot(act.astype(BF16), wdb_ref[...]) + bd_ref[0])
        for sb, y in enumerate(ys):
            ylo = pltpu.bitcast(y[:, :half].astype(BF16).astype(F32), U32) >> 16
            yhi = pltpu.bitcast(y[:, half:].astype(BF16).astype(F32), U32) & jnp.uint32(0xFFFF0000)
            y_ref[sb * rsub:(sb + 1) * rsub, :] = ylo | yhi
        if active < nsub:
            y_ref[active * rsub:, :] = jnp.zeros(((nsub - active) * rsub, half), U32)

    active = (br_ref[i] + rsub - 1) // rsub
    for n in range(1, nsub + 1):
        pl.when((i < nv_ref[0]) & (active == n))(functools.partial(compute, n))


def _ffn(blk_e, nvalid, blk_rows, blk_pos, used, xin, wgu, bgu, wd, bd):
    p = xin.shape[0]
    bm = FFN_BLOCK
    rows = lambda i, be, nv, br, ep, es: (jnp.minimum(i, nv[0] - 1), 0)
    wsel = lambda i, be, nv, br, ep, es: (be[i], 0, 0)
    return pl.pallas_call(
        _ffn_body,
        grid_spec=pltpu.PrefetchScalarGridSpec(
            num_scalar_prefetch=5,
            grid=(p // bm,),
            in_specs=[pl.BlockSpec((bm, D_MODEL // 2), rows),
                      pl.BlockSpec(memory_space=pl.ANY),
                      pl.BlockSpec((1, 1, 2 * D_FF), wsel),
                      pl.BlockSpec(memory_space=pl.ANY),
                      pl.BlockSpec((1, 1, D_MODEL), wsel)],
            out_specs=pl.BlockSpec((bm, D_MODEL // 2), lambda i, be, nv, br, ep, es: (i, 0)),
            scratch_shapes=[pltpu.VMEM((D_MODEL, 2 * D_FF), BF16), pltpu.VMEM((D_FF, D_MODEL), BF16),
                            pltpu.VMEM((2, D_MODEL, 2 * D_FF), F32), pltpu.VMEM((2, D_FF, D_MODEL), F32),
                            pltpu.SemaphoreType.DMA((2,)), pltpu.SemaphoreType.DMA((2,))],
        ),
        out_shape=jax.ShapeDtypeStruct((p, D_MODEL // 2), U32),
        compiler_params=pltpu.CompilerParams(dimension_semantics=("arbitrary",),
                                             vmem_limit_bytes=VMEM_LIMIT),
        name="ffn",
    )(blk_e, nvalid, blk_rows, blk_pos, used, xin, wgu, bgu, wd, bd)


def _sc_gather(table, idx):
    r, d = idx.shape[0], table.shape[1]
    mesh = plsc.VectorSubcoreMesh(core_axis_name="c", subcore_axis_name="s")
    nc, workers = mesh.num_cores, mesh.num_cores * mesh.num_subcores
    chunk = SC_GATHER_ROWS
    per_w = r // workers
    pairs = per_w // (2 * chunk)
    assert per_w * workers == r and pairs * 2 * chunk == per_w

    @functools.partial(
        pl.kernel, mesh=mesh, out_type=jax.ShapeDtypeStruct((r, d), table.dtype),
        scratch_types=[pltpu.VMEM((per_w,), I32), pltpu.VMEM((2, chunk, d), table.dtype),
                       pltpu.SemaphoreType.DMA((2,)), pltpu.SemaphoreType.DMA((2,))],
        name="sc_gather")
    def gather(table_hbm, idx_hbm, out_hbm, idx_v, rows_v, gsem, wsem):
        base_w = (lax.axis_index("s") * nc + lax.axis_index("c")) * per_w
        pltpu.sync_copy(idx_hbm.at[pl.ds(base_w, per_w)], idx_v)

        def fetch(b, c):
            return pltpu.make_async_copy(table_hbm.at[idx_v.at[pl.ds(c * chunk, chunk)]],
                                         rows_v.at[b], gsem.at[b])

        def flush(b, base):
            return pltpu.make_async_copy(rows_v.at[b], out_hbm.at[pl.ds(base, chunk)], wsem.at[b])

        def body(j, carry):
            for b in range(2):
                @pl.when(j > 0)
                def _():
                    flush(b, base_w).wait()

                fetch(b, 2 * j + b).start()
            for b in range(2):
                fetch(b, 2 * j + b).wait()
                flush(b, base_w + (2 * j + b) * chunk).start()
            return carry

        lax.fori_loop(0, pairs, body, 0)
        for b in range(2):
            flush(b, base_w).wait()

    return gather(table, idx)


def _combine_body(h_ref, gate_ref, fg_ref, *refs):
    yg_refs, o_ref = refs[:TOP_K], refs[-1]
    half = D_MODEL // 2
    gates = gate_ref[...]
    h = h_ref[...]
    lo, hi = h[:, :half], h[:, half:]
    for kk in range(TOP_K):
        yp = yg_refs[kk][...]
        g = gates[:, kk:kk + 1]
        lo = lo + g * pltpu.bitcast(yp << 16, F32)
        hi = hi + g * pltpu.bitcast(yp & jnp.uint32(0xFFFF0000), F32)
    out = jnp.concatenate([lo, hi], axis=1)
    o_ref[...] = _rms(out, fg_ref[...])


def _combine(h, gates, fg, ygath, out_prev, *, part, parts):
    t = h.shape[0]
    tm = min(COMBINE_TILE, t // parts)
    nt = t // parts // tm
    row = lambda i: (part * nt + i, 0)
    slot = lambda kk: pl.BlockSpec((tm, D_MODEL // 2), lambda i: (kk * nt + i, 0))
    in_specs = [pl.BlockSpec((tm, D_MODEL), row), pl.BlockSpec((tm, LANES), row),
                pl.BlockSpec((1, D_MODEL), lambda i: (0, 0))] + [slot(kk) for kk in range(TOP_K)]
    args = [h, gates, fg] + [ygath] * TOP_K
    aliases = {}
    if out_prev is not None:
        aliases = {len(args): 0}
        in_specs.append(pl.BlockSpec(memory_space=pl.ANY))
        args.append(out_prev)
    return pl.pallas_call(
        _combine_body,
        grid=(nt,),
        in_specs=in_specs,
        out_specs=pl.BlockSpec((tm, D_MODEL), row),
        out_shape=jax.ShapeDtypeStruct((t, D_MODEL), F32),
        input_output_aliases=aliases,
        compiler_params=pltpu.CompilerParams(dimension_semantics=("arbitrary",),
                                             vmem_limit_bytes=VMEM_LIMIT),
        name="combine",
    )(*args)


def _block_tril(n, chunk, strict):
    r = jnp.arange(n)[:, None]
    c = jnp.arange(n)[None, :]
    keep = ((r // chunk) == (c // chunk)) & ((r > c) if strict else (r >= c))
    return keep.astype(BF16)


def _pad_lanes(a, fill=0.0):
    a = a.reshape(1, -1).astype(F32)
    return jnp.pad(a, ((0, 0), (0, LANES - a.shape[1])), constant_values=fill)


def _layer(h, norm_mix_g, w_in, gm_norm_g, gm_ws, gm_bs, dn_conv_w, dn_a_log, dn_dt_bias,
           dn_norm_g, w_out, norm_ffn_g, router_w, router_b, exp_w_gu, exp_b_gu, exp_w_down,
           exp_b_down, out_g):
    batch, seq, d = h.shape
    t = batch * seq
    x2 = h.reshape(t, d)

    c0, c1, c2 = GM_WIDTH, 2 * GM_WIDTH, 2 * GM_WIDTH + 3 * DN_KEY
    c3 = c2 + DN_VAL
    wb = w_in.astype(BF16)
    wu, wv, wqkv, wz = wb[:, :c0], wb[:, c0:c1], wb[:, c1:c2], wb[:, c2:c3]
    wab = jnp.pad(wb[:, c3:], ((0, 0), (0, LANES - 2 * DN_HEADS)))
    gmg = gm_norm_g.reshape(1, GM_WIDTH).astype(F32)
    bsb = jnp.repeat(gm_bs.T, HEAD_DIM, axis=1).astype(F32)
    alog = _pad_lanes(dn_a_log)
    dtb = _pad_lanes(dn_dt_bias)
    tri_incl = _block_tril(ROW_TILE, DN_CHUNK, strict=False)

    oa, q, k, v, z, gb, gt = _inproj(
        x2, norm_mix_g.reshape(1, d), wu, wv, wqkv, wz, wab, gmg, gm_ws.astype(F32), bsb,
        dn_conv_w.astype(F32), alog, dtb, tri_incl, seq=seq)

    grow = gt[:DN_HEADS].reshape(DN_HEADS, t // DN_CHUNK, DN_CHUNK).transpose(1, 0, 2)
    grow = grow.reshape(t // DN_CHUNK, DN_HEADS * DN_CHUNK)
    ob = _gdn(q, k, v, z, gb, grow, dn_norm_g.reshape(1, HEAD_DIM).astype(F32), batch=batch, seq=seq)

    wo = w_out.astype(BF16)
    rw = jnp.pad(router_w.astype(BF16), ((0, 0), (0, LANES - N_EXPERTS)))
    rb = _pad_lanes(router_b, fill=NEG_BIG)
    tri_strict = _block_tril(OUTPROJ_TILE, OUTPROJ_TILE, strict=True)
    hres, hp, meta, gates, cnt = _outproj(oa, ob, x2, wo[:GM_WIDTH], wo[GM_WIDTH:],
                                          norm_ffn_g.reshape(1, d), rw, rb, tri_strict)

    bm = FFN_BLOCK
    counts = cnt[0, :N_EXPERTS]
    padded = (counts + bm - 1) // bm * bm
    pad_end = jnp.cumsum(padded)
    pad_start = (pad_end - padded).astype(I32)
    n_blocks = (t * TOP_K + N_EXPERTS * bm) // bm
    nvalid = (pad_end[-1] // bm).astype(I32).reshape(1)
    blk = jnp.minimum(jnp.arange(n_blocks, dtype=I32), nvalid[0] - 1)
    blk_e = jnp.minimum(jnp.sum(pad_end[None, :] <= (blk * bm)[:, None], axis=1), N_EXPERTS - 1).astype(I32)
    eid = meta[:TOP_K]
    start_of = jnp.sum(jnp.where(eid[..., None] == jnp.arange(N_EXPERTS, dtype=I32), pad_start, 0), axis=-1)
    dest = start_of + meta[TOP_K:2 * TOP_K]

    xin = _sc_dispatch(hp, dest.reshape(-1), n_blocks * bm)
    eids = jnp.arange(N_EXPERTS, dtype=I32)
    of_blk = blk_e[:, None] == eids
    blk_rows = jnp.clip(jnp.sum(jnp.where(of_blk, counts + pad_start, 0), axis=1) - blk * bm, 0, bm)
    has_rows = counts > 0
    used = jnp.minimum(jnp.sort(jnp.where(has_rows, eids, N_EXPERTS + eids)), N_EXPERTS - 1)
    used = jnp.concatenate([used, jnp.sum(has_rows).reshape(1)]).astype(I32)
    blk_pos = jnp.sum(jnp.where(of_blk, jnp.cumsum(has_rows) - 1, 0), axis=1).astype(I32)
    blk_rows = blk_rows.astype(I32)
    y = _ffn(blk_e, nvalid, blk_rows, blk_pos, used, xin, exp_w_gu, exp_b_gu[:, None, :].astype(F32),
             exp_w_down, exp_b_down[:, None, :].astype(F32))
    out = None
    tp = t // COMBINE_PARTS
    for part in range(COMBINE_PARTS):
        ygath = _sc_gather(y, dest[:, part * tp:(part + 1) * tp].reshape(-1))
        out = _combine(hres, gates, out_g.reshape(1, d).astype(F32), ygath, out,
                       part=part, parts=COMBINE_PARTS)
    return out.reshape(batch, seq, d)


def kernel(x, norm_mix_g, w_in, gm_norm_g, gm_ws, gm_bs, dn_conv_w, dn_a_log, dn_dt_bias, dn_norm_g, w_out, norm_ffn_g, router_w, router_b, exp_w_gu, exp_b_gu, exp_w_down, exp_b_down, final_norm_g):
    depth = norm_mix_g.shape[0]
    assert depth == 1, "single-layer problem"
    return _layer(x, norm_mix_g[0], w_in[0], gm_norm_g[0], gm_ws[0], gm_bs[0], dn_conv_w[0],
                  dn_a_log[0], dn_dt_bias[0], dn_norm_g[0], w_out[0], norm_ffn_g[0], router_w[0],
                  router_b[0], exp_w_gu[0], exp_b_gu[0], exp_w_down[0], exp_b_down[0], final_norm_g)
```

```python
import functools

import jax
import jax.numpy as jnp
from jax import lax
from jax.experimental import pallas as pl
from jax.experimental.pallas import tpu as pltpu
from jax.experimental.pallas import tpu_sc as plsc

F32 = jnp.float32
BF16 = jnp.bfloat16
I32 = jnp.int32
U32 = jnp.uint32

D_MODEL = 1024
HEAD_DIM = 128
GM_HEADS = 4
GM_WIDTH = GM_HEADS * HEAD_DIM
GM_CHUNK = 128
DN_HEADS = 4
DN_DK = 128
DN_KEY = DN_HEADS * DN_DK
DN_VAL = DN_HEADS * HEAD_DIM
DN_CHUNK = 64
CONV_W = 4
N_EXPERTS = 32
TOP_K = 4
D_FF = D_MODEL
SWIGLU_LIMIT = 7.0
SWIGLU_ALPHA = 1.702
EPS = 1e-6

LANES = 128
INPROJ_COLS = 512
INPROJ_SUBBLOCKS = 2
ROW_TILE = 512
OUTPROJ_TILE = 1024
GDN_TILE = 1024
FFN_BLOCK = 1024
FFN_SUBBLOCKS = 4
SC_GATHER_ROWS = 64
COMBINE_PARTS = 4
COMBINE_TILE = 512
VMEM_LIMIT = 56 * 1024 * 1024
NEG_BIG = -1e30
GDN_GROUP = 4


def _dot(a, b):
    return jnp.dot(a, b, preferred_element_type=F32)


def _dot_nt(a, b):
    return lax.dot_general(a, b, (((1,), (1,)), ((), ())), preferred_element_type=F32)


def _dot_tn(a, b):
    return lax.dot_general(a, b, (((0,), (0,)), ((), ())), preferred_element_type=F32)


def _rms(x, g):
    return x * lax.rsqrt(jnp.mean(x * x, axis=-1, keepdims=True) + EPS) * g


def _gelu(x):
    return 0.5 * x * (1.0 + lax.erf(x * (2.0 ** -0.5)))


def _sigmoid(x):
    return 1.0 / (1.0 + jnp.exp(-x))


def _softplus(x):
    return jnp.maximum(x, 0.0) + jnp.log1p(jnp.exp(-jnp.abs(x)))


def _inproj_body(x_ref, ng_ref, wu_ref, wv_ref, wqkv_ref, wz_ref, wab_ref, gmg_ref, ws_ref,
                 bs_ref, cw_ref, alog_ref, dtb_ref, tri_ref,
                 oa_ref, q_ref, k_ref, v_ref, z_ref, gb_ref, gt_ref, cbuf_ref, ybuf_ref,
                 *, tm, tiles_per_seq):
    i = pl.program_id(0)
    ybuf_ref[...] = _rms(x_ref[...], ng_ref[...]).astype(BF16)
    nb = INPROJ_COLS
    heads_per_block = nb // HEAD_DIM

    ri = lax.broadcasted_iota(I32, (GM_CHUNK, GM_CHUNK), 0)
    ci = lax.broadcasted_iota(I32, (GM_CHUNK, GM_CHUNK), 1)
    causal = ri >= ci

    def gmlp_matmul(blk, r0, r1):
        cs = slice(blk * nb, (blk + 1) * nb)
        return _dot(ybuf_ref[r0:r1, :], wu_ref[:, cs]), _dot(ybuf_ref[r0:r1, :], wv_ref[:, cs])

    def gmlp_epilogue(blk, r0, r1, uv):
        u, vg = _gelu(uv[0]), _gelu(uv[1])
        for hh in range(heads_per_block):
            h = blk * heads_per_block + hh
            sl = slice(h * HEAD_DIM, (h + 1) * HEAD_DIM)
            ls = slice(hh * HEAD_DIM, (hh + 1) * HEAD_DIM)
            vh = _rms(vg[:, ls], gmg_ref[:, sl]).astype(BF16)
            wm = jnp.where(causal, ws_ref[h], 0.0).astype(BF16)
            for c in range((r1 - r0) // GM_CHUNK):
                rs = slice(c * GM_CHUNK, (c + 1) * GM_CHUNK)
                gate = _dot(wm, vh[rs]) + bs_ref[:, sl]
                oa_ref[r0 + c * GM_CHUNK:r0 + (c + 1) * GM_CHUNK, sl] = (u[rs, ls] * gate).astype(BF16)

    def qkv_matmul(blk, r0, r1):
        return _dot(ybuf_ref[r0:r1, :], wqkv_ref[:, blk * nb:(blk + 1) * nb])

    def qkv_epilogue(blk, r0, r1, pq):
        cs = slice(blk * nb, (blk + 1) * nb)
        cbuf_ref[8 + r0:8 + r1, cs] = pq
        acc = cw_ref[CONV_W - 1:CONV_W, cs] * pq
        for j in range(CONV_W - 1):
            off = 8 - (CONV_W - 1) + j
            acc = acc + cw_ref[j:j + 1, cs] * cbuf_ref[off + r0:off + r1, cs]
        if r1 == tm:
            cbuf_ref[0:8, cs] = pq[r1 - r0 - 8:, :]
        s = acc * _sigmoid(acc)
        for hh in range(heads_per_block):
            c0 = blk * nb + hh * HEAD_DIM
            sh = s[:, hh * HEAD_DIM:(hh + 1) * HEAD_DIM]
            if c0 < DN_KEY:
                q_ref[r0:r1, c0:c0 + HEAD_DIM] = (
                    sh * lax.rsqrt(jnp.sum(sh * sh, axis=-1, keepdims=True) + EPS)
                    * (DN_DK ** -0.5)).astype(BF16)
            elif c0 < 2 * DN_KEY:
                k_ref[r0:r1, c0 - DN_KEY:c0 - DN_KEY + HEAD_DIM] = (
                    sh * lax.rsqrt(jnp.sum(sh * sh, axis=-1, keepdims=True) + EPS)).astype(BF16)
            else:
                v_ref[r0:r1, c0 - 2 * DN_KEY:c0 - 2 * DN_KEY + HEAD_DIM] = sh.astype(BF16)

    def z_matmul(blk, r0, r1):
        return _dot(ybuf_ref[r0:r1, :], wz_ref[:, blk * nb:(blk + 1) * nb])

    def z_epilogue(blk, r0, r1, zz):
        z_ref[r0:r1, blk * nb:(blk + 1) * nb] = zz.astype(BF16)

    @pl.when(i % tiles_per_seq == 0)
    def _():
        cbuf_ref[0:8, :] = jnp.zeros((8, 3 * DN_KEY), F32)

    stages = ([(gmlp_matmul, gmlp_epilogue, blk) for blk in range(GM_WIDTH // nb)]
              + [(qkv_matmul, qkv_epilogue, blk) for blk in range(3 * DN_KEY // nb)]
              + [(z_matmul, z_epilogue, blk) for blk in range(DN_VAL // nb)])
    rsub = tm // INPROJ_SUBBLOCKS
    spans = [(sb * rsub, (sb + 1) * rsub) for sb in range(INPROJ_SUBBLOCKS)]
    for matmul, epilogue, blk in stages:
        results = [matmul(blk, r0, r1) for r0, r1 in spans]
        for (r0, r1), res in zip(spans, results):
            epilogue(blk, r0, r1, res)

    ab = _dot(ybuf_ref[...], wab_ref[...])
    g = -jnp.exp(alog_ref[...]) * _softplus(ab + dtb_ref[...])
    beta = _sigmoid(ab)
    g_hi = g.astype(BF16)
    r1 = g - g_hi.astype(F32)
    g_mid = r1.astype(BF16)
    g_lo = (r1 - g_mid.astype(F32)).astype(BF16)
    tri = tri_ref[...]
    gc = _dot(tri, g_hi) + _dot(tri, g_mid) + _dot(tri, g_lo)
    lane = lax.broadcasted_iota(I32, (tm, LANES), 1)
    gbv = jnp.where(lane < DN_HEADS, gc, beta)
    gb_ref[...] = gbv
    gt_ref[...] = gbv.T[0:8, :]


def _inproj(x2, ng, wu, wv, wqkv, wz, wab, gmg, ws, bsb, cw, alog, dtb, tri, *, seq):
    t = x2.shape[0]
    tm = ROW_TILE
    row = lambda i: (i, 0)
    full = lambda a: pl.BlockSpec(a.shape, (lambda i: (0,) * a.ndim))
    out_shapes = (
        jax.ShapeDtypeStruct((t, GM_WIDTH), BF16),
        jax.ShapeDtypeStruct((t, DN_KEY), BF16),
        jax.ShapeDtypeStruct((t, DN_KEY), BF16),
        jax.ShapeDtypeStruct((t, DN_VAL), BF16),
        jax.ShapeDtypeStruct((t, DN_VAL), BF16),
        jax.ShapeDtypeStruct((t, LANES), F32),
        jax.ShapeDtypeStruct((8, t), F32),
    )
    return pl.pallas_call(
        functools.partial(_inproj_body, tm=tm, tiles_per_seq=seq // tm),
        grid=(t // tm,),
        in_specs=[pl.BlockSpec((tm, D_MODEL), row), full(ng), full(wu), full(wv), full(wqkv),
                  full(wz), full(wab), full(gmg), full(ws), full(bsb), full(cw), full(alog),
                  full(dtb), full(tri)],
        out_specs=(pl.BlockSpec((tm, GM_WIDTH), row), pl.BlockSpec((tm, DN_KEY), row),
                   pl.BlockSpec((tm, DN_KEY), row), pl.BlockSpec((tm, DN_VAL), row),
                   pl.BlockSpec((tm, DN_VAL), row), pl.BlockSpec((tm, LANES), row),
                   pl.BlockSpec((8, tm), lambda i: (0, i))),
        out_shape=out_shapes,
        scratch_shapes=[pltpu.VMEM((tm + 8, 3 * DN_KEY), F32), pltpu.VMEM((tm, D_MODEL), BF16)],
        compiler_params=pltpu.CompilerParams(dimension_semantics=("arbitrary",),
                                             vmem_limit_bytes=VMEM_LIMIT),
        name="inproj",
    )(x2, ng, wu, wv, wqkv, wz, wab, gmg, ws, bsb, cw, alog, dtb, tri)


def _gdn_body(q_ref, k_ref, v_ref, z_ref, gb_ref, gr_ref, ng_ref, ob_ref, s_ref, *, nchunk,
              group_size):
    j = pl.program_id(1)

    @pl.when(j == 0)
    def _():
        s_ref[...] = jnp.zeros(s_ref.shape, F32)

    c = DN_CHUNK
    n = DN_HEADS * c
    ri = lax.broadcasted_iota(I32, (n, n), 0)
    ci = lax.broadcasted_iota(I32, (n, n), 1)
    same = (ri // c) == (ci // c)
    incl = same & ((ri % c) >= (ci % c))
    strict = same & ((ri % c) > (ci % c))
    ng = ng_ref[...]

    def stack(a):
        return jnp.concatenate([a[:, h * HEAD_DIM:(h + 1) * HEAD_DIM] for h in range(DN_HEADS)], axis=0)

    def prepare(ic):
        r0 = ic * c
        kst = stack(k_ref[pl.ds(r0, c), :])
        qst = stack(q_ref[pl.ds(r0, c), :])
        vst = stack(v_ref[pl.ds(r0, c), :])
        gbc = gb_ref[pl.ds(r0, c), :]
        grow = gr_ref[pl.ds(ic, 1), :]
        gcol = jnp.concatenate([gbc[:, h:h + 1] for h in range(DN_HEADS)], axis=0)
        bcol = jnp.concatenate([gbc[:, DN_HEADS + h:DN_HEADS + h + 1] for h in range(DN_HEADS)], axis=0)
        glast = jnp.concatenate(
            [jnp.broadcast_to(gbc[c - 1:c, h:h + 1], (c, 1)) for h in range(DN_HEADS)], axis=0)
        decay = jnp.where(incl, jnp.exp(jnp.where(incl, gcol - grow, 0.0)), 0.0)
        kf = kst.astype(F32)
        kb = kf * bcol
        lmat = jnp.where(strict, _dot_nt(kb.astype(BF16), kst) * decay, 0.0)
        eg = jnp.exp(gcol)
        rhs = jnp.concatenate([vst.astype(F32) * bcol, kb * eg], axis=1)
        attn = jnp.where(incl, _dot_nt(qst, kst) * decay, 0.0).astype(BF16)
        qd = (qst.astype(F32) * eg).astype(BF16)
        ke = (kf * jnp.exp(glast - gcol)).astype(BF16)
        return dict(r0=r0, gbc=gbc, lmat=lmat, rhs=rhs, attn=attn, qd=qd, ke=ke)

    def advance_state(p, sol):
        u = sol[:, :HEAD_DIM]
        wb = sol[:, HEAD_DIM:].astype(BF16)
        vn, qs = [], []
        for h in range(DN_HEADS):
            rs = slice(h * c, (h + 1) * c)
            sb = s_ref[h].astype(BF16)
            vn.append(u[rs] - _dot(wb[rs], sb))
            qs.append(_dot(p["qd"][rs], sb))
        vnb = jnp.concatenate(vn, axis=0).astype(BF16)
        o = jnp.concatenate(qs, axis=0) + _dot(p["attn"], vnb)
        zc = z_ref[pl.ds(p["r0"], c), :]
        for h in range(DN_HEADS):
            rs = slice(h * c, (h + 1) * c)
            sl = slice(h * HEAD_DIM, (h + 1) * HEAD_DIM)
            s_ref[h] = (s_ref[h] * jnp.exp(p["gbc"][c - 1:c, h:h + 1])
                        + _dot_tn(p["ke"][rs], vnb[rs]))
            zf = zc[:, sl].astype(F32)
            ob_ref[pl.ds(p["r0"], c), sl] = (_rms(o[rs], ng) * (zf * _sigmoid(zf))).astype(BF16)

    def solve_group(ig):
        ps = [prepare(ig * group_size + b) for b in range(group_size)]
        sol = [p["rhs"] for p in ps]
        pw = [-p["lmat"] for p in ps]
        for step in range(6):
            pb = [x.astype(BF16) for x in pw]
            sol = [s + _dot(xb, s.astype(BF16)) for s, xb in zip(sol, pb)]
            if step < 5:
                pw = [_dot(xb, xb) for xb in pb]
        return ps, sol

    ngroups = nchunk // group_size
    solved = solve_group(0)
    for ig in range(ngroups):
        ahead = solve_group(ig + 1) if ig + 1 < ngroups else None
        for p, s in zip(*solved):
            advance_state(p, s)
        solved = ahead


def _gdn(q, k, v, z, gb, grow, ng, *, batch, seq):
    tm = GDN_TILE
    nchunk = tm // DN_CHUNK
    steps = seq // tm
    rows = lambda b, j: (b * steps + j, 0)
    return pl.pallas_call(
        functools.partial(_gdn_body, nchunk=nchunk, group_size=GDN_GROUP),
        grid=(batch, steps),
        in_specs=[pl.BlockSpec((tm, DN_KEY), rows), pl.BlockSpec((tm, DN_KEY), rows),
                  pl.BlockSpec((tm, DN_VAL), rows), pl.BlockSpec((tm, DN_VAL), rows),
                  pl.BlockSpec((tm, LANES), rows),
                  pl.BlockSpec((nchunk, DN_HEADS * DN_CHUNK), rows),
                  pl.BlockSpec((1, HEAD_DIM), lambda b, j: (0, 0))],
        out_specs=pl.BlockSpec((tm, DN_VAL), rows),
        out_shape=jax.ShapeDtypeStruct((batch * seq, DN_VAL), BF16),
        scratch_shapes=[pltpu.VMEM((DN_HEADS, DN_DK, HEAD_DIM), F32)],
        compiler_params=pltpu.CompilerParams(dimension_semantics=("arbitrary", "arbitrary"),
                                             vmem_limit_bytes=VMEM_LIMIT),
        name="gdn",
    )(q, k, v, z, gb, grow, ng)


def _outproj_body(oa_ref, ob_ref, x_ref, woa_ref, wob_ref, ng_ref, rw_ref, rb_ref, tri_ref,
                  h_ref, hp_ref, meta_ref, gate_ref, cnt_ref, run_ref, *, tm):
    i = pl.program_id(0)

    @pl.when(i == 0)
    def _():
        run_ref[...] = jnp.zeros(run_ref.shape, F32)

    h = x_ref[...] + _dot(oa_ref[...], woa_ref[...]) + _dot(ob_ref[...], wob_ref[...])
    h_ref[...] = h
    hb = _rms(h, ng_ref[...]).astype(BF16)
    half = D_MODEL // 2
    lo = pltpu.bitcast(hb[:, :half].astype(F32), U32) >> 16
    hi = pltpu.bitcast(hb[:, half:].astype(F32), U32) & jnp.uint32(0xFFFF0000)
    hp_ref[...] = lo | hi

    logits = _dot(hb, rw_ref[...]) + rb_ref[...]
    lane = lax.broadcasted_iota(I32, (tm, LANES), 1)
    lanef = lane.astype(F32)
    work = logits
    onehot = jnp.zeros((tm, LANES), F32)
    vals, sels = [], []
    for _ in range(TOP_K):
        m = jnp.max(work, axis=-1, keepdims=True)
        idx = jnp.min(jnp.where(work == m, lanef, float(LANES)), axis=-1, keepdims=True)
        sel = lanef == idx
        work = jnp.where(sel, -3e38, work)
        onehot = onehot + jnp.where(sel, 1.0, 0.0)
        vals.append(m)
        sels.append((sel, idx))
    ex = [jnp.exp(v - vals[0]) for v in vals]
    den = ex[0] + ex[1] + ex[2] + ex[3]
    pref = _dot(tri_ref[...], onehot.astype(BF16)) + run_ref[0:1, :]
    meta = jnp.zeros((tm, LANES), F32)
    gates = jnp.zeros((tm, LANES), F32)
    for kk in range(TOP_K):
        sel, idx = sels[kk]
        rank = jnp.sum(jnp.where(sel, pref, 0.0), axis=-1, keepdims=True)
        meta = meta + jnp.where(lane == kk, idx, 0.0) + jnp.where(lane == TOP_K + kk, rank, 0.0)
        gates = gates + jnp.where(lane == kk, ex[kk] / den, 0.0)
    meta_ref[...] = meta.T[0:8, :].astype(I32)
    gate_ref[...] = gates
    run = run_ref[...] + jnp.sum(onehot, axis=0, keepdims=True)
    run_ref[...] = run
    cnt_ref[...] = run.astype(I32)


def _outproj(oa, ob, x2, woa, wob, ng, rw, rb, tri):
    t = x2.shape[0]
    tm = OUTPROJ_TILE
    row = lambda i: (i, 0)
    full = lambda a: pl.BlockSpec(a.shape, (lambda i: (0,) * a.ndim))
    out_shapes = (
        jax.ShapeDtypeStruct((t, D_MODEL), F32),
        jax.ShapeDtypeStruct((t, D_MODEL // 2), U32),
        jax.ShapeDtypeStruct((8, t), I32),
        jax.ShapeDtypeStruct((t, LANES), F32),
        jax.ShapeDtypeStruct((8, LANES), I32),
    )
    return pl.pallas_call(
        functools.partial(_outproj_body, tm=tm),
        grid=(t // tm,),
        in_specs=[pl.BlockSpec((tm, GM_WIDTH), row), pl.BlockSpec((tm, DN_VAL), row),
                  pl.BlockSpec((tm, D_MODEL), row), full(woa), full(wob), full(ng), full(rw),
                  full(rb), full(tri)],
        out_specs=(pl.BlockSpec((tm, D_MODEL), row), pl.BlockSpec((tm, D_MODEL // 2), row),
                   pl.BlockSpec((8, tm), lambda i: (0, i)), pl.BlockSpec((tm, LANES), row),
                   pl.BlockSpec((8, LANES), lambda i: (0, 0))),
        out_shape=out_shapes,
        scratch_shapes=[pltpu.VMEM((8, LANES), F32)],
        compiler_params=pltpu.CompilerParams(dimension_semantics=("arbitrary",),
                                             vmem_limit_bytes=VMEM_LIMIT),
        name="outproj",
    )(oa, ob, x2, woa, wob, ng, rw, rb, tri)


def _sc_dispatch(hp, dest_k, n_rows):
    t, d = hp.shape
    mesh = plsc.VectorSubcoreMesh(core_axis_name="c", subcore_axis_name="s")
    nc, workers = mesh.num_cores, mesh.num_cores * mesh.num_subcores
    chunk = SC_GATHER_ROWS
    per_w = t // workers
    pairs = per_w // (2 * chunk)
    assert per_w * workers == t and pairs * 2 * chunk == per_w

    @functools.partial(
        pl.kernel, mesh=mesh, out_type=jax.ShapeDtypeStruct((n_rows, d), hp.dtype),
        scratch_types=[pltpu.VMEM((2, TOP_K, chunk), I32), pltpu.VMEM((2, chunk, d), hp.dtype),
                       pltpu.SemaphoreType.DMA((2,)), pltpu.SemaphoreType.DMA((2,)),
                       pltpu.SemaphoreType.DMA((2,))],
        name="sc_dispatch")
    def scatter(hp_hbm, idx_hbm, xin_hbm, idx_v, rows_v, lsem, ssem, isem):
        base_w = (lax.axis_index("s") * nc + lax.axis_index("c")) * per_w

        def index(b, kk, base):
            return pltpu.make_async_copy(idx_hbm.at[pl.ds(kk * t + base, chunk)], idx_v.at[b, kk],
                                         isem.at[b])

        def load(b, base):
            return pltpu.make_async_copy(hp_hbm.at[pl.ds(base, chunk)], rows_v.at[b], lsem.at[b])

        def put(b, kk):
            return pltpu.make_async_copy(rows_v.at[b], xin_hbm.at[idx_v.at[b, kk]], ssem.at[b])

        def body(j, carry):
            for b in range(2):
                base = base_w + (2 * j + b) * chunk

                @pl.when(j > 0)
                def _():
                    for kk in range(TOP_K):
                        put(b, kk).wait()

                for kk in range(TOP_K):
                    index(b, kk, base).start()
                load(b, base).start()
            for b in range(2):
                for kk in range(TOP_K):
                    index(b, kk, base_w).wait()
                load(b, base_w).wait()
                for kk in range(TOP_K):
                    put(b, kk).start()
            return carry

        lax.fori_loop(0, pairs, body, 0)
        for b in range(2):
            for kk in range(TOP_K):
                put(b, kk).wait()

    return scatter(hp, dest_k)


def _ffn_body(be_ref, nv_ref, br_ref, ep_ref, es_ref, x_ref, wgu_hbm, bgu_ref, wd_hbm, bd_ref, y_ref,
              wgub_ref, wdb_ref, wguf_ref, wdf_ref, gsem, dsem):
    i = pl.program_id(0)

    @pl.when(i >= nv_ref[0])
    def _():
        y_ref[...] = jnp.zeros(y_ref.shape, U32)

    def fetch(pos):
        e, slot = es_ref[pos], pos % 2
        return (pltpu.make_async_copy(wgu_hbm.at[e], wguf_ref.at[slot], gsem.at[slot]),
                pltpu.make_async_copy(wd_hbm.at[e], wdf_ref.at[slot], dsem.at[slot]))

    @pl.when((i < nv_ref[0]) & ((i == 0) | (be_ref[i] != be_ref[jnp.maximum(i - 1, 0)])))
    def _():
        pos = ep_ref[i]

        @pl.when(i == 0)
        def _():
            for cp in fetch(pos):
                cp.start()

        for cp in fetch(pos):
            cp.wait()

        @pl.when(pos + 1 < es_ref[N_EXPERTS])
        def _():
            for cp in fetch(pos + 1):
                cp.start()

        slot = pos % 2
        wgub_ref[...] = wguf_ref[slot].astype(BF16)
        wdb_ref[...] = wdf_ref[slot].astype(BF16)

    nsub = FFN_SUBBLOCKS
    rsub = x_ref.shape[0] // nsub

    def compute(active):
        half = D_MODEL // 2
        row = lax.broadcasted_iota(I32, (rsub, 1), 0)
        gus = []
        for sb in range(active):
            rs = slice(sb * rsub, (sb + 1) * rsub)
            xp = jnp.where(row + sb * rsub < br_ref[i], x_ref[rs, :], jnp.uint32(0))
            lo = pltpu.bitcast(xp << 16, F32).astype(BF16)
            hi = pltpu.bitcast(xp & jnp.uint32(0xFFFF0000), F32).astype(BF16)
            gus.append(_dot(lo, wgub_ref[:half, :]) + _dot(hi, wgub_ref[half:, :]) + bgu_ref[0])
        ys = []
        for gu in gus:
            gate = jnp.minimum(gu[:, :D_FF], SWIGLU_LIMIT)
            up = jnp.clip(gu[:, D_FF:], -SWIGLU_LIMIT, SWIGLU_LIMIT)
            act = (up + 1.0) * (gate * _sigmoid(SWIGLU_ALPHA * gate))
            ys.append(_dot(act.astype(BF16), wdb_ref[...]) + bd_ref[0])
        for sb, y in enumerate(ys):
            ylo = pltpu.bitcast(y[:, :half].astype(BF16).astype(F32), U32) >> 16
            yhi = pltpu.bitcast(y[:, half:].astype(BF16).astype(F32), U32) & jnp.uint32(0xFFFF0000)
            y_ref[sb * rsub:(sb + 1) * rsub, :] = ylo | yhi
        if active < nsub:
            y_ref[active * rsub:, :] = jnp.zeros(((nsub - active) * rsub, half), U32)

    active = (br_ref[i] + rsub - 1) // rsub
    for n in range(1, nsub + 1):
        pl.when((i < nv_ref[0]) & (active == n))(functools.partial(compute, n))


def _ffn(blk_e, nvalid, blk_rows, blk_pos, used, xin, wgu, bgu, wd, bd):
    p = xin.shape[0]
    bm = FFN_BLOCK
    rows = lambda i, be, nv, br, ep, es: (jnp.minimum(i, nv[0] - 1), 0)
    wsel = lambda i, be, nv, br, ep, es: (be[i], 0, 0)
    return pl.pallas_call(
        _ffn_body,
        grid_spec=pltpu.PrefetchScalarGridSpec(
            num_scalar_prefetch=5,
            grid=(p // bm,),
            in_specs=[pl.BlockSpec((bm, D_MODEL // 2), rows),
                      pl.BlockSpec(memory_space=pl.ANY),
                      pl.BlockSpec((1, 1, 2 * D_FF), wsel),
                      pl.BlockSpec(memory_space=pl.ANY),
                      pl.BlockSpec((1, 1, D_MODEL), wsel)],
            out_specs=pl.BlockSpec((bm, D_MODEL // 2), lambda i, be, nv, br, ep, es: (i, 0)),
            scratch_shapes=[pltpu.VMEM((D_MODEL, 2 * D_FF), BF16), pltpu.VMEM((D_FF, D_MODEL), BF16),
                            pltpu.VMEM((2, D_MODEL, 2 * D_FF), F32), pltpu.VMEM((2, D_FF, D_MODEL), F32),
                            pltpu.SemaphoreType.DMA((2,)), pltpu.SemaphoreType.DMA((2,))],
        ),
        out_shape=jax.ShapeDtypeStruct((p, D_MODEL // 2), U32),
        compiler_params=pltpu.CompilerParams(dimension_semantics=("arbitrary",),
                                             vmem_limit_bytes=VMEM_LIMIT),
        name="ffn",
    )(blk_e, nvalid, blk_rows, blk_pos, used, xin, wgu, bgu, wd, bd)


def _sc_gather(table, idx):
    r, d = idx.shape[0], table.shape[1]
    mesh = plsc.VectorSubcoreMesh(core_axis_name="c", subcore_axis_name="s")
    nc, workers = mesh.num_cores, mesh.num_cores * mesh.num_subcores
    chunk = SC_GATHER_ROWS
    per_w = r // workers
    pairs = per_w // (2 * chunk)
    assert per_w * workers == r and pairs * 2 * chunk == per_w

    @functools.partial(
        pl.kernel, mesh=mesh, out_type=jax.ShapeDtypeStruct((r, d), table.dtype),
        scratch_types=[pltpu.VMEM((per_w,), I32), pltpu.VMEM((2, chunk, d), table.dtype),
                       pltpu.SemaphoreType.DMA((2,)), pltpu.SemaphoreType.DMA((2,))],
        name="sc_gather")
    def gather(table_hbm, idx_hbm, out_hbm, idx_v, rows_v, gsem, wsem):
        base_w = (lax.axis_index("s") * nc + lax.axis_index("c")) * per_w
        pltpu.sync_copy(idx_hbm.at[pl.ds(base_w, per_w)], idx_v)

        def fetch(b, c):
            return pltpu.make_async_copy(table_hbm.at[idx_v.at[pl.ds(c * chunk, chunk)]],
                                         rows_v.at[b], gsem.at[b])

        def flush(b, base):
            return pltpu.make_async_copy(rows_v.at[b], out_hbm.at[pl.ds(base, chunk)], wsem.at[b])

        def body(j, carry):
            for b in range(2):
                @pl.when(j > 0)
                def _():
                    flush(b, base_w).wait()

                fetch(b, 2 * j + b).start()
            for b in range(2):
                fetch(b, 2 * j + b).wait()
                flush(b, base_w + (2 * j + b) * chunk).start()
            return carry

        lax.fori_loop(0, pairs, body, 0)
        for b in range(2):
            flush(b, base_w).wait()

    return gather(table, idx)


def _combine_body(h_ref, gate_ref, fg_ref, *refs):
    yg_refs, o_ref = refs[:TOP_K], refs[-1]
    half = D_MODEL // 2
    gates = gate_ref[...]
    h = h_ref[...]
    lo, hi = h[:, :half], h[:, half:]
    for kk in range(TOP_K):
        yp = yg_refs[kk][...]
        g = gates[:, kk:kk + 1]
        lo = lo + g * pltpu.bitcast(yp << 16, F32)
        hi = hi + g * pltpu.bitcast(yp & jnp.uint32(0xFFFF0000), F32)
    out = jnp.concatenate([lo, hi], axis=1)
    o_ref[...] = _rms(out, fg_ref[...])


def _combine(h, gates, fg, ygath, out_prev, *, part, parts):
    t = h.shape[0]
    tm = min(COMBINE_TILE, t // parts)
    nt = t // parts // tm
    row = lambda i: (part * nt + i, 0)
    slot = lambda kk: pl.BlockSpec((tm, D_MODEL // 2), lambda i: (kk * nt + i, 0))
    in_specs = [pl.BlockSpec((tm, D_MODEL), row), pl.BlockSpec((tm, LANES), row),
                pl.BlockSpec((1, D_MODEL), lambda i: (0, 0))] + [slot(kk) for kk in range(TOP_K)]
    args = [h, gates, fg] + [ygath] * TOP_K
    aliases = {}
    if out_prev is not None:
        aliases = {len(args): 0}
        in_specs.append(pl.BlockSpec(memory_space=pl.ANY))
        args.append(out_prev)
    return pl.pallas_call(
        _combine_body,
        grid=(nt,),
        in_specs=in_specs,
        out_specs=pl.BlockSpec((tm, D_MODEL), row),
        out_shape=jax.ShapeDtypeStruct((t, D_MODEL), F32),
        input_output_aliases=aliases,
        compiler_params=pltpu.CompilerParams(dimension_semantics=("arbitrary",),
                                             vmem_limit_bytes=VMEM_LIMIT),
        name="combine",
    )(*args)


def _block_tril(n, chunk, strict):
    r = jnp.arange(n)[:, None]
    c = jnp.arange(n)[None, :]
    keep = ((r // chunk) == (c // chunk)) & ((r > c) if strict else (r >= c))
    return keep.astype(BF16)


def _pad_lanes(a, fill=0.0):
    a = a.reshape(1, -1).astype(F32)
    return jnp.pad(a, ((0, 0), (0, LANES - a.shape[1])), constant_values=fill)


def _layer(h, norm_mix_g, w_in, gm_norm_g, gm_ws, gm_bs, dn_conv_w, dn_a_log, dn_dt_bias,
           dn_norm_g, w_out, norm_ffn_g, router_w, router_b, exp_w_gu, exp_b_gu, exp_w_down,
           exp_b_down, out_g):
    batch, seq, d = h.shape
    t = batch * seq
    x2 = h.reshape(t, d)

    c0, c1, c2 = GM_WIDTH, 2 * GM_WIDTH, 2 * GM_WIDTH + 3 * DN_KEY
    c3 = c2 + DN_VAL
    wb = w_in.astype(BF16)
    wu, wv, wqkv, wz = wb[:, :c0], wb[:, c0:c1], wb[:, c1:c2], wb[:, c2:c3]
    wab = jnp.pad(wb[:, c3:], ((0, 0), (0, LANES - 2 * DN_HEADS)))
    gmg = gm_norm_g.reshape(1, GM_WIDTH).astype(F32)
    bsb = jnp.repeat(gm_bs.T, HEAD_DIM, axis=1).astype(F32)
    alog = _pad_lanes(dn_a_log)
    dtb = _pad_lanes(dn_dt_bias)
    tri_incl = _block_tril(ROW_TILE, DN_CHUNK, strict=False)

    oa, q, k, v, z, gb, gt = _inproj(
        x2, norm_mix_g.reshape(1, d), wu, wv, wqkv, wz, wab, gmg, gm_ws.astype(F32), bsb,
        dn_conv_w.astype(F32), alog, dtb, tri_incl, seq=seq)

    grow = gt[:DN_HEADS].reshape(DN_HEADS, t // DN_CHUNK, DN_CHUNK).transpose(1, 0, 2)
    grow = grow.reshape(t // DN_CHUNK, DN_HEADS * DN_CHUNK)
    ob = _gdn(q, k, v, z, gb, grow, dn_norm_g.reshape(1, HEAD_DIM).astype(F32), batch=batch, seq=seq)

    wo = w_out.astype(BF16)
    rw = jnp.pad(router_w.astype(BF16), ((0, 0), (0, LANES - N_EXPERTS)))
    rb = _pad_lanes(router_b, fill=NEG_BIG)
    tri_strict = _block_tril(OUTPROJ_TILE, OUTPROJ_TILE, strict=True)
    hres, hp, meta, gates, cnt = _outproj(oa, ob, x2, wo[:GM_WIDTH], wo[GM_WIDTH:],
                                          norm_ffn_g.reshape(1, d), rw, rb, tri_strict)

    bm = FFN_BLOCK
    counts = cnt[0, :N_EXPERTS]
    padded = (counts + bm - 1) // bm * bm
    pad_end = jnp.cumsum(padded)
    pad_start = (pad_end - padded).astype(I32)
    n_blocks = (t * TOP_K + N_EXPERTS * bm) // bm
    nvalid = (pad_end[-1] // bm).astype(I32).reshape(1)
    blk = jnp.minimum(jnp.arange(n_blocks, dtype=I32), nvalid[0] - 1)
    blk_e = jnp.minimum(jnp.sum(pad_end[None, :] <= (blk * bm)[:, None], axis=1), N_EXPERTS - 1).astype(I32)
    eid = meta[:TOP_K]
    start_of = jnp.sum(jnp.where(eid[..., None] == jnp.arange(N_EXPERTS, dtype=I32), pad_start, 0), axis=-1)
    dest = start_of + meta[TOP_K:2 * TOP_K]

    xin = _sc_dispatch(hp, dest.reshape(-1), n_blocks * bm)
    eids = jnp.arange(N_EXPERTS, dtype=I32)
    of_blk = blk_e[:, None] == eids
    blk_rows = jnp.clip(jnp.sum(jnp.where(of_blk, counts + pad_start, 0), axis=1) - blk * bm, 0, bm)
    has_rows = counts > 0
    used = jnp.minimum(jnp.sort(jnp.where(has_rows, eids, N_EXPERTS + eids)), N_EXPERTS - 1)
    used = jnp.concatenate([used, jnp.sum(has_rows).reshape(1)]).astype(I32)
    blk_pos = jnp.sum(jnp.where(of_blk, jnp.cumsum(has_rows) - 1, 0), axis=1).astype(I32)
    blk_rows = blk_rows.astype(I32)
    y = _ffn(blk_e, nvalid, blk_rows, blk_pos, used, xin, exp_w_gu, exp_b_gu[:, None, :].astype(F32),
             exp_w_down, exp_b_down[:, None, :].astype(F32))
    out = None
    tp = t // COMBINE_PARTS
    for part in range(COMBINE_PARTS):
        ygath = _sc_gather(y, dest[:, part * tp:(part + 1) * tp].reshape(-1))
        out = _combine(hres, gates, out_g.reshape(1, d).astype(F32), ygath, out,
                       part=part, parts=COMBINE_PARTS)
    return out.reshape(batch, seq, d)


def kernel(x, norm_mix_g, w_in, gm_norm_g, gm_ws, gm_bs, dn_conv_w, dn_a_log, dn_dt_bias, dn_norm_g, w_out, norm_ffn_g, router_w, router_b, exp_w_gu, exp_b_gu, exp_w_down, exp_b_down, final_norm_g):
    depth = norm_mix_g.shape[0]
    assert depth == 1, "single-layer problem"
    return _layer(x, norm_mix_g[0], w_in[0], gm_norm_g[0], gm_ws[0], gm_bs[0], dn_conv_w[0],
                  dn_a_log[0], dn_dt_bias[0], dn_norm_g[0], w_out[0], norm_ffn_g[0], router_w[0],
                  router_b[0], exp_w_gu[0], exp_b_gu[0], exp_w_down[0], exp_b_down[0], final_norm_g)
```

```python
import functools

import jax
import jax.numpy as jnp
from jax import lax
from jax.experimental import pallas as pl
from jax.experimental.pallas import tpu as pltpu
from jax.experimental.pallas import tpu_sc as plsc

F32 = jnp.float32
BF16 = jnp.bfloat16
I32 = jnp.int32
U32 = jnp.uint32

D_MODEL = 1024
HEAD_DIM = 128
GM_HEADS = 4
GM_WIDTH = GM_HEADS * HEAD_DIM
GM_CHUNK = 128
DN_HEADS = 4
DN_DK = 128
DN_KEY = DN_HEADS * DN_DK
DN_VAL = DN_HEADS * HEAD_DIM
DN_CHUNK = 64
CONV_W = 4
N_EXPERTS = 32
TOP_K = 4
D_FF = D_MODEL
SWIGLU_LIMIT = 7.0
SWIGLU_ALPHA = 1.702
EPS = 1e-6

LANES = 128
INPROJ_COLS = 512
INPROJ_SUBBLOCKS = 2
ROW_TILE = 512
OUTPROJ_TILE = 1024
GDN_TILE = 1024
GDN_GROUP = 4
FFN_BLOCK = 1024
FFN_SUBBLOCKS = 4
SC_GATHER_ROWS = 64
COMBINE_PARTS = 4
COMBINE_TILE = 512
VMEM_LIMIT = 56 * 1024 * 1024
NEG_BIG = -1e30


def _dot(a, b):
    return jnp.dot(a, b, preferred_element_type=F32)


def _dot_nt(a, b):
    return lax.dot_general(a, b, (((1,), (1,)), ((), ())), preferred_element_type=F32)


def _dot_tn(a, b):
    return lax.dot_general(a, b, (((0,), (0,)), ((), ())), preferred_element_type=F32)


def _rms(x, g):
    return x * lax.rsqrt(jnp.mean(x * x, axis=-1, keepdims=True) + EPS) * g


def _gelu(x):
    return 0.5 * x * (1.0 + lax.erf(x * (2.0 ** -0.5)))


def _sigmoid(x):
    return 1.0 / (1.0 + jnp.exp(-x))


def _softplus(x):
    return jnp.maximum(x, 0.0) + jnp.log1p(jnp.exp(-jnp.abs(x)))


def _inproj_body(x_ref, ng_ref, wu_ref, wv_ref, wqkv_ref, wz_ref, wab_ref, gmg_ref, ws_ref,
                 bs_ref, cw_ref, alog_ref, dtb_ref, tri_ref,
                 oa_ref, q_ref, k_ref, v_ref, z_ref, gb_ref, gt_ref, cbuf_ref, ybuf_ref,
                 *, tm, tiles_per_seq):
    i = pl.program_id(0)
    ybuf_ref[...] = _rms(x_ref[...], ng_ref[...]).astype(BF16)
    nb = INPROJ_COLS
    heads_per_block = nb // HEAD_DIM

    ri = lax.broadcasted_iota(I32, (GM_CHUNK, GM_CHUNK), 0)
    ci = lax.broadcasted_iota(I32, (GM_CHUNK, GM_CHUNK), 1)
    causal = ri >= ci

    def gmlp_matmul(blk, r0, r1):
        cs = slice(blk * nb, (blk + 1) * nb)
        return _dot(ybuf_ref[r0:r1, :], wu_ref[:, cs]), _dot(ybuf_ref[r0:r1, :], wv_ref[:, cs])

    def gmlp_epilogue(blk, r0, r1, uv):
        u, vg = _gelu(uv[0]), _gelu(uv[1])
        for hh in range(heads_per_block):
            h = blk * heads_per_block + hh
            sl = slice(h * HEAD_DIM, (h + 1) * HEAD_DIM)
            ls = slice(hh * HEAD_DIM, (hh + 1) * HEAD_DIM)
            vh = _rms(vg[:, ls], gmg_ref[:, sl]).astype(BF16)
            wm = jnp.where(causal, ws_ref[h], 0.0).astype(BF16)
            for c in range((r1 - r0) // GM_CHUNK):
                rs = slice(c * GM_CHUNK, (c + 1) * GM_CHUNK)
                gate = _dot(wm, vh[rs]) + bs_ref[:, sl]
                oa_ref[r0 + c * GM_CHUNK:r0 + (c + 1) * GM_CHUNK, sl] = (u[rs, ls] * gate).astype(BF16)

    def qkv_matmul(blk, r0, r1):
        return _dot(ybuf_ref[r0:r1, :], wqkv_ref[:, blk * nb:(blk + 1) * nb])

    def qkv_epilogue(blk, r0, r1, pq):
        cs = slice(blk * nb, (blk + 1) * nb)
        cbuf_ref[8 + r0:8 + r1, cs] = pq
        acc = cw_ref[CONV_W - 1:CONV_W, cs] * pq
        for j in range(CONV_W - 1):
            off = 8 - (CONV_W - 1) + j
            acc = acc + cw_ref[j:j + 1, cs] * cbuf_ref[off + r0:off + r1, cs]
        if r1 == tm:
            cbuf_ref[0:8, cs] = pq[r1 - r0 - 8:, :]
        s = acc * _sigmoid(acc)
        for hh in range(heads_per_block):
            c0 = blk * nb + hh * HEAD_DIM
            sh = s[:, hh * HEAD_DIM:(hh + 1) * HEAD_DIM]
            if c0 < DN_KEY:
                q_ref[r0:r1, c0:c0 + HEAD_DIM] = (
                    sh * lax.rsqrt(jnp.sum(sh * sh, axis=-1, keepdims=True) + EPS)
                    * (DN_DK ** -0.5)).astype(BF16)
            elif c0 < 2 * DN_KEY:
                k_ref[r0:r1, c0 - DN_KEY:c0 - DN_KEY + HEAD_DIM] = (
                    sh * lax.rsqrt(jnp.sum(sh * sh, axis=-1, keepdims=True) + EPS)).astype(BF16)
            else:
                v_ref[r0:r1, c0 - 2 * DN_KEY:c0 - 2 * DN_KEY + HEAD_DIM] = sh.astype(BF16)

    def z_matmul(blk, r0, r1):
        return _dot(ybuf_ref[r0:r1, :], wz_ref[:, blk * nb:(blk + 1) * nb])

    def z_epilogue(blk, r0, r1, zz):
        z_ref[r0:r1, blk * nb:(blk + 1) * nb] = zz.astype(BF16)

    @pl.when(i % tiles_per_seq == 0)
    def _():
        cbuf_ref[0:8, :] = jnp.zeros((8, 3 * DN_KEY), F32)

    stages = ([(gmlp_matmul, gmlp_epilogue, blk) for blk in range(GM_WIDTH // nb)]
              + [(qkv_matmul, qkv_epilogue, blk) for blk in range(3 * DN_KEY // nb)]
              + [(z_matmul, z_epilogue, blk) for blk in range(DN_VAL // nb)])
    rsub = tm // INPROJ_SUBBLOCKS
    spans = [(sb * rsub, (sb + 1) * rsub) for sb in range(INPROJ_SUBBLOCKS)]
    for matmul, epilogue, blk in stages:
        results = [matmul(blk, r0, r1) for r0, r1 in spans]
        for (r0, r1), res in zip(spans, results):
            epilogue(blk, r0, r1, res)

    ab = _dot(ybuf_ref[...], wab_ref[...])
    g = -jnp.exp(alog_ref[...]) * _softplus(ab + dtb_ref[...])
    beta = _sigmoid(ab)
    g_hi = g.astype(BF16)
    r1 = g - g_hi.astype(F32)
    g_mid = r1.astype(BF16)
    g_lo = (r1 - g_mid.astype(F32)).astype(BF16)
    tri = tri_ref[...]
    gc = _dot(tri, g_hi) + _dot(tri, g_mid) + _dot(tri, g_lo)
    lane = lax.broadcasted_iota(I32, (tm, LANES), 1)
    gbv = jnp.where(lane < DN_HEADS, gc, beta)
    gb_ref[...] = gbv
    gt_ref[...] = gbv.T[0:8, :]


def _inproj(x2, ng, wu, wv, wqkv, wz, wab, gmg, ws, bsb, cw, alog, dtb, tri, *, seq):
    t = x2.shape[0]
    tm = ROW_TILE
    row = lambda i: (i, 0)
    full = lambda a: pl.BlockSpec(a.shape, (lambda i: (0,) * a.ndim))
    out_shapes = (
        jax.ShapeDtypeStruct((t, GM_WIDTH), BF16),
        jax.ShapeDtypeStruct((t, DN_KEY), BF16),
        jax.ShapeDtypeStruct((t, DN_KEY), BF16),
        jax.ShapeDtypeStruct((t, DN_VAL), BF16),
        jax.ShapeDtypeStruct((t, DN_VAL), BF16),
        jax.ShapeDtypeStruct((t, LANES), F32),
        jax.ShapeDtypeStruct((8, t), F32),
    )
    return pl.pallas_call(
        functools.partial(_inproj_body, tm=tm, tiles_per_seq=seq // tm),
        grid=(t // tm,),
        in_specs=[pl.BlockSpec((tm, D_MODEL), row), full(ng), full(wu), full(wv), full(wqkv),
                  full(wz), full(wab), full(gmg), full(ws), full(bsb), full(cw), full(alog),
                  full(dtb), full(tri)],
        out_specs=(pl.BlockSpec((tm, GM_WIDTH), row), pl.BlockSpec((tm, DN_KEY), row),
                   pl.BlockSpec((tm, DN_KEY), row), pl.BlockSpec((tm, DN_VAL), row),
                   pl.BlockSpec((tm, DN_VAL), row), pl.BlockSpec((tm, LANES), row),
                   pl.BlockSpec((8, tm), lambda i: (0, i))),
        out_shape=out_shapes,
        scratch_shapes=[pltpu.VMEM((tm + 8, 3 * DN_KEY), F32), pltpu.VMEM((tm, D_MODEL), BF16)],
        compiler_params=pltpu.CompilerParams(dimension_semantics=("arbitrary",),
                                             vmem_limit_bytes=VMEM_LIMIT),
        name="inproj",
    )(x2, ng, wu, wv, wqkv, wz, wab, gmg, ws, bsb, cw, alog, dtb, tri)


def _gdn_body(q_ref, k_ref, v_ref, z_ref, gb_ref, gr_ref, ng_ref, ob_ref, s_ref, *, nchunk,
              group_size):
    j = pl.program_id(1)

    @pl.when(j == 0)
    def _():
        s_ref[...] = jnp.zeros(s_ref.shape, F32)

    c = DN_CHUNK
    n = DN_HEADS * c
    ri = lax.broadcasted_iota(I32, (n, n), 0)
    ci = lax.broadcasted_iota(I32, (n, n), 1)
    same = (ri // c) == (ci // c)
    incl = same & ((ri % c) >= (ci % c))
    strict = same & ((ri % c) > (ci % c))
    ng = ng_ref[...]

    def stack(a):
        return jnp.concatenate([a[:, h * HEAD_DIM:(h + 1) * HEAD_DIM] for h in range(DN_HEADS)], axis=0)

    def prepare(ic):
        r0 = ic * c
        kst = stack(k_ref[pl.ds(r0, c), :])
        qst = stack(q_ref[pl.ds(r0, c), :])
        vst = stack(v_ref[pl.ds(r0, c), :])
        gbc = gb_ref[pl.ds(r0, c), :]
        grow = gr_ref[pl.ds(ic, 1), :]
        gcol = jnp.concatenate([gbc[:, h:h + 1] for h in range(DN_HEADS)], axis=0)
        bcol = jnp.concatenate([gbc[:, DN_HEADS + h:DN_HEADS + h + 1] for h in range(DN_HEADS)], axis=0)
        glast = jnp.concatenate(
            [jnp.broadcast_to(gbc[c - 1:c, h:h + 1], (c, 1)) for h in range(DN_HEADS)], axis=0)
        decay = jnp.where(incl, jnp.exp(jnp.where(incl, gcol - grow, 0.0)), 0.0)
        kf = kst.astype(F32)
        kb = kf * bcol
        lmat = jnp.where(strict, _dot_nt(kb.astype(BF16), kst) * decay, 0.0)
        eg = jnp.exp(gcol)
        rhs = jnp.concatenate([vst.astype(F32) * bcol, kb * eg], axis=1)
        attn = jnp.where(incl, _dot_nt(qst, kst) * decay, 0.0).astype(BF16)
        qd = (qst.astype(F32) * eg).astype(BF16)
        ke = (kf * jnp.exp(glast - gcol)).astype(BF16)
        return dict(r0=r0, gbc=gbc, lmat=lmat, rhs=rhs, attn=attn, qd=qd, ke=ke)

    def advance_state(p, sol):
        u = sol[:, :HEAD_DIM]
        wb = sol[:, HEAD_DIM:].astype(BF16)
        vn, qs = [], []
        for h in range(DN_HEADS):
            rs = slice(h * c, (h + 1) * c)
            sb = s_ref[h].astype(BF16)
            vn.append(u[rs] - _dot(wb[rs], sb))
            qs.append(_dot(p["qd"][rs], sb))
        vnb = jnp.concatenate(vn, axis=0).astype(BF16)
        o = jnp.concatenate(qs, axis=0) + _dot(p["attn"], vnb)
        zc = z_ref[pl.ds(p["r0"], c), :]
        for h in range(DN_HEADS):
            rs = slice(h * c, (h + 1) * c)
            sl = slice(h * HEAD_DIM, (h + 1) * HEAD_DIM)
            s_ref[h] = (s_ref[h] * jnp.exp(p["gbc"][c - 1:c, h:h + 1])
                        + _dot_tn(p["ke"][rs], vnb[rs]))
            zf = zc[:, sl].astype(F32)
            ob_ref[pl.ds(p["r0"], c), sl] = (_rms(o[rs], ng) * (zf * _sigmoid(zf))).astype(BF16)

    def solve_group(ig):
        ps = [prepare(ig * group_size + b) for b in range(group_size)]
        sol = [p["rhs"] for p in ps]
        pw = [-p["lmat"] for p in ps]
        for step in range(6):
            pb = [x.astype(BF16) for x in pw]
            sol = [s + _dot(xb, s.astype(BF16)) for s, xb in zip(sol, pb)]
            if step < 5:
                pw = [_dot(xb, xb) for xb in pb]
        return ps, sol

    ngroups = nchunk // group_size
    solved = solve_group(0)
    for ig in range(ngroups):
        ahead = solve_group(ig + 1) if ig + 1 < ngroups else None
        for p, s in zip(*solved):
            advance_state(p, s)
        solved = ahead


def _gdn(q, k, v, z, gb, grow, ng, *, batch, seq):
    tm = GDN_TILE
    nchunk = tm // DN_CHUNK
    steps = seq // tm
    rows = lambda b, j: (b * steps + j, 0)
    return pl.pallas_call(
        functools.partial(_gdn_body, nchunk=nchunk, group_size=GDN_GROUP),
        grid=(batch, steps),
        in_specs=[pl.BlockSpec((tm, DN_KEY), rows), pl.BlockSpec((tm, DN_KEY), rows),
                  pl.BlockSpec((tm, DN_VAL), rows), pl.BlockSpec((tm, DN_VAL), rows),
                  pl.BlockSpec((tm, LANES), rows),
                  pl.BlockSpec((nchunk, DN_HEADS * DN_CHUNK), rows),
                  pl.BlockSpec((1, HEAD_DIM), lambda b, j: (0, 0))],
        out_specs=pl.BlockSpec((tm, DN_VAL), rows),
        out_shape=jax.ShapeDtypeStruct((batch * seq, DN_VAL), BF16),
        scratch_shapes=[pltpu.VMEM((DN_HEADS, DN_DK, HEAD_DIM), F32)],
        compiler_params=pltpu.CompilerParams(dimension_semantics=("arbitrary", "arbitrary"),
                                             vmem_limit_bytes=VMEM_LIMIT),
        name="gdn",
    )(q, k, v, z, gb, grow, ng)


def _outproj_body(oa_ref, ob_ref, x_ref, woa_ref, wob_ref, ng_ref, rw_ref, rb_ref, tri_ref,
                  h_ref, hp_ref, meta_ref, gate_ref, cnt_ref, run_ref, *, tm):
    i = pl.program_id(0)

    @pl.when(i == 0)
    def _():
        run_ref[...] = jnp.zeros(run_ref.shape, F32)

    h = x_ref[...] + _dot(oa_ref[...], woa_ref[...]) + _dot(ob_ref[...], wob_ref[...])
    h_ref[...] = h
    hb = _rms(h, ng_ref[...]).astype(BF16)
    half = D_MODEL // 2
    lo = pltpu.bitcast(hb[:, :half].astype(F32), U32) >> 16
    hi = pltpu.bitcast(hb[:, half:].astype(F32), U32) & jnp.uint32(0xFFFF0000)
    hp_ref[...] = lo | hi

    logits = _dot(hb, rw_ref[...]) + rb_ref[...]
    lane = lax.broadcasted_iota(I32, (tm, LANES), 1)
    lanef = lane.astype(F32)
    work = logits
    onehot = jnp.zeros((tm, LANES), F32)
    vals, sels = [], []
    for _ in range(TOP_K):
        m = jnp.max(work, axis=-1, keepdims=True)
        idx = jnp.min(jnp.where(work == m, lanef, float(LANES)), axis=-1, keepdims=True)
        sel = lanef == idx
        work = jnp.where(sel, -3e38, work)
        onehot = onehot + jnp.where(sel, 1.0, 0.0)
        vals.append(m)
        sels.append((sel, idx))
    ex = [jnp.exp(v - vals[0]) for v in vals]
    den = ex[0] + ex[1] + ex[2] + ex[3]
    pref = _dot(tri_ref[...], onehot.astype(BF16)) + run_ref[0:1, :]
    meta = jnp.zeros((tm, LANES), F32)
    gates = jnp.zeros((tm, LANES), F32)
    for kk in range(TOP_K):
        sel, idx = sels[kk]
        rank = jnp.sum(jnp.where(sel, pref, 0.0), axis=-1, keepdims=True)
        meta = meta + jnp.where(lane == kk, idx, 0.0) + jnp.where(lane == TOP_K + kk, rank, 0.0)
        gates = gates + jnp.where(lane == kk, ex[kk] / den, 0.0)
    meta_ref[...] = meta.T[0:8, :].astype(I32)
    gate_ref[...] = gates
    run = run_ref[...] + jnp.sum(onehot, axis=0, keepdims=True)
    run_ref[...] = run
    cnt_ref[...] = run.astype(I32)


def _outproj(oa, ob, x2, woa, wob, ng, rw, rb, tri):
    t = x2.shape[0]
    tm = OUTPROJ_TILE
    row = lambda i: (i, 0)
    full = lambda a: pl.BlockSpec(a.shape, (lambda i: (0,) * a.ndim))
    out_shapes = (
        jax.ShapeDtypeStruct((t, D_MODEL), F32),
        jax.ShapeDtypeStruct((t, D_MODEL // 2), U32),
        jax.ShapeDtypeStruct((8, t), I32),
        jax.ShapeDtypeStruct((t, LANES), F32),
        jax.ShapeDtypeStruct((8, LANES), I32),
    )
    return pl.pallas_call(
        functools.partial(_outproj_body, tm=tm),
        grid=(t // tm,),
        in_specs=[pl.BlockSpec((tm, GM_WIDTH), row), pl.BlockSpec((tm, DN_VAL), row),
                  pl.BlockSpec((tm, D_MODEL), row), full(woa), full(wob), full(ng), full(rw),
                  full(rb), full(tri)],
        out_specs=(pl.BlockSpec((tm, D_MODEL), row), pl.BlockSpec((tm, D_MODEL // 2), row),
                   pl.BlockSpec((8, tm), lambda i: (0, i)), pl.BlockSpec((tm, LANES), row),
                   pl.BlockSpec((8, LANES), lambda i: (0, 0))),
        out_shape=out_shapes,
        scratch_shapes=[pltpu.VMEM((8, LANES), F32)],
        compiler_params=pltpu.CompilerParams(dimension_semantics=("arbitrary",),
                                             vmem_limit_bytes=VMEM_LIMIT),
        name="outproj",
    )(oa, ob, x2, woa, wob, ng, rw, rb, tri)


def _sc_dispatch(hp, dest_k, n_rows):
    t, d = hp.shape
    mesh = plsc.VectorSubcoreMesh(core_axis_name="c", subcore_axis_name="s")
    nc, workers = mesh.num_cores, mesh.num_cores * mesh.num_subcores
    chunk = SC_GATHER_ROWS
    per_w = t // workers
    pairs = per_w // (2 * chunk)
    assert per_w * workers == t and pairs * 2 * chunk == per_w

    @functools.partial(
        pl.kernel, mesh=mesh, out_type=jax.ShapeDtypeStruct((n_rows, d), hp.dtype),
        scratch_types=[pltpu.VMEM((2, TOP_K, chunk), I32), pltpu.VMEM((2, chunk, d), hp.dtype),
                       pltpu.SemaphoreType.DMA((2,)), pltpu.SemaphoreType.DMA((2,)),
                       pltpu.SemaphoreType.DMA((2,))],
        name="sc_dispatch")
    def scatter(hp_hbm, idx_hbm, xin_hbm, idx_v, rows_v, lsem, ssem, isem):
        base_w = (lax.axis_index("s") * nc + lax.axis_index("c")) * per_w

        def index(b, kk, base):
            return pltpu.make_async_copy(idx_hbm.at[pl.ds(kk * t + base, chunk)], idx_v.at[b, kk],
                                         isem.at[b])

        def load(b, base):
            return pltpu.make_async_copy(hp_hbm.at[pl.ds(base, chunk)], rows_v.at[b], lsem.at[b])

        def put(b, kk):
            return pltpu.make_async_copy(rows_v.at[b], xin_hbm.at[idx_v.at[b, kk]], ssem.at[b])

        def body(j, carry):
            for b in range(2):
                base = base_w + (2 * j + b) * chunk

                @pl.when(j > 0)
                def _():
                    for kk in range(TOP_K):
                        put(b, kk).wait()

                for kk in range(TOP_K):
                    index(b, kk, base).start()
                load(b, base).start()
            for b in range(2):
                for kk in range(TOP_K):
                    index(b, kk, base_w).wait()
                load(b, base_w).wait()
                for kk in range(TOP_K):
                    put(b, kk).start()
            return carry

        lax.fori_loop(0, pairs, body, 0)
        for b in range(2):
            for kk in range(TOP_K):
                put(b, kk).wait()

    return scatter(hp, dest_k)


def _ffn_body(be_ref, nv_ref, br_ref, ep_ref, es_ref, x_ref, wgu_hbm, bgu_ref, wd_hbm, bd_ref, y_ref,
              wgub_ref, wdb_ref, wguf_ref, wdf_ref, gsem, dsem):
    i = pl.program_id(0)

    @pl.when(i >= nv_ref[0])
    def _():
        y_ref[...] = jnp.zeros(y_ref.shape, U32)

    def fetch(pos):
        e, slot = es_ref[pos], pos % 2
        return (pltpu.make_async_copy(wgu_hbm.at[e], wguf_ref.at[slot], gsem.at[slot]),
                pltpu.make_async_copy(wd_hbm.at[e], wdf_ref.at[slot], dsem.at[slot]))

    @pl.when((i < nv_ref[0]) & ((i == 0) | (be_ref[i] != be_ref[jnp.maximum(i - 1, 0)])))
    def _():
        pos = ep_ref[i]

        @pl.when(i == 0)
        def _():
            for cp in fetch(pos):
                cp.start()

        for cp in fetch(pos):
            cp.wait()

        @pl.when(pos + 1 < es_ref[N_EXPERTS])
        def _():
            for cp in fetch(pos + 1):
                cp.start()

        slot = pos % 2
        wgub_ref[...] = wguf_ref[slot].astype(BF16)
        wdb_ref[...] = wdf_ref[slot].astype(BF16)

    nsub = FFN_SUBBLOCKS
    rsub = x_ref.shape[0] // nsub

    def compute(active):
        half = D_MODEL // 2
        row = lax.broadcasted_iota(I32, (rsub, 1), 0)
        gus = []
        for sb in range(active):
            rs = slice(sb * rsub, (sb + 1) * rsub)
            xp = jnp.where(row + sb * rsub < br_ref[i], x_ref[rs, :], jnp.uint32(0))
            lo = pltpu.bitcast(xp << 16, F32).astype(BF16)
            hi = pltpu.bitcast(xp & jnp.uint32(0xFFFF0000), F32).astype(BF16)
            gus.append(_dot(lo, wgub_ref[:half, :]) + _dot(hi, wgub_ref[half:, :]) + bgu_ref[0])
        ys = []
        for gu in gus:
            gate = jnp.minimum(gu[:, :D_FF], SWIGLU_LIMIT)
            up = jnp.clip(gu[:, D_FF:], -SWIGLU_LIMIT, SWIGLU_LIMIT)
            act = (up + 1.0) * (gate * _sigmoid(SWIGLU_ALPHA * gate))
            ys.append(_dot(act.astype(BF16), wdb_ref[...]) + bd_ref[0])
        for sb, y in enumerate(ys):
            ylo = pltpu.bitcast(y[:, :half].astype(BF16).astype(F32), U32) >> 16
            yhi = pltpu.bitcast(y[:, half:].astype(BF16).astype(F32), U32) & jnp.uint32(0xFFFF0000)
            y_ref[sb * rsub:(sb + 1) * rsub, :] = ylo | yhi
        if active < nsub:
            y_ref[active * rsub:, :] = jnp.zeros(((nsub - active) * rsub, half), U32)

    active = (br_ref[i] + rsub - 1) // rsub
    for n in range(1, nsub + 1):
        pl.when((i < nv_ref[0]) & (active == n))(functools.partial(compute, n))


def _ffn(blk_e, nvalid, blk_rows, blk_pos, used, xin, wgu, bgu, wd, bd):
    p = xin.shape[0]
    bm = FFN_BLOCK
    rows = lambda i, be, nv, br, ep, es: (jnp.minimum(i, nv[0] - 1), 0)
    wsel = lambda i, be, nv, br, ep, es: (be[i], 0, 0)
    return pl.pallas_call(
        _ffn_body,
        grid_spec=pltpu.PrefetchScalarGridSpec(
            num_scalar_prefetch=5,
            grid=(p // bm,),
            in_specs=[pl.BlockSpec((bm, D_MODEL // 2), rows),
                      pl.BlockSpec(memory_space=pl.ANY),
                      pl.BlockSpec((1, 1, 2 * D_FF), wsel),
                      pl.BlockSpec(memory_space=pl.ANY),
                      pl.BlockSpec((1, 1, D_MODEL), wsel)],
            out_specs=pl.BlockSpec((bm, D_MODEL // 2), lambda i, be, nv, br, ep, es: (i, 0)),
            scratch_shapes=[pltpu.VMEM((D_MODEL, 2 * D_FF), BF16), pltpu.VMEM((D_FF, D_MODEL), BF16),
                            pltpu.VMEM((2, D_MODEL, 2 * D_FF), F32), pltpu.VMEM((2, D_FF, D_MODEL), F32),
                            pltpu.SemaphoreType.DMA((2,)), pltpu.SemaphoreType.DMA((2,))],
        ),
        out_shape=jax.ShapeDtypeStruct((p, D_MODEL // 2), U32),
        compiler_params=pltpu.CompilerParams(dimension_semantics=("arbitrary",),
                                             vmem_limit_bytes=VMEM_LIMIT),
        name="ffn",
    )(blk_e, nvalid, blk_rows, blk_pos, used, xin, wgu, bgu, wd, bd)


def _sc_gather(table, idx):
    r, d = idx.shape[0], table.shape[1]
    mesh = plsc.VectorSubcoreMesh(core_axis_name="c", subcore_axis_name="s")
    nc, workers = mesh.num_cores, mesh.num_cores * mesh.num_subcores
    chunk = SC_GATHER_ROWS
    per_w = r // workers
    pairs = per_w // (2 * chunk)
    assert per_w * workers == r and pairs * 2 * chunk == per_w

    @functools.partial(
        pl.kernel, mesh=mesh, out_type=jax.ShapeDtypeStruct((r, d), table.dtype),
        scratch_types=[pltpu.VMEM((per_w,), I32), pltpu.VMEM((2, chunk, d), table.dtype),
                       pltpu.SemaphoreType.DMA((2,)), pltpu.SemaphoreType.DMA((2,))],
        name="sc_gather")
    def gather(table_hbm, idx_hbm, out_hbm, idx_v, rows_v, gsem, wsem):
        base_w = (lax.axis_index("s") * nc + lax.axis_index("c")) * per_w
        pltpu.sync_copy(idx_hbm.at[pl.ds(base_w, per_w)], idx_v)

        def fetch(b, c):
            return pltpu.make_async_copy(table_hbm.at[idx_v.at[pl.ds(c * chunk, chunk)]],
                                         rows_v.at[b], gsem.at[b])

        def flush(b, base):
            return pltpu.make_async_copy(rows_v.at[b], out_hbm.at[pl.ds(base, chunk)], wsem.at[b])

        def body(j, carry):
            for b in range(2):
                @pl.when(j > 0)
                def _():
                    flush(b, base_w).wait()

                fetch(b, 2 * j + b).start()
            for b in range(2):
                fetch(b, 2 * j + b).wait()
                flush(b, base_w + (2 * j + b) * chunk).start()
            return carry

        lax.fori_loop(0, pairs, body, 0)
        for b in range(2):
            flush(b, base_w).wait()

    return gather(table, idx)


def _combine_body(h_ref, gate_ref, fg_ref, *refs):
    yg_refs, o_ref = refs[:TOP_K], refs[-1]
    half = D_MODEL // 2
    gates = gate_ref[...]
    h = h_ref[...]
    lo, hi = h[:, :half], h[:, half:]
    for kk in range(TOP_K):
        yp = yg_refs[kk][...]
        g = gates[:, kk:kk + 1]
        lo = lo + g * pltpu.bitcast(yp << 16, F32)
        hi = hi + g * pltpu.bitcast(yp & jnp.uint32(0xFFFF0000), F32)
    out = jnp.concatenate([lo, hi], axis=1)
    o_ref[...] = _rms(out, fg_ref[...])


def _combine(h, gates, fg, ygath, out_prev, *, part, parts):
    t = h.shape[0]
    tm = min(COMBINE_TILE, t // parts)
    nt = t // parts // tm
    row = lambda i: (part * nt + i, 0)
    slot = lambda kk: pl.BlockSpec((tm, D_MODEL // 2), lambda i: (kk * nt + i, 0))
    in_specs = [pl.BlockSpec((tm, D_MODEL), row), pl.BlockSpec((tm, LANES), row),
                pl.BlockSpec((1, D_MODEL), lambda i: (0, 0))] + [slot(kk) for kk in range(TOP_K)]
    args = [h, gates, fg] + [ygath] * TOP_K
    aliases = {}
    if out_prev is not None:
        aliases = {len(args): 0}
        in_specs.append(pl.BlockSpec(memory_space=pl.ANY))
        args.append(out_prev)
    return pl.pallas_call(
        _combine_body,
        grid=(nt,),
        in_specs=in_specs,
        out_specs=pl.BlockSpec((tm, D_MODEL), row),
        out_shape=jax.ShapeDtypeStruct((t, D_MODEL), F32),
        input_output_aliases=aliases,
        compiler_params=pltpu.CompilerParams(dimension_semantics=("arbitrary",),
                                             vmem_limit_bytes=VMEM_LIMIT),
        name="combine",
    )(*args)


def _block_tril(n, chunk, strict):
    r = jnp.arange(n)[:, None]
    c = jnp.arange(n)[None, :]
    keep = ((r // chunk) == (c // chunk)) & ((r > c) if strict else (r >= c))
    return keep.astype(BF16)


def _pad_lanes(a, fill=0.0):
    a = a.reshape(1, -1).astype(F32)
    return jnp.pad(a, ((0, 0), (0, LANES - a.shape[1])), constant_values=fill)


def _layer(h, norm_mix_g, w_in, gm_norm_g, gm_ws, gm_bs, dn_conv_w, dn_a_log, dn_dt_bias,
           dn_norm_g, w_out, norm_ffn_g, router_w, router_b, exp_w_gu, exp_b_gu, exp_w_down,
           exp_b_down, out_g):
    batch, seq, d = h.shape
    t = batch * seq
    x2 = h.reshape(t, d)

    c0, c1, c2 = GM_WIDTH, 2 * GM_WIDTH, 2 * GM_WIDTH + 3 * DN_KEY
    c3 = c2 + DN_VAL
    wb = w_in.astype(BF16)
    wu, wv, wqkv, wz = wb[:, :c0], wb[:, c0:c1], wb[:, c1:c2], wb[:, c2:c3]
    wab = jnp.pad(wb[:, c3:], ((0, 0), (0, LANES - 2 * DN_HEADS)))
    gmg = gm_norm_g.reshape(1, GM_WIDTH).astype(F32)
    bsb = jnp.repeat(gm_bs.T, HEAD_DIM, axis=1).astype(F32)
    alog = _pad_lanes(dn_a_log)
    dtb = _pad_lanes(dn_dt_bias)
    tri_incl = _block_tril(ROW_TILE, DN_CHUNK, strict=False)

    oa, q, k, v, z, gb, gt = _inproj(
        x2, norm_mix_g.reshape(1, d), wu, wv, wqkv, wz, wab, gmg, gm_ws.astype(F32), bsb,
        dn_conv_w.astype(F32), alog, dtb, tri_incl, seq=seq)

    grow = gt[:DN_HEADS].reshape(DN_HEADS, t // DN_CHUNK, DN_CHUNK).transpose(1, 0, 2)
    grow = grow.reshape(t // DN_CHUNK, DN_HEADS * DN_CHUNK)
    ob = _gdn(q, k, v, z, gb, grow, dn_norm_g.reshape(1, HEAD_DIM).astype(F32), batch=batch, seq=seq)

    wo = w_out.astype(BF16)
    rw = jnp.pad(router_w.astype(BF16), ((0, 0), (0, LANES - N_EXPERTS)))
    rb = _pad_lanes(router_b, fill=NEG_BIG)
    tri_strict = _block_tril(OUTPROJ_TILE, OUTPROJ_TILE, strict=True)
    hres, hp, meta, gates, cnt = _outproj(oa, ob, x2, wo[:GM_WIDTH], wo[GM_WIDTH:],
                                          norm_ffn_g.reshape(1, d), rw, rb, tri_strict)

    bm = FFN_BLOCK
    counts = cnt[0, :N_EXPERTS]
    padded = (counts + bm - 1) // bm * bm
    pad_end = jnp.cumsum(padded)
    pad_start = (pad_end - padded).astype(I32)
    n_blocks = (t * TOP_K + N_EXPERTS * bm) // bm
    nvalid = (pad_end[-1] // bm).astype(I32).reshape(1)
    blk = jnp.minimum(jnp.arange(n_blocks, dtype=I32), nvalid[0] - 1)
    blk_e = jnp.minimum(jnp.sum(pad_end[None, :] <= (blk * bm)[:, None], axis=1), N_EXPERTS - 1).astype(I32)
    eid = meta[:TOP_K]
    start_of = jnp.sum(jnp.where(eid[..., None] == jnp.arange(N_EXPERTS, dtype=I32), pad_start, 0), axis=-1)
    dest = start_of + meta[TOP_K:2 * TOP_K]

    xin = _sc_dispatch(hp, dest.reshape(-1), n_blocks * bm)
    eids = jnp.arange(N_EXPERTS, dtype=I32)
    of_blk = blk_e[:, None] == eids
    blk_rows = jnp.clip(jnp.sum(jnp.where(of_blk, counts + pad_start, 0), axis=1) - blk * bm, 0, bm)
    has_rows = counts > 0
    used = jnp.minimum(jnp.sort(jnp.where(has_rows, eids, N_EXPERTS + eids)), N_EXPERTS - 1)
    used = jnp.concatenate([used, jnp.sum(has_rows).reshape(1)]).astype(I32)
    blk_pos = jnp.sum(jnp.where(of_blk, jnp.cumsum(has_rows) - 1, 0), axis=1).astype(I32)
    blk_rows = blk_rows.astype(I32)
    y = _ffn(blk_e, nvalid, blk_rows, blk_pos, used, xin, exp_w_gu, exp_b_gu[:, None, :].astype(F32),
             exp_w_down, exp_b_down[:, None, :].astype(F32))
    out = None
    tp = t // COMBINE_PARTS
    for part in range(COMBINE_PARTS):
        ygath = _sc_gather(y, dest[:, part * tp:(part + 1) * tp].reshape(-1))
        out = _combine(hres, gates, out_g.reshape(1, d).astype(F32), ygath, out,
                       part=part, parts=COMBINE_PARTS)
    return out.reshape(batch, seq, d)


def kernel(x, norm_mix_g, w_in, gm_norm_g, gm_ws, gm_bs, dn_conv_w, dn_a_log, dn_dt_bias, dn_norm_g, w_out, norm_ffn_g, router_w, router_b, exp_w_gu, exp_b_gu, exp_w_down, exp_b_down, final_norm_g):
    depth = norm_mix_g.shape[0]
    assert depth == 1, "single-layer problem"
    return _layer(x, norm_mix_g[0], w_in[0], gm_norm_g[0], gm_ws[0], gm_bs[0], dn_conv_w[0],
                  dn_a_log[0], dn_dt_bias[0], dn_norm_g[0], w_out[0], norm_ffn_g[0], router_w[0],
                  router_b[0], exp_w_gu[0], exp_b_gu[0], exp_w_down[0], exp_b_down[0], final_norm_g)
```

```python
import functools

import jax
import jax.numpy as jnp
from jax import lax
from jax.experimental import pallas as pl
from jax.experimental.pallas import tpu as pltpu
from jax.experimental.pallas import tpu_sc as plsc

F32 = jnp.float32
BF16 = jnp.bfloat16
I32 = jnp.int32
U32 = jnp.uint32

D_MODEL = 1024
HEAD_DIM = 128
GM_HEADS = 4
GM_WIDTH = GM_HEADS * HEAD_DIM
GM_CHUNK = 128
DN_HEADS = 4
DN_DK = 128
DN_KEY = DN_HEADS * DN_DK
DN_VAL = DN_HEADS * HEAD_DIM
DN_CHUNK = 64
CONV_W = 4
N_EXPERTS = 32
TOP_K = 4
D_FF = D_MODEL
SWIGLU_LIMIT = 7.0
SWIGLU_ALPHA = 1.702
EPS = 1e-6

LANES = 128
INPROJ_COLS = 512
INPROJ_SUBBLOCKS = 2
ROW_TILE = 512
OUTPROJ_TILE = 1024
GDN_TILE = 1024
GDN_GROUP = 4
FFN_BLOCK = 1024
FFN_SUBBLOCKS = 4
SC_GATHER_ROWS = 64
COMBINE_PARTS = 4
COMBINE_TILE = 512
VMEM_LIMIT = 56 * 1024 * 1024
NEG_BIG = -1e30


def _dot(a, b):
    return jnp.dot(a, b, preferred_element_type=F32)


def _dot_nt(a, b):
    return lax.dot_general(a, b, (((1,), (1,)), ((), ())), preferred_element_type=F32)


def _dot_tn(a, b):
    return lax.dot_general(a, b, (((0,), (0,)), ((), ())), preferred_element_type=F32)


def _rms(x, g):
    return x * lax.rsqrt(jnp.mean(x * x, axis=-1, keepdims=True) + EPS) * g


def _gelu(x):
    return 0.5 * x * (1.0 + lax.erf(x * (2.0 ** -0.5)))


def _sigmoid(x):
    return 1.0 / (1.0 + jnp.exp(-x))


def _softplus(x):
    return jnp.maximum(x, 0.0) + jnp.log1p(jnp.exp(-jnp.abs(x)))


def _inproj_body(x_ref, ng_ref, wu_ref, wv_ref, wqkv_ref, wz_ref, wab_ref, gmg_ref, ws_ref,
                 bs_ref, cw_ref, alog_ref, dtb_ref, tri_ref,
                 oa_ref, q_ref, k_ref, v_ref, z_ref, gb_ref, gt_ref, cbuf_ref, ybuf_ref,
                 *, tm, tiles_per_seq):
    i = pl.program_id(0)
    ybuf_ref[...] = _rms(x_ref[...], ng_ref[...]).astype(BF16)
    nb = INPROJ_COLS
    heads_per_block = nb // HEAD_DIM

    ri = lax.broadcasted_iota(I32, (GM_CHUNK, GM_CHUNK), 0)
    ci = lax.broadcasted_iota(I32, (GM_CHUNK, GM_CHUNK), 1)
    causal = ri >= ci

    def gmlp_matmul(blk, r0, r1):
        cs = slice(blk * nb, (blk + 1) * nb)
        return _dot(ybuf_ref[r0:r1, :], wu_ref[:, cs]), _dot(ybuf_ref[r0:r1, :], wv_ref[:, cs])

    def gmlp_epilogue(blk, r0, r1, uv):
        u, vg = _gelu(uv[0]), _gelu(uv[1])
        for hh in range(heads_per_block):
            h = blk * heads_per_block + hh
            sl = slice(h * HEAD_DIM, (h + 1) * HEAD_DIM)
            ls = slice(hh * HEAD_DIM, (hh + 1) * HEAD_DIM)
            vh = _rms(vg[:, ls], gmg_ref[:, sl]).astype(BF16)
            wm = jnp.where(causal, ws_ref[h], 0.0).astype(BF16)
            for c in range((r1 - r0) // GM_CHUNK):
                rs = slice(c * GM_CHUNK, (c + 1) * GM_CHUNK)
                gate = _dot(wm, vh[rs]) + bs_ref[:, sl]
                oa_ref[r0 + c * GM_CHUNK:r0 + (c + 1) * GM_CHUNK, sl] = (u[rs, ls] * gate).astype(BF16)

    def qkv_matmul(blk, r0, r1):
        return _dot(ybuf_ref[r0:r1, :], wqkv_ref[:, blk * nb:(blk + 1) * nb])

    def qkv_epilogue(blk, r0, r1, pq):
        cs = slice(blk * nb, (blk + 1) * nb)
        cbuf_ref[8 + r0:8 + r1, cs] = pq
        acc = cw_ref[CONV_W - 1:CONV_W, cs] * pq
        for j in range(CONV_W - 1):
            off = 8 - (CONV_W - 1) + j
            acc = acc + cw_ref[j:j + 1, cs] * cbuf_ref[off + r0:off + r1, cs]
        if r1 == tm:
            cbuf_ref[0:8, cs] = pq[r1 - r0 - 8:, :]
        s = acc * _sigmoid(acc)
        for hh in range(heads_per_block):
            c0 = blk * nb + hh * HEAD_DIM
            sh = s[:, hh * HEAD_DIM:(hh + 1) * HEAD_DIM]
            if c0 < DN_KEY:
                q_ref[r0:r1, c0:c0 + HEAD_DIM] = (
                    sh * lax.rsqrt(jnp.sum(sh * sh, axis=-1, keepdims=True) + EPS)
                    * (DN_DK ** -0.5)).astype(BF16)
            elif c0 < 2 * DN_KEY:
                k_ref[r0:r1, c0 - DN_KEY:c0 - DN_KEY + HEAD_DIM] = (
                    sh * lax.rsqrt(jnp.sum(sh * sh, axis=-1, keepdims=True) + EPS)).astype(BF16)
            else:
                v_ref[r0:r1, c0 - 2 * DN_KEY:c0 - 2 * DN_KEY + HEAD_DIM] = sh.astype(BF16)

    def z_matmul(blk, r0, r1):
        return _dot(ybuf_ref[r0:r1, :], wz_ref[:, blk * nb:(blk + 1) * nb])

    def z_epilogue(blk, r0, r1, zz):
        z_ref[r0:r1, blk * nb:(blk + 1) * nb] = zz.astype(BF16)

    @pl.when(i % tiles_per_seq == 0)
    def _():
        cbuf_ref[0:8, :] = jnp.zeros((8, 3 * DN_KEY), F32)

    stages = ([(gmlp_matmul, gmlp_epilogue, blk) for blk in range(GM_WIDTH // nb)]
              + [(qkv_matmul, qkv_epilogue, blk) for blk in range(3 * DN_KEY // nb)]
              + [(z_matmul, z_epilogue, blk) for blk in range(DN_VAL // nb)])
    rsub = tm // INPROJ_SUBBLOCKS
    spans = [(sb * rsub, (sb + 1) * rsub) for sb in range(INPROJ_SUBBLOCKS)]
    for matmul, epilogue, blk in stages:
        results = [matmul(blk, r0, r1) for r0, r1 in spans]
        for (r0, r1), res in zip(spans, results):
            epilogue(blk, r0, r1, res)

    ab = _dot(ybuf_ref[...], wab_ref[...])
    g = -jnp.exp(alog_ref[...]) * _softplus(ab + dtb_ref[...])
    beta = _sigmoid(ab)
    g_hi = g.astype(BF16)
    r1 = g - g_hi.astype(F32)
    g_mid = r1.astype(BF16)
    g_lo = (r1 - g_mid.astype(F32)).astype(BF16)
    tri = tri_ref[...]
    gc = jnp.concatenate(
        [_dot(tri, g_hi[r:r + GM_CHUNK]) + _dot(tri, g_mid[r:r + GM_CHUNK])
         + _dot(tri, g_lo[r:r + GM_CHUNK]) for r in range(0, tm, GM_CHUNK)], axis=0)
    lane = lax.broadcasted_iota(I32, (tm, LANES), 1)
    gbv = jnp.where(lane < DN_HEADS, gc, beta)
    gb_ref[...] = gbv
    gt_ref[...] = gbv.T[0:8, :]


def _inproj(x2, ng, wu, wv, wqkv, wz, wab, gmg, ws, bsb, cw, alog, dtb, tri, *, seq):
    t = x2.shape[0]
    tm = ROW_TILE
    row = lambda i: (i, 0)
    full = lambda a: pl.BlockSpec(a.shape, (lambda i: (0,) * a.ndim))
    out_shapes = (
        jax.ShapeDtypeStruct((t, GM_WIDTH), BF16),
        jax.ShapeDtypeStruct((t, DN_KEY), BF16),
        jax.ShapeDtypeStruct((t, DN_KEY), BF16),
        jax.ShapeDtypeStruct((t, DN_VAL), BF16),
        jax.ShapeDtypeStruct((t, DN_VAL), BF16),
        jax.ShapeDtypeStruct((t, LANES), F32),
        jax.ShapeDtypeStruct((8, t), F32),
    )
    return pl.pallas_call(
        functools.partial(_inproj_body, tm=tm, tiles_per_seq=seq // tm),
        grid=(t // tm,),
        in_specs=[pl.BlockSpec((tm, D_MODEL), row), full(ng), full(wu), full(wv), full(wqkv),
                  full(wz), full(wab), full(gmg), full(ws), full(bsb), full(cw), full(alog),
                  full(dtb), full(tri)],
        out_specs=(pl.BlockSpec((tm, GM_WIDTH), row), pl.BlockSpec((tm, DN_KEY), row),
                   pl.BlockSpec((tm, DN_KEY), row), pl.BlockSpec((tm, DN_VAL), row),
                   pl.BlockSpec((tm, DN_VAL), row), pl.BlockSpec((tm, LANES), row),
                   pl.BlockSpec((8, tm), lambda i: (0, i))),
        out_shape=out_shapes,
        scratch_shapes=[pltpu.VMEM((tm + 8, 3 * DN_KEY), F32), pltpu.VMEM((tm, D_MODEL), BF16)],
        compiler_params=pltpu.CompilerParams(dimension_semantics=("arbitrary",),
                                             vmem_limit_bytes=VMEM_LIMIT),
        name="inproj",
    )(x2, ng, wu, wv, wqkv, wz, wab, gmg, ws, bsb, cw, alog, dtb, tri)


def _gdn_body(q_ref, k_ref, v_ref, z_ref, gb_ref, gr_ref, ng_ref, ob_ref, s_ref, *, nchunk,
              group_size):
    j = pl.program_id(1)

    @pl.when(j == 0)
    def _():
        s_ref[...] = jnp.zeros(s_ref.shape, F32)

    c = DN_CHUNK
    n = DN_HEADS * c
    ri = lax.broadcasted_iota(I32, (n, n), 0)
    ci = lax.broadcasted_iota(I32, (n, n), 1)
    same = (ri // c) == (ci // c)
    incl = same & ((ri % c) >= (ci % c))
    strict = same & ((ri % c) > (ci % c))
    ng = ng_ref[...]

    def stack(a):
        return jnp.concatenate([a[:, h * HEAD_DIM:(h + 1) * HEAD_DIM] for h in range(DN_HEADS)], axis=0)

    def prepare(ic):
        r0 = ic * c
        kst = stack(k_ref[pl.ds(r0, c), :])
        qst = stack(q_ref[pl.ds(r0, c), :])
        vst = stack(v_ref[pl.ds(r0, c), :])
        gbc = gb_ref[pl.ds(r0, c), :]
        grow = gr_ref[pl.ds(ic, 1), :]
        gcol = jnp.concatenate([gbc[:, h:h + 1] for h in range(DN_HEADS)], axis=0)
        bcol = jnp.concatenate([gbc[:, DN_HEADS + h:DN_HEADS + h + 1] for h in range(DN_HEADS)], axis=0)
        glast = jnp.concatenate(
            [jnp.broadcast_to(gbc[c - 1:c, h:h + 1], (c, 1)) for h in range(DN_HEADS)], axis=0)
        decay = jnp.where(incl, jnp.exp(jnp.where(incl, gcol - grow, 0.0)), 0.0)
        kf = kst.astype(F32)
        kb = kf * bcol
        lmat = jnp.where(strict, _dot_nt(kb.astype(BF16), kst) * decay, 0.0)
        eg = jnp.exp(gcol)
        rhs = jnp.concatenate([vst.astype(F32) * bcol, kb * eg], axis=1)
        attn = jnp.where(incl, _dot_nt(qst, kst) * decay, 0.0).astype(BF16)
        qd = (qst.astype(F32) * eg).astype(BF16)
        ke = (kf * jnp.exp(glast - gcol)).astype(BF16)
        return dict(r0=r0, gbc=gbc, lmat=lmat, rhs=rhs, attn=attn, qd=qd, ke=ke)

    def advance_state(p, sol):
        u = sol[:, :HEAD_DIM]
        wb = sol[:, HEAD_DIM:].astype(BF16)
        vn, qs = [], []
        for h in range(DN_HEADS):
            rs = slice(h * c, (h + 1) * c)
            sb = s_ref[h].astype(BF16)
            vn.append(u[rs] - _dot(wb[rs], sb))
            qs.append(_dot(p["qd"][rs], sb))
        vnb = jnp.concatenate(vn, axis=0).astype(BF16)
        o = jnp.concatenate(qs, axis=0) + _dot(p["attn"], vnb)
        zc = z_ref[pl.ds(p["r0"], c), :]
        for h in range(DN_HEADS):
            rs = slice(h * c, (h + 1) * c)
            sl = slice(h * HEAD_DIM, (h + 1) * HEAD_DIM)
            s_ref[h] = (s_ref[h] * jnp.exp(p["gbc"][c - 1:c, h:h + 1])
                        + _dot_tn(p["ke"][rs], vnb[rs]))
            zf = zc[:, sl].astype(F32)
            ob_ref[pl.ds(p["r0"], c), sl] = (_rms(o[rs], ng) * (zf * _sigmoid(zf))).astype(BF16)

    def solve_group(ig):
        ps = [prepare(ig * group_size + b) for b in range(group_size)]
        sol = [p["rhs"] for p in ps]
        pw = [-p["lmat"] for p in ps]
        for step in range(6):
            pb = [x.astype(BF16) for x in pw]
            sol = [s + _dot(xb, s.astype(BF16)) for s, xb in zip(sol, pb)]
            if step < 5:
                pw = [_dot(xb, xb) for xb in pb]
        return ps, sol

    ngroups = nchunk // group_size
    solved = solve_group(0)
    for ig in range(ngroups):
        ahead = solve_group(ig + 1) if ig + 1 < ngroups else None
        for p, s in zip(*solved):
            advance_state(p, s)
        solved = ahead


def _gdn(q, k, v, z, gb, grow, ng, *, batch, seq):
    tm = GDN_TILE
    nchunk = tm // DN_CHUNK
    steps = seq // tm
    rows = lambda b, j: (b * steps + j, 0)
    return pl.pallas_call(
        functools.partial(_gdn_body, nchunk=nchunk, group_size=GDN_GROUP),
        grid=(batch, steps),
        in_specs=[pl.BlockSpec((tm, DN_KEY), rows), pl.BlockSpec((tm, DN_KEY), rows),
                  pl.BlockSpec((tm, DN_VAL), rows), pl.BlockSpec((tm, DN_VAL), rows),
                  pl.BlockSpec((tm, LANES), rows),
                  pl.BlockSpec((nchunk, DN_HEADS * DN_CHUNK), rows),
                  pl.BlockSpec((1, HEAD_DIM), lambda b, j: (0, 0))],
        out_specs=pl.BlockSpec((tm, DN_VAL), rows),
        out_shape=jax.ShapeDtypeStruct((batch * seq, DN_VAL), BF16),
        scratch_shapes=[pltpu.VMEM((DN_HEADS, DN_DK, HEAD_DIM), F32)],
        compiler_params=pltpu.CompilerParams(dimension_semantics=("arbitrary", "arbitrary"),
                                             vmem_limit_bytes=VMEM_LIMIT),
        name="gdn",
    )(q, k, v, z, gb, grow, ng)


def _outproj_body(oa_ref, ob_ref, x_ref, woa_ref, wob_ref, ng_ref, rw_ref, rb_ref, tri_ref,
                  h_ref, hp_ref, meta_ref, gate_ref, cnt_ref, run_ref, *, tm):
    i = pl.program_id(0)

    @pl.when(i == 0)
    def _():
        run_ref[...] = jnp.zeros(run_ref.shape, F32)

    h = x_ref[...] + _dot(oa_ref[...], woa_ref[...]) + _dot(ob_ref[...], wob_ref[...])
    h_ref[...] = h
    hb = _rms(h, ng_ref[...]).astype(BF16)
    half = D_MODEL // 2
    lo = pltpu.bitcast(hb[:, :half].astype(F32), U32) >> 16
    hi = pltpu.bitcast(hb[:, half:].astype(F32), U32) & jnp.uint32(0xFFFF0000)
    hp_ref[...] = lo | hi

    logits = _dot(hb, rw_ref[...]) + rb_ref[...]
    lane = lax.broadcasted_iota(I32, (tm, LANES), 1)
    lanef = lane.astype(F32)
    work = logits
    onehot = jnp.zeros((tm, LANES), F32)
    vals, sels = [], []
    for _ in range(TOP_K):
        m = jnp.max(work, axis=-1, keepdims=True)
        idx = jnp.min(jnp.where(work == m, lanef, float(LANES)), axis=-1, keepdims=True)
        sel = lanef == idx
        work = jnp.where(sel, -3e38, work)
        onehot = onehot + jnp.where(sel, 1.0, 0.0)
        vals.append(m)
        sels.append((sel, idx))
    ex = [jnp.exp(v - vals[0]) for v in vals]
    den = ex[0] + ex[1] + ex[2] + ex[3]
    pref = _dot(tri_ref[...], onehot.astype(BF16)) + run_ref[0:1, :]
    meta = jnp.zeros((tm, LANES), F32)
    gates = jnp.zeros((tm, LANES), F32)
    for kk in range(TOP_K):
        sel, idx = sels[kk]
        rank = jnp.sum(jnp.where(sel, pref, 0.0), axis=-1, keepdims=True)
        meta = meta + jnp.where(lane == kk, idx, 0.0) + jnp.where(lane == TOP_K + kk, rank, 0.0)
        gates = gates + jnp.where(lane == kk, ex[kk] / den, 0.0)
    meta_ref[...] = meta.T[0:8, :].astype(I32)
    gate_ref[...] = gates
    run = run_ref[...] + jnp.sum(onehot, axis=0, keepdims=True)
    run_ref[...] = run
    cnt_ref[...] = run.astype(I32)


def _outproj(oa, ob, x2, woa, wob, ng, rw, rb, tri):
    t = x2.shape[0]
    tm = OUTPROJ_TILE
    row = lambda i: (i, 0)
    full = lambda a: pl.BlockSpec(a.shape, (lambda i: (0,) * a.ndim))
    out_shapes = (
        jax.ShapeDtypeStruct((t, D_MODEL), F32),
        jax.ShapeDtypeStruct((t, D_MODEL // 2), U32),
        jax.ShapeDtypeStruct((8, t), I32),
        jax.ShapeDtypeStruct((t, LANES), F32),
        jax.ShapeDtypeStruct((8, LANES), I32),
    )
    return pl.pallas_call(
        functools.partial(_outproj_body, tm=tm),
        grid=(t // tm,),
        in_specs=[pl.BlockSpec((tm, GM_WIDTH), row), pl.BlockSpec((tm, DN_VAL), row),
                  pl.BlockSpec((tm, D_MODEL), row), full(woa), full(wob), full(ng), full(rw),
                  full(rb), full(tri)],
        out_specs=(pl.BlockSpec((tm, D_MODEL), row), pl.BlockSpec((tm, D_MODEL // 2), row),
                   pl.BlockSpec((8, tm), lambda i: (0, i)), pl.BlockSpec((tm, LANES), row),
                   pl.BlockSpec((8, LANES), lambda i: (0, 0))),
        out_shape=out_shapes,
        scratch_shapes=[pltpu.VMEM((8, LANES), F32)],
        compiler_params=pltpu.CompilerParams(dimension_semantics=("arbitrary",),
                                             vmem_limit_bytes=VMEM_LIMIT),
        name="outproj",
    )(oa, ob, x2, woa, wob, ng, rw, rb, tri)


def _sc_dispatch(hp, dest_k, n_rows):
    t, d = hp.shape
    mesh = plsc.VectorSubcoreMesh(core_axis_name="c", subcore_axis_name="s")
    nc, workers = mesh.num_cores, mesh.num_cores * mesh.num_subcores
    chunk = SC_GATHER_ROWS
    per_w = t // workers
    pairs = per_w // (2 * chunk)
    assert per_w * workers == t and pairs * 2 * chunk == per_w

    @functools.partial(
        pl.kernel, mesh=mesh, out_type=jax.ShapeDtypeStruct((n_rows, d), hp.dtype),
        scratch_types=[pltpu.VMEM((2, TOP_K, chunk), I32), pltpu.VMEM((2, chunk, d), hp.dtype),
                       pltpu.SemaphoreType.DMA((2,)), pltpu.SemaphoreType.DMA((2,)),
                       pltpu.SemaphoreType.DMA((2,))],
        name="sc_dispatch")
    def scatter(hp_hbm, idx_hbm, xin_hbm, idx_v, rows_v, lsem, ssem, isem):
        base_w = (lax.axis_index("s") * nc + lax.axis_index("c")) * per_w

        def index(b, kk, base):
            return pltpu.make_async_copy(idx_hbm.at[pl.ds(kk * t + base, chunk)], idx_v.at[b, kk],
                                         isem.at[b])

        def load(b, base):
            return pltpu.make_async_copy(hp_hbm.at[pl.ds(base, chunk)], rows_v.at[b], lsem.at[b])

        def put(b, kk):
            return pltpu.make_async_copy(rows_v.at[b], xin_hbm.at[idx_v.at[b, kk]], ssem.at[b])

        def body(j, carry):
            for b in range(2):
                base = base_w + (2 * j + b) * chunk

                @pl.when(j > 0)
                def _():
                    for kk in range(TOP_K):
                        put(b, kk).wait()

                for kk in range(TOP_K):
                    index(b, kk, base).start()
                load(b, base).start()
            for b in range(2):
                for kk in range(TOP_K):
                    index(b, kk, base_w).wait()
                load(b, base_w).wait()
                for kk in range(TOP_K):
                    put(b, kk).start()
            return carry

        lax.fori_loop(0, pairs, body, 0)
        for b in range(2):
            for kk in range(TOP_K):
                put(b, kk).wait()

    return scatter(hp, dest_k)


def _ffn_body(be_ref, nv_ref, br_ref, ep_ref, es_ref, x_ref, wgu_hbm, bgu_ref, wd_hbm, bd_ref, y_ref,
              wgub_ref, wdb_ref, wguf_ref, wdf_ref, gsem, dsem):
    i = pl.program_id(0)

    @pl.when(i >= nv_ref[0])
    def _():
        y_ref[...] = jnp.zeros(y_ref.shape, U32)

    def fetch(pos):
        e, slot = es_ref[pos], pos % 2
        return (pltpu.make_async_copy(wgu_hbm.at[e], wguf_ref.at[slot], gsem.at[slot]),
                pltpu.make_async_copy(wd_hbm.at[e], wdf_ref.at[slot], dsem.at[slot]))

    @pl.when((i < nv_ref[0]) & ((i == 0) | (be_ref[i] != be_ref[jnp.maximum(i - 1, 0)])))
    def _():
        pos = ep_ref[i]

        @pl.when(i == 0)
        def _():
            for cp in fetch(pos):
                cp.start()

        for cp in fetch(pos):
            cp.wait()

        @pl.when(pos + 1 < es_ref[N_EXPERTS])
        def _():
            for cp in fetch(pos + 1):
                cp.start()

        slot = pos % 2
        wgub_ref[...] = wguf_ref[slot].astype(BF16)
        wdb_ref[...] = wdf_ref[slot].astype(BF16)

    nsub = FFN_SUBBLOCKS
    rsub = x_ref.shape[0] // nsub

    def compute(active):
        half = D_MODEL // 2
        row = lax.broadcasted_iota(I32, (rsub, 1), 0)
        gus = []
        for sb in range(active):
            rs = slice(sb * rsub, (sb + 1) * rsub)
            xp = jnp.where(row + sb * rsub < br_ref[i], x_ref[rs, :], jnp.uint32(0))
            lo = pltpu.bitcast(xp << 16, F32).astype(BF16)
            hi = pltpu.bitcast(xp & jnp.uint32(0xFFFF0000), F32).astype(BF16)
            gus.append(_dot(lo, wgub_ref[:half, :]) + _dot(hi, wgub_ref[half:, :]) + bgu_ref[0])
        ys = []
        for gu in gus:
            gate = jnp.minimum(gu[:, :D_FF], SWIGLU_LIMIT)
            up = jnp.clip(gu[:, D_FF:], -SWIGLU_LIMIT, SWIGLU_LIMIT)
            act = (up + 1.0) * (gate * _sigmoid(SWIGLU_ALPHA * gate))
            ys.append(_dot(act.astype(BF16), wdb_ref[...]) + bd_ref[0])
        for sb, y in enumerate(ys):
            ylo = pltpu.bitcast(y[:, :half].astype(BF16).astype(F32), U32) >> 16
            yhi = pltpu.bitcast(y[:, half:].astype(BF16).astype(F32), U32) & jnp.uint32(0xFFFF0000)
            y_ref[sb * rsub:(sb + 1) * rsub, :] = ylo | yhi
        if active < nsub:
            y_ref[active * rsub:, :] = jnp.zeros(((nsub - active) * rsub, half), U32)

    active = (br_ref[i] + rsub - 1) // rsub
    for n in range(1, nsub + 1):
        pl.when((i < nv_ref[0]) & (active == n))(functools.partial(compute, n))


def _ffn(blk_e, nvalid, blk_rows, blk_pos, used, xin, wgu, bgu, wd, bd):
    p = xin.shape[0]
    bm = FFN_BLOCK
    rows = lambda i, be, nv, br, ep, es: (jnp.minimum(i, nv[0] - 1), 0)
    wsel = lambda i, be, nv, br, ep, es: (be[i], 0, 0)
    return pl.pallas_call(
        _ffn_body,
        grid_spec=pltpu.PrefetchScalarGridSpec(
            num_scalar_prefetch=5,
            grid=(p // bm,),
            in_specs=[pl.BlockSpec((bm, D_MODEL // 2), rows),
                      pl.BlockSpec(memory_space=pl.ANY),
                      pl.BlockSpec((1, 1, 2 * D_FF), wsel),
                      pl.BlockSpec(memory_space=pl.ANY),
                      pl.BlockSpec((1, 1, D_MODEL), wsel)],
            out_specs=pl.BlockSpec((bm, D_MODEL // 2), lambda i, be, nv, br, ep, es: (i, 0)),
            scratch_shapes=[pltpu.VMEM((D_MODEL, 2 * D_FF), BF16), pltpu.VMEM((D_FF, D_MODEL), BF16),
                            pltpu.VMEM((2, D_MODEL, 2 * D_FF), F32), pltpu.VMEM((2, D_FF, D_MODEL), F32),
                            pltpu.SemaphoreType.DMA((2,)), pltpu.SemaphoreType.DMA((2,))],
        ),
        out_shape=jax.ShapeDtypeStruct((p, D_MODEL // 2), U32),
        compiler_params=pltpu.CompilerParams(dimension_semantics=("arbitrary",),
                                             vmem_limit_bytes=VMEM_LIMIT),
        name="ffn",
    )(blk_e, nvalid, blk_rows, blk_pos, used, xin, wgu, bgu, wd, bd)


def _sc_gather(table, idx):
    r, d = idx.shape[0], table.shape[1]
    mesh = plsc.VectorSubcoreMesh(core_axis_name="c", subcore_axis_name="s")
    nc, workers = mesh.num_cores, mesh.num_cores * mesh.num_subcores
    chunk = SC_GATHER_ROWS
    per_w = r // workers
    pairs = per_w // (2 * chunk)
    assert per_w * workers == r and pairs * 2 * chunk == per_w

    @functools.partial(
        pl.kernel, mesh=mesh, out_type=jax.ShapeDtypeStruct((r, d), table.dtype),
        scratch_types=[pltpu.VMEM((per_w,), I32), pltpu.VMEM((2, chunk, d), table.dtype),
                       pltpu.SemaphoreType.DMA((2,)), pltpu.SemaphoreType.DMA((2,))],
        name="sc_gather")
    def gather(table_hbm, idx_hbm, out_hbm, idx_v, rows_v, gsem, wsem):
        base_w = (lax.axis_index("s") * nc + lax.axis_index("c")) * per_w
        pltpu.sync_copy(idx_hbm.at[pl.ds(base_w, per_w)], idx_v)

        def fetch(b, c):
            return pltpu.make_async_copy(table_hbm.at[idx_v.at[pl.ds(c * chunk, chunk)]],
                                         rows_v.at[b], gsem.at[b])

        def flush(b, base):
            return pltpu.make_async_copy(rows_v.at[b], out_hbm.at[pl.ds(base, chunk)], wsem.at[b])

        def body(j, carry):
            for b in range(2):
                @pl.when(j > 0)
                def _():
                    flush(b, base_w).wait()

                fetch(b, 2 * j + b).start()
            for b in range(2):
                fetch(b, 2 * j + b).wait()
                flush(b, base_w + (2 * j + b) * chunk).start()
            return carry

        lax.fori_loop(0, pairs, body, 0)
        for b in range(2):
            flush(b, base_w).wait()

    return gather(table, idx)


def _combine_body(h_ref, gate_ref, fg_ref, *refs):
    yg_refs, o_ref = refs[:TOP_K], refs[-1]
    half = D_MODEL // 2
    gates = gate_ref[...]
    h = h_ref[...]
    lo, hi = h[:, :half], h[:, half:]
    for kk in range(TOP_K):
        yp = yg_refs[kk][...]
        g = gates[:, kk:kk + 1]
        lo = lo + g * pltpu.bitcast(yp << 16, F32)
        hi = hi + g * pltpu.bitcast(yp & jnp.uint32(0xFFFF0000), F32)
    out = jnp.concatenate([lo, hi], axis=1)
    o_ref[...] = _rms(out, fg_ref[...])


def _combine(h, gates, fg, ygath, out_prev, *, part, parts):
    t = h.shape[0]
    tm = min(COMBINE_TILE, t // parts)
    nt = t // parts // tm
    row = lambda i: (part * nt + i, 0)
    slot = lambda kk: pl.BlockSpec((tm, D_MODEL // 2), lambda i: (kk * nt + i, 0))
    in_specs = [pl.BlockSpec((tm, D_MODEL), row), pl.BlockSpec((tm, LANES), row),
                pl.BlockSpec((1, D_MODEL), lambda i: (0, 0))] + [slot(kk) for kk in range(TOP_K)]
    args = [h, gates, fg] + [ygath] * TOP_K
    aliases = {}
    if out_prev is not None:
        aliases = {len(args): 0}
        in_specs.append(pl.BlockSpec(memory_space=pl.ANY))
        args.append(out_prev)
    return pl.pallas_call(
        _combine_body,
        grid=(nt,),
        in_specs=in_specs,
        out_specs=pl.BlockSpec((tm, D_MODEL), row),
        out_shape=jax.ShapeDtypeStruct((t, D_MODEL), F32),
        input_output_aliases=aliases,
        compiler_params=pltpu.CompilerParams(dimension_semantics=("arbitrary",),
                                             vmem_limit_bytes=VMEM_LIMIT),
        name="combine",
    )(*args)


def _block_tril(n, chunk, strict):
    r = jnp.arange(n)[:, None]
    c = jnp.arange(n)[None, :]
    keep = ((r // chunk) == (c // chunk)) & ((r > c) if strict else (r >= c))
    return keep.astype(BF16)


def _pad_lanes(a, fill=0.0):
    a = a.reshape(1, -1).astype(F32)
    return jnp.pad(a, ((0, 0), (0, LANES - a.shape[1])), constant_values=fill)


def _layer(h, norm_mix_g, w_in, gm_norm_g, gm_ws, gm_bs, dn_conv_w, dn_a_log, dn_dt_bias,
           dn_norm_g, w_out, norm_ffn_g, router_w, router_b, exp_w_gu, exp_b_gu, exp_w_down,
           exp_b_down, out_g):
    batch, seq, d = h.shape
    t = batch * seq
    x2 = h.reshape(t, d)

    c0, c1, c2 = GM_WIDTH, 2 * GM_WIDTH, 2 * GM_WIDTH + 3 * DN_KEY
    c3 = c2 + DN_VAL
    wb = w_in.astype(BF16)
    wu, wv, wqkv, wz = wb[:, :c0], wb[:, c0:c1], wb[:, c1:c2], wb[:, c2:c3]
    wab = jnp.pad(wb[:, c3:], ((0, 0), (0, LANES - 2 * DN_HEADS)))
    gmg = gm_norm_g.reshape(1, GM_WIDTH).astype(F32)
    bsb = jnp.repeat(gm_bs.T, HEAD_DIM, axis=1).astype(F32)
    alog = _pad_lanes(dn_a_log)
    dtb = _pad_lanes(dn_dt_bias)
    tri_incl = _block_tril(GM_CHUNK, DN_CHUNK, strict=False)

    oa, q, k, v, z, gb, gt = _inproj(
        x2, norm_mix_g.reshape(1, d), wu, wv, wqkv, wz, wab, gmg, gm_ws.astype(F32), bsb,
        dn_conv_w.astype(F32), alog, dtb, tri_incl, seq=seq)

    grow = gt[:DN_HEADS].reshape(DN_HEADS, t // DN_CHUNK, DN_CHUNK).transpose(1, 0, 2)
    grow = grow.reshape(t // DN_CHUNK, DN_HEADS * DN_CHUNK)
    ob = _gdn(q, k, v, z, gb, grow, dn_norm_g.reshape(1, HEAD_DIM).astype(F32), batch=batch, seq=seq)

    wo = w_out.astype(BF16)
    rw = jnp.pad(router_w.astype(BF16), ((0, 0), (0, LANES - N_EXPERTS)))
    rb = _pad_lanes(router_b, fill=NEG_BIG)
    tri_strict = _block_tril(OUTPROJ_TILE, OUTPROJ_TILE, strict=True)
    hres, hp, meta, gates, cnt = _outproj(oa, ob, x2, wo[:GM_WIDTH], wo[GM_WIDTH:],
                                          norm_ffn_g.reshape(1, d), rw, rb, tri_strict)

    bm = FFN_BLOCK
    counts = cnt[0, :N_EXPERTS]
    padded = (counts + bm - 1) // bm * bm
    pad_end = jnp.cumsum(padded)
    pad_start = (pad_end - padded).astype(I32)
    n_blocks = (t * TOP_K + N_EXPERTS * bm) // bm
    nvalid = (pad_end[-1] // bm).astype(I32).reshape(1)
    blk = jnp.minimum(jnp.arange(n_blocks, dtype=I32), nvalid[0] - 1)
    blk_e = jnp.minimum(jnp.sum(pad_end[None, :] <= (blk * bm)[:, None], axis=1), N_EXPERTS - 1).astype(I32)
    eid = meta[:TOP_K]
    start_of = jnp.sum(jnp.where(eid[..., None] == jnp.arange(N_EXPERTS, dtype=I32), pad_start, 0), axis=-1)
    dest = start_of + meta[TOP_K:2 * TOP_K]

    xin = _sc_dispatch(hp, dest.reshape(-1), n_blocks * bm)
    eids = jnp.arange(N_EXPERTS, dtype=I32)
    of_blk = blk_e[:, None] == eids
    blk_rows = jnp.clip(jnp.sum(jnp.where(of_blk, counts + pad_start, 0), axis=1) - blk * bm, 0, bm)
    has_rows = counts > 0
    used = jnp.minimum(jnp.sort(jnp.where(has_rows, eids, N_EXPERTS + eids)), N_EXPERTS - 1)
    used = jnp.concatenate([used, jnp.sum(has_rows).reshape(1)]).astype(I32)
    blk_pos = jnp.sum(jnp.where(of_blk, jnp.cumsum(has_rows) - 1, 0), axis=1).astype(I32)
    blk_rows = blk_rows.astype(I32)
    y = _ffn(blk_e, nvalid, blk_rows, blk_pos, used, xin, exp_w_gu, exp_b_gu[:, None, :].astype(F32),
             exp_w_down, exp_b_down[:, None, :].astype(F32))
    out = None
    tp = t // COMBINE_PARTS
    for part in range(COMBINE_PARTS):
        ygath = _sc_gather(y, dest[:, part * tp:(part + 1) * tp].reshape(-1))
        out = _combine(hres, gates, out_g.reshape(1, d).astype(F32), ygath, out,
                       part=part, parts=COMBINE_PARTS)
    return out.reshape(batch, seq, d)


def kernel(x, norm_mix_g, w_in, gm_norm_g, gm_ws, gm_bs, dn_conv_w, dn_a_log, dn_dt_bias, dn_norm_g, w_out, norm_ffn_g, router_w, router_b, exp_w_gu, exp_b_gu, exp_w_down, exp_b_down, final_norm_g):
    depth = norm_mix_g.shape[0]
    assert depth == 1, "single-layer problem"
    return _layer(x, norm_mix_g[0], w_in[0], gm_norm_g[0], gm_ws[0], gm_bs[0], dn_conv_w[0],
                  dn_a_log[0], dn_dt_bias[0], dn_norm_g[0], w_out[0], norm_ffn_g[0], router_w[0],
                  router_b[0], exp_w_gu[0], exp_b_gu[0], exp_w_down[0], exp_b_down[0], final_norm_g)
```

```python
import functools

import jax
import jax.numpy as jnp
from jax import lax
from jax.experimental import pallas as pl
from jax.experimental.pallas import tpu as pltpu
from jax.experimental.pallas import tpu_sc as plsc

F32 = jnp.float32
BF16 = jnp.bfloat16
I32 = jnp.int32
U32 = jnp.uint32

D_MODEL = 1024
HEAD_DIM = 128
GM_HEADS = 4
GM_WIDTH = GM_HEADS * HEAD_DIM
GM_CHUNK = 128
DN_HEADS = 4
DN_DK = 128
DN_KEY = DN_HEADS * DN_DK
DN_VAL = DN_HEADS * HEAD_DIM
DN_CHUNK = 64
CONV_W = 4
N_EXPERTS = 32
TOP_K = 4
D_FF = D_MODEL
SWIGLU_LIMIT = 7.0
SWIGLU_ALPHA = 1.702
EPS = 1e-6

LANES = 128
INPROJ_COLS = 512
INPROJ_SUBBLOCKS = 2
ROW_TILE = 512
OUTPROJ_TILE = 1024
GDN_TILE = 1024
GDN_GROUP = 4
FFN_BLOCK = 1024
FFN_SUBBLOCKS = 4
SC_GATHER_ROWS = 64
COMBINE_PARTS = 4
COMBINE_TILE = 512
VMEM_LIMIT = 56 * 1024 * 1024
NEG_BIG = -1e30


def _dot(a, b):
    return jnp.dot(a, b, preferred_element_type=F32)


def _dot_nt(a, b):
    return lax.dot_general(a, b, (((1,), (1,)), ((), ())), preferred_element_type=F32)


def _dot_tn(a, b):
    return lax.dot_general(a, b, (((0,), (0,)), ((), ())), preferred_element_type=F32)


def _rms(x, g):
    return x * lax.rsqrt(jnp.mean(x * x, axis=-1, keepdims=True) + EPS) * g


def _gelu(x):
    return 0.5 * x * (1.0 + lax.erf(x * (2.0 ** -0.5)))


def _sigmoid(x):
    return 1.0 / (1.0 + jnp.exp(-x))


def _softplus(x):
    return jnp.maximum(x, 0.0) + jnp.log1p(jnp.exp(-jnp.abs(x)))


def _inproj_body(x_ref, ng_ref, wu_ref, wv_ref, wqkv_ref, wz_ref, wab_ref, gmg_ref, ws_ref,
                 bs_ref, cw_ref, alog_ref, dtb_ref, tri_ref,
                 oa_ref, q_ref, k_ref, v_ref, z_ref, gb_ref, gt_ref, cbuf_ref, ybuf_ref,
                 *, tm, tiles_per_seq):
    i = pl.program_id(0)
    ybuf_ref[...] = _rms(x_ref[...], ng_ref[...]).astype(BF16)
    nb = INPROJ_COLS
    heads_per_block = nb // HEAD_DIM

    ri = lax.broadcasted_iota(I32, (GM_CHUNK, GM_CHUNK), 0)
    ci = lax.broadcasted_iota(I32, (GM_CHUNK, GM_CHUNK), 1)
    causal = ri >= ci

    def gmlp_matmul(blk, r0, r1):
        cs = slice(blk * nb, (blk + 1) * nb)
        return _dot(ybuf_ref[r0:r1, :], wu_ref[:, cs]), _dot(ybuf_ref[r0:r1, :], wv_ref[:, cs])

    def gmlp_epilogue(blk, r0, r1, uv):
        u, vg = _gelu(uv[0]), _gelu(uv[1])
        for hh in range(heads_per_block):
            h = blk * heads_per_block + hh
            sl = slice(h * HEAD_DIM, (h + 1) * HEAD_DIM)
            ls = slice(hh * HEAD_DIM, (hh + 1) * HEAD_DIM)
            vh = _rms(vg[:, ls], gmg_ref[:, sl]).astype(BF16)
            wm = jnp.where(causal, ws_ref[h], 0.0).astype(BF16)
            for c in range((r1 - r0) // GM_CHUNK):
                rs = slice(c * GM_CHUNK, (c + 1) * GM_CHUNK)
                gate = _dot(wm, vh[rs]) + bs_ref[:, sl]
                oa_ref[r0 + c * GM_CHUNK:r0 + (c + 1) * GM_CHUNK, sl] = (u[rs, ls] * gate).astype(BF16)

    def qkv_matmul(blk, r0, r1):
        return _dot(ybuf_ref[r0:r1, :], wqkv_ref[:, blk * nb:(blk + 1) * nb])

    def qkv_epilogue(blk, r0, r1, pq):
        cs = slice(blk * nb, (blk + 1) * nb)
        cbuf_ref[8 + r0:8 + r1, cs] = pq
        acc = cw_ref[CONV_W - 1:CONV_W, cs] * pq
        for j in range(CONV_W - 1):
            off = 8 - (CONV_W - 1) + j
            acc = acc + cw_ref[j:j + 1, cs] * cbuf_ref[off + r0:off + r1, cs]
        if r1 == tm:
            cbuf_ref[0:8, cs] = pq[r1 - r0 - 8:, :]
        s = acc * _sigmoid(acc)
        for hh in range(heads_per_block):
            c0 = blk * nb + hh * HEAD_DIM
            sh = s[:, hh * HEAD_DIM:(hh + 1) * HEAD_DIM]
            if c0 < DN_KEY:
                q_ref[r0:r1, c0:c0 + HEAD_DIM] = (
                    sh * lax.rsqrt(jnp.sum(sh * sh, axis=-1, keepdims=True) + EPS)
                    * (DN_DK ** -0.5)).astype(BF16)
            elif c0 < 2 * DN_KEY:
                k_ref[r0:r1, c0 - DN_KEY:c0 - DN_KEY + HEAD_DIM] = (
                    sh * lax.rsqrt(jnp.sum(sh * sh, axis=-1, keepdims=True) + EPS)).astype(BF16)
            else:
                v_ref[r0:r1, c0 - 2 * DN_KEY:c0 - 2 * DN_KEY + HEAD_DIM] = sh.astype(BF16)

    def z_matmul(blk, r0, r1):
        return _dot(ybuf_ref[r0:r1, :], wz_ref[:, blk * nb:(blk + 1) * nb])

    def z_epilogue(blk, r0, r1, zz):
        z_ref[r0:r1, blk * nb:(blk + 1) * nb] = zz.astype(BF16)

    @pl.when(i % tiles_per_seq == 0)
    def _():
        cbuf_ref[0:8, :] = jnp.zeros((8, 3 * DN_KEY), F32)

    stages = ([(gmlp_matmul, gmlp_epilogue, blk) for blk in range(GM_WIDTH // nb)]
              + [(qkv_matmul, qkv_epilogue, blk) for blk in range(3 * DN_KEY // nb)]
              + [(z_matmul, z_epilogue, blk) for blk in range(DN_VAL // nb)])
    rsub = tm // INPROJ_SUBBLOCKS
    spans = [(sb * rsub, (sb + 1) * rsub) for sb in range(INPROJ_SUBBLOCKS)]
    for matmul, epilogue, blk in stages:
        results = [matmul(blk, r0, r1) for r0, r1 in spans]
        for (r0, r1), res in zip(spans, results):
            epilogue(blk, r0, r1, res)

    ab = _dot(ybuf_ref[...], wab_ref[...])
    g = -jnp.exp(alog_ref[...]) * _softplus(ab + dtb_ref[...])
    beta = _sigmoid(ab)
    g_hi = g.astype(BF16)
    r1 = g - g_hi.astype(F32)
    g_mid = r1.astype(BF16)
    g_lo = (r1 - g_mid.astype(F32)).astype(BF16)
    tri = tri_ref[...]
    gc = jnp.concatenate(
        [_dot(tri, g_hi[r:r + GM_CHUNK]) + _dot(tri, g_mid[r:r + GM_CHUNK])
         + _dot(tri, g_lo[r:r + GM_CHUNK]) for r in range(0, tm, GM_CHUNK)], axis=0)
    lane = lax.broadcasted_iota(I32, (tm, LANES), 1)
    gbv = jnp.where(lane < DN_HEADS, gc, beta)
    gb_ref[...] = gbv
    gt_ref[...] = gbv.T[0:8, :]


def _inproj(x2, ng, wu, wv, wqkv, wz, wab, gmg, ws, bsb, cw, alog, dtb, tri, *, seq):
    t = x2.shape[0]
    tm = ROW_TILE
    row = lambda i: (i, 0)
    full = lambda a: pl.BlockSpec(a.shape, (lambda i: (0,) * a.ndim))
    out_shapes = (
        jax.ShapeDtypeStruct((t, GM_WIDTH), BF16),
        jax.ShapeDtypeStruct((t, DN_KEY), BF16),
        jax.ShapeDtypeStruct((t, DN_KEY), BF16),
        jax.ShapeDtypeStruct((t, DN_VAL), BF16),
        jax.ShapeDtypeStruct((t, DN_VAL), BF16),
        jax.ShapeDtypeStruct((t, LANES), F32),
        jax.ShapeDtypeStruct((8, t), F32),
    )
    return pl.pallas_call(
        functools.partial(_inproj_body, tm=tm, tiles_per_seq=seq // tm),
        grid=(t // tm,),
        in_specs=[pl.BlockSpec((tm, D_MODEL), row), full(ng), full(wu), full(wv), full(wqkv),
                  full(wz), full(wab), full(gmg), full(ws), full(bsb), full(cw), full(alog),
                  full(dtb), full(tri)],
        out_specs=(pl.BlockSpec((tm, GM_WIDTH), row), pl.BlockSpec((tm, DN_KEY), row),
                   pl.BlockSpec((tm, DN_KEY), row), pl.BlockSpec((tm, DN_VAL), row),
                   pl.BlockSpec((tm, DN_VAL), row), pl.BlockSpec((tm, LANES), row),
                   pl.BlockSpec((8, tm), lambda i: (0, i))),
        out_shape=out_shapes,
        scratch_shapes=[pltpu.VMEM((tm + 8, 3 * DN_KEY), F32), pltpu.VMEM((tm, D_MODEL), BF16)],
        compiler_params=pltpu.CompilerParams(dimension_semantics=("arbitrary",),
                                             vmem_limit_bytes=VMEM_LIMIT),
        name="inproj",
    )(x2, ng, wu, wv, wqkv, wz, wab, gmg, ws, bsb, cw, alog, dtb, tri)


def _gdn_body(q_ref, k_ref, v_ref, z_ref, gb_ref, gr_ref, ng_ref, ob_ref, s_ref, *, nchunk,
              group_size):
    j = pl.program_id(1)

    @pl.when(j == 0)
    def _():
        s_ref[...] = jnp.zeros(s_ref.shape, F32)

    c = DN_CHUNK
    n = DN_HEADS * c
    ri = lax.broadcasted_iota(I32, (n, n), 0)
    ci = lax.broadcasted_iota(I32, (n, n), 1)
    same = (ri // c) == (ci // c)
    incl = same & ((ri % c) >= (ci % c))
    strict = same & ((ri % c) > (ci % c))
    ng = ng_ref[...]

    def stack(a):
        return jnp.concatenate([a[:, h * HEAD_DIM:(h + 1) * HEAD_DIM] for h in range(DN_HEADS)], axis=0)

    def prepare(ic):
        r0 = ic * c
        kst = stack(k_ref[pl.ds(r0, c), :])
        qst = stack(q_ref[pl.ds(r0, c), :])
        vst = stack(v_ref[pl.ds(r0, c), :])
        gbc = gb_ref[pl.ds(r0, c), :]
        grow = gr_ref[pl.ds(ic, 1), :]
        gcol = jnp.concatenate([gbc[:, h:h + 1] for h in range(DN_HEADS)], axis=0)
        bcol = jnp.concatenate([gbc[:, DN_HEADS + h:DN_HEADS + h + 1] for h in range(DN_HEADS)], axis=0)
        glast = jnp.concatenate(
            [jnp.broadcast_to(gbc[c - 1:c, h:h + 1], (c, 1)) for h in range(DN_HEADS)], axis=0)
        decay = jnp.where(incl, jnp.exp(jnp.where(incl, gcol - grow, 0.0)), 0.0)
        kf = kst.astype(F32)
        kb = kf * bcol
        lmat = jnp.where(strict, _dot_nt(kb.astype(BF16), kst) * decay, 0.0)
        eg = jnp.exp(gcol)
        rhs = jnp.concatenate([vst.astype(F32) * bcol, kb * eg], axis=1)
        attn = jnp.where(incl, _dot_nt(qst, kst) * decay, 0.0).astype(BF16)
        qd = (qst.astype(F32) * eg).astype(BF16)
        ke = (kf * jnp.exp(glast - gcol)).astype(BF16)
        return dict(r0=r0, gbc=gbc, lmat=lmat, rhs=rhs, attn=attn, qd=qd, ke=ke)

    def advance_state(p, sol):
        u = sol[:, :HEAD_DIM]
        wb = sol[:, HEAD_DIM:].astype(BF16)
        vn, qs = [], []
        for h in range(DN_HEADS):
            rs = slice(h * c, (h + 1) * c)
            sb = s_ref[h].astype(BF16)
            vn.append(u[rs] - _dot(wb[rs], sb))
            qs.append(_dot(p["qd"][rs], sb))
        vnb = jnp.concatenate(vn, axis=0).astype(BF16)
        o = jnp.concatenate(qs, axis=0) + _dot(p["attn"], vnb)
        zc = z_ref[pl.ds(p["r0"], c), :]
        for h in range(DN_HEADS):
            rs = slice(h * c, (h + 1) * c)
            sl = slice(h * HEAD_DIM, (h + 1) * HEAD_DIM)
            s_ref[h] = (s_ref[h] * jnp.exp(p["gbc"][c - 1:c, h:h + 1])
                        + _dot_tn(p["ke"][rs], vnb[rs]))
            zf = zc[:, sl].astype(F32)
            ob_ref[pl.ds(p["r0"], c), sl] = (_rms(o[rs], ng) * (zf * _sigmoid(zf))).astype(BF16)

    def solve_group(ig):
        ps = [prepare(ig * group_size + b) for b in range(group_size)]
        sol = [p["rhs"] for p in ps]
        pw = [-p["lmat"] for p in ps]
        for step in range(6):
            pb = [x.astype(BF16) for x in pw]
            sol = [s + _dot(xb, s.astype(BF16)) for s, xb in zip(sol, pb)]
            if step < 5:
                pw = [_dot(xb, xb) for xb in pb]
        return ps, sol

    ngroups = nchunk // group_size
    solved = solve_group(0)
    for ig in range(ngroups):
        ahead = solve_group(ig + 1) if ig + 1 < ngroups else None
        for p, s in zip(*solved):
            advance_state(p, s)
        solved = ahead


def _gdn(q, k, v, z, gb, grow, ng, *, batch, seq):
    tm = GDN_TILE
    nchunk = tm // DN_CHUNK
    steps = seq // tm
    rows = lambda b, j: (b * steps + j, 0)
    return pl.pallas_call(
        functools.partial(_gdn_body, nchunk=nchunk, group_size=GDN_GROUP),
        grid=(batch, steps),
        in_specs=[pl.BlockSpec((tm, DN_KEY), rows), pl.BlockSpec((tm, DN_KEY), rows),
                  pl.BlockSpec((tm, DN_VAL), rows), pl.BlockSpec((tm, DN_VAL), rows),
                  pl.BlockSpec((tm, LANES), rows),
                  pl.BlockSpec((nchunk, DN_HEADS * DN_CHUNK), rows),
                  pl.BlockSpec((1, HEAD_DIM), lambda b, j: (0, 0))],
        out_specs=pl.BlockSpec((tm, DN_VAL), rows),
        out_shape=jax.ShapeDtypeStruct((batch * seq, DN_VAL), BF16),
        scratch_shapes=[pltpu.VMEM((DN_HEADS, DN_DK, HEAD_DIM), F32)],
        compiler_params=pltpu.CompilerParams(dimension_semantics=("arbitrary", "arbitrary"),
                                             vmem_limit_bytes=VMEM_LIMIT),
        name="gdn",
    )(q, k, v, z, gb, grow, ng)


def _outproj_body(oa_ref, ob_ref, x_ref, woa_ref, wob_ref, ng_ref, rw_ref, rb_ref, tri_ref,
                  h_ref, hp_ref, meta_ref, gate_ref, cnt_ref, run_ref, *, tm):
    i = pl.program_id(0)

    @pl.when(i == 0)
    def _():
        run_ref[...] = jnp.zeros(run_ref.shape, F32)

    h = x_ref[...] + _dot(oa_ref[...], woa_ref[...]) + _dot(ob_ref[...], wob_ref[...])
    h_ref[...] = h
    hb = _rms(h, ng_ref[...]).astype(BF16)
    half = D_MODEL // 2
    lo = pltpu.bitcast(hb[:, :half].astype(F32), U32) >> 16
    hi = pltpu.bitcast(hb[:, half:].astype(F32), U32) & jnp.uint32(0xFFFF0000)
    hp_ref[...] = lo | hi

    logits = _dot(hb, rw_ref[...]) + rb_ref[...]
    lane = lax.broadcasted_iota(I32, (tm, LANES), 1)
    lanef = lane.astype(F32)
    work = logits
    onehot = jnp.zeros((tm, LANES), F32)
    vals, sels = [], []
    for _ in range(TOP_K):
        m = jnp.max(work, axis=-1, keepdims=True)
        idx = jnp.min(jnp.where(work == m, lanef, float(LANES)), axis=-1, keepdims=True)
        sel = lanef == idx
        work = jnp.where(sel, -3e38, work)
        onehot = onehot + jnp.where(sel, 1.0, 0.0)
        vals.append(m)
        sels.append((sel, idx))
    ex = [jnp.exp(v - vals[0]) for v in vals]
    den = ex[0] + ex[1] + ex[2] + ex[3]
    tri = tri_ref[...]
    carried = run_ref[...]
    prefs = []
    for r in range(0, tm, LANES):
        blk = onehot[r:r + LANES]
        prefs.append(_dot(tri, blk.astype(BF16)) + carried[0:1, :])
        carried = carried + jnp.sum(blk, axis=0, keepdims=True)
    pref = jnp.concatenate(prefs, axis=0)
    meta = jnp.zeros((tm, LANES), F32)
    gates = jnp.zeros((tm, LANES), F32)
    for kk in range(TOP_K):
        sel, idx = sels[kk]
        rank = jnp.sum(jnp.where(sel, pref, 0.0), axis=-1, keepdims=True)
        meta = meta + jnp.where(lane == kk, idx, 0.0) + jnp.where(lane == TOP_K + kk, rank, 0.0)
        gates = gates + jnp.where(lane == kk, ex[kk] / den, 0.0)
    meta_ref[...] = meta.T[0:8, :].astype(I32)
    gate_ref[...] = gates
    run_ref[...] = carried
    cnt_ref[...] = carried.astype(I32)


def _outproj(oa, ob, x2, woa, wob, ng, rw, rb, tri):
    t = x2.shape[0]
    tm = OUTPROJ_TILE
    row = lambda i: (i, 0)
    full = lambda a: pl.BlockSpec(a.shape, (lambda i: (0,) * a.ndim))
    out_shapes = (
        jax.ShapeDtypeStruct((t, D_MODEL), F32),
        jax.ShapeDtypeStruct((t, D_MODEL // 2), U32),
        jax.ShapeDtypeStruct((8, t), I32),
        jax.ShapeDtypeStruct((t, LANES), F32),
        jax.ShapeDtypeStruct((8, LANES), I32),
    )
    return pl.pallas_call(
        functools.partial(_outproj_body, tm=tm),
        grid=(t // tm,),
        in_specs=[pl.BlockSpec((tm, GM_WIDTH), row), pl.BlockSpec((tm, DN_VAL), row),
                  pl.BlockSpec((tm, D_MODEL), row), full(woa), full(wob), full(ng), full(rw),
                  full(rb), full(tri)],
        out_specs=(pl.BlockSpec((tm, D_MODEL), row), pl.BlockSpec((tm, D_MODEL // 2), row),
                   pl.BlockSpec((8, tm), lambda i: (0, i)), pl.BlockSpec((tm, LANES), row),
                   pl.BlockSpec((8, LANES), lambda i: (0, 0))),
        out_shape=out_shapes,
        scratch_shapes=[pltpu.VMEM((8, LANES), F32)],
        compiler_params=pltpu.CompilerParams(dimension_semantics=("arbitrary",),
                                             vmem_limit_bytes=VMEM_LIMIT),
        name="outproj",
    )(oa, ob, x2, woa, wob, ng, rw, rb, tri)


def _sc_dispatch(hp, dest_k, n_rows):
    t, d = hp.shape
    mesh = plsc.VectorSubcoreMesh(core_axis_name="c", subcore_axis_name="s")
    nc, workers = mesh.num_cores, mesh.num_cores * mesh.num_subcores
    chunk = SC_GATHER_ROWS
    per_w = t // workers
    pairs = per_w // (2 * chunk)
    assert per_w * workers == t and pairs * 2 * chunk == per_w

    @functools.partial(
        pl.kernel, mesh=mesh, out_type=jax.ShapeDtypeStruct((n_rows, d), hp.dtype),
        scratch_types=[pltpu.VMEM((2, TOP_K, chunk), I32), pltpu.VMEM((2, chunk, d), hp.dtype),
                       pltpu.SemaphoreType.DMA((2,)), pltpu.SemaphoreType.DMA((2,)),
                       pltpu.SemaphoreType.DMA((2,))],
        name="sc_dispatch")
    def scatter(hp_hbm, idx_hbm, xin_hbm, idx_v, rows_v, lsem, ssem, isem):
        base_w = (lax.axis_index("s") * nc + lax.axis_index("c")) * per_w

        def index(b, kk, base):
            return pltpu.make_async_copy(idx_hbm.at[pl.ds(kk * t + base, chunk)], idx_v.at[b, kk],
                                         isem.at[b])

        def load(b, base):
            return pltpu.make_async_copy(hp_hbm.at[pl.ds(base, chunk)], rows_v.at[b], lsem.at[b])

        def put(b, kk):
            return pltpu.make_async_copy(rows_v.at[b], xin_hbm.at[idx_v.at[b, kk]], ssem.at[b])

        def body(j, carry):
            for b in range(2):
                base = base_w + (2 * j + b) * chunk

                @pl.when(j > 0)
                def _():
                    for kk in range(TOP_K):
                        put(b, kk).wait()

                for kk in range(TOP_K):
                    index(b, kk, base).start()
                load(b, base).start()
            for b in range(2):
                for kk in range(TOP_K):
                    index(b, kk, base_w).wait()
                load(b, base_w).wait()
                for kk in range(TOP_K):
                    put(b, kk).start()
            return carry

        lax.fori_loop(0, pairs, body, 0)
        for b in range(2):
            for kk in range(TOP_K):
                put(b, kk).wait()

    return scatter(hp, dest_k)


def _ffn_body(be_ref, nv_ref, br_ref, ep_ref, es_ref, x_ref, wgu_hbm, bgu_ref, wd_hbm, bd_ref, y_ref,
              wgub_ref, wdb_ref, wguf_ref, wdf_ref, gsem, dsem):
    i = pl.program_id(0)

    @pl.when(i >= nv_ref[0])
    def _():
        y_ref[...] = jnp.zeros(y_ref.shape, U32)

    def fetch(pos):
        e, slot = es_ref[pos], pos % 2
        return (pltpu.make_async_copy(wgu_hbm.at[e], wguf_ref.at[slot], gsem.at[slot]),
                pltpu.make_async_copy(wd_hbm.at[e], wdf_ref.at[slot], dsem.at[slot]))

    @pl.when((i < nv_ref[0]) & ((i == 0) | (be_ref[i] != be_ref[jnp.maximum(i - 1, 0)])))
    def _():
        pos = ep_ref[i]

        @pl.when(i == 0)
        def _():
            for cp in fetch(pos):
                cp.start()

        for cp in fetch(pos):
            cp.wait()

        @pl.when(pos + 1 < es_ref[N_EXPERTS])
        def _():
            for cp in fetch(pos + 1):
                cp.start()

        slot = pos % 2
        wgub_ref[...] = wguf_ref[slot].astype(BF16)
        wdb_ref[...] = wdf_ref[slot].astype(BF16)

    nsub = FFN_SUBBLOCKS
    rsub = x_ref.shape[0] // nsub

    def compute(active):
        half = D_MODEL // 2
        row = lax.broadcasted_iota(I32, (rsub, 1), 0)
        gus = []
        for sb in range(active):
            rs = slice(sb * rsub, (sb + 1) * rsub)
            xp = jnp.where(row + sb * rsub < br_ref[i], x_ref[rs, :], jnp.uint32(0))
            lo = pltpu.bitcast(xp << 16, F32).astype(BF16)
            hi = pltpu.bitcast(xp & jnp.uint32(0xFFFF0000), F32).astype(BF16)
            gus.append(_dot(lo, wgub_ref[:half, :]) + _dot(hi, wgub_ref[half:, :]) + bgu_ref[0])
        ys = []
        for gu in gus:
            gate = jnp.minimum(gu[:, :D_FF], SWIGLU_LIMIT)
            up = jnp.clip(gu[:, D_FF:], -SWIGLU_LIMIT, SWIGLU_LIMIT)
            act = (up + 1.0) * (gate * _sigmoid(SWIGLU_ALPHA * gate))
            ys.append(_dot(act.astype(BF16), wdb_ref[...]) + bd_ref[0])
        for sb, y in enumerate(ys):
            ylo = pltpu.bitcast(y[:, :half].astype(BF16).astype(F32), U32) >> 16
            yhi = pltpu.bitcast(y[:, half:].astype(BF16).astype(F32), U32) & jnp.uint32(0xFFFF0000)
            y_ref[sb * rsub:(sb + 1) * rsub, :] = ylo | yhi
        if active < nsub:
            y_ref[active * rsub:, :] = jnp.zeros(((nsub - active) * rsub, half), U32)

    active = (br_ref[i] + rsub - 1) // rsub
    for n in range(1, nsub + 1):
        pl.when((i < nv_ref[0]) & (active == n))(functools.partial(compute, n))


def _ffn(blk_e, nvalid, blk_rows, blk_pos, used, xin, wgu, bgu, wd, bd):
    p = xin.shape[0]
    bm = FFN_BLOCK
    rows = lambda i, be, nv, br, ep, es: (jnp.minimum(i, nv[0] - 1), 0)
    wsel = lambda i, be, nv, br, ep, es: (be[i], 0, 0)
    return pl.pallas_call(
        _ffn_body,
        grid_spec=pltpu.PrefetchScalarGridSpec(
            num_scalar_prefetch=5,
            grid=(p // bm,),
            in_specs=[pl.BlockSpec((bm, D_MODEL // 2), rows),
                      pl.BlockSpec(memory_space=pl.ANY),
                      pl.BlockSpec((1, 1, 2 * D_FF), wsel),
                      pl.BlockSpec(memory_space=pl.ANY),
                      pl.BlockSpec((1, 1, D_MODEL), wsel)],
            out_specs=pl.BlockSpec((bm, D_MODEL // 2), lambda i, be, nv, br, ep, es: (i, 0)),
            scratch_shapes=[pltpu.VMEM((D_MODEL, 2 * D_FF), BF16), pltpu.VMEM((D_FF, D_MODEL), BF16),
                            pltpu.VMEM((2, D_MODEL, 2 * D_FF), F32), pltpu.VMEM((2, D_FF, D_MODEL), F32),
                            pltpu.SemaphoreType.DMA((2,)), pltpu.SemaphoreType.DMA((2,))],
        ),
        out_shape=jax.ShapeDtypeStruct((p, D_MODEL // 2), U32),
        compiler_params=pltpu.CompilerParams(dimension_semantics=("arbitrary",),
                                             vmem_limit_bytes=VMEM_LIMIT),
        name="ffn",
    )(blk_e, nvalid, blk_rows, blk_pos, used, xin, wgu, bgu, wd, bd)


def _sc_gather(table, idx):
    r, d = idx.shape[0], table.shape[1]
    mesh = plsc.VectorSubcoreMesh(core_axis_name="c", subcore_axis_name="s")
    nc, workers = mesh.num_cores, mesh.num_cores * mesh.num_subcores
    chunk = SC_GATHER_ROWS
    per_w = r // workers
    pairs = per_w // (2 * chunk)
    assert per_w * workers == r and pairs * 2 * chunk == per_w

    @functools.partial(
        pl.kernel, mesh=mesh, out_type=jax.ShapeDtypeStruct((r, d), table.dtype),
        scratch_types=[pltpu.VMEM((per_w,), I32), pltpu.VMEM((2, chunk, d), table.dtype),
                       pltpu.SemaphoreType.DMA((2,)), pltpu.SemaphoreType.DMA((2,))],
        name="sc_gather")
    def gather(table_hbm, idx_hbm, out_hbm, idx_v, rows_v, gsem, wsem):
        base_w = (lax.axis_index("s") * nc + lax.axis_index("c")) * per_w
        pltpu.sync_copy(idx_hbm.at[pl.ds(base_w, per_w)], idx_v)

        def fetch(b, c):
            return pltpu.make_async_copy(table_hbm.at[idx_v.at[pl.ds(c * chunk, chunk)]],
                                         rows_v.at[b], gsem.at[b])

        def flush(b, base):
            return pltpu.make_async_copy(rows_v.at[b], out_hbm.at[pl.ds(base, chunk)], wsem.at[b])

        def body(j, carry):
            for b in range(2):
                @pl.when(j > 0)
                def _():
                    flush(b, base_w).wait()

                fetch(b, 2 * j + b).start()
            for b in range(2):
                fetch(b, 2 * j + b).wait()
                flush(b, base_w + (2 * j + b) * chunk).start()
            return carry

        lax.fori_loop(0, pairs, body, 0)
        for b in range(2):
            flush(b, base_w).wait()

    return gather(table, idx)


def _combine_body(h_ref, gate_ref, fg_ref, *refs):
    yg_refs, o_ref = refs[:TOP_K], refs[-1]
    half = D_MODEL // 2
    gates = gate_ref[...]
    h = h_ref[...]
    lo, hi = h[:, :half], h[:, half:]
    for kk in range(TOP_K):
        yp = yg_refs[kk][...]
        g = gates[:, kk:kk + 1]
        lo = lo + g * pltpu.bitcast(yp << 16, F32)
        hi = hi + g * pltpu.bitcast(yp & jnp.uint32(0xFFFF0000), F32)
    out = jnp.concatenate([lo, hi], axis=1)
    o_ref[...] = _rms(out, fg_ref[...])


def _combine(h, gates, fg, ygath, out_prev, *, part, parts):
    t = h.shape[0]
    tm = min(COMBINE_TILE, t // parts)
    nt = t // parts // tm
    row = lambda i: (part * nt + i, 0)
    slot = lambda kk: pl.BlockSpec((tm, D_MODEL // 2), lambda i: (kk * nt + i, 0))
    in_specs = [pl.BlockSpec((tm, D_MODEL), row), pl.BlockSpec((tm, LANES), row),
                pl.BlockSpec((1, D_MODEL), lambda i: (0, 0))] + [slot(kk) for kk in range(TOP_K)]
    args = [h, gates, fg] + [ygath] * TOP_K
    aliases = {}
    if out_prev is not None:
        aliases = {len(args): 0}
        in_specs.append(pl.BlockSpec(memory_space=pl.ANY))
        args.append(out_prev)
    return pl.pallas_call(
        _combine_body,
        grid=(nt,),
        in_specs=in_specs,
        out_specs=pl.BlockSpec((tm, D_MODEL), row),
        out_shape=jax.ShapeDtypeStruct((t, D_MODEL), F32),
        input_output_aliases=aliases,
        compiler_params=pltpu.CompilerParams(dimension_semantics=("arbitrary",),
                                             vmem_limit_bytes=VMEM_LIMIT),
        name="combine",
    )(*args)


def _block_tril(n, chunk, strict):
    r = jnp.arange(n)[:, None]
    c = jnp.arange(n)[None, :]
    keep = ((r // chunk) == (c // chunk)) & ((r > c) if strict else (r >= c))
    return keep.astype(BF16)


def _pad_lanes(a, fill=0.0):
    a = a.reshape(1, -1).astype(F32)
    return jnp.pad(a, ((0, 0), (0, LANES - a.shape[1])), constant_values=fill)


def _layer(h, norm_mix_g, w_in, gm_norm_g, gm_ws, gm_bs, dn_conv_w, dn_a_log, dn_dt_bias,
           dn_norm_g, w_out, norm_ffn_g, router_w, router_b, exp_w_gu, exp_b_gu, exp_w_down,
           exp_b_down, out_g):
    batch, seq, d = h.shape
    t = batch * seq
    x2 = h.reshape(t, d)

    c0, c1, c2 = GM_WIDTH, 2 * GM_WIDTH, 2 * GM_WIDTH + 3 * DN_KEY
    c3 = c2 + DN_VAL
    wb = w_in.astype(BF16)
    wu, wv, wqkv, wz = wb[:, :c0], wb[:, c0:c1], wb[:, c1:c2], wb[:, c2:c3]
    wab = jnp.pad(wb[:, c3:], ((0, 0), (0, LANES - 2 * DN_HEADS)))
    gmg = gm_norm_g.reshape(1, GM_WIDTH).astype(F32)
    bsb = jnp.repeat(gm_bs.T, HEAD_DIM, axis=1).astype(F32)
    alog = _pad_lanes(dn_a_log)
    dtb = _pad_lanes(dn_dt_bias)
    tri_incl = _block_tril(GM_CHUNK, DN_CHUNK, strict=False)

    oa, q, k, v, z, gb, gt = _inproj(
        x2, norm_mix_g.reshape(1, d), wu, wv, wqkv, wz, wab, gmg, gm_ws.astype(F32), bsb,
        dn_conv_w.astype(F32), alog, dtb, tri_incl, seq=seq)

    grow = gt[:DN_HEADS].reshape(DN_HEADS, t // DN_CHUNK, DN_CHUNK).transpose(1, 0, 2)
    grow = grow.reshape(t // DN_CHUNK, DN_HEADS * DN_CHUNK)
    ob = _gdn(q, k, v, z, gb, grow, dn_norm_g.reshape(1, HEAD_DIM).astype(F32), batch=batch, seq=seq)

    wo = w_out.astype(BF16)
    rw = jnp.pad(router_w.astype(BF16), ((0, 0), (0, LANES - N_EXPERTS)))
    rb = _pad_lanes(router_b, fill=NEG_BIG)
    tri_strict = _block_tril(LANES, LANES, strict=True)
    hres, hp, meta, gates, cnt = _outproj(oa, ob, x2, wo[:GM_WIDTH], wo[GM_WIDTH:],
                                          norm_ffn_g.reshape(1, d), rw, rb, tri_strict)

    bm = FFN_BLOCK
    counts = cnt[0, :N_EXPERTS]
    padded = (counts + bm - 1) // bm * bm
    pad_end = jnp.cumsum(padded)
    pad_start = (pad_end - padded).astype(I32)
    n_blocks = (t * TOP_K + N_EXPERTS * bm) // bm
    nvalid = (pad_end[-1] // bm).astype(I32).reshape(1)
    blk = jnp.minimum(jnp.arange(n_blocks, dtype=I32), nvalid[0] - 1)
    blk_e = jnp.minimum(jnp.sum(pad_end[None, :] <= (blk * bm)[:, None], axis=1), N_EXPERTS - 1).astype(I32)
    eid = meta[:TOP_K]
    start_of = jnp.sum(jnp.where(eid[..., None] == jnp.arange(N_EXPERTS, dtype=I32), pad_start, 0), axis=-1)
    dest = start_of + meta[TOP_K:2 * TOP_K]

    xin = _sc_dispatch(hp, dest.reshape(-1), n_blocks * bm)
    eids = jnp.arange(N_EXPERTS, dtype=I32)
    of_blk = blk_e[:, None] == eids
    blk_rows = jnp.clip(jnp.sum(jnp.where(of_blk, counts + pad_start, 0), axis=1) - blk * bm, 0, bm)
    has_rows = counts > 0
    used = jnp.minimum(jnp.sort(jnp.where(has_rows, eids, N_EXPERTS + eids)), N_EXPERTS - 1)
    used = jnp.concatenate([used, jnp.sum(has_rows).reshape(1)]).astype(I32)
    blk_pos = jnp.sum(jnp.where(of_blk, jnp.cumsum(has_rows) - 1, 0), axis=1).astype(I32)
    blk_rows = blk_rows.astype(I32)
    y = _ffn(blk_e, nvalid, blk_rows, blk_pos, used, xin, exp_w_gu, exp_b_gu[:, None, :].astype(F32),
             exp_w_down, exp_b_down[:, None, :].astype(F32))
    out = None
    tp = t // COMBINE_PARTS
    for part in range(COMBINE_PARTS):
        ygath = _sc_gather(y, dest[:, part * tp:(part + 1) * tp].reshape(-1))
        out = _combine(hres, gates, out_g.reshape(1, d).astype(F32), ygath, out,
                       part=part, parts=COMBINE_PARTS)
    return out.reshape(batch, seq, d)


def kernel(x, norm_mix_g, w_in, gm_norm_g, gm_ws, gm_bs, dn_conv_w, dn_a_log, dn_dt_bias, dn_norm_g, w_out, norm_ffn_g, router_w, router_b, exp_w_gu, exp_b_gu, exp_w_down, exp_b_down, final_norm_g):
    depth = norm_mix_g.shape[0]
    assert depth == 1, "single-layer problem"
    return _layer(x, norm_mix_g[0], w_in[0], gm_norm_g[0], gm_ws[0], gm_bs[0], dn_conv_w[0],
                  dn_a_log[0], dn_dt_bias[0], dn_norm_g[0], w_out[0], norm_ffn_g[0], router_w[0],
                  router_b[0], exp_w_gu[0], exp_b_gu[0], exp_w_down[0], exp_b_down[0], final_norm_g)
```
